```python
import math
import jax, jax.numpy as jnp
from jax import lax
import numpy as np

D_MODEL = 4096
BATCH = 1
SEQ = 8192
DEPTH = 2

N_META = 16
RMS_EPS = 1e-5

RWKV_HEAD = 64
RWKV_HEADS = D_MODEL // RWKV_HEAD
DECAY_LORA = max(32, int(round(1.8 * D_MODEL ** 0.5 / 32)) * 32)
AAA_LORA = max(32, int(round(1.8 * D_MODEL ** 0.5 / 32)) * 32)
GATE_LORA = max(32, int(round(0.6 * D_MODEL ** 0.8 / 32)) * 32)
LN_X_EPS = 64e-5
N_SHIFT_MIX = 6

ATT_HEAD = 64
ATT_HEADS = D_MODEL // ATT_HEAD
ATT_KV_HEADS = 8
ATT_GROUP = ATT_HEADS // ATT_KV_HEADS
WINDOW = 128
BLOCK = 128

D_FF = 256 * (-(-int(8 * D_MODEL / 3) // 256))
N_EXPERTS = 8
TOP_K = 2
D_EXPERT = 7 * D_MODEL // 8

N_A = DEPTH // 2
N_B = DEPTH - N_A
N_DENSE = (DEPTH + 1) // 2
N_MOE = DEPTH // 2

kernel_name = "yoco_rwkv7_swa_sink_moe_hybrid"

F32 = jnp.float32


def _rmsnorm(x, g):
    xf = x.astype(F32)
    y = xf * lax.rsqrt(jnp.mean(xf * xf, axis=-1, keepdims=True) + RMS_EPS)
    return (y * g.astype(F32)).astype(x.dtype)


def _alibi_slopes():
    h = jnp.arange(1, ATT_HEADS + 1, dtype=F32)
    return (2.0 ** (-8.0 * h / ATT_HEADS)).reshape(ATT_KV_HEADS, ATT_GROUP)


def _wkv7_scan(r, w, k, v, a_vec, b_vec):
    def step(S, inp):
        r_t, w_t, k_t, v_t, a_t, b_t = inp
        sa = jnp.einsum('bhij,bhj->bhi', S, a_t)
        S = S * w_t[:, :, None, :] + sa[..., None] * b_t[:, :, None, :] + v_t[..., None] * k_t[:, :, None, :]
        return S, jnp.einsum('bhij,bhj->bhi', S, r_t)
    B_, _, H, N = r.shape
    S0 = jnp.zeros((B_, H, N, N), F32)
    xs = tuple(jnp.moveaxis(t, 1, 0) for t in (r, w, k, v, a_vec, b_vec))
    _, y = lax.scan(step, S0, xs)
    return jnp.moveaxis(y, 0, 1)


def _rwkv7_time_mix(x, mix, w_rkv, w0, w1, w2, a0, a1, a2, g1, g2, k_k, k_a, r_k, lnx_w, lnx_b, w_out):
    B_, L, D = x.shape
    H, N = RWKV_HEADS, RWKV_HEAD
    xx = jnp.pad(x, ((0, 0), (1, 0), (0, 0)))[:, :-1] - x
    xr, xw, xk, xv, xa, xg = (x + xx * mix[i] for i in range(N_SHIFT_MIX))
    r, k, v = jnp.einsum('pbld,pde->pble', jnp.stack([xr, xk, xv]), w_rkv)
    w_log = -jax.nn.softplus(-(w0 + jnp.tanh(xw @ w1) @ w2).astype(F32)) - 0.5
    decay = jnp.exp(-jnp.exp(w_log))
    a = jax.nn.sigmoid((a0 + (xa @ a1) @ a2).astype(F32))
    g = jax.nn.sigmoid(xg @ g1) @ g2
    kk = (k * k_k).astype(F32).reshape(B_, L, H, N)
    kk = kk / jnp.maximum(jnp.linalg.norm(kk, axis=-1, keepdims=True), 1e-12)
    k = k.astype(F32) * (1.0 + (a - 1.0) * k_a.astype(F32))
    rh = r.astype(F32).reshape(B_, L, H, N)
    kh = k.reshape(B_, L, H, N)
    vh = v.astype(F32).reshape(B_, L, H, N)
    ah = a.reshape(B_, L, H, N)
    y = _wkv7_scan(rh, decay.reshape(B_, L, H, N), kh, vh, -kk, kk * ah)
    mu = jnp.mean(y, axis=-1, keepdims=True)
    var = jnp.mean(jnp.square(y - mu), axis=-1, keepdims=True)
    yn = ((y - mu) * lax.rsqrt(var + LN_X_EPS)).reshape(B_, L, D) * lnx_w.astype(F32) + lnx_b.astype(F32)
    bonus = jnp.sum(rh * kh * r_k.astype(F32), axis=-1, keepdims=True) * vh
    out = (yn + bonus.reshape(B_, L, D)).astype(x.dtype) * g
    return out @ w_out


def _swa_sink_attention(h, w_q, sinks, w_out, k_meta, v_meta, k_real, v_real):
    B_, S, D = h.shape
    NB = S // BLOCK
    q = (h @ w_q).reshape(B_, NB, BLOCK, ATT_KV_HEADS, ATT_GROUP, ATT_HEAD) * (ATT_HEAD ** -0.5)

    def band(t):
        t = t.reshape(B_, NB, BLOCK, ATT_KV_HEADS, ATT_HEAD)
        prev = jnp.pad(t, ((0, 0), (1, 0), (0, 0), (0, 0), (0, 0)))[:, :-1]
        return jnp.concatenate([prev, t], axis=2)

    kb, vb = band(k_real), band(v_real)
    slopes = _alibi_slopes()

    s_band = jnp.einsum('bnqkgd,bnskd->bnkgqs', q, kb).astype(F32)
    qi = jnp.arange(BLOCK)[:, None]
    kj = jnp.arange(2 * BLOCK)[None, :]
    dist_band = BLOCK + qi - kj
    blk = jnp.arange(NB)[:, None, None]
    valid = (dist_band >= 0) & (dist_band < WINDOW) & ((blk > 0) | (kj >= BLOCK))
    s_band = jnp.where(valid[None, :, None, None],
                       s_band - slopes[:, :, None, None] * dist_band.astype(F32),
                       -jnp.inf)

    s_meta = jnp.einsum('bnqkgd,bmkd->bnkgqm', q, k_meta).astype(F32)
    q_pos = N_META + jnp.arange(NB)[:, None] * BLOCK + jnp.arange(BLOCK)[None, :]
    dist_meta = (q_pos[:, :, None] - jnp.arange(N_META)[None, None, :]).astype(F32)
    s_meta = s_meta - slopes[None, :, :, None, None] * dist_meta[:, None, None]

    sink = jnp.broadcast_to(sinks.astype(F32).reshape(ATT_KV_HEADS, ATT_GROUP, 1, 1),
                            s_meta.shape[:-1] + (1,))
    p = jax.nn.softmax(jnp.concatenate([s_meta, s_band, sink], axis=-1), axis=-1)
    p_meta = p[..., :N_META].astype(h.dtype)
    p_band = p[..., N_META:N_META + 2 * BLOCK].astype(h.dtype)
    o = (jnp.einsum('bnkgqm,bmkd->bnqkgd', p_meta, v_meta)
         + jnp.einsum('bnkgqs,bnskd->bnqkgd', p_band, vb))
    return o.reshape(B_, S, D) @ w_out


def _swiglu(x, wg, wu, wd):
    return (jax.nn.silu(x @ wg) * (x @ wu)) @ wd


def _moe(x, router, wg, wu, wd):
    logits = (x @ router).astype(F32)
    top_v, top_i = lax.top_k(logits, TOP_K)
    gates = jax.nn.softmax(top_v, axis=-1)
    combine = jnp.sum(jax.nn.one_hot(top_i, N_EXPERTS, dtype=F32) * gates[..., None], axis=-2)
    y = jnp.zeros_like(x)
    for e in range(N_EXPERTS):
        y = y + combine[..., e:e + 1].astype(x.dtype) * _swiglu(x, wg[e], wu[e], wd[e])
    return y


def setup_inputs(seed: int = 0) -> dict:
    key = jax.random.key(seed)
    ks = iter(jax.random.split(key, 64))
    D, H, N = D_MODEL, RWKV_HEADS, RWKV_HEAD

    def nrm(shape, scale):
        return jax.random.normal(next(ks), shape, F32) * scale

    def unif(shape, lo, hi):
        return jax.random.uniform(next(ks), shape, F32, lo, hi)

    def gain(shape):
        return 1.0 + nrm(shape, 0.02)

    return {
        "x": nrm((BATCH, SEQ, D), 1.0),
        "meta_tokens": nrm((N_META, D), 1.0),
        "a_norm": gain((N_A, D)),
        "a_mix": unif((N_A, N_SHIFT_MIX, D), 0.0, 1.0),
        "a_w_rkv": nrm((N_A, 3, D, D), D ** -0.5),
        "a_w0": unif((N_A, D), -6.0, 0.0),
        "a_w1": nrm((N_A, D, DECAY_LORA), D ** -0.5),
        "a_w2": nrm((N_A, DECAY_LORA, D), 0.5 * DECAY_LORA ** -0.5),
        "a_a0": nrm((N_A, D), 0.5),
        "a_a1": nrm((N_A, D, AAA_LORA), D ** -0.5),
        "a_a2": nrm((N_A, AAA_LORA, D), 0.5 * AAA_LORA ** -0.5),
        "a_g1": nrm((N_A, D, GATE_LORA), D ** -0.5),
        "a_g2": nrm((N_A, GATE_LORA, D), GATE_LORA ** -0.5),
        "a_k_k": 0.85 + nrm((N_A, D), 0.02),
        "a_k_a": gain((N_A, D)),
        "a_r_k": nrm((N_A, H, N), 0.1),
        "a_lnx_w": gain((N_A, D)),
        "a_lnx_b": nrm((N_A, D), 0.02),
        "a_w_out": nrm((N_A, D, D), D ** -0.5),
        "kv_norm": gain((D,)),
        "w_kv": nrm((D, 2 * ATT_KV_HEADS * ATT_HEAD), D ** -0.5),
        "b_norm": gain((N_B, D)),
        "b_w_q": nrm((N_B, D, ATT_HEADS * ATT_HEAD), D ** -0.5),
        "b_sinks": nrm((N_B, ATT_HEADS), 1.0),
        "b_w_out": nrm((N_B, ATT_HEADS * ATT_HEAD, D), D ** -0.5),
        "f_norm": gain((DEPTH, D)),
        "d_w_gate": nrm((N_DENSE, D, D_FF), D ** -0.5),
        "d_w_up": nrm((N_DENSE, D, D_FF), D ** -0.5),
        "d_w_down": nrm((N_DENSE, D_FF, D), D_FF ** -0.5),
        "e_router": nrm((N_MOE, D, N_EXPERTS), D ** -0.5),
        "e_w_gate": nrm((N_MOE, N_EXPERTS, D, D_EXPERT), D ** -0.5),
        "e_w_up": nrm((N_MOE, N_EXPERTS, D, D_EXPERT), D ** -0.5),
        "e_w_down": nrm((N_MOE, N_EXPERTS, D_EXPERT, D), D_EXPERT ** -0.5),
        "final_norm": gain((D,)),
    }


def reference(x, meta_tokens, a_norm, a_mix, a_w_rkv, a_w0, a_w1, a_w2, a_a0, a_a1, a_a2, a_g1, a_g2,
              a_k_k, a_k_a, a_r_k, a_lnx_w, a_lnx_b, a_w_out, kv_norm, w_kv, b_norm, b_w_q, b_sinks,
              b_w_out, f_norm, d_w_gate, d_w_up, d_w_down, e_router, e_w_gate, e_w_up, e_w_down,
              final_norm):
    B_ = x.shape[0]
    meta = jnp.broadcast_to(meta_tokens.astype(x.dtype)[None], (B_, N_META, D_MODEL))
    h = jnp.concatenate([meta, x], axis=1)
    shared = None
    for layer in range(DEPTH):
        if layer < N_A:
            i = layer
            h = h + _rwkv7_time_mix(_rmsnorm(h, a_norm[i]), a_mix[i], a_w_rkv[i], a_w0[i], a_w1[i], a_w2[i],
                                    a_a0[i], a_a1[i], a_a2[i], a_g1[i], a_g2[i], a_k_k[i], a_k_a[i],
                                    a_r_k[i], a_lnx_w[i], a_lnx_b[i], a_w_out[i])
        else:
            if shared is None:
                kv = (_rmsnorm(h, kv_norm) @ w_kv).reshape(B_, -1, 2, ATT_KV_HEADS, ATT_HEAD)
                k_all, v_all = kv[:, :, 0], kv[:, :, 1]
                shared = (k_all[:, :N_META], v_all[:, :N_META], k_all[:, N_META:], v_all[:, N_META:])
                h = h[:, N_META:]
            j = layer - N_A
            h = h + _swa_sink_attention(_rmsnorm(h, b_norm[j]), b_w_q[j], b_sinks[j], b_w_out[j],
                                        shared[0], shared[1], shared[2], shared[3])
        c = layer // 2
        hn = _rmsnorm(h, f_norm[layer])
        if layer % 2 == 0:
            h = h + _swiglu(hn, d_w_gate[c], d_w_up[c], d_w_down[c])
        else:
            h = h + _moe(hn, e_router[c], e_w_gate[c], e_w_up[c], e_w_down[c])
    return _rmsnorm(h, final_norm)
```

```python
import functools

import jax
import jax.numpy as jnp
from jax import lax
from jax.experimental import pallas as pl
from jax.experimental.pallas import tpu as pltpu

F32 = jnp.float32
BF16 = jnp.bfloat16

LANES = 128
VMEM_LIMIT_BYTES = 56 * 1024 * 1024

N_META = 16
RMS_EPS = 1e-5
LN_X_EPS = 64e-5
HEAD = 64
ATT_KV_HEADS = 8
WINDOW = 128
BLOCK = 128
TOP_K = 2
CHUNK = 64
NEG = -1e30


def _cparams(sem):
    return pltpu.CompilerParams(dimension_semantics=sem, vmem_limit_bytes=VMEM_LIMIT_BYTES)


def _dot(a, b):
    return jnp.dot(a, b, preferred_element_type=F32)


def _dot_nt(a, b):
    return lax.dot_general(a, b, (((1,), (1,)), ((), ())), preferred_element_type=F32)


def _rms(x):
    return x * lax.rsqrt(jnp.mean(x * x, axis=-1, keepdims=True) + RMS_EPS)


def _rmsnorm_body(h_ref, g_ref, *o_refs):
    y = _rms(h_ref[...])
    for j, o_ref in enumerate(o_refs):
        o_ref[...] = (y * g_ref[j:j + 1, :]).astype(o_ref.dtype)


def _rmsnorm(h, gains, out_dtypes, tm):
    m, d = h.shape
    n_out = len(out_dtypes)
    outs = pl.pallas_call(
        _rmsnorm_body,
        out_shape=[jax.ShapeDtypeStruct((m, d), dt) for dt in out_dtypes],
        grid=(m // tm,),
        in_specs=[pl.BlockSpec((tm, d), lambda i: (i, 0)),
                  pl.BlockSpec((n_out, d), lambda i: (0, 0))],
        out_specs=[pl.BlockSpec((tm, d), lambda i: (i, 0)) for _ in out_dtypes],
        compiler_params=_cparams(("parallel",)),
        name="rmsnorm",
    )(h, gains)
    return outs


def _premix_body(h_ref, hp_ref, g_ref, mix_ref, *o_refs):
    i = pl.program_id(0)
    g = g_ref[...]
    xn = _rms(h_ref[...]) * g
    pn = _rms(hp_ref[...]) * g
    prev_row = jnp.where(i > 0, pn[7:8, :], 0.0)
    sh = pltpu.roll(xn, 1, axis=0)
    row = lax.broadcasted_iota(jnp.int32, xn.shape, 0)
    sh = jnp.where(row == 0, prev_row, sh)
    xx = sh - xn
    for j, o_ref in enumerate(o_refs):
        o_ref[...] = (xn + xx * mix_ref[j:j + 1, :]).astype(o_ref.dtype)


def _premix(h, gain, mix, tm):
    m, d = h.shape
    n_mix = mix.shape[0]
    rows8 = tm // 8
    return pl.pallas_call(
        _premix_body,
        out_shape=[jax.ShapeDtypeStruct((m, d), BF16) for _ in range(n_mix)],
        grid=(m // tm,),
        in_specs=[pl.BlockSpec((tm, d), lambda i: (i, 0)),
                  pl.BlockSpec((8, d), lambda i: (jnp.maximum(i * rows8 - 1, 0), 0)),
                  pl.BlockSpec((1, d), lambda i: (0, 0)),
                  pl.BlockSpec((n_mix, d), lambda i: (0, 0))],
        out_specs=[pl.BlockSpec((tm, d), lambda i: (i, 0)) for _ in range(n_mix)],
        compiler_params=_cparams(("parallel",)),
        name="premix",
    )(h, h, gain, mix)


def _mm_body(x_ref, w_ref, *rest, has_res, scale):
    if has_res:
        res_ref, o_ref = rest
    else:
        (o_ref,) = rest
    acc = _dot(x_ref[...], w_ref[...])
    if scale is not None:
        acc = acc * scale
    if has_res:
        acc = acc + res_ref[...]
    o_ref[...] = acc.astype(o_ref.dtype)


def _mm(x, w, *, out_dtype, tm, tn, res=None, scale=None):
    m, k = x.shape
    n = w.shape[1]
    in_specs = [pl.BlockSpec((tm, k), lambda j, i: (i, 0)),
                pl.BlockSpec((k, tn), lambda j, i: (0, j))]
    args = [x, w]
    if res is not None:
        in_specs.append(pl.BlockSpec((tm, tn), lambda j, i: (i, j)))
        args.append(res)
    return pl.pallas_call(
        functools.partial(_mm_body, has_res=res is not None, scale=scale),
        out_shape=jax.ShapeDtypeStruct((m, n), out_dtype),
        grid=(n // tn, m // tm),
        in_specs=in_specs,
        out_specs=pl.BlockSpec((tm, tn), lambda j, i: (i, j)),
        compiler_params=_cparams(("parallel", "parallel")),
        name="matmul",
    )(*args)


def _mm_acc_body(x_ref, w_ref, res_ref, *rest, comb_col):
    if comb_col is None:
        o_ref, acc_ref = rest
    else:
        comb_ref, o_ref, acc_ref = rest
    kk = pl.program_id(2)

    @pl.when(kk == 0)
    def _():
        acc_ref[...] = jnp.zeros_like(acc_ref)

    acc_ref[...] += _dot(x_ref[...], w_ref[...])

    @pl.when(kk == pl.num_programs(2) - 1)
    def _():
        acc = acc_ref[...]
        if comb_col is not None:
            acc = acc * comb_ref[:, comb_col:comb_col + 1]
        o_ref[...] = res_ref[...] + acc


def _mm_acc(x, w, res, *, tm, tn, tk, comb=None, comb_col=None):
    m, k = x.shape
    n = w.shape[1]
    in_specs = [pl.BlockSpec((tm, tk), lambda j, i, q: (i, q)),
                pl.BlockSpec((tk, tn), lambda j, i, q: (q, j)),
                pl.BlockSpec((tm, tn), lambda j, i, q: (i, j))]
    args = [x, w, res]
    if comb is not None:
        in_specs.append(pl.BlockSpec((tm, comb.shape[1]), lambda j, i, q: (i, 0)))
        args.append(comb)
    return pl.pallas_call(
        functools.partial(_mm_acc_body, comb_col=comb_col),
        out_shape=jax.ShapeDtypeStruct((m, n), F32),
        grid=(n // tn, m // tm, k // tk),
        in_specs=in_specs,
        out_specs=pl.BlockSpec((tm, tn), lambda j, i, q: (i, j)),
        scratch_shapes=[pltpu.VMEM((tm, tn), F32)],
        compiler_params=_cparams(("parallel", "parallel", "arbitrary")),
        name="matmul_acc",
    )(*args)


def _swiglu_body(x_ref, wg_ref, wu_ref, o_ref):
    x = x_ref[...]
    g = _dot(x, wg_ref[...])
    u = _dot(x, wu_ref[...])
    o_ref[...] = (g * jax.nn.sigmoid(g) * u).astype(o_ref.dtype)


def _swiglu(x, wg, wu, *, tm, tn):
    m, k = x.shape
    n = wg.shape[1]
    return pl.pallas_call(
        _swiglu_body,
        out_shape=jax.ShapeDtypeStruct((m, n), BF16),
        grid=(n // tn, m // tm),
        in_specs=[pl.BlockSpec((tm, k), lambda j, i: (i, 0)),
                  pl.BlockSpec((k, tn), lambda j, i: (0, j)),
                  pl.BlockSpec((k, tn), lambda j, i: (0, j))],
        out_specs=pl.BlockSpec((tm, tn), lambda j, i: (i, j)),
        compiler_params=_cparams(("parallel", "parallel")),
        name="swiglu",
    )(x, wg, wu)


def _lora_body(x_ref, w1_ref, w2_ref, b_ref, o_ref, *, mid, post):
    t = _dot(x_ref[...], w1_ref[...])
    if mid == "tanh":
        t = jnp.tanh(t)
    elif mid == "sigmoid":
        t = jax.nn.sigmoid(t)
    z = _dot(t.astype(BF16), w2_ref[...]) + b_ref[...]
    if post == "log_decay":
        z = jax.nn.sigmoid(z) * (-0.6065306597126334)
    elif post == "sigmoid":
        z = jax.nn.sigmoid(z)
    o_ref[...] = z.astype(o_ref.dtype)


def _lora(x, w1, w2, bias, *, mid, post, out_dtype, tm):
    m, k = x.shape
    r = w1.shape[1]
    n = w2.shape[1]
    return pl.pallas_call(
        functools.partial(_lora_body, mid=mid, post=post),
        out_shape=jax.ShapeDtypeStruct((m, n), out_dtype),
        grid=(m // tm,),
        in_specs=[pl.BlockSpec((tm, k), lambda i: (i, 0)),
                  pl.BlockSpec((k, r), lambda i: (0, 0)),
                  pl.BlockSpec((r, n), lambda i: (0, 0)),
                  pl.BlockSpec((1, n), lambda i: (0, 0))],
        out_specs=pl.BlockSpec((tm, n), lambda i: (i, 0)),
        compiler_params=_cparams(("parallel",)),
        name="lora",
    )(x, w1, w2, bias)


def _split2(x):
    hi = x.astype(BF16)
    lo = (x - hi.astype(F32)).astype(BF16)
    return hi, lo


def _wkv_body(r_ref, k_ref, v_ref, lw_ref, a_ref, g_ref, prm_ref, o_ref, st_ref, *, pairs):
    c_idx = pl.program_id(1)

    @pl.when(c_idx == 0)
    def _():
        st_ref[...] = jnp.zeros_like(st_ref)

    C = CHUNK
    lane = lax.broadcasted_iota(jnp.int32, (C, LANES), 1)
    row = lax.broadcasted_iota(jnp.int32, (C, LANES), 0)
    col = lane % HEAD
    upper_half = lane >= HEAD
    tri_strict = col < row
    tri_incl = col <= row
    eye_pair = (col == row).astype(F32)

    def level_mask(bs):
        return ((row // bs) == (col // bs)) & ((row // (bs // 2)) != (col // (bs // 2)))

    row2 = lax.broadcasted_iota(jnp.int32, (2 * C, LANES), 0)
    lane2 = lax.broadcasted_iota(jnp.int32, (2 * C, LANES), 1)
    bd_mask = (row2 >= C) == (lane2 >= HEAD)
    ones_bd = bd_mask.astype(BF16)
    diag_mask = row2 == lane2
    t_r = lax.broadcasted_iota(jnp.int32, (C, C), 0)
    t_c = lax.broadcasted_iota(jnp.int32, (C, C), 1)
    tri_cc = (t_c <= t_r).astype(BF16)

    def bd(x):
        return jnp.concatenate([jnp.where(upper_half, 0.0, x), jnp.where(upper_half, x, 0.0)], axis=0)

    def headsum(x):
        hi, lo = _split2(x)
        return _dot(hi, ones_bd) + _dot(lo, ones_bd)

    zeros_cl = jnp.zeros((C, LANES), BF16)
    zeros_ll = jnp.zeros((2 * C, LANES), BF16)

    for p in range(pairs):
        sl = slice(p * LANES, (p + 1) * LANES)
        r = r_ref[:, sl]
        k = k_ref[:, sl]
        v = v_ref[:, sl]
        lw = lw_ref[:, sl]
        a = a_ref[:, sl]
        k_k = prm_ref[0:1, sl]
        k_a = prm_ref[1:2, sl]
        r_k = prm_ref[2:3, sl]
        lnx_w = prm_ref[3:4, sl]
        lnx_b = prm_ref[4:5, sl]

        kk = k * k_k
        k2 = k * (1.0 + (a - 1.0) * k_a)
        sums = headsum(jnp.concatenate([kk * kk, r * k2 * r_k], axis=0))
        kkn = kk * lax.rsqrt(jnp.maximum(sums[:C], 1e-24))
        bonus = sums[C:] * v
        avec = -kkn
        bvec = kkn * a

        l1 = lw.astype(BF16)
        rem = lw - l1.astype(F32)
        l2 = rem.astype(BF16)
        l3 = (rem - l2.astype(F32)).astype(BF16)
        cum = _dot(tri_cc, l1) + _dot(tri_cc, l2) + _dot(tri_cc, l3)
        cum_end = cum[C - 1:C, :]
        p_incl = jnp.exp(cum)
        p_inv = jnp.exp(-cum)
        a_t = avec * jnp.exp(cum - lw)
        r_t = r * p_incl
        b_t = bvec * p_inv
        k_t = k2 * p_inv
        p_end = jnp.exp(cum_end - cum)
        b_h = bvec * p_end
        k_h = k2 * p_end

        v16 = v.astype(BF16)
        gram = _dot_nt(jnp.concatenate([a_t, r_t], axis=0).astype(BF16),
                       jnp.concatenate([bd(b_t), bd(k_t)], axis=0).astype(BF16))
        a_ab = jnp.where(tri_strict, gram[:C, :LANES], 0.0)
        a_ak = jnp.where(tri_strict, gram[:C, LANES:], 0.0)
        a_rb = jnp.where(tri_incl, gram[C:, :LANES], 0.0)
        a_rk = jnp.where(tri_incl, gram[C:, LANES:], 0.0)

        t_inv = eye_pair + jnp.where(level_mask(2), a_ab, 0.0)
        bs = 4
        while bs <= C:
            x = jnp.where(level_mask(bs), a_ab, 0.0)
            t16 = t_inv.astype(BF16)
            tx = _dot(t16, bd(x).astype(BF16))
            t_inv = t_inv + _dot(tx.astype(BF16), bd(t16))
            bs *= 2
        t16 = t_inv.astype(BF16)

        akv = _dot(a_ak.astype(BF16), bd(v16))
        tt = _dot(t16, jnp.concatenate([bd(a_t.astype(BF16)), bd(akv.astype(BF16))], axis=1))
        a_hat = tt[:, :LANES].astype(BF16)
        u0 = tt[:, LANES:].astype(BF16)

        rhs_top = jnp.concatenate([bd(a_hat), bd(u0)], axis=1)
        rhs_bot = jnp.concatenate([zeros_ll, bd(v16)], axis=1)
        qy = _dot(jnp.concatenate([a_rb, a_rk], axis=1).astype(BF16),
                  jnp.concatenate([rhs_top, rhs_bot], axis=0))
        q_hat = r_t + qy[:, :LANES]
        y0 = qy[:, LANES:]

        lhs_t = jnp.concatenate([b_h, k_h], axis=0).T.astype(BF16)
        rhs = jnp.concatenate([jnp.concatenate([a_hat, u0], axis=1),
                               jnp.concatenate([zeros_cl, v16], axis=1)], axis=0)
        upd = _dot(lhs_t, rhs)
        m_off = jnp.where(bd_mask, upd[:, :LANES], 0.0)
        n_new = jnp.where(bd_mask, upd[:, LANES:], 0.0)
        p_col = jnp.sum(jnp.where(diag_mask, jnp.exp(cum_end), 0.0), axis=1, keepdims=True)

        s = st_ref[p]
        s_hi, s_lo = _split2(s)
        q16 = q_hat.astype(BF16)
        y = _dot(q16, s_hi) + _dot(q16, s_lo) + y0
        m16 = m_off.astype(BF16)
        st_ref[p] = p_col * s + _dot(m16, s_hi) + _dot(m16, s_lo) + n_new

        mu = headsum(y) * (1.0 / HEAD)
        d = y - mu
        var = headsum(d * d) * (1.0 / HEAD)
        yn = d * lax.rsqrt(var + LN_X_EPS) * lnx_w + lnx_b
        o_ref[:, sl] = ((yn + bonus) * g_ref[:, sl].astype(F32)).astype(o_ref.dtype)


def _wkv(r, k, v, lw, a, g, prm, *, pairs):
    length, d = r.shape
    width = pairs * LANES
    blk = lambda hb, c: (c, hb)
    return pl.pallas_call(
        functools.partial(_wkv_body, pairs=pairs),
        out_shape=jax.ShapeDtypeStruct((length, d), BF16),
        grid=(d // width, length // CHUNK),
        in_specs=[pl.BlockSpec((CHUNK, width), blk)] * 6 + [pl.BlockSpec((8, width), lambda hb, c: (0, hb))],
        out_specs=pl.BlockSpec((CHUNK, width), blk),
        scratch_shapes=[pltpu.VMEM((pairs, LANES, LANES), F32)],
        compiler_params=_cparams(("parallel", "arbitrary")),
        name="wkv7",
    )(r, k, v, lw, a, g, prm)


def _attn_body(q_ref, kp_ref, kc_ref, km_ref, vp_ref, vc_ref, vm_ref, sk_ref, o_ref, *, group, n_heads):
    kvh = pl.program_id(0)
    n = pl.program_id(1)
    rows = 2 * BLOCK
    rowi = lax.broadcasted_iota(jnp.int32, (rows, BLOCK), 0)
    kj = lax.broadcasted_iota(jnp.int32, (rows, BLOCK), 1)
    qi = rowi % BLOCK
    second = rowi >= BLOCK
    dist_prev = BLOCK + qi - kj
    dist_cur = qi - kj
    valid_prev = (dist_prev < WINDOW) & (n > 0)
    valid_cur = dist_cur >= 0
    valid_meta = kj < N_META
    dist_meta = N_META + n * BLOCK + qi - kj
    lane_o = lax.broadcasted_iota(jnp.int32, (BLOCK, LANES), 1)
    lane_q = lax.broadcasted_iota(jnp.int32, (rows, LANES), 1)
    row_q = lax.broadcasted_iota(jnp.int32, (rows, LANES), 0)
    q_keep = (row_q >= BLOCK) == (lane_q >= HEAD)

    kp = kp_ref[...]
    kc = kc_ref[...]
    km = km_ref[...]
    vp = vp_ref[...]
    vc = vc_ref[...]
    vm = vm_ref[...]
    for pr in range(group // 2):
        sl = slice(pr * LANES, (pr + 1) * LANES)
        q = q_ref[:, sl]
        q2 = jnp.where(q_keep, jnp.concatenate([q, q], axis=0), jnp.zeros((), q.dtype))
        head1 = (kvh * group + 2 * pr + 1).astype(F32)
        slope = jnp.exp2((jnp.where(second, 1.0, 0.0) + head1) * (-8.0 / n_heads))
        sink = jnp.where(second[:, :1], sk_ref[kvh, 2 * pr + 1], sk_ref[kvh, 2 * pr])
        s_prev = jnp.where(valid_prev, _dot_nt(q2, kp) - slope * dist_prev.astype(F32), NEG)
        s_cur = jnp.where(valid_cur, _dot_nt(q2, kc) - slope * dist_cur.astype(F32), NEG)
        s_meta = jnp.where(valid_meta, _dot_nt(q2, km) - slope * dist_meta.astype(F32), NEG)
        mx = jnp.maximum(jnp.maximum(jnp.max(s_prev, axis=1, keepdims=True), jnp.max(s_cur, axis=1, keepdims=True)),
                         jnp.maximum(jnp.max(s_meta, axis=1, keepdims=True), sink))
        e_prev = jnp.exp(s_prev - mx)
        e_cur = jnp.exp(s_cur - mx)
        e_meta = jnp.exp(s_meta - mx)
        den = (jnp.sum(e_prev, axis=1, keepdims=True) + jnp.sum(e_cur, axis=1, keepdims=True)
               + jnp.sum(e_meta, axis=1, keepdims=True) + jnp.exp(sink - mx))
        acc = (_dot(e_prev.astype(BF16), vp) + _dot(e_cur.astype(BF16), vc) + _dot(e_meta.astype(BF16), vm))
        acc = acc / den
        o_ref[:, sl] = jnp.where(lane_o >= HEAD, acc[BLOCK:], acc[:BLOCK]).astype(o_ref.dtype)


def _attention(q, k_dup, v_dup, km_dup, vm_dup, sinks, *, n_heads):
    s_len, d = q.shape
    group = n_heads // ATT_KV_HEADS
    gw = group * HEAD
    nb = s_len // BLOCK
    prev = lambda h, n: (jnp.maximum(n - 1, 0), h)
    cur = lambda h, n: (n, h)
    meta = lambda h, n: (0, h)
    return pl.pallas_call(
        functools.partial(_attn_body, group=group, n_heads=n_heads),
        out_shape=jax.ShapeDtypeStruct((s_len, d), BF16),
        grid=(ATT_KV_HEADS, nb),
        in_specs=[pl.BlockSpec((BLOCK, gw), cur),
                  pl.BlockSpec((BLOCK, LANES), prev), pl.BlockSpec((BLOCK, LANES), cur), pl.BlockSpec((BLOCK, LANES), meta),
                  pl.BlockSpec((BLOCK, LANES), prev), pl.BlockSpec((BLOCK, LANES), cur), pl.BlockSpec((BLOCK, LANES), meta),
                  pl.BlockSpec(memory_space=pltpu.SMEM)],
        out_specs=pl.BlockSpec((BLOCK, gw), cur),
        compiler_params=_cparams(("parallel", "parallel")),
        name="swa_attention",
    )(q, k_dup, k_dup, km_dup, v_dup, v_dup, vm_dup, sinks)


def _router_body(h_ref, g_ref, wr_ref, hn_ref, comb_ref, *, n_experts):
    y = _rms(h_ref[...]) * g_ref[...]
    hn_ref[...] = y.astype(hn_ref.dtype)
    logits = jnp.dot(y, wr_ref[...], preferred_element_type=F32, precision=lax.Precision.HIGHEST)
    lane = lax.broadcasted_iota(jnp.int32, logits.shape, 1)
    logits = jnp.where(lane < n_experts, logits, NEG)
    m1 = jnp.max(logits, axis=1, keepdims=True)
    i1 = jnp.min(jnp.where(logits == m1, lane, LANES), axis=1, keepdims=True)
    rest = jnp.where(lane == i1, NEG, logits)
    m2 = jnp.max(rest, axis=1, keepdims=True)
    i2 = jnp.min(jnp.where(rest == m2, lane, LANES), axis=1, keepdims=True)
    e2 = jnp.exp(m2 - m1)
    g1 = 1.0 / (1.0 + e2)
    g2 = e2 / (1.0 + e2)
    comb_ref[...] = jnp.where(lane == i1, g1, 0.0) + jnp.where(lane == i2, g2, 0.0)


def _router(h, gain, w_router_pad, *, n_experts, tm):
    m, d = h.shape
    return pl.pallas_call(
        functools.partial(_router_body, n_experts=n_experts),
        out_shape=[jax.ShapeDtypeStruct((m, d), BF16), jax.ShapeDtypeStruct((m, LANES), F32)],
        grid=(m // tm,),
        in_specs=[pl.BlockSpec((tm, d), lambda i: (i, 0)),
                  pl.BlockSpec((1, d), lambda i: (0, 0)),
                  pl.BlockSpec((d, LANES), lambda i: (0, 0))],
        out_specs=[pl.BlockSpec((tm, d), lambda i: (i, 0)), pl.BlockSpec((tm, LANES), lambda i: (i, 0))],
        compiler_params=_cparams(("parallel",)),
        name="moe_router",
    )(h, gain, w_router_pad)


def _largest_tile(n, cap, mult):
    best = None
    t = mult
    while t <= min(n, cap):
        if n % t == 0:
            best = t
        t += mult
    assert best is not None, (n, cap, mult)
    return best


def _pad_cols(w, n_to):
    return jnp.pad(w, ((0, 0), (0, n_to - w.shape[1])))


def _pad_rows(w, n_to):
    return jnp.pad(w, ((0, n_to - w.shape[0]), (0, 0)))


def _round_up(n, m):
    return -(-n // m) * m


def kernel(x, meta_tokens, a_norm, a_mix, a_w_rkv, a_w0, a_w1, a_w2, a_a0, a_a1, a_a2, a_g1, a_g2, a_k_k, a_k_a, a_r_k, a_lnx_w, a_lnx_b, a_w_out, kv_norm, w_kv, b_norm, b_w_q, b_sinks, b_w_out, f_norm, d_w_gate, d_w_up, d_w_down, e_router, e_w_gate, e_w_up, e_w_down, final_norm):
    assert x.shape[0] == 1
    seq, d = x.shape[1], x.shape[2]
    n_heads = d // HEAD
    n_experts = e_router.shape[-1]
    xs = x[0]

    l_real = N_META + seq
    l_pad = _round_up(l_real, CHUNK)
    h = jnp.concatenate([meta_tokens.astype(F32), xs, jnp.zeros((l_pad - l_real, d), F32)], axis=0)
    tm0 = _largest_tile(l_pad, 768, 16)
    tn = _largest_tile(d, 1024, LANES)

    xr, xw, xk, xv, xa, xg = _premix(h, a_norm[0:1], a_mix[0], _largest_tile(l_pad, 256, 16))
    w_rkv = a_w_rkv[0].astype(BF16)
    r = _mm(xr, w_rkv[0], out_dtype=F32, tm=tm0, tn=tn)
    k = _mm(xk, w_rkv[1], out_dtype=F32, tm=tm0, tn=tn)
    v = _mm(xv, w_rkv[2], out_dtype=F32, tm=tm0, tn=tn)
    tml = _largest_tile(l_pad, 384, 16)
    lw = _lora(xw, a_w1[0].astype(BF16), a_w2[0].astype(BF16), a_w0[0][None], mid="tanh", post="log_decay",
               out_dtype=F32, tm=tml)
    a = _lora(xa, a_a1[0].astype(BF16), a_a2[0].astype(BF16), a_a0[0][None], mid="none", post="sigmoid",
              out_dtype=F32, tm=tml)
    gate_rank = _round_up(a_g1.shape[-1], LANES)
    g = _lora(xg, _pad_cols(a_g1[0], gate_rank).astype(BF16), _pad_rows(a_g2[0], gate_rank).astype(BF16),
              jnp.zeros((1, d), F32), mid="sigmoid", post="none", out_dtype=BF16, tm=tml)
    prm = jnp.concatenate([a_k_k[0][None], a_k_a[0][None], a_r_k[0].reshape(1, d), a_lnx_w[0][None],
                           a_lnx_b[0][None], jnp.zeros((3, d), F32)], axis=0)
    pairs = 4 if n_heads % 8 == 0 else n_heads // 2
    mixed = _wkv(r, k, v, lw, a, g, prm, pairs=pairs)
    h = _mm(mixed, a_w_out[0].astype(BF16), out_dtype=F32, tm=tm0, tn=tn, res=h)

    (hn,) = _rmsnorm(h, f_norm[0:1], [BF16], _largest_tile(l_pad, 256, 16))
    d_ff = d_w_gate.shape[-1]
    ff_pad = _round_up(d_ff, 4 * LANES)
    act = _swiglu(hn, _pad_cols(d_w_gate[0], ff_pad).astype(BF16), _pad_cols(d_w_up[0], ff_pad).astype(BF16),
                  tm=tm0, tn=_largest_tile(ff_pad, 512, LANES))
    h = _mm_acc(act, _pad_rows(d_w_down[0], ff_pad).astype(BF16), h, tm=tm0, tn=tn,
                tk=_largest_tile(ff_pad, 3072, LANES))

    hkv, hq = _rmsnorm(h, jnp.stack([kv_norm, b_norm[0]]), [BF16, BF16], _largest_tile(l_pad, 256, 16))
    kv_w = ATT_KV_HEADS * HEAD
    kv = _mm(hkv, w_kv.astype(BF16), out_dtype=BF16, tm=tm0, tn=_largest_tile(2 * kv_w, 1024, LANES))

    def dup_heads(t):
        t = t.reshape(t.shape[0], ATT_KV_HEADS, 1, HEAD)
        return jnp.broadcast_to(t, (t.shape[0], ATT_KV_HEADS, 2, HEAD)).reshape(t.shape[0], ATT_KV_HEADS * LANES)

    k_all, v_all = dup_heads(kv[:, :kv_w]), dup_heads(kv[:, kv_w:])
    k_meta = _pad_rows(k_all[:N_META], BLOCK)
    v_meta = _pad_rows(v_all[:N_META], BLOCK)
    k_real = k_all[N_META:l_real]
    v_real = v_all[N_META:l_real]

    h = h[N_META:l_real]
    hq = hq[N_META:l_real]
    tm1 = _largest_tile(seq, 512, 16)
    q = _mm(hq, b_w_q[0].astype(BF16), out_dtype=BF16, tm=tm1, tn=tn, scale=HEAD ** -0.5)
    o = _attention(q, k_real, v_real, k_meta, v_meta, b_sinks[0].reshape(ATT_KV_HEADS, -1), n_heads=n_heads)
    h = _mm(o, b_w_out[0].astype(BF16), out_dtype=F32, tm=tm1, tn=tn, res=h)

    hn, comb = _router(h, f_norm[1:2], _pad_cols(e_router[0], LANES), n_experts=n_experts,
                       tm=_largest_tile(seq, 256, 16))
    d_exp = e_w_gate.shape[-1]
    tne = _largest_tile(d_exp, 512, LANES)
    for e in range(n_experts):
        act = _swiglu(hn, e_w_gate[0, e].astype(BF16), e_w_up[0, e].astype(BF16), tm=tm1, tn=tne)
        h = _mm_acc(act, e_w_down[0, e].astype(BF16), h, tm=tm1, tn=tn, tk=d_exp, comb=comb, comb_col=e)

    (out,) = _rmsnorm(h, final_norm[None], [F32], _largest_tile(seq, 256, 16))
    return out[None]
```

```python
import functools

import jax
import jax.numpy as jnp
from jax import lax
from jax.experimental import pallas as pl
from jax.experimental.pallas import tpu as pltpu

F32 = jnp.float32
BF16 = jnp.bfloat16

LANES = 128
VMEM_LIMIT_BYTES = 56 * 1024 * 1024

N_META = 16
RMS_EPS = 1e-5
LN_X_EPS = 64e-5
HEAD = 64
ATT_KV_HEADS = 8
WINDOW = 128
BLOCK = 128
TOP_K = 2
CHUNK = 64
NEG = -1e30


def _cparams(sem):
    return pltpu.CompilerParams(dimension_semantics=sem, vmem_limit_bytes=VMEM_LIMIT_BYTES)


def _dot(a, b):
    return jnp.dot(a, b, preferred_element_type=F32)


def _dot_nt(a, b):
    return lax.dot_general(a, b, (((1,), (1,)), ((), ())), preferred_element_type=F32)


def _rms(x):
    return x * lax.rsqrt(jnp.mean(x * x, axis=-1, keepdims=True) + RMS_EPS)


def _rmsnorm_body(h_ref, g_ref, *o_refs):
    y = _rms(h_ref[...])
    for j, o_ref in enumerate(o_refs):
        o_ref[...] = (y * g_ref[j:j + 1, :]).astype(o_ref.dtype)


def _rmsnorm(h, gains, out_dtypes, tm):
    m, d = h.shape
    n_out = len(out_dtypes)
    outs = pl.pallas_call(
        _rmsnorm_body,
        out_shape=[jax.ShapeDtypeStruct((m, d), dt) for dt in out_dtypes],
        grid=(m // tm,),
        in_specs=[pl.BlockSpec((tm, d), lambda i: (i, 0)),
                  pl.BlockSpec((n_out, d), lambda i: (0, 0))],
        out_specs=[pl.BlockSpec((tm, d), lambda i: (i, 0)) for _ in out_dtypes],
        compiler_params=_cparams(("parallel",)),
        name="rmsnorm",
    )(h, gains)
    return outs


def _premix_body(h_ref, hp_ref, g_ref, mix_ref, *o_refs):
    i = pl.program_id(0)
    g = g_ref[...]
    xn = _rms(h_ref[...]) * g
    pn = _rms(hp_ref[...]) * g
    prev_row = jnp.where(i > 0, pn[7:8, :], 0.0)
    sh = pltpu.roll(xn, 1, axis=0)
    row = lax.broadcasted_iota(jnp.int32, xn.shape, 0)
    sh = jnp.where(row == 0, prev_row, sh)
    xx = sh - xn
    for j, o_ref in enumerate(o_refs):
        o_ref[...] = (xn + xx * mix_ref[j:j + 1, :]).astype(o_ref.dtype)


def _premix(h, gain, mix, tm):
    m, d = h.shape
    n_mix = mix.shape[0]
    rows8 = tm // 8
    return pl.pallas_call(
        _premix_body,
        out_shape=[jax.ShapeDtypeStruct((m, d), BF16) for _ in range(n_mix)],
        grid=(m // tm,),
        in_specs=[pl.BlockSpec((tm, d), lambda i: (i, 0)),
                  pl.BlockSpec((8, d), lambda i: (jnp.maximum(i * rows8 - 1, 0), 0)),
                  pl.BlockSpec((1, d), lambda i: (0, 0)),
                  pl.BlockSpec((n_mix, d), lambda i: (0, 0))],
        out_specs=[pl.BlockSpec((tm, d), lambda i: (i, 0)) for _ in range(n_mix)],
        compiler_params=_cparams(("parallel",)),
        name="premix",
    )(h, h, gain, mix)


def _mm_body(x_ref, w_ref, *rest, has_res, scale):
    if has_res:
        res_ref, o_ref = rest
    else:
        (o_ref,) = rest
    acc = _dot(x_ref[...], w_ref[...])
    if scale is not None:
        acc = acc * scale
    if has_res:
        acc = acc + res_ref[...]
    o_ref[...] = acc.astype(o_ref.dtype)


def _mm(x, w, *, out_dtype, tm, tn, res=None, scale=None):
    m, k = x.shape
    n = w.shape[1]
    in_specs = [pl.BlockSpec((tm, k), lambda j, i: (i, 0)),
                pl.BlockSpec((k, tn), lambda j, i: (0, j))]
    args = [x, w]
    if res is not None:
        in_specs.append(pl.BlockSpec((tm, tn), lambda j, i: (i, j)))
        args.append(res)
    return pl.pallas_call(
        functools.partial(_mm_body, has_res=res is not None, scale=scale),
        out_shape=jax.ShapeDtypeStruct((m, n), out_dtype),
        grid=(n // tn, m // tm),
        in_specs=in_specs,
        out_specs=pl.BlockSpec((tm, tn), lambda j, i: (i, j)),
        compiler_params=_cparams(("parallel", "parallel")),
        name="matmul",
    )(*args)


def _mm_acc_body(x_ref, w_ref, res_ref, o_ref, acc_ref):
    kk = pl.program_id(2)

    @pl.when(kk == 0)
    def _():
        acc_ref[...] = jnp.zeros_like(acc_ref)

    acc_ref[...] += _dot(x_ref[...], w_ref[...])

    @pl.when(kk == pl.num_programs(2) - 1)
    def _():
        o_ref[...] = res_ref[...] + acc_ref[...]


def _mm_acc(x, w, res, *, tm, tn, tk):
    m, k = x.shape
    n = w.shape[1]
    return pl.pallas_call(
        _mm_acc_body,
        out_shape=jax.ShapeDtypeStruct((m, n), F32),
        grid=(n // tn, m // tm, k // tk),
        in_specs=[pl.BlockSpec((tm, tk), lambda j, i, q: (i, q)),
                  pl.BlockSpec((tk, tn), lambda j, i, q: (q, j)),
                  pl.BlockSpec((tm, tn), lambda j, i, q: (i, j))],
        out_specs=pl.BlockSpec((tm, tn), lambda j, i, q: (i, j)),
        scratch_shapes=[pltpu.VMEM((tm, tn), F32)],
        compiler_params=_cparams(("parallel", "parallel", "arbitrary")),
        name="matmul_acc",
    )(x, w, res)


def _swiglu_body(x_ref, wg_ref, wu_ref, o_ref):
    x = x_ref[...]
    g = _dot(x, wg_ref[...])
    u = _dot(x, wu_ref[...])
    o_ref[...] = (g * jax.nn.sigmoid(g) * u).astype(o_ref.dtype)


def _swiglu(x, wg, wu, *, tm, tn):
    m, k = x.shape
    n = wg.shape[1]
    return pl.pallas_call(
        _swiglu_body,
        out_shape=jax.ShapeDtypeStruct((m, n), BF16),
        grid=(n // tn, m // tm),
        in_specs=[pl.BlockSpec((tm, k), lambda j, i: (i, 0)),
                  pl.BlockSpec((k, tn), lambda j, i: (0, j)),
                  pl.BlockSpec((k, tn), lambda j, i: (0, j))],
        out_specs=pl.BlockSpec((tm, tn), lambda j, i: (i, j)),
        compiler_params=_cparams(("parallel", "parallel")),
        name="swiglu",
    )(x, wg, wu)


def _lora_body(x_ref, w1_ref, w2_ref, b_ref, o_ref, *, mid, post):
    t = _dot(x_ref[...], w1_ref[...])
    if mid == "tanh":
        t = jnp.tanh(t)
    elif mid == "sigmoid":
        t = jax.nn.sigmoid(t)
    z = _dot(t.astype(BF16), w2_ref[...]) + b_ref[...]
    if post == "log_decay":
        z = jax.nn.sigmoid(z) * (-0.6065306597126334)
    elif post == "sigmoid":
        z = jax.nn.sigmoid(z)
    o_ref[...] = z.astype(o_ref.dtype)


def _lora(x, w1, w2, bias, *, mid, post, out_dtype, tm):
    m, k = x.shape
    r = w1.shape[1]
    n = w2.shape[1]
    return pl.pallas_call(
        functools.partial(_lora_body, mid=mid, post=post),
        out_shape=jax.ShapeDtypeStruct((m, n), out_dtype),
        grid=(m // tm,),
        in_specs=[pl.BlockSpec((tm, k), lambda i: (i, 0)),
                  pl.BlockSpec((k, r), lambda i: (0, 0)),
                  pl.BlockSpec((r, n), lambda i: (0, 0)),
                  pl.BlockSpec((1, n), lambda i: (0, 0))],
        out_specs=pl.BlockSpec((tm, n), lambda i: (i, 0)),
        compiler_params=_cparams(("parallel",)),
        name="lora",
    )(x, w1, w2, bias)


def _split2(x):
    hi = x.astype(BF16)
    lo = (x - hi.astype(F32)).astype(BF16)
    return hi, lo


def _wkv_body(r_ref, k_ref, v_ref, lw_ref, a_ref, g_ref, prm_ref, o_ref, st_ref, *, pairs):
    c_idx = pl.program_id(1)

    @pl.when(c_idx == 0)
    def _():
        st_ref[...] = jnp.zeros_like(st_ref)

    C = CHUNK
    lane = lax.broadcasted_iota(jnp.int32, (C, LANES), 1)
    row = lax.broadcasted_iota(jnp.int32, (C, LANES), 0)
    col = lane % HEAD
    upper_half = lane >= HEAD
    tri_strict = col < row
    tri_incl = col <= row
    eye_pair = (col == row).astype(F32)

    def level_mask(bs):
        return ((row // bs) == (col // bs)) & ((row // (bs // 2)) != (col // (bs // 2)))

    row2 = lax.broadcasted_iota(jnp.int32, (2 * C, LANES), 0)
    lane2 = lax.broadcasted_iota(jnp.int32, (2 * C, LANES), 1)
    bd_mask = (row2 >= C) == (lane2 >= HEAD)
    ones_bd = bd_mask.astype(BF16)
    diag_mask = row2 == lane2
    t_r = lax.broadcasted_iota(jnp.int32, (C, C), 0)
    t_c = lax.broadcasted_iota(jnp.int32, (C, C), 1)
    tri_cc = (t_c <= t_r).astype(BF16)

    def bd(x):
        return jnp.concatenate([jnp.where(upper_half, 0.0, x), jnp.where(upper_half, x, 0.0)], axis=0)

    def headsum(x):
        hi, lo = _split2(x)
        return _dot(hi, ones_bd) + _dot(lo, ones_bd)

    zeros_cl = jnp.zeros((C, LANES), BF16)
    zeros_ll = jnp.zeros((2 * C, LANES), BF16)

    def pair_step(r, k, v, lw, a, g, prm, s):
        k_k, k_a, r_k, lnx_w, lnx_b = (prm[j:j + 1] for j in range(5))

        kk = k * k_k
        k2 = k * (1.0 + (a - 1.0) * k_a)
        sums = headsum(jnp.concatenate([kk * kk, r * k2 * r_k], axis=0))
        l1 = lw.astype(BF16)
        rem = lw - l1.astype(F32)
        l2 = rem.astype(BF16)
        l3 = (rem - l2.astype(F32)).astype(BF16)
        cum = _dot(tri_cc, l1) + _dot(tri_cc, l2) + _dot(tri_cc, l3)
        yield
        kkn = kk * lax.rsqrt(jnp.maximum(sums[:C], 1e-24))
        bonus = sums[C:] * v
        avec = -kkn
        bvec = kkn * a
        cum_end = cum[C - 1:C, :]
        p_incl = jnp.exp(cum)
        p_inv = jnp.exp(-cum)
        a_t = avec * jnp.exp(cum - lw)
        r_t = r * p_incl
        b_t = bvec * p_inv
        k_t = k2 * p_inv
        p_end = jnp.exp(cum_end - cum)
        b_h = bvec * p_end
        k_h = k2 * p_end

        v16 = v.astype(BF16)
        gram = _dot_nt(jnp.concatenate([a_t, r_t], axis=0).astype(BF16),
                       jnp.concatenate([bd(b_t), bd(k_t)], axis=0).astype(BF16))
        yield
        a_ab = jnp.where(tri_strict, gram[:C, :LANES], 0.0)
        a_ak = jnp.where(tri_strict, gram[:C, LANES:], 0.0)
        a_rb = jnp.where(tri_incl, gram[C:, :LANES], 0.0)
        a_rk = jnp.where(tri_incl, gram[C:, LANES:], 0.0)
        akv = _dot(a_ak.astype(BF16), bd(v16))

        t_inv = eye_pair + jnp.where(level_mask(2), a_ab, 0.0)
        bs = 4
        while bs <= C:
            x = jnp.where(level_mask(bs), a_ab, 0.0)
            t16 = t_inv.astype(BF16)
            tx = _dot(t16, bd(x).astype(BF16))
            yield
            t_inv = t_inv + _dot(tx.astype(BF16), bd(t16))
            yield
            bs *= 2
        t16 = t_inv.astype(BF16)

        tt = _dot(t16, jnp.concatenate([bd(a_t.astype(BF16)), bd(akv.astype(BF16))], axis=1))
        yield
        a_hat = tt[:, :LANES].astype(BF16)
        u0 = tt[:, LANES:].astype(BF16)

        rhs_top = jnp.concatenate([bd(a_hat), bd(u0)], axis=1)
        rhs_bot = jnp.concatenate([zeros_ll, bd(v16)], axis=1)
        qy = _dot(jnp.concatenate([a_rb, a_rk], axis=1).astype(BF16),
                  jnp.concatenate([rhs_top, rhs_bot], axis=0))
        lhs_t = jnp.concatenate([b_h, k_h], axis=0).T.astype(BF16)
        rhs = jnp.concatenate([jnp.concatenate([a_hat, u0], axis=1),
                               jnp.concatenate([zeros_cl, v16], axis=1)], axis=0)
        upd = _dot(lhs_t, rhs)
        yield
        q_hat = r_t + qy[:, :LANES]
        y0 = qy[:, LANES:]
        m_off = jnp.where(bd_mask, upd[:, :LANES], 0.0)
        n_new = jnp.where(bd_mask, upd[:, LANES:], 0.0)
        p_col = jnp.sum(jnp.where(diag_mask, jnp.exp(cum_end), 0.0), axis=1, keepdims=True)

        s_hi, s_lo = _split2(s)
        q16 = q_hat.astype(BF16)
        y = _dot(q16, s_hi) + _dot(q16, s_lo) + y0
        m16 = m_off.astype(BF16)
        s_new = p_col * s + _dot(m16, s_hi) + _dot(m16, s_lo) + n_new
        yield

        mu = headsum(y) * (1.0 / HEAD)
        yield
        d = y - mu
        var = headsum(d * d) * (1.0 / HEAD)
        yield
        yn = d * lax.rsqrt(var + LN_X_EPS) * lnx_w + lnx_b
        return ((yn + bonus) * g.astype(F32)).astype(o_ref.dtype), s_new

    lanes = [slice(p * LANES, (p + 1) * LANES) for p in range(pairs)]
    steps = [pair_step(r_ref[:, sl], k_ref[:, sl], v_ref[:, sl], lw_ref[:, sl], a_ref[:, sl], g_ref[:, sl],
                       prm_ref[:, sl], st_ref[p]) for p, sl in enumerate(lanes)]
    results = [None] * pairs
    while any(res is None for res in results):
        for p, step in enumerate(steps):
            if results[p] is None:
                try:
                    next(step)
                except StopIteration as done:
                    results[p] = done.value
    for p, sl in enumerate(lanes):
        o_ref[:, sl] = results[p][0]
        st_ref[p] = results[p][1]


def _wkv(r, k, v, lw, a, g, prm, *, pairs):
    length, d = r.shape
    width = pairs * LANES
    blk = lambda hb, c: (c, hb)
    return pl.pallas_call(
        functools.partial(_wkv_body, pairs=pairs),
        out_shape=jax.ShapeDtypeStruct((length, d), BF16),
        grid=(d // width, length // CHUNK),
        in_specs=[pl.BlockSpec((CHUNK, width), blk)] * 6 + [pl.BlockSpec((8, width), lambda hb, c: (0, hb))],
        out_specs=pl.BlockSpec((CHUNK, width), blk),
        scratch_shapes=[pltpu.VMEM((pairs, LANES, LANES), F32)],
        compiler_params=_cparams(("parallel", "arbitrary")),
        name="wkv7",
    )(r, k, v, lw, a, g, prm)


def _attn_body(q_ref, kp_ref, kc_ref, km_ref, vp_ref, vc_ref, vm_ref, sk_ref, o_ref, *, group, n_heads):
    kvh = pl.program_id(0)
    n = pl.program_id(1)
    rows = 2 * BLOCK
    rowi = lax.broadcasted_iota(jnp.int32, (rows, BLOCK), 0)
    kj = lax.broadcasted_iota(jnp.int32, (rows, BLOCK), 1)
    qi = rowi % BLOCK
    second = rowi >= BLOCK
    dist_prev = BLOCK + qi - kj
    dist_cur = qi - kj
    valid_prev = (dist_prev < WINDOW) & (n > 0)
    valid_cur = dist_cur >= 0
    valid_meta = kj < N_META
    dist_meta = N_META + n * BLOCK + qi - kj
    lane_o = lax.broadcasted_iota(jnp.int32, (BLOCK, LANES), 1)
    lane_q = lax.broadcasted_iota(jnp.int32, (rows, LANES), 1)
    row_q = lax.broadcasted_iota(jnp.int32, (rows, LANES), 0)
    q_keep = (row_q >= BLOCK) == (lane_q >= HEAD)

    kp = kp_ref[...]
    kc = kc_ref[...]
    km = km_ref[...]
    vp = vp_ref[...]
    vc = vc_ref[...]
    vm = vm_ref[...]
    for pr in range(group // 2):
        sl = slice(pr * LANES, (pr + 1) * LANES)
        q = q_ref[:, sl]
        q2 = jnp.where(q_keep, jnp.concatenate([q, q], axis=0), jnp.zeros((), q.dtype))
        head1 = (kvh * group + 2 * pr + 1).astype(F32)
        slope = jnp.exp2((jnp.where(second, 1.0, 0.0) + head1) * (-8.0 / n_heads))
        sink = jnp.where(second[:, :1], sk_ref[kvh, 2 * pr + 1], sk_ref[kvh, 2 * pr])
        s_prev = jnp.where(valid_prev, _dot_nt(q2, kp) - slope * dist_prev.astype(F32), NEG)
        s_cur = jnp.where(valid_cur, _dot_nt(q2, kc) - slope * dist_cur.astype(F32), NEG)
        s_meta = jnp.where(valid_meta, _dot_nt(q2, km) - slope * dist_meta.astype(F32), NEG)
        mx = jnp.maximum(jnp.maximum(jnp.max(s_prev, axis=1, keepdims=True), jnp.max(s_cur, axis=1, keepdims=True)),
                         jnp.maximum(jnp.max(s_meta, axis=1, keepdims=True), sink))
        e_prev = jnp.exp(s_prev - mx)
        e_cur = jnp.exp(s_cur - mx)
        e_meta = jnp.exp(s_meta - mx)
        den = (jnp.sum(e_prev, axis=1, keepdims=True) + jnp.sum(e_cur, axis=1, keepdims=True)
               + jnp.sum(e_meta, axis=1, keepdims=True) + jnp.exp(sink - mx))
        acc = (_dot(e_prev.astype(BF16), vp) + _dot(e_cur.astype(BF16), vc) + _dot(e_meta.astype(BF16), vm))
        acc = acc / den
        o_ref[:, sl] = jnp.where(lane_o >= HEAD, acc[BLOCK:], acc[:BLOCK]).astype(o_ref.dtype)


def _attention(q, k_dup, v_dup, km_dup, vm_dup, sinks, *, n_heads):
    s_len, d = q.shape
    group = n_heads // ATT_KV_HEADS
    gw = group * HEAD
    nb = s_len // BLOCK
    prev = lambda h, n: (jnp.maximum(n - 1, 0), h)
    cur = lambda h, n: (n, h)
    meta = lambda h, n: (0, h)
    return pl.pallas_call(
        functools.partial(_attn_body, group=group, n_heads=n_heads),
        out_shape=jax.ShapeDtypeStruct((s_len, d), BF16),
        grid=(ATT_KV_HEADS, nb),
        in_specs=[pl.BlockSpec((BLOCK, gw), cur),
                  pl.BlockSpec((BLOCK, LANES), prev), pl.BlockSpec((BLOCK, LANES), cur), pl.BlockSpec((BLOCK, LANES), meta),
                  pl.BlockSpec((BLOCK, LANES), prev), pl.BlockSpec((BLOCK, LANES), cur), pl.BlockSpec((BLOCK, LANES), meta),
                  pl.BlockSpec(memory_space=pltpu.SMEM)],
        out_specs=pl.BlockSpec((BLOCK, gw), cur),
        compiler_params=_cparams(("parallel", "parallel")),
        name="swa_attention",
    )(q, k_dup, k_dup, km_dup, v_dup, v_dup, vm_dup, sinks)


def _router_body(h_ref, g_ref, wr_ref, hn_ref, sel_ref, gate_ref, *, n_experts):
    y = _rms(h_ref[...]) * g_ref[...]
    hn_ref[...] = y
    logits = jnp.dot(y, wr_ref[...], preferred_element_type=F32, precision=lax.Precision.HIGHEST)
    lane = lax.broadcasted_iota(jnp.int32, logits.shape, 1)
    logits = jnp.where(lane < n_experts, logits, NEG)
    m1 = jnp.max(logits, axis=1, keepdims=True)
    i1 = jnp.min(jnp.where(logits == m1, lane, LANES), axis=1, keepdims=True)
    rest = jnp.where(lane == i1, NEG, logits)
    m2 = jnp.max(rest, axis=1, keepdims=True)
    i2 = jnp.min(jnp.where(rest == m2, lane, LANES), axis=1, keepdims=True)
    e2 = jnp.exp(m2 - m1)
    g1 = 1.0 / (1.0 + e2)
    g2 = e2 / (1.0 + e2)
    sel_ref[...] = jnp.where(lane == 0, i1, jnp.where(lane == 1, i2, 0))
    gate_ref[...] = jnp.where(lane == 0, g1, jnp.where(lane == 1, g2, 0.0))


def _router(h, gain, w_router_pad, *, n_experts, tm):
    m, d = h.shape
    return pl.pallas_call(
        functools.partial(_router_body, n_experts=n_experts),
        out_shape=[jax.ShapeDtypeStruct((m, d), F32), jax.ShapeDtypeStruct((m, LANES), jnp.int32),
                   jax.ShapeDtypeStruct((m, LANES), F32)],
        grid=(m // tm,),
        in_specs=[pl.BlockSpec((tm, d), lambda i: (i, 0)),
                  pl.BlockSpec((1, d), lambda i: (0, 0)),
                  pl.BlockSpec((d, LANES), lambda i: (0, 0))],
        out_specs=[pl.BlockSpec((tm, d), lambda i: (i, 0)), pl.BlockSpec((tm, LANES), lambda i: (i, 0)),
                   pl.BlockSpec((tm, LANES), lambda i: (i, 0))],
        compiler_params=_cparams(("parallel",)),
        name="moe_router",
    )(h, gain, w_router_pad)


EXPERT_TILE = 256


def _row_copy(src_hbm, dst, src_row, dst_row, sem):
    return pltpu.make_async_copy(src_hbm.at[pl.ds(src_row, 1)], dst.at[pl.ds(dst_row, 1)], sem)


def _gather_rows_body(src_ref, x_hbm, o_hbm, sem, *, rows):
    base = pl.program_id(0) * rows

    def issue(r, carry):
        _row_copy(x_hbm, o_hbm, src_ref[base + r], base + r, sem).start()
        return carry

    lax.fori_loop(0, rows, issue, 0)
    pltpu.make_async_copy(x_hbm.at[pl.ds(0, rows)], o_hbm.at[pl.ds(base, rows)], sem).wait()


def _gather_rows(row_src, x, n_rows, *, rows):
    d = x.shape[1]
    return pl.pallas_call(
        functools.partial(_gather_rows_body, rows=rows),
        out_shape=jax.ShapeDtypeStruct((n_rows, d), x.dtype),
        grid_spec=pltpu.PrefetchScalarGridSpec(
            num_scalar_prefetch=1,
            grid=(n_rows // rows,),
            in_specs=[pl.BlockSpec(memory_space=pl.ANY)],
            out_specs=pl.BlockSpec(memory_space=pl.ANY),
            scratch_shapes=[pltpu.SemaphoreType.DMA],
        ),
        compiler_params=_cparams(("arbitrary",)),
        name="moe_gather",
    )(row_src, x)


def _moe_swiglu_body(te_ref, x_ref, wg_ref, wu_ref, o_ref):
    x = x_ref[...].astype(BF16)
    g = _dot(x, wg_ref[...])
    u = _dot(x, wu_ref[...])
    o_ref[...] = (g * jax.nn.sigmoid(g) * u).astype(o_ref.dtype)


def _moe_swiglu(tile_expert, xs, wg, wu, *, tn):
    rows, d = xs.shape
    f = wg.shape[-1]
    w_spec = pl.BlockSpec((None, d, tn), lambda j, i, te: (te[i], 0, j))
    return pl.pallas_call(
        _moe_swiglu_body,
        out_shape=jax.ShapeDtypeStruct((rows, f), BF16),
        grid_spec=pltpu.PrefetchScalarGridSpec(
            num_scalar_prefetch=1,
            grid=(f // tn, rows // EXPERT_TILE),
            in_specs=[pl.BlockSpec((EXPERT_TILE, d), lambda j, i, te: (i, 0)), w_spec, w_spec],
            out_specs=pl.BlockSpec((EXPERT_TILE, tn), lambda j, i, te: (i, j)),
        ),
        compiler_params=_cparams(("parallel", "arbitrary")),
        name="moe_swiglu",
    )(tile_expert, xs, wg, wu)


def _moe_down_body(te_ref, x_ref, w_ref, o_ref):
    o_ref[...] = _dot(x_ref[...], w_ref[...])


def _moe_down(tile_expert, act, wd, *, tn):
    rows, f = act.shape
    d = wd.shape[-1]
    return pl.pallas_call(
        _moe_down_body,
        out_shape=jax.ShapeDtypeStruct((rows, d), F32),
        grid_spec=pltpu.PrefetchScalarGridSpec(
            num_scalar_prefetch=1,
            grid=(d // tn, rows // EXPERT_TILE),
            in_specs=[pl.BlockSpec((EXPERT_TILE, f), lambda j, i, te: (i, 0)),
                      pl.BlockSpec((None, f, tn), lambda j, i, te: (te[i], 0, j))],
            out_specs=pl.BlockSpec((EXPERT_TILE, tn), lambda j, i, te: (i, j)),
        ),
        compiler_params=_cparams(("parallel", "arbitrary")),
        name="moe_down",
    )(tile_expert, act, wd)


def _moe_combine_body(pos_ref, h_ref, gate_ref, fn_ref, eo_hbm, o_ref, buf, sem, *, tm):
    base = pl.program_id(0) * tm

    def issue(t, carry):
        for j in range(TOP_K):
            _row_copy(eo_hbm, buf.at[j], pos_ref[TOP_K * (base + t) + j], t, sem).start()
        return carry

    lax.fori_loop(0, tm, issue, 0)
    for j in range(TOP_K):
        pltpu.make_async_copy(eo_hbm.at[pl.ds(0, tm)], buf.at[j], sem).wait()
    y = h_ref[...]
    for j in range(TOP_K):
        y = y + gate_ref[:, j:j + 1] * buf[j]
    o_ref[...] = _rms(y) * fn_ref[...]


def _moe_combine(pos, h, gate, final_gain, eo, *, tm):
    m, d = h.shape
    return pl.pallas_call(
        functools.partial(_moe_combine_body, tm=tm),
        out_shape=jax.ShapeDtypeStruct((m, d), F32),
        grid_spec=pltpu.PrefetchScalarGridSpec(
            num_scalar_prefetch=1,
            grid=(m // tm,),
            in_specs=[pl.BlockSpec((tm, d), lambda i, pos: (i, 0)),
                      pl.BlockSpec((tm, LANES), lambda i, pos: (i, 0)),
                      pl.BlockSpec((1, d), lambda i, pos: (0, 0)),
                      pl.BlockSpec(memory_space=pl.ANY)],
            out_specs=pl.BlockSpec((tm, d), lambda i, pos: (i, 0)),
            scratch_shapes=[pltpu.VMEM((TOP_K, tm, d), F32), pltpu.SemaphoreType.DMA],
        ),
        compiler_params=_cparams(("arbitrary",)),
        name="moe_combine",
    )(pos, h, gate, final_gain, eo)


def _moe_plan(sel, n_experts):
    m = sel.shape[0]
    flat_e = sel[:, :TOP_K].reshape(-1)
    onehot = (flat_e[:, None] == jnp.arange(n_experts, dtype=jnp.int32)[None, :]).astype(jnp.int32)
    csum = jnp.cumsum(onehot, axis=0)
    rank = jnp.sum(csum * onehot, axis=1) - 1
    counts = csum[-1]
    padded = (counts + EXPERT_TILE - 1) // EXPERT_TILE * EXPERT_TILE
    ends = jnp.cumsum(padded)
    starts = ends - padded
    pos = (jnp.sum(onehot * starts[None, :], axis=1) + rank).astype(jnp.int32)
    n_rows = TOP_K * m + n_experts * EXPERT_TILE
    token = jnp.arange(TOP_K * m, dtype=jnp.int32) // TOP_K
    row_src = jnp.zeros((n_rows,), jnp.int32).at[pos].set(token)
    tile_start = jnp.arange(n_rows // EXPERT_TILE, dtype=jnp.int32) * EXPERT_TILE
    tile_expert = jnp.minimum(jnp.sum((tile_start[:, None] >= ends[None, :]).astype(jnp.int32), axis=1),
                              n_experts - 1).astype(jnp.int32)
    return pos, row_src, tile_expert, n_rows


def _largest_tile(n, cap, mult):
    best = None
    t = mult
    while t <= min(n, cap):
        if n % t == 0:
            best = t
        t += mult
    assert best is not None, (n, cap, mult)
    return best


def _pad_cols(w, n_to):
    return jnp.pad(w, ((0, 0), (0, n_to - w.shape[1])))


def _pad_rows(w, n_to):
    return jnp.pad(w, ((0, n_to - w.shape[0]), (0, 0)))


def _round_up(n, m):
    return -(-n // m) * m


def kernel(x, meta_tokens, a_norm, a_mix, a_w_rkv, a_w0, a_w1, a_w2, a_a0, a_a1, a_a2, a_g1, a_g2, a_k_k, a_k_a, a_r_k, a_lnx_w, a_lnx_b, a_w_out, kv_norm, w_kv, b_norm, b_w_q, b_sinks, b_w_out, f_norm, d_w_gate, d_w_up, d_w_down, e_router, e_w_gate, e_w_up, e_w_down, final_norm):
    assert x.shape[0] == 1
    seq, d = x.shape[1], x.shape[2]
    n_heads = d // HEAD
    n_experts = e_router.shape[-1]
    xs = x[0]

    l_real = N_META + seq
    l_pad = _round_up(l_real, CHUNK)
    h = jnp.concatenate([meta_tokens.astype(F32), xs, jnp.zeros((l_pad - l_real, d), F32)], axis=0)
    tm0 = _largest_tile(l_pad, 768, 16)
    tn = _largest_tile(d, 1024, LANES)

    xr, xw, xk, xv, xa, xg = _premix(h, a_norm[0:1], a_mix[0], _largest_tile(l_pad, 256, 16))
    w_rkv = a_w_rkv[0].astype(BF16)
    r = _mm(xr, w_rkv[0], out_dtype=F32, tm=tm0, tn=tn)
    k = _mm(xk, w_rkv[1], out_dtype=F32, tm=tm0, tn=tn)
    v = _mm(xv, w_rkv[2], out_dtype=F32, tm=tm0, tn=tn)
    tml = _largest_tile(l_pad, 384, 16)
    lw = _lora(xw, a_w1[0].astype(BF16), a_w2[0].astype(BF16), a_w0[0][None], mid="tanh", post="log_decay",
               out_dtype=F32, tm=tml)
    a = _lora(xa, a_a1[0].astype(BF16), a_a2[0].astype(BF16), a_a0[0][None], mid="none", post="sigmoid",
              out_dtype=F32, tm=tml)
    gate_rank = _round_up(a_g1.shape[-1], LANES)
    g = _lora(xg, _pad_cols(a_g1[0], gate_rank).astype(BF16), _pad_rows(a_g2[0], gate_rank).astype(BF16),
              jnp.zeros((1, d), F32), mid="sigmoid", post="none", out_dtype=BF16, tm=tml)
    prm = jnp.concatenate([a_k_k[0][None], a_k_a[0][None], a_r_k[0].reshape(1, d), a_lnx_w[0][None],
                           a_lnx_b[0][None], jnp.zeros((3, d), F32)], axis=0)
    pairs = 8 if n_heads % 16 == 0 else n_heads // 2
    mixed = _wkv(r, k, v, lw, a, g, prm, pairs=pairs)
    h = _mm(mixed, a_w_out[0].astype(BF16), out_dtype=F32, tm=tm0, tn=tn, res=h)

    (hn,) = _rmsnorm(h, f_norm[0:1], [BF16], _largest_tile(l_pad, 256, 16))
    d_ff = d_w_gate.shape[-1]
    ff_pad = _round_up(d_ff, 4 * LANES)
    act = _swiglu(hn, _pad_cols(d_w_gate[0], ff_pad).astype(BF16), _pad_cols(d_w_up[0], ff_pad).astype(BF16),
                  tm=tm0, tn=_largest_tile(ff_pad, 512, LANES))
    h = _mm_acc(act, _pad_rows(d_w_down[0], ff_pad).astype(BF16), h, tm=tm0, tn=tn,
                tk=_largest_tile(ff_pad, 3072, LANES))

    hkv, hq = _rmsnorm(h, jnp.stack([kv_norm, b_norm[0]]), [BF16, BF16], _largest_tile(l_pad, 256, 16))
    kv_w = ATT_KV_HEADS * HEAD
    kv = _mm(hkv, w_kv.astype(BF16), out_dtype=BF16, tm=tm0, tn=_largest_tile(2 * kv_w, 1024, LANES))

    def dup_heads(t):
        t = t.reshape(t.shape[0], ATT_KV_HEADS, 1, HEAD)
        return jnp.broadcast_to(t, (t.shape[0], ATT_KV_HEADS, 2, HEAD)).reshape(t.shape[0], ATT_KV_HEADS * LANES)

    k_all, v_all = dup_heads(kv[:, :kv_w]), dup_heads(kv[:, kv_w:])
    k_meta = _pad_rows(k_all[:N_META], BLOCK)
    v_meta = _pad_rows(v_all[:N_META], BLOCK)
    k_real = k_all[N_META:l_real]
    v_real = v_all[N_META:l_real]

    h = h[N_META:l_real]
    hq = hq[N_META:l_real]
    tm1 = _largest_tile(seq, 512, 16)
    q = _mm(hq, b_w_q[0].astype(BF16), out_dtype=BF16, tm=tm1, tn=tn, scale=HEAD ** -0.5)
    o = _attention(q, k_real, v_real, k_meta, v_meta, b_sinks[0].reshape(ATT_KV_HEADS, -1), n_heads=n_heads)
    h = _mm(o, b_w_out[0].astype(BF16), out_dtype=F32, tm=tm1, tn=tn, res=h)

    hn, sel, gate = _router(h, f_norm[1:2], _pad_cols(e_router[0], LANES), n_experts=n_experts,
                            tm=_largest_tile(seq, 256, 16))
    pos, row_src, tile_expert, n_rows = _moe_plan(sel, n_experts)
    xs_sorted = _gather_rows(row_src, hn, n_rows, rows=_largest_tile(n_rows, 512, EXPERT_TILE))
    d_exp = e_w_gate.shape[-1]
    act = _moe_swiglu(tile_expert, xs_sorted, e_w_gate[0].astype(BF16), e_w_up[0].astype(BF16),
                      tn=_largest_tile(d_exp, 896, LANES))
    eo = _moe_down(tile_expert, act, e_w_down[0].astype(BF16), tn=tn)
    out = _moe_combine(pos, h, gate, final_norm[None], eo, tm=_largest_tile(seq, 128, 8))
    return out[None]
```

```python
import functools

import jax
import jax.numpy as jnp
from jax import lax
from jax.experimental import pallas as pl
from jax.experimental.pallas import tpu as pltpu

F32 = jnp.float32
BF16 = jnp.bfloat16

LANES = 128
VMEM_LIMIT_BYTES = 56 * 1024 * 1024

N_META = 16
RMS_EPS = 1e-5
LN_X_EPS = 64e-5
HEAD = 64
ATT_KV_HEADS = 8
WINDOW = 128
BLOCK = 128
TOP_K = 2
CHUNK = 64
NEG = -1e30


def _cparams(sem):
    return pltpu.CompilerParams(dimension_semantics=sem, vmem_limit_bytes=VMEM_LIMIT_BYTES)


def _dot(a, b):
    return jnp.dot(a, b, preferred_element_type=F32)


def _dot_nt(a, b):
    return lax.dot_general(a, b, (((1,), (1,)), ((), ())), preferred_element_type=F32)


def _rms(x):
    return x * lax.rsqrt(jnp.mean(x * x, axis=-1, keepdims=True) + RMS_EPS)


def _rmsnorm_body(h_ref, g_ref, *o_refs):
    y = _rms(h_ref[...])
    for j, o_ref in enumerate(o_refs):
        o_ref[...] = (y * g_ref[j:j + 1, :]).astype(o_ref.dtype)


def _rmsnorm(h, gains, out_dtypes, tm):
    m, d = h.shape
    n_out = len(out_dtypes)
    outs = pl.pallas_call(
        _rmsnorm_body,
        out_shape=[jax.ShapeDtypeStruct((m, d), dt) for dt in out_dtypes],
        grid=(m // tm,),
        in_specs=[pl.BlockSpec((tm, d), lambda i: (i, 0)),
                  pl.BlockSpec((n_out, d), lambda i: (0, 0))],
        out_specs=[pl.BlockSpec((tm, d), lambda i: (i, 0)) for _ in out_dtypes],
        compiler_params=_cparams(("parallel",)),
        name="rmsnorm",
    )(h, gains)
    return outs


def _premix_body(h_ref, hp_ref, g_ref, mix_ref, *o_refs):
    i = pl.program_id(0)
    g = g_ref[...]
    xn = _rms(h_ref[...]) * g
    pn = _rms(hp_ref[...]) * g
    prev_row = jnp.where(i > 0, pn[7:8, :], 0.0)
    sh = pltpu.roll(xn, 1, axis=0)
    row = lax.broadcasted_iota(jnp.int32, xn.shape, 0)
    sh = jnp.where(row == 0, prev_row, sh)
    xx = sh - xn
    for j, o_ref in enumerate(o_refs):
        o_ref[...] = (xn + xx * mix_ref[j:j + 1, :]).astype(o_ref.dtype)


def _premix(h, gain, mix, tm):
    m, d = h.shape
    n_mix = mix.shape[0]
    rows8 = tm // 8
    return pl.pallas_call(
        _premix_body,
        out_shape=[jax.ShapeDtypeStruct((m, d), BF16) for _ in range(n_mix)],
        grid=(m // tm,),
        in_specs=[pl.BlockSpec((tm, d), lambda i: (i, 0)),
                  pl.BlockSpec((8, d), lambda i: (jnp.maximum(i * rows8 - 1, 0), 0)),
                  pl.BlockSpec((1, d), lambda i: (0, 0)),
                  pl.BlockSpec((n_mix, d), lambda i: (0, 0))],
        out_specs=[pl.BlockSpec((tm, d), lambda i: (i, 0)) for _ in range(n_mix)],
        compiler_params=_cparams(("parallel",)),
        name="premix",
    )(h, h, gain, mix)


def _mm_body(x_ref, w_ref, *rest, has_res, scale):
    if has_res:
        res_ref, o_ref = rest
    else:
        (o_ref,) = rest
    acc = _dot(x_ref[...], w_ref[...])
    if scale is not None:
        acc = acc * scale
    if has_res:
        acc = acc + res_ref[...]
    o_ref[...] = acc.astype(o_ref.dtype)


def _mm(x, w, *, out_dtype, tm, tn, res=None, scale=None):
    m, k = x.shape
    n = w.shape[1]
    in_specs = [pl.BlockSpec((tm, k), lambda j, i: (i, 0)),
                pl.BlockSpec((k, tn), lambda j, i: (0, j))]
    args = [x, w]
    if res is not None:
        in_specs.append(pl.BlockSpec((tm, tn), lambda j, i: (i, j)))
        args.append(res)
    return pl.pallas_call(
        functools.partial(_mm_body, has_res=res is not None, scale=scale),
        out_shape=jax.ShapeDtypeStruct((m, n), out_dtype),
        grid=(n // tn, m // tm),
        in_specs=in_specs,
        out_specs=pl.BlockSpec((tm, tn), lambda j, i: (i, j)),
        compiler_params=_cparams(("parallel", "parallel")),
        name="matmul",
    )(*args)


def _mm_acc_body(x_ref, w_ref, res_ref, o_ref, acc_ref):
    kk = pl.program_id(2)

    @pl.when(kk == 0)
    def _():
        acc_ref[...] = jnp.zeros_like(acc_ref)

    acc_ref[...] += _dot(x_ref[...], w_ref[...])

    @pl.when(kk == pl.num_programs(2) - 1)
    def _():
        o_ref[...] = res_ref[...] + acc_ref[...]


def _mm_acc(x, w, res, *, tm, tn, tk):
    m, k = x.shape
    n = w.shape[1]
    return pl.pallas_call(
        _mm_acc_body,
        out_shape=jax.ShapeDtypeStruct((m, n), F32),
        grid=(n // tn, m // tm, k // tk),
        in_specs=[pl.BlockSpec((tm, tk), lambda j, i, q: (i, q)),
                  pl.BlockSpec((tk, tn), lambda j, i, q: (q, j)),
                  pl.BlockSpec((tm, tn), lambda j, i, q: (i, j))],
        out_specs=pl.BlockSpec((tm, tn), lambda j, i, q: (i, j)),
        scratch_shapes=[pltpu.VMEM((tm, tn), F32)],
        compiler_params=_cparams(("parallel", "parallel", "arbitrary")),
        name="matmul_acc",
    )(x, w, res)


def _swiglu_body(x_ref, wg_ref, wu_ref, o_ref):
    x = x_ref[...]
    g = _dot(x, wg_ref[...])
    u = _dot(x, wu_ref[...])
    o_ref[...] = (g * jax.nn.sigmoid(g) * u).astype(o_ref.dtype)


def _swiglu(x, wg, wu, *, tm, tn):
    m, k = x.shape
    n = wg.shape[1]
    return pl.pallas_call(
        _swiglu_body,
        out_shape=jax.ShapeDtypeStruct((m, n), BF16),
        grid=(n // tn, m // tm),
        in_specs=[pl.BlockSpec((tm, k), lambda j, i: (i, 0)),
                  pl.BlockSpec((k, tn), lambda j, i: (0, j)),
                  pl.BlockSpec((k, tn), lambda j, i: (0, j))],
        out_specs=pl.BlockSpec((tm, tn), lambda j, i: (i, j)),
        compiler_params=_cparams(("parallel", "parallel")),
        name="swiglu",
    )(x, wg, wu)


def _lora_body(x_ref, w1_ref, w2_ref, b_ref, o_ref, *, mid, post):
    t = _dot(x_ref[...], w1_ref[...])
    if mid == "tanh":
        t = jnp.tanh(t)
    elif mid == "sigmoid":
        t = jax.nn.sigmoid(t)
    z = _dot(t.astype(BF16), w2_ref[...]) + b_ref[...]
    if post == "log_decay":
        z = jax.nn.sigmoid(z) * (-0.6065306597126334)
    elif post == "sigmoid":
        z = jax.nn.sigmoid(z)
    o_ref[...] = z.astype(o_ref.dtype)


def _lora(x, w1, w2, bias, *, mid, post, out_dtype, tm):
    m, k = x.shape
    r = w1.shape[1]
    n = w2.shape[1]
    return pl.pallas_call(
        functools.partial(_lora_body, mid=mid, post=post),
        out_shape=jax.ShapeDtypeStruct((m, n), out_dtype),
        grid=(m // tm,),
        in_specs=[pl.BlockSpec((tm, k), lambda i: (i, 0)),
                  pl.BlockSpec((k, r), lambda i: (0, 0)),
                  pl.BlockSpec((r, n), lambda i: (0, 0)),
                  pl.BlockSpec((1, n), lambda i: (0, 0))],
        out_specs=pl.BlockSpec((tm, n), lambda i: (i, 0)),
        compiler_params=_cparams(("parallel",)),
        name="lora",
    )(x, w1, w2, bias)


def _split2(x):
    hi = x.astype(BF16)
    lo = (x - hi.astype(F32)).astype(BF16)
    return hi, lo


def _wkv_body(r_ref, k_ref, v_ref, lw_ref, a_ref, g_ref, prm_ref, o_ref, st_ref, *, pairs):
    c_idx = pl.program_id(1)

    @pl.when(c_idx == 0)
    def _():
        st_ref[...] = jnp.zeros_like(st_ref)

    C = CHUNK
    lane = lax.broadcasted_iota(jnp.int32, (C, LANES), 1)
    row = lax.broadcasted_iota(jnp.int32, (C, LANES), 0)
    col = lane % HEAD
    upper_half = lane >= HEAD
    tri_strict = col < row
    tri_incl = col <= row
    eye_pair = (col == row).astype(F32)

    def level_mask(bs):
        return ((row // bs) == (col // bs)) & ((row // (bs // 2)) != (col // (bs // 2)))

    row2 = lax.broadcasted_iota(jnp.int32, (2 * C, LANES), 0)
    lane2 = lax.broadcasted_iota(jnp.int32, (2 * C, LANES), 1)
    bd_mask = (row2 >= C) == (lane2 >= HEAD)
    ones_bd = bd_mask.astype(BF16)
    diag_mask = row2 == lane2
    t_r = lax.broadcasted_iota(jnp.int32, (C, C), 0)
    t_c = lax.broadcasted_iota(jnp.int32, (C, C), 1)
    tri_cc = (t_c <= t_r).astype(BF16)

    def bd(x):
        return jnp.concatenate([jnp.where(upper_half, 0.0, x), jnp.where(upper_half, x, 0.0)], axis=0)

    def headsum(x):
        hi, lo = _split2(x)
        return _dot(hi, ones_bd) + _dot(lo, ones_bd)

    zeros_cl = jnp.zeros((C, LANES), BF16)
    zeros_ll = jnp.zeros((2 * C, LANES), BF16)

    def pair_step(r, k, v, lw, a, g, prm, s):
        k_k, k_a, r_k, lnx_w, lnx_b = (prm[j:j + 1] for j in range(5))

        kk = k * k_k
        k2 = k * (1.0 + (a - 1.0) * k_a)
        sums = headsum(jnp.concatenate([kk * kk, r * k2 * r_k], axis=0))
        l1 = lw.astype(BF16)
        rem = lw - l1.astype(F32)
        l2 = rem.astype(BF16)
        l3 = (rem - l2.astype(F32)).astype(BF16)
        cum = _dot(tri_cc, l1) + _dot(tri_cc, l2) + _dot(tri_cc, l3)
        yield
        kkn = kk * lax.rsqrt(jnp.maximum(sums[:C], 1e-24))
        bonus = sums[C:] * v
        avec = -kkn
        bvec = kkn * a
        cum_end = cum[C - 1:C, :]
        p_incl = jnp.exp(cum)
        p_inv = jnp.exp(-cum)
        a_t = avec * jnp.exp(cum - lw)
        r_t = r * p_incl
        b_t = bvec * p_inv
        k_t = k2 * p_inv
        p_end = jnp.exp(cum_end - cum)
        b_h = bvec * p_end
        k_h = k2 * p_end

        v16 = v.astype(BF16)
        gram = _dot_nt(jnp.concatenate([a_t, r_t], axis=0).astype(BF16),
                       jnp.concatenate([bd(b_t), bd(k_t)], axis=0).astype(BF16))
        yield
        a_ab = jnp.where(tri_strict, gram[:C, :LANES], 0.0)
        a_ak = jnp.where(tri_strict, gram[:C, LANES:], 0.0)
        a_rb = jnp.where(tri_incl, gram[C:, :LANES], 0.0)
        a_rk = jnp.where(tri_incl, gram[C:, LANES:], 0.0)
        akv = _dot(a_ak.astype(BF16), bd(v16))

        t_inv = eye_pair + jnp.where(level_mask(2), a_ab, 0.0)
        bs = 4
        while bs <= C:
            x = jnp.where(level_mask(bs), a_ab, 0.0)
            t16 = t_inv.astype(BF16)
            tx = _dot(t16, bd(x).astype(BF16))
            yield
            t_inv = t_inv + _dot(tx.astype(BF16), bd(t16))
            yield
            bs *= 2
        t16 = t_inv.astype(BF16)

        tt = _dot(t16, jnp.concatenate([bd(a_t.astype(BF16)), bd(akv.astype(BF16))], axis=1))
        yield
        a_hat = tt[:, :LANES].astype(BF16)
        u0 = tt[:, LANES:].astype(BF16)

        rhs_top = jnp.concatenate([bd(a_hat), bd(u0)], axis=1)
        rhs_bot = jnp.concatenate([zeros_ll, bd(v16)], axis=1)
        qy = _dot(jnp.concatenate([a_rb, a_rk], axis=1).astype(BF16),
                  jnp.concatenate([rhs_top, rhs_bot], axis=0))
        lhs_t = jnp.concatenate([b_h, k_h], axis=0).T.astype(BF16)
        rhs = jnp.concatenate([jnp.concatenate([a_hat, u0], axis=1),
                               jnp.concatenate([zeros_cl, v16], axis=1)], axis=0)
        upd = _dot(lhs_t, rhs)
        yield
        q_hat = r_t + qy[:, :LANES]
        y0 = qy[:, LANES:]
        m_off = jnp.where(bd_mask, upd[:, :LANES], 0.0)
        n_new = jnp.where(bd_mask, upd[:, LANES:], 0.0)
        p_col = jnp.sum(jnp.where(diag_mask, jnp.exp(cum_end), 0.0), axis=1, keepdims=True)

        s_hi, s_lo = _split2(s)
        q16 = q_hat.astype(BF16)
        y = _dot(q16, s_hi) + _dot(q16, s_lo) + y0
        m16 = m_off.astype(BF16)
        s_new = p_col * s + _dot(m16, s_hi) + _dot(m16, s_lo) + n_new
        yield

        mu = headsum(y) * (1.0 / HEAD)
        yield
        d = y - mu
        var = headsum(d * d) * (1.0 / HEAD)
        yield
        yn = d * lax.rsqrt(var + LN_X_EPS) * lnx_w + lnx_b
        return ((yn + bonus) * g.astype(F32)).astype(o_ref.dtype), s_new

    lanes = [slice(p * LANES, (p + 1) * LANES) for p in range(pairs)]
    steps = [pair_step(r_ref[:, sl], k_ref[:, sl], v_ref[:, sl], lw_ref[:, sl], a_ref[:, sl], g_ref[:, sl],
                       prm_ref[:, sl], st_ref[p]) for p, sl in enumerate(lanes)]
    results = [None] * pairs
    while any(res is None for res in results):
        for p, step in enumerate(steps):
            if results[p] is None:
                try:
                    next(step)
                except StopIteration as done:
                    results[p] = done.value
    for p, sl in enumerate(lanes):
        o_ref[:, sl] = results[p][0]
        st_ref[p] = results[p][1]


def _wkv(r, k, v, lw, a, g, prm, *, pairs):
    length, d = r.shape
    width = pairs * LANES
    blk = lambda hb, c: (c, hb)
    return pl.pallas_call(
        functools.partial(_wkv_body, pairs=pairs),
        out_shape=jax.ShapeDtypeStruct((length, d), BF16),
        grid=(d // width, length // CHUNK),
        in_specs=[pl.BlockSpec((CHUNK, width), blk)] * 6 + [pl.BlockSpec((8, width), lambda hb, c: (0, hb))],
        out_specs=pl.BlockSpec((CHUNK, width), blk),
        scratch_shapes=[pltpu.VMEM((pairs, LANES, LANES), F32)],
        compiler_params=_cparams(("parallel", "arbitrary")),
        name="wkv7",
    )(r, k, v, lw, a, g, prm)


def _attn_body(q_ref, kp_ref, kc_ref, km_ref, vp_ref, vc_ref, vm_ref, sk_ref, o_ref, *, group, n_heads):
    kvh = pl.program_id(0)
    n = pl.program_id(1)
    rows = 2 * BLOCK
    rowi = lax.broadcasted_iota(jnp.int32, (rows, BLOCK), 0)
    kj = lax.broadcasted_iota(jnp.int32, (rows, BLOCK), 1)
    qi = rowi % BLOCK
    second = rowi >= BLOCK
    dist_prev = BLOCK + qi - kj
    dist_cur = qi - kj
    valid_prev = (dist_prev < WINDOW) & (n > 0)
    valid_cur = dist_cur >= 0
    valid_meta = kj < N_META
    dist_meta = N_META + n * BLOCK + qi - kj
    lane_o = lax.broadcasted_iota(jnp.int32, (BLOCK, LANES), 1)
    lane_q = lax.broadcasted_iota(jnp.int32, (rows, LANES), 1)
    row_q = lax.broadcasted_iota(jnp.int32, (rows, LANES), 0)
    q_keep = (row_q >= BLOCK) == (lane_q >= HEAD)

    kp = kp_ref[...]
    kc = kc_ref[...]
    km = km_ref[...]
    vp = vp_ref[...]
    vc = vc_ref[...]
    vm = vm_ref[...]
    for pr in range(group // 2):
        sl = slice(pr * LANES, (pr + 1) * LANES)
        q = q_ref[:, sl]
        q2 = jnp.where(q_keep, jnp.concatenate([q, q], axis=0), jnp.zeros((), q.dtype))
        head1 = (kvh * group + 2 * pr + 1).astype(F32)
        slope = jnp.exp2((jnp.where(second, 1.0, 0.0) + head1) * (-8.0 / n_heads))
        sink = jnp.where(second[:, :1], sk_ref[kvh, 2 * pr + 1], sk_ref[kvh, 2 * pr])
        s_prev = jnp.where(valid_prev, _dot_nt(q2, kp) - slope * dist_prev.astype(F32), NEG)
        s_cur = jnp.where(valid_cur, _dot_nt(q2, kc) - slope * dist_cur.astype(F32), NEG)
        s_meta = jnp.where(valid_meta, _dot_nt(q2, km) - slope * dist_meta.astype(F32), NEG)
        mx = jnp.maximum(jnp.maximum(jnp.max(s_prev, axis=1, keepdims=True), jnp.max(s_cur, axis=1, keepdims=True)),
                         jnp.maximum(jnp.max(s_meta, axis=1, keepdims=True), sink))
        e_prev = jnp.exp(s_prev - mx)
        e_cur = jnp.exp(s_cur - mx)
        e_meta = jnp.exp(s_meta - mx)
        den = (jnp.sum(e_prev, axis=1, keepdims=True) + jnp.sum(e_cur, axis=1, keepdims=True)
               + jnp.sum(e_meta, axis=1, keepdims=True) + jnp.exp(sink - mx))
        acc = (_dot(e_prev.astype(BF16), vp) + _dot(e_cur.astype(BF16), vc) + _dot(e_meta.astype(BF16), vm))
        acc = acc / den
        o_ref[:, sl] = jnp.where(lane_o >= HEAD, acc[BLOCK:], acc[:BLOCK]).astype(o_ref.dtype)


def _attention(q, k_dup, v_dup, km_dup, vm_dup, sinks, *, n_heads):
    s_len, d = q.shape
    group = n_heads // ATT_KV_HEADS
    gw = group * HEAD
    nb = s_len // BLOCK
    prev = lambda h, n: (jnp.maximum(n - 1, 0), h)
    cur = lambda h, n: (n, h)
    meta = lambda h, n: (0, h)
    return pl.pallas_call(
        functools.partial(_attn_body, group=group, n_heads=n_heads),
        out_shape=jax.ShapeDtypeStruct((s_len, d), BF16),
        grid=(ATT_KV_HEADS, nb),
        in_specs=[pl.BlockSpec((BLOCK, gw), cur),
                  pl.BlockSpec((BLOCK, LANES), prev), pl.BlockSpec((BLOCK, LANES), cur), pl.BlockSpec((BLOCK, LANES), meta),
                  pl.BlockSpec((BLOCK, LANES), prev), pl.BlockSpec((BLOCK, LANES), cur), pl.BlockSpec((BLOCK, LANES), meta),
                  pl.BlockSpec(memory_space=pltpu.SMEM)],
        out_specs=pl.BlockSpec((BLOCK, gw), cur),
        compiler_params=_cparams(("parallel", "parallel")),
        name="swa_attention",
    )(q, k_dup, k_dup, km_dup, v_dup, v_dup, vm_dup, sinks)


def _router_body(h_ref, g_ref, wr_ref, hn_ref, sel_ref, gate_ref, *, n_experts):
    y = _rms(h_ref[...]) * g_ref[...]
    hn_ref[...] = y
    logits = jnp.dot(y, wr_ref[...], preferred_element_type=F32, precision=lax.Precision.HIGHEST)
    lane = lax.broadcasted_iota(jnp.int32, logits.shape, 1)
    logits = jnp.where(lane < n_experts, logits, NEG)
    m1 = jnp.max(logits, axis=1, keepdims=True)
    i1 = jnp.min(jnp.where(logits == m1, lane, LANES), axis=1, keepdims=True)
    rest = jnp.where(lane == i1, NEG, logits)
    m2 = jnp.max(rest, axis=1, keepdims=True)
    i2 = jnp.min(jnp.where(rest == m2, lane, LANES), axis=1, keepdims=True)
    e2 = jnp.exp(m2 - m1)
    g1 = 1.0 / (1.0 + e2)
    g2 = e2 / (1.0 + e2)
    sel_ref[...] = jnp.where(lane == 0, i1, jnp.where(lane == 1, i2, 0))
    gate_ref[...] = jnp.where(lane == 0, g1, jnp.where(lane == 1, g2, 0.0))


def _router(h, gain, w_router_pad, *, n_experts, tm):
    m, d = h.shape
    return pl.pallas_call(
        functools.partial(_router_body, n_experts=n_experts),
        out_shape=[jax.ShapeDtypeStruct((m, d), F32), jax.ShapeDtypeStruct((m, LANES), jnp.int32),
                   jax.ShapeDtypeStruct((m, LANES), F32)],
        grid=(m // tm,),
        in_specs=[pl.BlockSpec((tm, d), lambda i: (i, 0)),
                  pl.BlockSpec((1, d), lambda i: (0, 0)),
                  pl.BlockSpec((d, LANES), lambda i: (0, 0))],
        out_specs=[pl.BlockSpec((tm, d), lambda i: (i, 0)), pl.BlockSpec((tm, LANES), lambda i: (i, 0)),
                   pl.BlockSpec((tm, LANES), lambda i: (i, 0))],
        compiler_params=_cparams(("parallel",)),
        name="moe_router",
    )(h, gain, w_router_pad)


EXPERT_TILE = 256


def _row_copy(src_hbm, dst, src_row, dst_row, sem):
    return pltpu.make_async_copy(src_hbm.at[pl.ds(src_row, 1)], dst.at[pl.ds(dst_row, 1)], sem)


def _gather_rows_body(src_ref, x_hbm, o_ref, buf, sem, *, rows):
    base = pl.program_id(0) * rows

    def issue(r, carry):
        _row_copy(x_hbm, buf, src_ref[base + r], r, sem).start()
        return carry

    lax.fori_loop(0, rows, issue, 0)
    pltpu.make_async_copy(x_hbm.at[pl.ds(0, rows)], buf, sem).wait()
    o_ref[...] = buf[...].astype(o_ref.dtype)


def _gather_rows(row_src, x, n_rows, *, rows, out_dtype):
    d = x.shape[1]
    return pl.pallas_call(
        functools.partial(_gather_rows_body, rows=rows),
        out_shape=jax.ShapeDtypeStruct((n_rows, d), out_dtype),
        grid_spec=pltpu.PrefetchScalarGridSpec(
            num_scalar_prefetch=1,
            grid=(n_rows // rows,),
            in_specs=[pl.BlockSpec(memory_space=pl.ANY)],
            out_specs=pl.BlockSpec((rows, d), lambda i, src: (i, 0)),
            scratch_shapes=[pltpu.VMEM((rows, d), x.dtype), pltpu.SemaphoreType.DMA],
        ),
        compiler_params=_cparams(("arbitrary",)),
        name="moe_gather",
    )(row_src, x)


def _moe_swiglu_body(te_ref, x_ref, wg_ref, wu_ref, o_ref):
    x = x_ref[...]
    g = _dot(x, wg_ref[...])
    u = _dot(x, wu_ref[...])
    o_ref[...] = (g * jax.nn.sigmoid(g) * u).astype(o_ref.dtype)


def _moe_swiglu(tile_expert, xs, wg, wu, *, tn):
    rows, d = xs.shape
    f = wg.shape[-1]
    w_spec = pl.BlockSpec((None, d, tn), lambda j, i, te: (te[i], 0, j))
    return pl.pallas_call(
        _moe_swiglu_body,
        out_shape=jax.ShapeDtypeStruct((rows, f), BF16),
        grid_spec=pltpu.PrefetchScalarGridSpec(
            num_scalar_prefetch=1,
            grid=(f // tn, rows // EXPERT_TILE),
            in_specs=[pl.BlockSpec((EXPERT_TILE, d), lambda j, i, te: (i, 0)), w_spec, w_spec],
            out_specs=pl.BlockSpec((EXPERT_TILE, tn), lambda j, i, te: (i, j)),
        ),
        compiler_params=_cparams(("parallel", "arbitrary")),
        name="moe_swiglu",
    )(tile_expert, xs, wg, wu)


def _moe_down_body(te_ref, x_ref, w_ref, o_ref):
    o_ref[...] = _dot(x_ref[...], w_ref[...])


def _moe_down(tile_expert, act, wd, *, tn):
    rows, f = act.shape
    d = wd.shape[-1]
    return pl.pallas_call(
        _moe_down_body,
        out_shape=jax.ShapeDtypeStruct((rows, d), F32),
        grid_spec=pltpu.PrefetchScalarGridSpec(
            num_scalar_prefetch=1,
            grid=(d // tn, rows // EXPERT_TILE),
            in_specs=[pl.BlockSpec((EXPERT_TILE, f), lambda j, i, te: (i, 0)),
                      pl.BlockSpec((None, f, tn), lambda j, i, te: (te[i], 0, j))],
            out_specs=pl.BlockSpec((EXPERT_TILE, tn), lambda j, i, te: (i, j)),
        ),
        compiler_params=_cparams(("parallel", "arbitrary")),
        name="moe_down",
    )(tile_expert, act, wd)


def _moe_combine_body(pos_ref, h_ref, gate_ref, fn_ref, eo_hbm, o_ref, buf, sem, *, tm):
    base = pl.program_id(0) * tm

    def issue(t, carry):
        for j in range(TOP_K):
            _row_copy(eo_hbm, buf.at[j], pos_ref[TOP_K * (base + t) + j], t, sem).start()
        return carry

    lax.fori_loop(0, tm, issue, 0)
    for j in range(TOP_K):
        pltpu.make_async_copy(eo_hbm.at[pl.ds(0, tm)], buf.at[j], sem).wait()
    y = h_ref[...]
    for j in range(TOP_K):
        y = y + gate_ref[:, j:j + 1] * buf[j]
    o_ref[...] = _rms(y) * fn_ref[...]


def _moe_combine(pos, h, gate, final_gain, eo, *, tm):
    m, d = h.shape
    return pl.pallas_call(
        functools.partial(_moe_combine_body, tm=tm),
        out_shape=jax.ShapeDtypeStruct((m, d), F32),
        grid_spec=pltpu.PrefetchScalarGridSpec(
            num_scalar_prefetch=1,
            grid=(m // tm,),
            in_specs=[pl.BlockSpec((tm, d), lambda i, pos: (i, 0)),
                      pl.BlockSpec((tm, LANES), lambda i, pos: (i, 0)),
                      pl.BlockSpec((1, d), lambda i, pos: (0, 0)),
                      pl.BlockSpec(memory_space=pl.ANY)],
            out_specs=pl.BlockSpec((tm, d), lambda i, pos: (i, 0)),
            scratch_shapes=[pltpu.VMEM((TOP_K, tm, d), F32), pltpu.SemaphoreType.DMA],
        ),
        compiler_params=_cparams(("arbitrary",)),
        name="moe_combine",
    )(pos, h, gate, final_gain, eo)


def _moe_plan(sel, n_experts):
    m = sel.shape[0]
    flat_e = sel[:, :TOP_K].reshape(-1)
    onehot = (flat_e[:, None] == jnp.arange(n_experts, dtype=jnp.int32)[None, :]).astype(jnp.int32)
    csum = jnp.cumsum(onehot, axis=0)
    rank = jnp.sum(csum * onehot, axis=1) - 1
    counts = csum[-1]
    padded = (counts + EXPERT_TILE - 1) // EXPERT_TILE * EXPERT_TILE
    ends = jnp.cumsum(padded)
    starts = ends - padded
    pos = (jnp.sum(onehot * starts[None, :], axis=1) + rank).astype(jnp.int32)
    n_rows = TOP_K * m + n_experts * EXPERT_TILE
    token = jnp.arange(TOP_K * m, dtype=jnp.int32) // TOP_K
    row_src = jnp.zeros((n_rows,), jnp.int32).at[pos].set(token)
    tile_start = jnp.arange(n_rows // EXPERT_TILE, dtype=jnp.int32) * EXPERT_TILE
    tile_expert = jnp.minimum(jnp.sum((tile_start[:, None] >= ends[None, :]).astype(jnp.int32), axis=1),
                              n_experts - 1).astype(jnp.int32)
    return pos, row_src, tile_expert, n_rows


def _largest_tile(n, cap, mult):
    best = None
    t = mult
    while t <= min(n, cap):
        if n % t == 0:
            best = t
        t += mult
    assert best is not None, (n, cap, mult)
    return best


def _pad_cols(w, n_to):
    return jnp.pad(w, ((0, 0), (0, n_to - w.shape[1])))


def _pad_rows(w, n_to):
    return jnp.pad(w, ((0, n_to - w.shape[0]), (0, 0)))


def _round_up(n, m):
    return -(-n // m) * m


def kernel(x, meta_tokens, a_norm, a_mix, a_w_rkv, a_w0, a_w1, a_w2, a_a0, a_a1, a_a2, a_g1, a_g2, a_k_k, a_k_a, a_r_k, a_lnx_w, a_lnx_b, a_w_out, kv_norm, w_kv, b_norm, b_w_q, b_sinks, b_w_out, f_norm, d_w_gate, d_w_up, d_w_down, e_router, e_w_gate, e_w_up, e_w_down, final_norm):
    assert x.shape[0] == 1
    seq, d = x.shape[1], x.shape[2]
    n_heads = d // HEAD
    n_experts = e_router.shape[-1]
    xs = x[0]

    l_real = N_META + seq
    l_pad = _round_up(l_real, CHUNK)
    h = jnp.concatenate([meta_tokens.astype(F32), xs, jnp.zeros((l_pad - l_real, d), F32)], axis=0)
    tm0 = _largest_tile(l_pad, 768, 16)
    tn = _largest_tile(d, 1024, LANES)

    xr, xw, xk, xv, xa, xg = _premix(h, a_norm[0:1], a_mix[0], _largest_tile(l_pad, 256, 16))
    w_rkv = a_w_rkv[0].astype(BF16)
    r = _mm(xr, w_rkv[0], out_dtype=F32, tm=tm0, tn=tn)
    k = _mm(xk, w_rkv[1], out_dtype=F32, tm=tm0, tn=tn)
    v = _mm(xv, w_rkv[2], out_dtype=F32, tm=tm0, tn=tn)
    tml = _largest_tile(l_pad, 384, 16)
    lw = _lora(xw, a_w1[0].astype(BF16), a_w2[0].astype(BF16), a_w0[0][None], mid="tanh", post="log_decay",
               out_dtype=F32, tm=tml)
    a = _lora(xa, a_a1[0].astype(BF16), a_a2[0].astype(BF16), a_a0[0][None], mid="none", post="sigmoid",
              out_dtype=F32, tm=tml)
    gate_rank = _round_up(a_g1.shape[-1], LANES)
    g = _lora(xg, _pad_cols(a_g1[0], gate_rank).astype(BF16), _pad_rows(a_g2[0], gate_rank).astype(BF16),
              jnp.zeros((1, d), F32), mid="sigmoid", post="none", out_dtype=BF16, tm=tml)
    prm = jnp.concatenate([a_k_k[0][None], a_k_a[0][None], a_r_k[0].reshape(1, d), a_lnx_w[0][None],
                           a_lnx_b[0][None], jnp.zeros((3, d), F32)], axis=0)
    pairs = 8 if n_heads % 16 == 0 else n_heads // 2
    mixed = _wkv(r, k, v, lw, a, g, prm, pairs=pairs)
    h = _mm(mixed, a_w_out[0].astype(BF16), out_dtype=F32, tm=tm0, tn=tn, res=h)

    (hn,) = _rmsnorm(h, f_norm[0:1], [BF16], _largest_tile(l_pad, 256, 16))
    d_ff = d_w_gate.shape[-1]
    ff_pad = _round_up(d_ff, 4 * LANES)
    act = _swiglu(hn, _pad_cols(d_w_gate[0], ff_pad).astype(BF16), _pad_cols(d_w_up[0], ff_pad).astype(BF16),
                  tm=tm0, tn=_largest_tile(ff_pad, 512, LANES))
    h = _mm_acc(act, _pad_rows(d_w_down[0], ff_pad).astype(BF16), h, tm=tm0, tn=tn,
                tk=_largest_tile(ff_pad, 3072, LANES))

    hkv, hq = _rmsnorm(h, jnp.stack([kv_norm, b_norm[0]]), [BF16, BF16], _largest_tile(l_pad, 256, 16))
    kv_w = ATT_KV_HEADS * HEAD
    kv = _mm(hkv, w_kv.astype(BF16), out_dtype=BF16, tm=tm0, tn=_largest_tile(2 * kv_w, 1024, LANES))

    def dup_heads(t):
        t = t.reshape(t.shape[0], ATT_KV_HEADS, 1, HEAD)
        return jnp.broadcast_to(t, (t.shape[0], ATT_KV_HEADS, 2, HEAD)).reshape(t.shape[0], ATT_KV_HEADS * LANES)

    k_all, v_all = dup_heads(kv[:, :kv_w]), dup_heads(kv[:, kv_w:])
    k_meta = _pad_rows(k_all[:N_META], BLOCK)
    v_meta = _pad_rows(v_all[:N_META], BLOCK)
    k_real = k_all[N_META:l_real]
    v_real = v_all[N_META:l_real]

    h = h[N_META:l_real]
    hq = hq[N_META:l_real]
    tm1 = _largest_tile(seq, 512, 16)
    q = _mm(hq, b_w_q[0].astype(BF16), out_dtype=BF16, tm=tm1, tn=tn, scale=HEAD ** -0.5)
    o = _attention(q, k_real, v_real, k_meta, v_meta, b_sinks[0].reshape(ATT_KV_HEADS, -1), n_heads=n_heads)
    h = _mm(o, b_w_out[0].astype(BF16), out_dtype=F32, tm=tm1, tn=tn, res=h)

    hn, sel, gate = _router(h, f_norm[1:2], _pad_cols(e_router[0], LANES), n_experts=n_experts,
                            tm=_largest_tile(seq, 256, 16))
    pos, row_src, tile_expert, n_rows = _moe_plan(sel, n_experts)
    xs_sorted = _gather_rows(row_src, hn, n_rows, rows=_largest_tile(n_rows, 512, EXPERT_TILE), out_dtype=BF16)
    d_exp = e_w_gate.shape[-1]
    act = _moe_swiglu(tile_expert, xs_sorted, e_w_gate[0].astype(BF16), e_w_up[0].astype(BF16),
                      tn=_largest_tile(d_exp, 896, LANES))
    eo = _moe_down(tile_expert, act, e_w_down[0].astype(BF16), tn=tn)
    out = _moe_combine(pos, h, gate, final_norm[None], eo, tm=_largest_tile(seq, 128, 8))
    return out[None]
```

```python
import functools

import jax
import jax.numpy as jnp
from jax import lax
from jax.experimental import pallas as pl
from jax.experimental.pallas import tpu as pltpu

F32 = jnp.float32
BF16 = jnp.bfloat16

LANES = 128
VMEM_LIMIT_BYTES = 56 * 1024 * 1024

N_META = 16
RMS_EPS = 1e-5
LN_X_EPS = 64e-5
HEAD = 64
ATT_KV_HEADS = 8
WINDOW = 128
BLOCK = 128
TOP_K = 2
CHUNK = 64
NEG = -1e30


def _cparams(sem):
    return pltpu.CompilerParams(dimension_semantics=sem, vmem_limit_bytes=VMEM_LIMIT_BYTES)


def _dot(a, b):
    return jnp.dot(a, b, preferred_element_type=F32)


def _dot_nt(a, b):
    return lax.dot_general(a, b, (((1,), (1,)), ((), ())), preferred_element_type=F32)


def _rms(x):
    return x * lax.rsqrt(jnp.mean(x * x, axis=-1, keepdims=True) + RMS_EPS)


def _run_interleaved(steps):
    results = [None] * len(steps)
    pending = set(range(len(steps)))
    while pending:
        for idx in sorted(pending):
            try:
                next(steps[idx])
            except StopIteration as done:
                results[idx] = done.value
                pending.discard(idx)
    return results


def _rmsnorm_body(h_ref, g_ref, *o_refs):
    y = _rms(h_ref[...])
    for j, o_ref in enumerate(o_refs):
        o_ref[...] = (y * g_ref[j:j + 1, :]).astype(o_ref.dtype)


def _rmsnorm(h, gains, out_dtypes, tm):
    m, d = h.shape
    n_out = len(out_dtypes)
    outs = pl.pallas_call(
        _rmsnorm_body,
        out_shape=[jax.ShapeDtypeStruct((m, d), dt) for dt in out_dtypes],
        grid=(m // tm,),
        in_specs=[pl.BlockSpec((tm, d), lambda i: (i, 0)),
                  pl.BlockSpec((n_out, d), lambda i: (0, 0))],
        out_specs=[pl.BlockSpec((tm, d), lambda i: (i, 0)) for _ in out_dtypes],
        compiler_params=_cparams(("parallel",)),
        name="rmsnorm",
    )(h, gains)
    return outs


def _premix_body(h_ref, hp_ref, g_ref, mix_ref, *o_refs):
    i = pl.program_id(0)
    g = g_ref[...]
    xn = _rms(h_ref[...]) * g
    pn = _rms(hp_ref[...]) * g
    prev_row = jnp.where(i > 0, pn[7:8, :], 0.0)
    sh = pltpu.roll(xn, 1, axis=0)
    row = lax.broadcasted_iota(jnp.int32, xn.shape, 0)
    sh = jnp.where(row == 0, prev_row, sh)
    xx = sh - xn
    for j, o_ref in enumerate(o_refs):
        o_ref[...] = (xn + xx * mix_ref[j:j + 1, :]).astype(o_ref.dtype)


def _premix(h, gain, mix, tm):
    m, d = h.shape
    n_mix = mix.shape[0]
    rows8 = tm // 8
    return pl.pallas_call(
        _premix_body,
        out_shape=[jax.ShapeDtypeStruct((m, d), BF16) for _ in range(n_mix)],
        grid=(m // tm,),
        in_specs=[pl.BlockSpec((tm, d), lambda i: (i, 0)),
                  pl.BlockSpec((8, d), lambda i: (jnp.maximum(i * rows8 - 1, 0), 0)),
                  pl.BlockSpec((1, d), lambda i: (0, 0)),
                  pl.BlockSpec((n_mix, d), lambda i: (0, 0))],
        out_specs=[pl.BlockSpec((tm, d), lambda i: (i, 0)) for _ in range(n_mix)],
        compiler_params=_cparams(("parallel",)),
        name="premix",
    )(h, h, gain, mix)


def _mm_body(x_ref, w_ref, *rest, has_res, scale):
    if has_res:
        res_ref, o_ref = rest
    else:
        (o_ref,) = rest
    acc = _dot(x_ref[...], w_ref[...])
    if scale is not None:
        acc = acc * scale
    if has_res:
        acc = acc + res_ref[...]
    o_ref[...] = acc.astype(o_ref.dtype)


def _mm(x, w, *, out_dtype, tm, tn, res=None, scale=None):
    m, k = x.shape
    n = w.shape[1]
    in_specs = [pl.BlockSpec((tm, k), lambda j, i: (i, 0)),
                pl.BlockSpec((k, tn), lambda j, i: (0, j))]
    args = [x, w]
    if res is not None:
        in_specs.append(pl.BlockSpec((tm, tn), lambda j, i: (i, j)))
        args.append(res)
    return pl.pallas_call(
        functools.partial(_mm_body, has_res=res is not None, scale=scale),
        out_shape=jax.ShapeDtypeStruct((m, n), out_dtype),
        grid=(n // tn, m // tm),
        in_specs=in_specs,
        out_specs=pl.BlockSpec((tm, tn), lambda j, i: (i, j)),
        compiler_params=_cparams(("parallel", "parallel")),
        name="matmul",
    )(*args)


def _mm_acc_body(x_ref, w_ref, res_ref, o_ref, acc_ref):
    kk = pl.program_id(2)

    @pl.when(kk == 0)
    def _():
        acc_ref[...] = jnp.zeros_like(acc_ref)

    acc_ref[...] += _dot(x_ref[...], w_ref[...])

    @pl.when(kk == pl.num_programs(2) - 1)
    def _():
        o_ref[...] = res_ref[...] + acc_ref[...]


def _mm_acc(x, w, res, *, tm, tn, tk):
    m, k = x.shape
    n = w.shape[1]
    return pl.pallas_call(
        _mm_acc_body,
        out_shape=jax.ShapeDtypeStruct((m, n), F32),
        grid=(n // tn, m // tm, k // tk),
        in_specs=[pl.BlockSpec((tm, tk), lambda j, i, q: (i, q)),
                  pl.BlockSpec((tk, tn), lambda j, i, q: (q, j)),
                  pl.BlockSpec((tm, tn), lambda j, i, q: (i, j))],
        out_specs=pl.BlockSpec((tm, tn), lambda j, i, q: (i, j)),
        scratch_shapes=[pltpu.VMEM((tm, tn), F32)],
        compiler_params=_cparams(("parallel", "parallel", "arbitrary")),
        name="matmul_acc",
    )(x, w, res)


def _swiglu_body(x_ref, wg_ref, wu_ref, o_ref):
    x = x_ref[...]
    g = _dot(x, wg_ref[...])
    u = _dot(x, wu_ref[...])
    o_ref[...] = (g * jax.nn.sigmoid(g) * u).astype(o_ref.dtype)


def _swiglu(x, wg, wu, *, tm, tn):
    m, k = x.shape
    n = wg.shape[1]
    return pl.pallas_call(
        _swiglu_body,
        out_shape=jax.ShapeDtypeStruct((m, n), BF16),
        grid=(n // tn, m // tm),
        in_specs=[pl.BlockSpec((tm, k), lambda j, i: (i, 0)),
                  pl.BlockSpec((k, tn), lambda j, i: (0, j)),
                  pl.BlockSpec((k, tn), lambda j, i: (0, j))],
        out_specs=pl.BlockSpec((tm, tn), lambda j, i: (i, j)),
        compiler_params=_cparams(("parallel", "parallel")),
        name="swiglu",
    )(x, wg, wu)


def _lora_body(x_ref, w1_ref, w2_ref, b_ref, o_ref, *, mid, post):
    t = _dot(x_ref[...], w1_ref[...])
    if mid == "tanh":
        t = jnp.tanh(t)
    elif mid == "sigmoid":
        t = jax.nn.sigmoid(t)
    z = _dot(t.astype(BF16), w2_ref[...]) + b_ref[...]
    if post == "log_decay":
        z = jax.nn.sigmoid(z) * (-0.6065306597126334)
    elif post == "sigmoid":
        z = jax.nn.sigmoid(z)
    o_ref[...] = z.astype(o_ref.dtype)


def _lora(x, w1, w2, bias, *, mid, post, out_dtype, tm):
    m, k = x.shape
    r = w1.shape[1]
    n = w2.shape[1]
    return pl.pallas_call(
        functools.partial(_lora_body, mid=mid, post=post),
        out_shape=jax.ShapeDtypeStruct((m, n), out_dtype),
        grid=(m // tm,),
        in_specs=[pl.BlockSpec((tm, k), lambda i: (i, 0)),
                  pl.BlockSpec((k, r), lambda i: (0, 0)),
                  pl.BlockSpec((r, n), lambda i: (0, 0)),
                  pl.BlockSpec((1, n), lambda i: (0, 0))],
        out_specs=pl.BlockSpec((tm, n), lambda i: (i, 0)),
        compiler_params=_cparams(("parallel",)),
        name="lora",
    )(x, w1, w2, bias)


def _split2(x):
    hi = x.astype(BF16)
    lo = (x - hi.astype(F32)).astype(BF16)
    return hi, lo


def _wkv_body(r_ref, k_ref, v_ref, lw_ref, a_ref, g_ref, prm_ref, o_ref, st_ref, *, pairs):
    c_idx = pl.program_id(1)

    @pl.when(c_idx == 0)
    def _():
        st_ref[...] = jnp.zeros_like(st_ref)

    C = CHUNK
    lane = lax.broadcasted_iota(jnp.int32, (C, LANES), 1)
    row = lax.broadcasted_iota(jnp.int32, (C, LANES), 0)
    col = lane % HEAD
    upper_half = lane >= HEAD
    tri_strict = col < row
    tri_incl = col <= row
    eye_pair = (col == row).astype(F32)

    def level_mask(bs):
        return ((row // bs) == (col // bs)) & ((row // (bs // 2)) != (col // (bs // 2)))

    row2 = lax.broadcasted_iota(jnp.int32, (2 * C, LANES), 0)
    lane2 = lax.broadcasted_iota(jnp.int32, (2 * C, LANES), 1)
    bd_mask = (row2 >= C) == (lane2 >= HEAD)
    ones_bd = bd_mask.astype(BF16)
    diag_mask = row2 == lane2
    t_r = lax.broadcasted_iota(jnp.int32, (C, C), 0)
    t_c = lax.broadcasted_iota(jnp.int32, (C, C), 1)
    tri_cc = (t_c <= t_r).astype(BF16)

    def bd(x):
        return jnp.concatenate([jnp.where(upper_half, 0.0, x), jnp.where(upper_half, x, 0.0)], axis=0)

    def headsum(x):
        return _dot(x.astype(BF16), ones_bd)

    zeros_cl = jnp.zeros((C, LANES), BF16)
    zeros_ll = jnp.zeros((2 * C, LANES), BF16)

    def pair_step(r, k, v, lw, a, g, prm, s):
        k_k, k_a, r_k, lnx_w, lnx_b = (prm[j:j + 1] for j in range(5))

        kk = k * k_k
        k2 = k * (1.0 + (a - 1.0) * k_a)
        sums = headsum(jnp.concatenate([kk * kk, r * k2 * r_k], axis=0))
        l1, l2 = _split2(lw)
        cum = _dot(tri_cc, l1) + _dot(tri_cc, l2)
        yield
        kkn = kk * lax.rsqrt(jnp.maximum(sums[:C], 1e-24))
        bonus = sums[C:] * v
        avec = -kkn
        bvec = kkn * a
        cum_end = cum[C - 1:C, :]
        p_incl = jnp.exp(cum)
        p_inv = jnp.exp(-cum)
        a_t = avec * jnp.exp(cum - lw)
        r_t = r * p_incl
        b_t = bvec * p_inv
        k_t = k2 * p_inv
        p_end = jnp.exp(cum_end - cum)
        b_h = bvec * p_end
        k_h = k2 * p_end

        v16 = v.astype(BF16)
        gram = _dot_nt(jnp.concatenate([a_t, r_t], axis=0).astype(BF16),
                       jnp.concatenate([bd(b_t), bd(k_t)], axis=0).astype(BF16))
        yield
        a_ab = jnp.where(tri_strict, gram[:C, :LANES], 0.0)
        a_ak = jnp.where(tri_strict, gram[:C, LANES:], 0.0)
        a_rb = jnp.where(tri_incl, gram[C:, :LANES], 0.0)
        a_rk = jnp.where(tri_incl, gram[C:, LANES:], 0.0)
        akv = _dot(a_ak.astype(BF16), bd(v16))

        t_inv = eye_pair + jnp.where(level_mask(2), a_ab, 0.0)
        bs = 4
        while bs <= C:
            x = jnp.where(level_mask(bs), a_ab, 0.0)
            t16 = t_inv.astype(BF16)
            tx = _dot(t16, bd(x).astype(BF16))
            yield
            t_inv = t_inv + _dot(tx.astype(BF16), bd(t16))
            yield
            bs *= 2
        t16 = t_inv.astype(BF16)

        tt = _dot(t16, jnp.concatenate([bd(a_t.astype(BF16)), bd(akv.astype(BF16))], axis=1))
        yield
        a_hat = tt[:, :LANES].astype(BF16)
        u0 = tt[:, LANES:].astype(BF16)

        rhs_top = jnp.concatenate([bd(a_hat), bd(u0)], axis=1)
        rhs_bot = jnp.concatenate([zeros_ll, bd(v16)], axis=1)
        qy = _dot(jnp.concatenate([a_rb, a_rk], axis=1).astype(BF16),
                  jnp.concatenate([rhs_top, rhs_bot], axis=0))
        lhs_t = jnp.concatenate([b_h, k_h], axis=0).T.astype(BF16)
        rhs = jnp.concatenate([jnp.concatenate([a_hat, u0], axis=1),
                               jnp.concatenate([zeros_cl, v16], axis=1)], axis=0)
        upd = _dot(lhs_t, rhs)
        yield
        q_hat = r_t + qy[:, :LANES]
        y0 = qy[:, LANES:]
        m_off = jnp.where(bd_mask, upd[:, :LANES], 0.0)
        n_new = jnp.where(bd_mask, upd[:, LANES:], 0.0)
        p_col = jnp.sum(jnp.where(diag_mask, jnp.exp(cum_end), 0.0), axis=1, keepdims=True)

        s_hi, s_lo = _split2(s)
        q16 = q_hat.astype(BF16)
        y = _dot(q16, s_hi) + y0
        m16 = m_off.astype(BF16)
        s_new = p_col * s + _dot(m16, s_hi) + _dot(m16, s_lo) + n_new
        yield

        mu = headsum(y) * (1.0 / HEAD)
        yield
        d = y - mu
        var = headsum(d * d) * (1.0 / HEAD)
        yield
        yn = d * lax.rsqrt(var + LN_X_EPS) * lnx_w + lnx_b
        return ((yn + bonus) * g.astype(F32)).astype(o_ref.dtype), s_new

    lanes = [slice(p * LANES, (p + 1) * LANES) for p in range(pairs)]
    results = _run_interleaved([pair_step(r_ref[:, sl], k_ref[:, sl], v_ref[:, sl], lw_ref[:, sl], a_ref[:, sl],
                                          g_ref[:, sl], prm_ref[:, sl], st_ref[p]) for p, sl in enumerate(lanes)])
    for p, sl in enumerate(lanes):
        o_ref[:, sl] = results[p][0]
        st_ref[p] = results[p][1]


def _wkv(r, k, v, lw, a, g, prm, *, pairs):
    length, d = r.shape
    width = pairs * LANES
    blk = lambda hb, c: (c, hb)
    return pl.pallas_call(
        functools.partial(_wkv_body, pairs=pairs),
        out_shape=jax.ShapeDtypeStruct((length, d), BF16),
        grid=(d // width, length // CHUNK),
        in_specs=[pl.BlockSpec((CHUNK, width), blk)] * 6 + [pl.BlockSpec((8, width), lambda hb, c: (0, hb))],
        out_specs=pl.BlockSpec((CHUNK, width), blk),
        scratch_shapes=[pltpu.VMEM((pairs, LANES, LANES), F32)],
        compiler_params=_cparams(("parallel", "arbitrary")),
        name="wkv7",
    )(r, k, v, lw, a, g, prm)


def _attn_body(q_ref, kp_ref, kc_ref, km_ref, vp_ref, vc_ref, vm_ref, sk_ref, o_ref, *, group, n_heads):
    kvh = pl.program_id(0)
    n = pl.program_id(1)
    rows = 2 * BLOCK
    rowi = lax.broadcasted_iota(jnp.int32, (rows, BLOCK), 0)
    kj = lax.broadcasted_iota(jnp.int32, (rows, BLOCK), 1)
    qi = rowi % BLOCK
    second = rowi >= BLOCK
    use_prev = kj > qi
    dist_band = (qi - kj + jnp.where(use_prev, BLOCK, 0)).astype(F32)
    valid_band = jnp.logical_not(use_prev & (n == 0))
    valid_meta = kj < N_META
    dist_meta = (N_META + n * BLOCK + qi - kj).astype(F32)
    lane_o = lax.broadcasted_iota(jnp.int32, (BLOCK, LANES), 1)
    lane_q = lax.broadcasted_iota(jnp.int32, (rows, LANES), 1)
    row_q = lax.broadcasted_iota(jnp.int32, (rows, LANES), 0)
    q_keep = (row_q >= BLOCK) == (lane_q >= HEAD)

    kp = kp_ref[...]
    kc = kc_ref[...]
    km = km_ref[...]
    vp = vp_ref[...]
    vc = vc_ref[...]
    vm = vm_ref[...]

    def pair_step(pr, q):
        q2 = jnp.where(q_keep, jnp.concatenate([q, q], axis=0), jnp.zeros((), q.dtype))
        qk_prev = _dot_nt(q2, kp)
        qk_cur = _dot_nt(q2, kc)
        qk_meta = _dot_nt(q2, km)
        yield
        head1 = (kvh * group + 2 * pr + 1).astype(F32)
        slope = jnp.exp2((jnp.where(second, 1.0, 0.0) + head1) * (-8.0 / n_heads))
        sink = jnp.where(second[:, :1], sk_ref[kvh, 2 * pr + 1], sk_ref[kvh, 2 * pr])
        s_band = jnp.where(valid_band, jnp.where(use_prev, qk_prev, qk_cur) - slope * dist_band, NEG)
        s_meta = jnp.where(valid_meta, qk_meta - slope * dist_meta, NEG)
        mx = jnp.maximum(jnp.max(jnp.maximum(s_band, s_meta), axis=1, keepdims=True), sink)
        e_band = jnp.exp(s_band - mx)
        e_meta = jnp.exp(s_meta - mx)
        den = jnp.sum(e_band + e_meta, axis=1, keepdims=True) + jnp.exp(sink - mx)
        acc = (_dot(jnp.where(use_prev, e_band, 0.0).astype(BF16), vp)
               + _dot(jnp.where(use_prev, 0.0, e_band).astype(BF16), vc)
               + _dot(e_meta.astype(BF16), vm))
        yield
        acc = acc / den
        return jnp.where(lane_o >= HEAD, acc[BLOCK:], acc[:BLOCK]).astype(o_ref.dtype)

    lanes = [slice(pr * LANES, (pr + 1) * LANES) for pr in range(group // 2)]
    results = _run_interleaved([pair_step(pr, q_ref[:, sl]) for pr, sl in enumerate(lanes)])
    for pr, sl in enumerate(lanes):
        o_ref[:, sl] = results[pr]


def _attention(q, k_dup, v_dup, km_dup, vm_dup, sinks, *, n_heads):
    s_len, d = q.shape
    group = n_heads // ATT_KV_HEADS
    gw = group * HEAD
    nb = s_len // BLOCK
    prev = lambda h, n: (jnp.maximum(n - 1, 0), h)
    cur = lambda h, n: (n, h)
    meta = lambda h, n: (0, h)
    return pl.pallas_call(
        functools.partial(_attn_body, group=group, n_heads=n_heads),
        out_shape=jax.ShapeDtypeStruct((s_len, d), BF16),
        grid=(ATT_KV_HEADS, nb),
        in_specs=[pl.BlockSpec((BLOCK, gw), cur),
                  pl.BlockSpec((BLOCK, LANES), prev), pl.BlockSpec((BLOCK, LANES), cur), pl.BlockSpec((BLOCK, LANES), meta),
                  pl.BlockSpec((BLOCK, LANES), prev), pl.BlockSpec((BLOCK, LANES), cur), pl.BlockSpec((BLOCK, LANES), meta),
                  pl.BlockSpec(memory_space=pltpu.SMEM)],
        out_specs=pl.BlockSpec((BLOCK, gw), cur),
        compiler_params=_cparams(("parallel", "parallel")),
        name="swa_attention",
    )(q, k_dup, k_dup, km_dup, v_dup, v_dup, vm_dup, sinks)


def _router_body(h_ref, g_ref, wr_ref, hn_ref, sel_ref, gate_ref, *, n_experts):
    y = _rms(h_ref[...]) * g_ref[...]
    hn_ref[...] = y
    logits = jnp.dot(y, wr_ref[...], preferred_element_type=F32, precision=lax.Precision.HIGHEST)
    lane = lax.broadcasted_iota(jnp.int32, logits.shape, 1)
    logits = jnp.where(lane < n_experts, logits, NEG)
    m1 = jnp.max(logits, axis=1, keepdims=True)
    i1 = jnp.min(jnp.where(logits == m1, lane, LANES), axis=1, keepdims=True)
    rest = jnp.where(lane == i1, NEG, logits)
    m2 = jnp.max(rest, axis=1, keepdims=True)
    i2 = jnp.min(jnp.where(rest == m2, lane, LANES), axis=1, keepdims=True)
    e2 = jnp.exp(m2 - m1)
    g1 = 1.0 / (1.0 + e2)
    g2 = e2 / (1.0 + e2)
    sel_ref[...] = jnp.where(lane == 0, i1, jnp.where(lane == 1, i2, 0))
    gate_ref[...] = jnp.where(lane == 0, g1, jnp.where(lane == 1, g2, 0.0))


def _router(h, gain, w_router_pad, *, n_experts, tm):
    m, d = h.shape
    return pl.pallas_call(
        functools.partial(_router_body, n_experts=n_experts),
        out_shape=[jax.ShapeDtypeStruct((m, d), F32), jax.ShapeDtypeStruct((m, LANES), jnp.int32),
                   jax.ShapeDtypeStruct((m, LANES), F32)],
        grid=(m // tm,),
        in_specs=[pl.BlockSpec((tm, d), lambda i: (i, 0)),
                  pl.BlockSpec((1, d), lambda i: (0, 0)),
                  pl.BlockSpec((d, LANES), lambda i: (0, 0))],
        out_specs=[pl.BlockSpec((tm, d), lambda i: (i, 0)), pl.BlockSpec((tm, LANES), lambda i: (i, 0)),
                   pl.BlockSpec((tm, LANES), lambda i: (i, 0))],
        compiler_params=_cparams(("parallel",)),
        name="moe_router",
    )(h, gain, w_router_pad)


EXPERT_TILE = 256


def _row_copy(src_hbm, dst, src_row, dst_row, sem):
    return pltpu.make_async_copy(src_hbm.at[pl.ds(src_row, 1)], dst.at[pl.ds(dst_row, 1)], sem)


def _gather_rows_body(src_ref, x_hbm, o_ref, buf, sem, *, rows):
    base = pl.program_id(0) * rows

    def issue(r, carry):
        _row_copy(x_hbm, buf, src_ref[base + r], r, sem).start()
        return carry

    lax.fori_loop(0, rows, issue, 0)
    pltpu.make_async_copy(x_hbm.at[pl.ds(0, rows)], buf, sem).wait()
    o_ref[...] = buf[...].astype(o_ref.dtype)


def _gather_rows(row_src, x, n_rows, *, rows, out_dtype):
    d = x.shape[1]
    return pl.pallas_call(
        functools.partial(_gather_rows_body, rows=rows),
        out_shape=jax.ShapeDtypeStruct((n_rows, d), out_dtype),
        grid_spec=pltpu.PrefetchScalarGridSpec(
            num_scalar_prefetch=1,
            grid=(n_rows // rows,),
            in_specs=[pl.BlockSpec(memory_space=pl.ANY)],
            out_specs=pl.BlockSpec((rows, d), lambda i, src: (i, 0)),
            scratch_shapes=[pltpu.VMEM((rows, d), x.dtype), pltpu.SemaphoreType.DMA],
        ),
        compiler_params=_cparams(("arbitrary",)),
        name="moe_gather",
    )(row_src, x)


def _new_weight_block(te_ref):
    i = pl.program_id(1)
    return (i == 0) | (te_ref[i] != te_ref[jnp.maximum(i - 1, 0)])


def _moe_swiglu_body(te_ref, x_ref, wg_ref, wu_ref, o_ref, wg16_ref, wu16_ref):
    @pl.when(_new_weight_block(te_ref))
    def _():
        wg16_ref[...] = wg_ref[...].astype(BF16)
        wu16_ref[...] = wu_ref[...].astype(BF16)

    x = x_ref[...]
    g = _dot(x, wg16_ref[...])
    u = _dot(x, wu16_ref[...])
    o_ref[...] = (g * jax.nn.sigmoid(g) * u).astype(o_ref.dtype)


def _moe_swiglu(tile_expert, xs, wg, wu, *, tn):
    rows, d = xs.shape
    f = wg.shape[-1]
    w_spec = pl.BlockSpec((None, d, tn), lambda j, i, te: (te[i], 0, j))
    return pl.pallas_call(
        _moe_swiglu_body,
        out_shape=jax.ShapeDtypeStruct((rows, f), BF16),
        grid_spec=pltpu.PrefetchScalarGridSpec(
            num_scalar_prefetch=1,
            grid=(f // tn, rows // EXPERT_TILE),
            in_specs=[pl.BlockSpec((EXPERT_TILE, d), lambda j, i, te: (i, 0)), w_spec, w_spec],
            out_specs=pl.BlockSpec((EXPERT_TILE, tn), lambda j, i, te: (i, j)),
            scratch_shapes=[pltpu.VMEM((d, tn), BF16), pltpu.VMEM((d, tn), BF16)],
        ),
        compiler_params=_cparams(("parallel", "arbitrary")),
        name="moe_swiglu",
    )(tile_expert, xs, wg, wu)


def _moe_down_body(te_ref, x_ref, w_ref, o_ref, w16_ref):
    @pl.when(_new_weight_block(te_ref))
    def _():
        w16_ref[...] = w_ref[...].astype(BF16)

    o_ref[...] = _dot(x_ref[...], w16_ref[...])


def _moe_down(tile_expert, act, wd, *, tn):
    rows, f = act.shape
    d = wd.shape[-1]
    return pl.pallas_call(
        _moe_down_body,
        out_shape=jax.ShapeDtypeStruct((rows, d), F32),
        grid_spec=pltpu.PrefetchScalarGridSpec(
            num_scalar_prefetch=1,
            grid=(d // tn, rows // EXPERT_TILE),
            in_specs=[pl.BlockSpec((EXPERT_TILE, f), lambda j, i, te: (i, 0)),
                      pl.BlockSpec((None, f, tn), lambda j, i, te: (te[i], 0, j))],
            out_specs=pl.BlockSpec((EXPERT_TILE, tn), lambda j, i, te: (i, j)),
            scratch_shapes=[pltpu.VMEM((f, tn), BF16)],
        ),
        compiler_params=_cparams(("parallel", "arbitrary")),
        name="moe_down",
    )(tile_expert, act, wd)


def _moe_combine_body(pos_ref, h_ref, gate_ref, fn_ref, eo_hbm, o_ref, buf, sem, *, tm):
    base = pl.program_id(0) * tm

    def issue(t, carry):
        for j in range(TOP_K):
            _row_copy(eo_hbm, buf.at[j], pos_ref[TOP_K * (base + t) + j], t, sem).start()
        return carry

    lax.fori_loop(0, tm, issue, 0)
    for j in range(TOP_K):
        pltpu.make_async_copy(eo_hbm.at[pl.ds(0, tm)], buf.at[j], sem).wait()
    y = h_ref[...]
    for j in range(TOP_K):
        y = y + gate_ref[:, j:j + 1] * buf[j]
    o_ref[...] = _rms(y) * fn_ref[...]


def _moe_combine(pos, h, gate, final_gain, eo, *, tm):
    m, d = h.shape
    return pl.pallas_call(
        functools.partial(_moe_combine_body, tm=tm),
        out_shape=jax.ShapeDtypeStruct((m, d), F32),
        grid_spec=pltpu.PrefetchScalarGridSpec(
            num_scalar_prefetch=1,
            grid=(m // tm,),
            in_specs=[pl.BlockSpec((tm, d), lambda i, pos: (i, 0)),
                      pl.BlockSpec((tm, LANES), lambda i, pos: (i, 0)),
                      pl.BlockSpec((1, d), lambda i, pos: (0, 0)),
                      pl.BlockSpec(memory_space=pl.ANY)],
            out_specs=pl.BlockSpec((tm, d), lambda i, pos: (i, 0)),
            scratch_shapes=[pltpu.VMEM((TOP_K, tm, d), F32), pltpu.SemaphoreType.DMA],
        ),
        compiler_params=_cparams(("arbitrary",)),
        name="moe_combine",
    )(pos, h, gate, final_gain, eo)


def _moe_plan(sel, n_experts):
    m = sel.shape[0]
    flat_e = sel[:, :TOP_K].reshape(-1)
    onehot = (flat_e[:, None] == jnp.arange(n_experts, dtype=jnp.int32)[None, :]).astype(jnp.int32)
    csum = jnp.cumsum(onehot, axis=0)
    rank = jnp.sum(csum * onehot, axis=1) - 1
    counts = csum[-1]
    padded = (counts + EXPERT_TILE - 1) // EXPERT_TILE * EXPERT_TILE
    ends = jnp.cumsum(padded)
    starts = ends - padded
    pos = (jnp.sum(onehot * starts[None, :], axis=1) + rank).astype(jnp.int32)
    n_rows = TOP_K * m + n_experts * EXPERT_TILE
    token = jnp.arange(TOP_K * m, dtype=jnp.int32) // TOP_K
    row_src = jnp.zeros((n_rows,), jnp.int32).at[pos].set(token)
    tile_start = jnp.arange(n_rows // EXPERT_TILE, dtype=jnp.int32) * EXPERT_TILE
    tile_expert = jnp.minimum(jnp.sum((tile_start[:, None] >= ends[None, :]).astype(jnp.int32), axis=1),
                              n_experts - 1).astype(jnp.int32)
    return pos, row_src, tile_expert, n_rows


def _largest_tile(n, cap, mult):
    best = None
    t = mult
    while t <= min(n, cap):
        if n % t == 0:
            best = t
        t += mult
    assert best is not None, (n, cap, mult)
    return best


def _pad_cols(w, n_to):
    return jnp.pad(w, ((0, 0), (0, n_to - w.shape[1])))


def _pad_rows(w, n_to):
    return jnp.pad(w, ((0, n_to - w.shape[0]), (0, 0)))


def _round_up(n, m):
    return -(-n // m) * m


def kernel(x, meta_tokens, a_norm, a_mix, a_w_rkv, a_w0, a_w1, a_w2, a_a0, a_a1, a_a2, a_g1, a_g2, a_k_k, a_k_a, a_r_k, a_lnx_w, a_lnx_b, a_w_out, kv_norm, w_kv, b_norm, b_w_q, b_sinks, b_w_out, f_norm, d_w_gate, d_w_up, d_w_down, e_router, e_w_gate, e_w_up, e_w_down, final_norm):
    assert x.shape[0] == 1
    seq, d = x.shape[1], x.shape[2]
    n_heads = d // HEAD
    n_experts = e_router.shape[-1]
    xs = x[0]

    l_real = N_META + seq
    l_pad = _round_up(l_real, CHUNK)
    h = jnp.concatenate([meta_tokens.astype(F32), xs, jnp.zeros((l_pad - l_real, d), F32)], axis=0)
    tm0 = _largest_tile(l_pad, 768, 16)
    tn = _largest_tile(d, 1024, LANES)

    xr, xw, xk, xv, xa, xg = _premix(h, a_norm[0:1], a_mix[0], _largest_tile(l_pad, 256, 16))
    w_rkv = a_w_rkv[0].astype(BF16)
    r = _mm(xr, w_rkv[0], out_dtype=F32, tm=tm0, tn=tn)
    k = _mm(xk, w_rkv[1], out_dtype=F32, tm=tm0, tn=tn)
    v = _mm(xv, w_rkv[2], out_dtype=F32, tm=tm0, tn=tn)
    tml = _largest_tile(l_pad, 384, 16)
    lw = _lora(xw, a_w1[0].astype(BF16), a_w2[0].astype(BF16), a_w0[0][None], mid="tanh", post="log_decay",
               out_dtype=F32, tm=tml)
    a = _lora(xa, a_a1[0].astype(BF16), a_a2[0].astype(BF16), a_a0[0][None], mid="none", post="sigmoid",
              out_dtype=F32, tm=tml)
    gate_rank = _round_up(a_g1.shape[-1], LANES)
    g = _lora(xg, _pad_cols(a_g1[0], gate_rank).astype(BF16), _pad_rows(a_g2[0], gate_rank).astype(BF16),
              jnp.zeros((1, d), F32), mid="sigmoid", post="none", out_dtype=BF16, tm=tml)
    prm = jnp.concatenate([a_k_k[0][None], a_k_a[0][None], a_r_k[0].reshape(1, d), a_lnx_w[0][None],
                           a_lnx_b[0][None], jnp.zeros((3, d), F32)], axis=0)
    pairs = 8 if n_heads % 16 == 0 else n_heads // 2
    mixed = _wkv(r, k, v, lw, a, g, prm, pairs=pairs)
    h = _mm(mixed, a_w_out[0].astype(BF16), out_dtype=F32, tm=tm0, tn=tn, res=h)

    (hn,) = _rmsnorm(h, f_norm[0:1], [BF16], _largest_tile(l_pad, 256, 16))
    d_ff = d_w_gate.shape[-1]
    ff_pad = _round_up(d_ff, 4 * LANES)
    act = _swiglu(hn, _pad_cols(d_w_gate[0], ff_pad).astype(BF16), _pad_cols(d_w_up[0], ff_pad).astype(BF16),
                  tm=tm0, tn=_largest_tile(ff_pad, 512, LANES))
    h = _mm_acc(act, _pad_rows(d_w_down[0], ff_pad).astype(BF16), h, tm=tm0, tn=tn,
                tk=_largest_tile(ff_pad, 3072, LANES))

    hkv, hq = _rmsnorm(h, jnp.stack([kv_norm, b_norm[0]]), [BF16, BF16], _largest_tile(l_pad, 256, 16))
    kv_w = ATT_KV_HEADS * HEAD
    kv = _mm(hkv, w_kv.astype(BF16), out_dtype=BF16, tm=tm0, tn=_largest_tile(2 * kv_w, 1024, LANES))

    def dup_heads(t):
        t = t.reshape(t.shape[0], ATT_KV_HEADS, 1, HEAD)
        return jnp.broadcast_to(t, (t.shape[0], ATT_KV_HEADS, 2, HEAD)).reshape(t.shape[0], ATT_KV_HEADS * LANES)

    k_all, v_all = dup_heads(kv[:, :kv_w]), dup_heads(kv[:, kv_w:])
    k_meta = _pad_rows(k_all[:N_META], BLOCK)
    v_meta = _pad_rows(v_all[:N_META], BLOCK)
    k_real = k_all[N_META:l_real]
    v_real = v_all[N_META:l_real]

    h = h[N_META:l_real]
    hq = hq[N_META:l_real]
    tm1 = _largest_tile(seq, 512, 16)
    q = _mm(hq, b_w_q[0].astype(BF16), out_dtype=BF16, tm=tm1, tn=tn, scale=HEAD ** -0.5)
    o = _attention(q, k_real, v_real, k_meta, v_meta, b_sinks[0].reshape(ATT_KV_HEADS, -1), n_heads=n_heads)
    h = _mm(o, b_w_out[0].astype(BF16), out_dtype=F32, tm=tm1, tn=tn, res=h)

    hn, sel, gate = _router(h, f_norm[1:2], _pad_cols(e_router[0], LANES), n_experts=n_experts,
                            tm=_largest_tile(seq, 256, 16))
    pos, row_src, tile_expert, n_rows = _moe_plan(sel, n_experts)
    xs_sorted = _gather_rows(row_src, hn, n_rows, rows=_largest_tile(n_rows, 512, EXPERT_TILE), out_dtype=BF16)
    d_exp = e_w_gate.shape[-1]
    act = _moe_swiglu(tile_expert, xs_sorted, e_w_gate[0], e_w_up[0], tn=_largest_tile(d_exp, 512, LANES))
    eo = _moe_down(tile_expert, act, e_w_down[0], tn=tn)
    out = _moe_combine(pos, h, gate, final_norm[None], eo, tm=_largest_tile(seq, 128, 8))
    return out[None]
```

```python
import functools

import jax
import jax.numpy as jnp
from jax import lax
from jax.experimental import pallas as pl
from jax.experimental.pallas import tpu as pltpu

F32 = jnp.float32
BF16 = jnp.bfloat16

LANES = 128
VMEM_LIMIT_BYTES = 56 * 1024 * 1024

N_META = 16
RMS_EPS = 1e-5
LN_X_EPS = 64e-5
HEAD = 64
ATT_KV_HEADS = 8
WINDOW = 128
BLOCK = 128
TOP_K = 2
CHUNK = 64
NEG = -1e30


def _cparams(sem):
    return pltpu.CompilerParams(dimension_semantics=sem, vmem_limit_bytes=VMEM_LIMIT_BYTES)


def _dot(a, b):
    return jnp.dot(a, b, preferred_element_type=F32)


def _dot_nt(a, b):
    return lax.dot_general(a, b, (((1,), (1,)), ((), ())), preferred_element_type=F32)


def _rms(x):
    return x * lax.rsqrt(jnp.mean(x * x, axis=-1, keepdims=True) + RMS_EPS)


def _run_interleaved(steps, shared=None):
    results = [None] * len(steps)
    pending = list(range(len(steps)))
    inbox = {idx: None for idx in pending}
    while pending:
        requests = {}
        for idx in list(pending):
            try:
                req = steps[idx].send(inbox[idx])
            except StopIteration as done:
                results[idx] = done.value
                pending.remove(idx)
                continue
            inbox[idx] = None
            if req is not None:
                requests.setdefault(req[0], []).append((idx, req[1]))
        for key, members in requests.items():
            out = shared[key](jnp.concatenate([rows for _, rows in members], axis=0))
            start = 0
            for idx, rows in members:
                inbox[idx] = out[start:start + rows.shape[0]]
                start += rows.shape[0]
    return results


def _rmsnorm_body(h_ref, g_ref, *o_refs):
    y = _rms(h_ref[...])
    for j, o_ref in enumerate(o_refs):
        o_ref[...] = (y * g_ref[j:j + 1, :]).astype(o_ref.dtype)


def _rmsnorm(h, gains, out_dtypes, tm):
    m, d = h.shape
    n_out = len(out_dtypes)
    outs = pl.pallas_call(
        _rmsnorm_body,
        out_shape=[jax.ShapeDtypeStruct((m, d), dt) for dt in out_dtypes],
        grid=(m // tm,),
        in_specs=[pl.BlockSpec((tm, d), lambda i: (i, 0)),
                  pl.BlockSpec((n_out, d), lambda i: (0, 0))],
        out_specs=[pl.BlockSpec((tm, d), lambda i: (i, 0)) for _ in out_dtypes],
        compiler_params=_cparams(("parallel",)),
        name="rmsnorm",
    )(h, gains)
    return outs


def _premix_body(h_ref, hp_ref, g_ref, mix_ref, *o_refs):
    i = pl.program_id(0)
    g = g_ref[...]
    xn = _rms(h_ref[...]) * g
    pn = _rms(hp_ref[...]) * g
    prev_row = jnp.where(i > 0, pn[7:8, :], 0.0)
    sh = pltpu.roll(xn, 1, axis=0)
    row = lax.broadcasted_iota(jnp.int32, xn.shape, 0)
    sh = jnp.where(row == 0, prev_row, sh)
    xx = sh - xn
    for j, o_ref in enumerate(o_refs):
        o_ref[...] = (xn + xx * mix_ref[j:j + 1, :]).astype(o_ref.dtype)


def _premix(h, gain, mix, tm):
    m, d = h.shape
    n_mix = mix.shape[0]
    rows8 = tm // 8
    return pl.pallas_call(
        _premix_body,
        out_shape=[jax.ShapeDtypeStruct((m, d), BF16) for _ in range(n_mix)],
        grid=(m // tm,),
        in_specs=[pl.BlockSpec((tm, d), lambda i: (i, 0)),
                  pl.BlockSpec((8, d), lambda i: (jnp.maximum(i * rows8 - 1, 0), 0)),
                  pl.BlockSpec((1, d), lambda i: (0, 0)),
                  pl.BlockSpec((n_mix, d), lambda i: (0, 0))],
        out_specs=[pl.BlockSpec((tm, d), lambda i: (i, 0)) for _ in range(n_mix)],
        compiler_params=_cparams(("parallel",)),
        name="premix",
    )(h, h, gain, mix)


def _mm_body(x_ref, w_ref, *rest, has_res, scale):
    if has_res:
        res_ref, o_ref, w16_ref = rest
    else:
        o_ref, w16_ref = rest

    @pl.when(pl.program_id(1) == 0)
    def _():
        w16_ref[...] = w_ref[...].astype(BF16)

    acc = _dot(x_ref[...], w16_ref[...])
    if scale is not None:
        acc = acc * scale
    if has_res:
        acc = acc + res_ref[...]
    o_ref[...] = acc.astype(o_ref.dtype)


def _mm(x, w, *, out_dtype, tm, tn, res=None, scale=None):
    m, k = x.shape
    n = w.shape[1]
    in_specs = [pl.BlockSpec((tm, k), lambda j, i: (i, 0)),
                pl.BlockSpec((k, tn), lambda j, i: (0, j))]
    args = [x, w]
    if res is not None:
        in_specs.append(pl.BlockSpec((tm, tn), lambda j, i: (i, j)))
        args.append(res)
    return pl.pallas_call(
        functools.partial(_mm_body, has_res=res is not None, scale=scale),
        out_shape=jax.ShapeDtypeStruct((m, n), out_dtype),
        grid=(n // tn, m // tm),
        in_specs=in_specs,
        out_specs=pl.BlockSpec((tm, tn), lambda j, i: (i, j)),
        scratch_shapes=[pltpu.VMEM((k, tn), BF16)],
        compiler_params=_cparams(("parallel", "arbitrary")),
        name="matmul",
    )(*args)


def _mm_acc_body(x_ref, w_ref, res_ref, o_ref, acc_ref):
    kk = pl.program_id(2)

    @pl.when(kk == 0)
    def _():
        acc_ref[...] = jnp.zeros_like(acc_ref)

    acc_ref[...] += _dot(x_ref[...], w_ref[...])

    @pl.when(kk == pl.num_programs(2) - 1)
    def _():
        o_ref[...] = res_ref[...] + acc_ref[...]


def _mm_acc(x, w, res, *, tm, tn, tk):
    m, k = x.shape
    n = w.shape[1]
    return pl.pallas_call(
        _mm_acc_body,
        out_shape=jax.ShapeDtypeStruct((m, n), F32),
        grid=(n // tn, m // tm, k // tk),
        in_specs=[pl.BlockSpec((tm, tk), lambda j, i, q: (i, q)),
                  pl.BlockSpec((tk, tn), lambda j, i, q: (q, j)),
                  pl.BlockSpec((tm, tn), lambda j, i, q: (i, j))],
        out_specs=pl.BlockSpec((tm, tn), lambda j, i, q: (i, j)),
        scratch_shapes=[pltpu.VMEM((tm, tn), F32)],
        compiler_params=_cparams(("parallel", "parallel", "arbitrary")),
        name="matmul_acc",
    )(x, w, res)


def _swiglu_body(x_ref, wg_ref, wu_ref, o_ref, wg16_ref, wu16_ref):
    @pl.when(pl.program_id(1) == 0)
    def _():
        wg16_ref[...] = wg_ref[...].astype(BF16)
        wu16_ref[...] = wu_ref[...].astype(BF16)

    x = x_ref[...]
    g = _dot(x, wg16_ref[...])
    u = _dot(x, wu16_ref[...])
    o_ref[...] = (g * jax.nn.sigmoid(g) * u).astype(o_ref.dtype)


def _swiglu(x, wg, wu, *, tm, tn):
    m, k = x.shape
    n = wg.shape[1]
    return pl.pallas_call(
        _swiglu_body,
        out_shape=jax.ShapeDtypeStruct((m, n), BF16),
        grid=(n // tn, m // tm),
        in_specs=[pl.BlockSpec((tm, k), lambda j, i: (i, 0)),
                  pl.BlockSpec((k, tn), lambda j, i: (0, j)),
                  pl.BlockSpec((k, tn), lambda j, i: (0, j))],
        out_specs=pl.BlockSpec((tm, tn), lambda j, i: (i, j)),
        scratch_shapes=[pltpu.VMEM((k, tn), BF16), pltpu.VMEM((k, tn), BF16)],
        compiler_params=_cparams(("parallel", "arbitrary")),
        name="swiglu",
    )(x, wg, wu)


def _lora_body(x_ref, w1_ref, w2_ref, b_ref, o_ref, *, mid, post):
    t = _dot(x_ref[...], w1_ref[...])
    if mid == "tanh":
        t = jnp.tanh(t)
    elif mid == "sigmoid":
        t = jax.nn.sigmoid(t)
    z = _dot(t.astype(BF16), w2_ref[...]) + b_ref[...]
    if post == "log_decay":
        z = jax.nn.sigmoid(z) * (-0.6065306597126334)
    elif post == "sigmoid":
        z = jax.nn.sigmoid(z)
    o_ref[...] = z.astype(o_ref.dtype)


def _lora(x, w1, w2, bias, *, mid, post, out_dtype, tm):
    m, k = x.shape
    r = w1.shape[1]
    n = w2.shape[1]
    return pl.pallas_call(
        functools.partial(_lora_body, mid=mid, post=post),
        out_shape=jax.ShapeDtypeStruct((m, n), out_dtype),
        grid=(m // tm,),
        in_specs=[pl.BlockSpec((tm, k), lambda i: (i, 0)),
                  pl.BlockSpec((k, r), lambda i: (0, 0)),
                  pl.BlockSpec((r, n), lambda i: (0, 0)),
                  pl.BlockSpec((1, n), lambda i: (0, 0))],
        out_specs=pl.BlockSpec((tm, n), lambda i: (i, 0)),
        compiler_params=_cparams(("parallel",)),
        name="lora",
    )(x, w1, w2, bias)


def _split2(x):
    hi = x.astype(BF16)
    lo = (x - hi.astype(F32)).astype(BF16)
    return hi, lo


def _wkv_body(r_ref, k_ref, v_ref, lw_ref, a_ref, g_ref, prm_ref, o_ref, st_ref, *, pairs):
    c_idx = pl.program_id(1)

    @pl.when(c_idx == 0)
    def _():
        st_ref[...] = jnp.zeros_like(st_ref)

    C = CHUNK
    lane = lax.broadcasted_iota(jnp.int32, (C, LANES), 1)
    row = lax.broadcasted_iota(jnp.int32, (C, LANES), 0)
    col = lane % HEAD
    upper_half = lane >= HEAD
    tri_strict = col < row
    tri_incl = col <= row
    eye_pair = (col == row).astype(F32)

    def level_mask(bs):
        return ((row // bs) == (col // bs)) & ((row // (bs // 2)) != (col // (bs // 2)))

    row2 = lax.broadcasted_iota(jnp.int32, (2 * C, LANES), 0)
    lane2 = lax.broadcasted_iota(jnp.int32, (2 * C, LANES), 1)
    bd_mask = (row2 >= C) == (lane2 >= HEAD)
    ones_bd = bd_mask.astype(BF16)
    diag_mask = row2 == lane2

    def bd(x):
        return jnp.concatenate([jnp.where(upper_half, 0.0, x), jnp.where(upper_half, x, 0.0)], axis=0)

    def headsum(x16):
        return _dot(x16, ones_bd)

    zeros_cl = jnp.zeros((C, LANES), BF16)

    def pair_step(r, k, v, lw, a, g, prm, s):
        k_k, k_a, r_k, lnx_w, lnx_b = (prm[j:j + 1] for j in range(5))

        kk = k * k_k
        k2 = k * (1.0 + (a - 1.0) * k_a)
        sums = yield ("headsum", jnp.concatenate([kk * kk, r * k2 * r_k], axis=0).astype(BF16))
        cum = lw
        shift = 1
        while shift < C:
            cum = cum + jnp.where(row >= shift, pltpu.roll(cum, shift, axis=0), 0.0)
            shift *= 2
        kkn = kk * lax.rsqrt(jnp.maximum(sums[:C], 1e-24))
        bonus = sums[C:] * v
        avec = -kkn
        bvec = kkn * a
        cum_end = cum[C - 1:C, :]
        p_incl = jnp.exp(cum)
        p_inv = jnp.exp(-cum)
        a_t = avec * jnp.exp(cum - lw)
        r_t = r * p_incl
        b_t = bvec * p_inv
        k_t = k2 * p_inv
        p_end = jnp.exp(cum_end - cum)
        b_h = bvec * p_end
        k_h = k2 * p_end

        v16 = v.astype(BF16)
        gram = _dot_nt(jnp.concatenate([a_t, r_t], axis=0).astype(BF16),
                       jnp.concatenate([bd(b_t), bd(k_t)], axis=0).astype(BF16))
        yield
        a_ab = jnp.where(tri_strict, gram[:C, :LANES], 0.0)
        a_ak = jnp.where(tri_strict, gram[:C, LANES:], 0.0)
        a_rb = jnp.where(tri_incl, gram[C:, :LANES], 0.0)
        a_rk = jnp.where(tri_incl, gram[C:, LANES:], 0.0)
        akv = _dot(a_ak.astype(BF16), bd(v16))

        t_inv = eye_pair + jnp.where(level_mask(2), a_ab, 0.0)
        bs = 4
        while bs <= C:
            x = jnp.where(level_mask(bs), a_ab, 0.0)
            t16 = t_inv.astype(BF16)
            tx = _dot(t16, bd(x).astype(BF16))
            yield
            t_inv = t_inv + _dot(tx.astype(BF16), bd(t16))
            yield
            bs *= 2
        t16 = t_inv.astype(BF16)

        tt = _dot(t16, jnp.concatenate([bd(a_t.astype(BF16)), bd(akv.astype(BF16))], axis=1))
        yield
        a_hat = tt[:, :LANES].astype(BF16)
        u0 = tt[:, LANES:].astype(BF16)

        a_rb16 = a_rb.astype(BF16)
        qa = _dot(a_rb16, bd(a_hat))
        y0 = _dot(jnp.concatenate([a_rb16, a_rk.astype(BF16)], axis=1),
                  jnp.concatenate([bd(u0), bd(v16)], axis=0))
        lhs_t = jnp.concatenate([b_h, k_h], axis=0).T.astype(BF16)
        rhs = jnp.concatenate([jnp.concatenate([a_hat, u0], axis=1),
                               jnp.concatenate([zeros_cl, v16], axis=1)], axis=0)
        upd = _dot(lhs_t, rhs)
        yield
        q_hat = r_t + qa
        m_off = jnp.where(bd_mask, upd[:, :LANES], 0.0)
        n_new = jnp.where(bd_mask, upd[:, LANES:], 0.0)
        p_col = jnp.sum(jnp.where(diag_mask, jnp.exp(cum_end), 0.0), axis=1, keepdims=True)

        s_hi, s_lo = _split2(s)
        m16 = m_off.astype(BF16)
        ys = _dot(jnp.concatenate([q_hat.astype(BF16), m16], axis=0), s_hi)
        y = ys[:C] + y0
        s_new = p_col * s + ys[C:] + _dot(m16, s_lo) + n_new

        mu = (yield ("headsum", y.astype(BF16))) * (1.0 / HEAD)
        d = y - mu
        var = (yield ("headsum", (d * d).astype(BF16))) * (1.0 / HEAD)
        yn = d * lax.rsqrt(var + LN_X_EPS) * lnx_w + lnx_b
        return ((yn + bonus) * g.astype(F32)).astype(o_ref.dtype), s_new

    lanes = [slice(p * LANES, (p + 1) * LANES) for p in range(pairs)]
    results = _run_interleaved([pair_step(r_ref[:, sl], k_ref[:, sl], v_ref[:, sl], lw_ref[:, sl], a_ref[:, sl],
                                          g_ref[:, sl], prm_ref[:, sl], st_ref[p]) for p, sl in enumerate(lanes)],
                               shared={"headsum": headsum})
    for p, sl in enumerate(lanes):
        o_ref[:, sl] = results[p][0]
        st_ref[p] = results[p][1]


def _wkv(r, k, v, lw, a, g, prm, *, pairs):
    length, d = r.shape
    width = pairs * LANES
    blk = lambda hb, c: (c, hb)
    return pl.pallas_call(
        functools.partial(_wkv_body, pairs=pairs),
        out_shape=jax.ShapeDtypeStruct((length, d), BF16),
        grid=(d // width, length // CHUNK),
        in_specs=[pl.BlockSpec((CHUNK, width), blk)] * 6 + [pl.BlockSpec((8, width), lambda hb, c: (0, hb))],
        out_specs=pl.BlockSpec((CHUNK, width), blk),
        scratch_shapes=[pltpu.VMEM((pairs, LANES, LANES), F32)],
        compiler_params=_cparams(("parallel", "arbitrary")),
        name="wkv7",
    )(r, k, v, lw, a, g, prm)


def _attn_body(q_ref, kp_ref, kc_ref, km_ref, vp_ref, vc_ref, vm_ref, sk_ref, o_ref, *, group, n_heads):
    kvh = pl.program_id(0)
    n = pl.program_id(1)
    rows = 2 * BLOCK
    rowi = lax.broadcasted_iota(jnp.int32, (rows, BLOCK), 0)
    kj = lax.broadcasted_iota(jnp.int32, (rows, BLOCK), 1)
    qi = rowi % BLOCK
    second = rowi >= BLOCK
    use_prev = kj > qi
    dist_band = (qi - kj + jnp.where(use_prev, BLOCK, 0)).astype(F32)
    valid_band = jnp.logical_not(use_prev & (n == 0))
    valid_meta = kj < N_META
    dist_meta = (N_META + n * BLOCK + qi - kj).astype(F32)
    lane_o = lax.broadcasted_iota(jnp.int32, (BLOCK, LANES), 1)
    lane_q = lax.broadcasted_iota(jnp.int32, (rows, LANES), 1)
    row_q = lax.broadcasted_iota(jnp.int32, (rows, LANES), 0)
    q_keep = (row_q >= BLOCK) == (lane_q >= HEAD)

    kp = kp_ref[...]
    kc = kc_ref[...]
    km = km_ref[...]
    vp = vp_ref[...]
    vc = vc_ref[...]
    vm = vm_ref[...]

    def pair_step(pr, q):
        q2 = jnp.where(q_keep, jnp.concatenate([q, q], axis=0), jnp.zeros((), q.dtype))
        qk_prev = _dot_nt(q2, kp)
        qk_cur = _dot_nt(q2, kc)
        qk_meta = _dot_nt(q2, km)
        yield
        head1 = (kvh * group + 2 * pr + 1).astype(F32)
        slope = jnp.exp2((jnp.where(second, 1.0, 0.0) + head1) * (-8.0 / n_heads))
        sink = jnp.where(second[:, :1], sk_ref[kvh, 2 * pr + 1], sk_ref[kvh, 2 * pr])
        s_band = jnp.where(valid_band, jnp.where(use_prev, qk_prev, qk_cur) - slope * dist_band, NEG)
        s_meta = jnp.where(valid_meta, qk_meta - slope * dist_meta, NEG)
        mx = jnp.maximum(jnp.max(jnp.maximum(s_band, s_meta), axis=1, keepdims=True), sink)
        e_band = jnp.exp(s_band - mx)
        e_meta = jnp.exp(s_meta - mx)
        den = jnp.sum(e_band + e_meta, axis=1, keepdims=True) + jnp.exp(sink - mx)
        acc = (_dot(jnp.where(use_prev, e_band, 0.0).astype(BF16), vp)
               + _dot(jnp.where(use_prev, 0.0, e_band).astype(BF16), vc)
               + _dot(e_meta.astype(BF16), vm))
        yield
        acc = acc / den
        return jnp.where(lane_o >= HEAD, acc[BLOCK:], acc[:BLOCK]).astype(o_ref.dtype)

    lanes = [slice(pr * LANES, (pr + 1) * LANES) for pr in range(group // 2)]
    results = _run_interleaved([pair_step(pr, q_ref[:, sl]) for pr, sl in enumerate(lanes)])
    for pr, sl in enumerate(lanes):
        o_ref[:, sl] = results[pr]


def _attention(q, k_dup, v_dup, km_dup, vm_dup, sinks, *, n_heads):
    s_len, d = q.shape
    group = n_heads // ATT_KV_HEADS
    gw = group * HEAD
    nb = s_len // BLOCK
    prev = lambda h, n: (jnp.maximum(n - 1, 0), h)
    cur = lambda h, n: (n, h)
    meta = lambda h, n: (0, h)
    return pl.pallas_call(
        functools.partial(_attn_body, group=group, n_heads=n_heads),
        out_shape=jax.ShapeDtypeStruct((s_len, d), BF16),
        grid=(ATT_KV_HEADS, nb),
        in_specs=[pl.BlockSpec((BLOCK, gw), cur),
                  pl.BlockSpec((BLOCK, LANES), prev), pl.BlockSpec((BLOCK, LANES), cur), pl.BlockSpec((BLOCK, LANES), meta),
                  pl.BlockSpec((BLOCK, LANES), prev), pl.BlockSpec((BLOCK, LANES), cur), pl.BlockSpec((BLOCK, LANES), meta),
                  pl.BlockSpec(memory_space=pltpu.SMEM)],
        out_specs=pl.BlockSpec((BLOCK, gw), cur),
        compiler_params=_cparams(("parallel", "parallel")),
        name="swa_attention",
    )(q, k_dup, k_dup, km_dup, v_dup, v_dup, vm_dup, sinks)


def _router_body(h_ref, g_ref, wr_ref, hn_ref, sel_ref, gate_ref, *, n_experts):
    y = _rms(h_ref[...]) * g_ref[...]
    hn_ref[...] = y
    logits = jnp.dot(y, wr_ref[...], preferred_element_type=F32, precision=lax.Precision.HIGHEST)
    lane = lax.broadcasted_iota(jnp.int32, logits.shape, 1)
    logits = jnp.where(lane < n_experts, logits, NEG)
    m1 = jnp.max(logits, axis=1, keepdims=True)
    i1 = jnp.min(jnp.where(logits == m1, lane, LANES), axis=1, keepdims=True)
    rest = jnp.where(lane == i1, NEG, logits)
    m2 = jnp.max(rest, axis=1, keepdims=True)
    i2 = jnp.min(jnp.where(rest == m2, lane, LANES), axis=1, keepdims=True)
    e2 = jnp.exp(m2 - m1)
    g1 = 1.0 / (1.0 + e2)
    g2 = e2 / (1.0 + e2)
    sel_ref[...] = jnp.where(lane == 0, i1, jnp.where(lane == 1, i2, 0))
    gate_ref[...] = jnp.where(lane == 0, g1, jnp.where(lane == 1, g2, 0.0))


def _router(h, gain, w_router_pad, *, n_experts, tm):
    m, d = h.shape
    return pl.pallas_call(
        functools.partial(_router_body, n_experts=n_experts),
        out_shape=[jax.ShapeDtypeStruct((m, d), F32), jax.ShapeDtypeStruct((m, LANES), jnp.int32),
                   jax.ShapeDtypeStruct((m, LANES), F32)],
        grid=(m // tm,),
        in_specs=[pl.BlockSpec((tm, d), lambda i: (i, 0)),
                  pl.BlockSpec((1, d), lambda i: (0, 0)),
                  pl.BlockSpec((d, LANES), lambda i: (0, 0))],
        out_specs=[pl.BlockSpec((tm, d), lambda i: (i, 0)), pl.BlockSpec((tm, LANES), lambda i: (i, 0)),
                   pl.BlockSpec((tm, LANES), lambda i: (i, 0))],
        compiler_params=_cparams(("parallel",)),
        name="moe_router",
    )(h, gain, w_router_pad)


EXPERT_TILE = 256


def _row_copy(src_hbm, dst, src_row, dst_row, sem):
    return pltpu.make_async_copy(src_hbm.at[pl.ds(src_row, 1)], dst.at[pl.ds(dst_row, 1)], sem)


def _gather_rows_body(src_ref, x_hbm, o_ref, buf, sem, *, rows):
    base = pl.program_id(0) * rows

    def issue(r, carry):
        _row_copy(x_hbm, buf, src_ref[base + r], r, sem).start()
        return carry

    lax.fori_loop(0, rows, issue, 0)
    pltpu.make_async_copy(x_hbm.at[pl.ds(0, rows)], buf, sem).wait()
    o_ref[...] = buf[...].astype(o_ref.dtype)


def _gather_rows(row_src, x, n_rows, *, rows, out_dtype):
    d = x.shape[1]
    return pl.pallas_call(
        functools.partial(_gather_rows_body, rows=rows),
        out_shape=jax.ShapeDtypeStruct((n_rows, d), out_dtype),
        grid_spec=pltpu.PrefetchScalarGridSpec(
            num_scalar_prefetch=1,
            grid=(n_rows // rows,),
            in_specs=[pl.BlockSpec(memory_space=pl.ANY)],
            out_specs=pl.BlockSpec((rows, d), lambda i, src: (i, 0)),
            scratch_shapes=[pltpu.VMEM((rows, d), x.dtype), pltpu.SemaphoreType.DMA],
        ),
        compiler_params=_cparams(("arbitrary",)),
        name="moe_gather",
    )(row_src, x)


def _new_weight_block(te_ref):
    i = pl.program_id(1)
    return (i == 0) | (te_ref[i] != te_ref[jnp.maximum(i - 1, 0)])


def _moe_swiglu_body(te_ref, x_ref, wg_ref, wu_ref, o_ref, wg16_ref, wu16_ref):
    @pl.when(_new_weight_block(te_ref))
    def _():
        wg16_ref[...] = wg_ref[...].astype(BF16)
        wu16_ref[...] = wu_ref[...].astype(BF16)

    x = x_ref[...]
    g = _dot(x, wg16_ref[...])
    u = _dot(x, wu16_ref[...])
    o_ref[...] = (g * jax.nn.sigmoid(g) * u).astype(o_ref.dtype)


def _moe_swiglu(tile_expert, xs, wg, wu, *, tn):
    rows, d = xs.shape
    f = wg.shape[-1]
    w_spec = pl.BlockSpec((None, d, tn), lambda j, i, te: (te[i], 0, j))
    return pl.pallas_call(
        _moe_swiglu_body,
        out_shape=jax.ShapeDtypeStruct((rows, f), BF16),
        grid_spec=pltpu.PrefetchScalarGridSpec(
            num_scalar_prefetch=1,
            grid=(f // tn, rows // EXPERT_TILE),
            in_specs=[pl.BlockSpec((EXPERT_TILE, d), lambda j, i, te: (i, 0)), w_spec, w_spec],
            out_specs=pl.BlockSpec((EXPERT_TILE, tn), lambda j, i, te: (i, j)),
            scratch_shapes=[pltpu.VMEM((d, tn), BF16), pltpu.VMEM((d, tn), BF16)],
        ),
        compiler_params=_cparams(("parallel", "arbitrary")),
        name="moe_swiglu",
    )(tile_expert, xs, wg, wu)


def _moe_down_body(te_ref, x_ref, w_ref, o_ref, w16_ref):
    @pl.when(_new_weight_block(te_ref))
    def _():
        w16_ref[...] = w_ref[...].astype(BF16)

    o_ref[...] = _dot(x_ref[...], w16_ref[...])


def _moe_down(tile_expert, act, wd, *, tn):
    rows, f = act.shape
    d = wd.shape[-1]
    return pl.pallas_call(
        _moe_down_body,
        out_shape=jax.ShapeDtypeStruct((rows, d), F32),
        grid_spec=pltpu.PrefetchScalarGridSpec(
            num_scalar_prefetch=1,
            grid=(d // tn, rows // EXPERT_TILE),
            in_specs=[pl.BlockSpec((EXPERT_TILE, f), lambda j, i, te: (i, 0)),
                      pl.BlockSpec((None, f, tn), lambda j, i, te: (te[i], 0, j))],
            out_specs=pl.BlockSpec((EXPERT_TILE, tn), lambda j, i, te: (i, j)),
            scratch_shapes=[pltpu.VMEM((f, tn), BF16)],
        ),
        compiler_params=_cparams(("parallel", "arbitrary")),
        name="moe_down",
    )(tile_expert, act, wd)


def _moe_combine_body(pos_ref, h_ref, gate_ref, fn_ref, eo_hbm, o_ref, buf, sem, *, tm):
    base = pl.program_id(0) * tm

    def issue(t, carry):
        for j in range(TOP_K):
            _row_copy(eo_hbm, buf.at[j], pos_ref[TOP_K * (base + t) + j], t, sem).start()
        return carry

    lax.fori_loop(0, tm, issue, 0)
    for j in range(TOP_K):
        pltpu.make_async_copy(eo_hbm.at[pl.ds(0, tm)], buf.at[j], sem).wait()
    y = h_ref[...]
    for j in range(TOP_K):
        y = y + gate_ref[:, j:j + 1] * buf[j]
    o_ref[...] = _rms(y) * fn_ref[...]


def _moe_combine(pos, h, gate, final_gain, eo, *, tm):
    m, d = h.shape
    return pl.pallas_call(
        functools.partial(_moe_combine_body, tm=tm),
        out_shape=jax.ShapeDtypeStruct((m, d), F32),
        grid_spec=pltpu.PrefetchScalarGridSpec(
            num_scalar_prefetch=1,
            grid=(m // tm,),
            in_specs=[pl.BlockSpec((tm, d), lambda i, pos: (i, 0)),
                      pl.BlockSpec((tm, LANES), lambda i, pos: (i, 0)),
                      pl.BlockSpec((1, d), lambda i, pos: (0, 0)),
                      pl.BlockSpec(memory_space=pl.ANY)],
            out_specs=pl.BlockSpec((tm, d), lambda i, pos: (i, 0)),
            scratch_shapes=[pltpu.VMEM((TOP_K, tm, d), F32), pltpu.SemaphoreType.DMA],
        ),
        compiler_params=_cparams(("arbitrary",)),
        name="moe_combine",
    )(pos, h, gate, final_gain, eo)


def _moe_plan(sel, n_experts):
    m = sel.shape[0]
    flat_e = sel[:, :TOP_K].reshape(-1)
    onehot = (flat_e[:, None] == jnp.arange(n_experts, dtype=jnp.int32)[None, :]).astype(jnp.int32)
    csum = jnp.cumsum(onehot, axis=0)
    rank = jnp.sum(csum * onehot, axis=1) - 1
    counts = csum[-1]
    padded = (counts + EXPERT_TILE - 1) // EXPERT_TILE * EXPERT_TILE
    ends = jnp.cumsum(padded)
    starts = ends - padded
    pos = (jnp.sum(onehot * starts[None, :], axis=1) + rank).astype(jnp.int32)
    n_rows = TOP_K * m + n_experts * EXPERT_TILE
    token = jnp.arange(TOP_K * m, dtype=jnp.int32) // TOP_K
    row_src = jnp.zeros((n_rows,), jnp.int32).at[pos].set(token)
    tile_start = jnp.arange(n_rows // EXPERT_TILE, dtype=jnp.int32) * EXPERT_TILE
    tile_expert = jnp.minimum(jnp.sum((tile_start[:, None] >= ends[None, :]).astype(jnp.int32), axis=1),
                              n_experts - 1).astype(jnp.int32)
    return pos, row_src, tile_expert, n_rows


def _largest_tile(n, cap, mult):
    best = None
    t = mult
    while t <= min(n, cap):
        if n % t == 0:
            best = t
        t += mult
    assert best is not None, (n, cap, mult)
    return best


def _pad_cols(w, n_to):
    return jnp.pad(w, ((0, 0), (0, n_to - w.shape[1])))


def _pad_rows(w, n_to):
    return jnp.pad(w, ((0, n_to - w.shape[0]), (0, 0)))


def _round_up(n, m):
    return -(-n // m) * m


def kernel(x, meta_tokens, a_norm, a_mix, a_w_rkv, a_w0, a_w1, a_w2, a_a0, a_a1, a_a2, a_g1, a_g2, a_k_k, a_k_a, a_r_k, a_lnx_w, a_lnx_b, a_w_out, kv_norm, w_kv, b_norm, b_w_q, b_sinks, b_w_out, f_norm, d_w_gate, d_w_up, d_w_down, e_router, e_w_gate, e_w_up, e_w_down, final_norm):
    assert x.shape[0] == 1
    seq, d = x.shape[1], x.shape[2]
    n_heads = d // HEAD
    n_experts = e_router.shape[-1]
    xs = x[0]

    l_real = N_META + seq
    l_pad = _round_up(l_real, CHUNK)
    h = jnp.concatenate([meta_tokens.astype(F32), xs, jnp.zeros((l_pad - l_real, d), F32)], axis=0)
    tm0 = _largest_tile(l_pad, 768, 16)
    tn = _largest_tile(d, 512, LANES)

    xr, xw, xk, xv, xa, xg = _premix(h, a_norm[0:1], a_mix[0], _largest_tile(l_pad, 256, 16))
    r = _mm(xr, a_w_rkv[0, 0], out_dtype=F32, tm=tm0, tn=tn)
    k = _mm(xk, a_w_rkv[0, 1], out_dtype=F32, tm=tm0, tn=tn)
    v = _mm(xv, a_w_rkv[0, 2], out_dtype=F32, tm=tm0, tn=tn)
    tml = _largest_tile(l_pad, 384, 16)
    lw = _lora(xw, a_w1[0].astype(BF16), a_w2[0].astype(BF16), a_w0[0][None], mid="tanh", post="log_decay",
               out_dtype=F32, tm=tml)
    a = _lora(xa, a_a1[0].astype(BF16), a_a2[0].astype(BF16), a_a0[0][None], mid="none", post="sigmoid",
              out_dtype=F32, tm=tml)
    gate_rank = _round_up(a_g1.shape[-1], LANES)
    g = _lora(xg, _pad_cols(a_g1[0], gate_rank).astype(BF16), _pad_rows(a_g2[0], gate_rank).astype(BF16),
              jnp.zeros((1, d), F32), mid="sigmoid", post="none", out_dtype=BF16, tm=tml)
    prm = jnp.concatenate([a_k_k[0][None], a_k_a[0][None], a_r_k[0].reshape(1, d), a_lnx_w[0][None],
                           a_lnx_b[0][None], jnp.zeros((3, d), F32)], axis=0)
    pairs = 16 if n_heads % 32 == 0 else n_heads // 2
    mixed = _wkv(r, k, v, lw, a, g, prm, pairs=pairs)
    h = _mm(mixed, a_w_out[0], out_dtype=F32, tm=tm0, tn=tn, res=h)

    (hn,) = _rmsnorm(h, f_norm[0:1], [BF16], _largest_tile(l_pad, 256, 16))
    d_ff = d_w_gate.shape[-1]
    act = _swiglu(hn, d_w_gate[0], d_w_up[0], tm=tm0, tn=_largest_tile(d_ff, 256, LANES))
    h = _mm_acc(act, d_w_down[0].astype(BF16), h, tm=tm0, tn=tn, tk=_largest_tile(d_ff, 6144, LANES))

    hkv, hq = _rmsnorm(h, jnp.stack([kv_norm, b_norm[0]]), [BF16, BF16], _largest_tile(l_pad, 256, 16))
    kv_w = ATT_KV_HEADS * HEAD
    kv = _mm(hkv, w_kv, out_dtype=BF16, tm=tm0, tn=_largest_tile(2 * kv_w, 512, LANES))

    def dup_heads(t):
        t = t.reshape(t.shape[0], ATT_KV_HEADS, 1, HEAD)
        return jnp.broadcast_to(t, (t.shape[0], ATT_KV_HEADS, 2, HEAD)).reshape(t.shape[0], ATT_KV_HEADS * LANES)

    k_all, v_all = dup_heads(kv[:, :kv_w]), dup_heads(kv[:, kv_w:])
    k_meta = _pad_rows(k_all[:N_META], BLOCK)
    v_meta = _pad_rows(v_all[:N_META], BLOCK)
    k_real = k_all[N_META:l_real]
    v_real = v_all[N_META:l_real]

    h = h[N_META:l_real]
    hq = hq[N_META:l_real]
    tm1 = _largest_tile(seq, 512, 16)
    q = _mm(hq, b_w_q[0], out_dtype=BF16, tm=tm1, tn=tn, scale=HEAD ** -0.5)
    o = _attention(q, k_real, v_real, k_meta, v_meta, b_sinks[0].reshape(ATT_KV_HEADS, -1), n_heads=n_heads)
    h = _mm(o, b_w_out[0], out_dtype=F32, tm=tm1, tn=tn, res=h)

    hn, sel, gate = _router(h, f_norm[1:2], _pad_cols(e_router[0], LANES), n_experts=n_experts,
                            tm=_largest_tile(seq, 256, 16))
    pos, row_src, tile_expert, n_rows = _moe_plan(sel, n_experts)
    xs_sorted = _gather_rows(row_src, hn, n_rows, rows=_largest_tile(n_rows, 512, EXPERT_TILE), out_dtype=BF16)
    d_exp = e_w_gate.shape[-1]
    act = _moe_swiglu(tile_expert, xs_sorted, e_w_gate[0], e_w_up[0], tn=_largest_tile(d_exp, 512, LANES))
    eo = _moe_down(tile_expert, act, e_w_down[0], tn=_largest_tile(d, 1024, LANES))
    out = _moe_combine(pos, h, gate, final_norm[None], eo, tm=_largest_tile(seq, 128, 8))
    return out[None]
```

```python
import functools

import jax
import jax.numpy as jnp
from jax import lax
from jax.experimental import pallas as pl
from jax.experimental.pallas import tpu as pltpu

F32 = jnp.float32
BF16 = jnp.bfloat16

LANES = 128
VMEM_LIMIT_BYTES = 56 * 1024 * 1024

N_META = 16
RMS_EPS = 1e-5
LN_X_EPS = 64e-5
HEAD = 64
ATT_KV_HEADS = 8
WINDOW = 128
BLOCK = 128
TOP_K = 2
CHUNK = 64
NEG = -1e30


def _cparams(sem):
    return pltpu.CompilerParams(dimension_semantics=sem, vmem_limit_bytes=VMEM_LIMIT_BYTES)


def _dot(a, b):
    return jnp.dot(a, b, preferred_element_type=F32)


def _dot_nt(a, b):
    return lax.dot_general(a, b, (((1,), (1,)), ((), ())), preferred_element_type=F32)


def _rms(x):
    return x * lax.rsqrt(jnp.mean(x * x, axis=-1, keepdims=True) + RMS_EPS)


def _run_interleaved(steps, shared=None):
    results = [None] * len(steps)
    pending = list(range(len(steps)))
    inbox = {idx: None for idx in pending}
    while pending:
        requests = {}
        for idx in list(pending):
            try:
                req = steps[idx].send(inbox[idx])
            except StopIteration as done:
                results[idx] = done.value
                pending.remove(idx)
                continue
            inbox[idx] = None
            if req is not None:
                requests.setdefault(req[0], []).append((idx, req[1]))
        for key, members in requests.items():
            out = shared[key](jnp.concatenate([rows for _, rows in members], axis=0))
            start = 0
            for idx, rows in members:
                inbox[idx] = out[start:start + rows.shape[0]]
                start += rows.shape[0]
    return results


def _rmsnorm_body(h_ref, g_ref, *o_refs):
    y = _rms(h_ref[...])
    for j, o_ref in enumerate(o_refs):
        o_ref[...] = (y * g_ref[j:j + 1, :]).astype(o_ref.dtype)


def _rmsnorm(h, gains, out_dtypes, tm):
    m, d = h.shape
    n_out = len(out_dtypes)
    outs = pl.pallas_call(
        _rmsnorm_body,
        out_shape=[jax.ShapeDtypeStruct((m, d), dt) for dt in out_dtypes],
        grid=(m // tm,),
        in_specs=[pl.BlockSpec((tm, d), lambda i: (i, 0)),
                  pl.BlockSpec((n_out, d), lambda i: (0, 0))],
        out_specs=[pl.BlockSpec((tm, d), lambda i: (i, 0)) for _ in out_dtypes],
        compiler_params=_cparams(("parallel",)),
        name="rmsnorm",
    )(h, gains)
    return outs


def _premix_body(h_ref, hp_ref, g_ref, mix_ref, *o_refs):
    i = pl.program_id(0)
    g = g_ref[...]
    xn = _rms(h_ref[...]) * g
    pn = _rms(hp_ref[...]) * g
    prev_row = jnp.where(i > 0, pn[7:8, :], 0.0)
    sh = pltpu.roll(xn, 1, axis=0)
    row = lax.broadcasted_iota(jnp.int32, xn.shape, 0)
    sh = jnp.where(row == 0, prev_row, sh)
    xx = sh - xn
    for j, o_ref in enumerate(o_refs):
        o_ref[...] = (xn + xx * mix_ref[j:j + 1, :]).astype(o_ref.dtype)


def _premix(h, gain, mix, tm):
    m, d = h.shape
    n_mix = mix.shape[0]
    rows8 = tm // 8
    return pl.pallas_call(
        _premix_body,
        out_shape=[jax.ShapeDtypeStruct((m, d), BF16) for _ in range(n_mix)],
        grid=(m // tm,),
        in_specs=[pl.BlockSpec((tm, d), lambda i: (i, 0)),
                  pl.BlockSpec((8, d), lambda i: (jnp.maximum(i * rows8 - 1, 0), 0)),
                  pl.BlockSpec((1, d), lambda i: (0, 0)),
                  pl.BlockSpec((n_mix, d), lambda i: (0, 0))],
        out_specs=[pl.BlockSpec((tm, d), lambda i: (i, 0)) for _ in range(n_mix)],
        compiler_params=_cparams(("parallel",)),
        name="premix",
    )(h, h, gain, mix)


def _mm_body(x_ref, w_ref, *rest, has_res, scale):
    if has_res:
        res_ref, o_ref, w16_ref = rest
    else:
        o_ref, w16_ref = rest

    @pl.when(pl.program_id(1) == 0)
    def _():
        w16_ref[...] = w_ref[...].astype(BF16)

    acc = _dot(x_ref[...], w16_ref[...])
    if scale is not None:
        acc = acc * scale
    if has_res:
        acc = acc + res_ref[...]
    o_ref[...] = acc.astype(o_ref.dtype)


def _mm(x, w, *, out_dtype, tm, tn, res=None, scale=None):
    m, k = x.shape
    n = w.shape[1]
    in_specs = [pl.BlockSpec((tm, k), lambda j, i: (i, 0)),
                pl.BlockSpec((k, tn), lambda j, i: (0, j))]
    args = [x, w]
    if res is not None:
        in_specs.append(pl.BlockSpec((tm, tn), lambda j, i: (i, j)))
        args.append(res)
    return pl.pallas_call(
        functools.partial(_mm_body, has_res=res is not None, scale=scale),
        out_shape=jax.ShapeDtypeStruct((m, n), out_dtype),
        grid=(n // tn, m // tm),
        in_specs=in_specs,
        out_specs=pl.BlockSpec((tm, tn), lambda j, i: (i, j)),
        scratch_shapes=[pltpu.VMEM((k, tn), BF16)],
        compiler_params=_cparams(("parallel", "arbitrary")),
        name="matmul",
    )(*args)


def _mm_acc_body(x_ref, w_ref, res_ref, o_ref, acc_ref):
    kk = pl.program_id(2)

    @pl.when(kk == 0)
    def _():
        acc_ref[...] = jnp.zeros_like(acc_ref)

    acc_ref[...] += _dot(x_ref[...], w_ref[...])

    @pl.when(kk == pl.num_programs(2) - 1)
    def _():
        o_ref[...] = res_ref[...] + acc_ref[...]


def _mm_acc(x, w, res, *, tm, tn, tk):
    m, k = x.shape
    n = w.shape[1]
    return pl.pallas_call(
        _mm_acc_body,
        out_shape=jax.ShapeDtypeStruct((m, n), F32),
        grid=(n // tn, m // tm, k // tk),
        in_specs=[pl.BlockSpec((tm, tk), lambda j, i, q: (i, q)),
                  pl.BlockSpec((tk, tn), lambda j, i, q: (q, j)),
                  pl.BlockSpec((tm, tn), lambda j, i, q: (i, j))],
        out_specs=pl.BlockSpec((tm, tn), lambda j, i, q: (i, j)),
        scratch_shapes=[pltpu.VMEM((tm, tn), F32)],
        compiler_params=_cparams(("parallel", "parallel", "arbitrary")),
        name="matmul_acc",
    )(x, w, res)


def _swiglu_body(x_ref, wg_ref, wu_ref, o_ref, wg16_ref, wu16_ref):
    @pl.when(pl.program_id(1) == 0)
    def _():
        wg16_ref[...] = wg_ref[...].astype(BF16)
        wu16_ref[...] = wu_ref[...].astype(BF16)

    x = x_ref[...]
    g = _dot(x, wg16_ref[...])
    u = _dot(x, wu16_ref[...])
    o_ref[...] = (g * jax.nn.sigmoid(g) * u).astype(o_ref.dtype)


def _swiglu(x, wg, wu, *, tm, tn):
    m, k = x.shape
    n = wg.shape[1]
    return pl.pallas_call(
        _swiglu_body,
        out_shape=jax.ShapeDtypeStruct((m, n), BF16),
        grid=(n // tn, m // tm),
        in_specs=[pl.BlockSpec((tm, k), lambda j, i: (i, 0)),
                  pl.BlockSpec((k, tn), lambda j, i: (0, j)),
                  pl.BlockSpec((k, tn), lambda j, i: (0, j))],
        out_specs=pl.BlockSpec((tm, tn), lambda j, i: (i, j)),
        scratch_shapes=[pltpu.VMEM((k, tn), BF16), pltpu.VMEM((k, tn), BF16)],
        compiler_params=_cparams(("parallel", "arbitrary")),
        name="swiglu",
    )(x, wg, wu)


def _lora_body(x_ref, w1_ref, w2_ref, b_ref, o_ref, *, mid, post):
    t = _dot(x_ref[...], w1_ref[...])
    if mid == "tanh":
        t = jnp.tanh(t)
    elif mid == "sigmoid":
        t = jax.nn.sigmoid(t)
    z = _dot(t.astype(BF16), w2_ref[...]) + b_ref[...]
    if post == "log_decay":
        z = jax.nn.sigmoid(z) * (-0.6065306597126334)
    elif post == "sigmoid":
        z = jax.nn.sigmoid(z)
    o_ref[...] = z.astype(o_ref.dtype)


def _lora(x, w1, w2, bias, *, mid, post, out_dtype, tm):
    m, k = x.shape
    r = w1.shape[1]
    n = w2.shape[1]
    return pl.pallas_call(
        functools.partial(_lora_body, mid=mid, post=post),
        out_shape=jax.ShapeDtypeStruct((m, n), out_dtype),
        grid=(m // tm,),
        in_specs=[pl.BlockSpec((tm, k), lambda i: (i, 0)),
                  pl.BlockSpec((k, r), lambda i: (0, 0)),
                  pl.BlockSpec((r, n), lambda i: (0, 0)),
                  pl.BlockSpec((1, n), lambda i: (0, 0))],
        out_specs=pl.BlockSpec((tm, n), lambda i: (i, 0)),
        compiler_params=_cparams(("parallel",)),
        name="lora",
    )(x, w1, w2, bias)


def _split2(x):
    hi = x.astype(BF16)
    lo = (x - hi.astype(F32)).astype(BF16)
    return hi, lo


def _wkv_body(r_ref, k_ref, v_ref, lw_ref, a_ref, g_ref, prm_ref, o_ref, st_ref, *, pairs):
    c_idx = pl.program_id(1)

    @pl.when(c_idx == 0)
    def _():
        st_ref[...] = jnp.zeros_like(st_ref)

    C = CHUNK
    lane = lax.broadcasted_iota(jnp.int32, (C, LANES), 1)
    row = lax.broadcasted_iota(jnp.int32, (C, LANES), 0)
    col = lane % HEAD
    upper_half = lane >= HEAD
    tri_strict = col < row
    tri_incl = col <= row
    eye_pair = (col == row).astype(F32)

    def level_mask(bs):
        return ((row // bs) == (col // bs)) & ((row // (bs // 2)) != (col // (bs // 2)))

    row2 = lax.broadcasted_iota(jnp.int32, (2 * C, LANES), 0)
    lane2 = lax.broadcasted_iota(jnp.int32, (2 * C, LANES), 1)
    bd_mask = (row2 >= C) == (lane2 >= HEAD)
    ones_bd = bd_mask.astype(BF16)
    diag_mask = row2 == lane2

    def bd(x):
        return jnp.concatenate([jnp.where(upper_half, 0.0, x), jnp.where(upper_half, x, 0.0)], axis=0)

    def headsum(x16):
        return _dot(x16, ones_bd)

    zeros_cl = jnp.zeros((C, LANES), BF16)

    def pair_step(r, k, v, lw, a, g, prm, s):
        k_k, k_a, r_k, lnx_w, lnx_b = (prm[j:j + 1] for j in range(5))

        kk = k * k_k
        k2 = k * (1.0 + (a - 1.0) * k_a)
        sums = yield ("headsum", jnp.concatenate([kk * kk, r * k2 * r_k], axis=0).astype(BF16))
        cum = lw
        shift = 1
        while shift < C:
            cum = cum + jnp.where(row >= shift, pltpu.roll(cum, shift, axis=0), 0.0)
            shift *= 2
        kkn = kk * lax.rsqrt(jnp.maximum(sums[:C], 1e-24))
        bonus = sums[C:] * v
        avec = -kkn
        bvec = kkn * a
        cum_end = cum[C - 1:C, :]
        p_incl = jnp.exp(cum)
        p_inv = jnp.exp(-cum)
        a_t = avec * jnp.exp(cum - lw)
        r_t = r * p_incl
        b_t = bvec * p_inv
        k_t = k2 * p_inv
        p_end = jnp.exp(cum_end - cum)
        b_h = bvec * p_end
        k_h = k2 * p_end

        v16 = v.astype(BF16)
        gram = _dot_nt(jnp.concatenate([a_t, r_t], axis=0).astype(BF16),
                       jnp.concatenate([bd(b_t), bd(k_t)], axis=0).astype(BF16))
        yield
        a_ab = jnp.where(tri_strict, gram[:C, :LANES], 0.0)
        a_ak = jnp.where(tri_strict, gram[:C, LANES:], 0.0)
        a_rb = jnp.where(tri_incl, gram[C:, :LANES], 0.0)
        a_rk = jnp.where(tri_incl, gram[C:, LANES:], 0.0)
        akv = _dot(a_ak.astype(BF16), bd(v16))

        t_inv = eye_pair + jnp.where(level_mask(2), a_ab, 0.0)
        bs = 4
        while bs <= C:
            x = jnp.where(level_mask(bs), a_ab, 0.0)
            t16 = t_inv.astype(BF16)
            tx = _dot(t16, bd(x).astype(BF16))
            yield
            t_inv = t_inv + _dot(tx.astype(BF16), bd(t16))
            yield
            bs *= 2
        t16 = t_inv.astype(BF16)

        tt = _dot(t16, jnp.concatenate([bd(a_t.astype(BF16)), bd(akv.astype(BF16))], axis=1))
        yield
        a_hat = tt[:, :LANES].astype(BF16)
        u0 = tt[:, LANES:].astype(BF16)

        a_rb16 = a_rb.astype(BF16)
        qa = _dot(a_rb16, bd(a_hat))
        y0 = _dot(jnp.concatenate([a_rb16, a_rk.astype(BF16)], axis=1),
                  jnp.concatenate([bd(u0), bd(v16)], axis=0))
        lhs_t = jnp.concatenate([b_h, k_h], axis=0).T.astype(BF16)
        rhs = jnp.concatenate([jnp.concatenate([a_hat, u0], axis=1),
                               jnp.concatenate([zeros_cl, v16], axis=1)], axis=0)
        upd = _dot(lhs_t, rhs)
        yield
        q_hat = r_t + qa
        m_off = jnp.where(bd_mask, upd[:, :LANES], 0.0)
        n_new = jnp.where(bd_mask, upd[:, LANES:], 0.0)
        p_col = jnp.sum(jnp.where(diag_mask, jnp.exp(cum_end), 0.0), axis=1, keepdims=True)

        s_hi, s_lo = _split2(s)
        m16 = m_off.astype(BF16)
        ys = _dot(jnp.concatenate([q_hat.astype(BF16), m16], axis=0), s_hi)
        y = ys[:C] + y0
        s_new = p_col * s + ys[C:] + _dot(m16, s_lo) + n_new

        mu = (yield ("headsum", y.astype(BF16))) * (1.0 / HEAD)
        d = y - mu
        var = (yield ("headsum", (d * d).astype(BF16))) * (1.0 / HEAD)
        yn = d * lax.rsqrt(var + LN_X_EPS) * lnx_w + lnx_b
        return ((yn + bonus) * g.astype(F32)).astype(o_ref.dtype), s_new

    lanes = [slice(p * LANES, (p + 1) * LANES) for p in range(pairs)]
    results = _run_interleaved([pair_step(r_ref[:, sl], k_ref[:, sl], v_ref[:, sl], lw_ref[:, sl], a_ref[:, sl],
                                          g_ref[:, sl], prm_ref[:, sl], st_ref[p]) for p, sl in enumerate(lanes)],
                               shared={"headsum": headsum})
    for p, sl in enumerate(lanes):
        o_ref[:, sl] = results[p][0]
        st_ref[p] = results[p][1]


def _wkv(r, k, v, lw, a, g, prm, *, pairs):
    length, d = r.shape
    width = pairs * LANES
    blk = lambda hb, c: (c, hb)
    return pl.pallas_call(
        functools.partial(_wkv_body, pairs=pairs),
        out_shape=jax.ShapeDtypeStruct((length, d), BF16),
        grid=(d // width, length // CHUNK),
        in_specs=[pl.BlockSpec((CHUNK, width), blk)] * 6 + [pl.BlockSpec((8, width), lambda hb, c: (0, hb))],
        out_specs=pl.BlockSpec((CHUNK, width), blk),
        scratch_shapes=[pltpu.VMEM((pairs, LANES, LANES), F32)],
        compiler_params=_cparams(("parallel", "arbitrary")),
        name="wkv7",
    )(r, k, v, lw, a, g, prm)


def _attn_body(q_ref, kp_ref, kc_ref, km_ref, vp_ref, vc_ref, vm_ref, sk_ref, o_ref, *, group, n_heads):
    kvh = pl.program_id(0)
    n = pl.program_id(1)
    rows = 2 * BLOCK
    rowi = lax.broadcasted_iota(jnp.int32, (rows, BLOCK), 0)
    kj = lax.broadcasted_iota(jnp.int32, (rows, BLOCK), 1)
    qi = rowi % BLOCK
    second = rowi >= BLOCK
    use_prev = kj > qi
    dist_band = (qi - kj + jnp.where(use_prev, BLOCK, 0)).astype(F32)
    valid_band = jnp.logical_not(use_prev & (n == 0))
    valid_meta = kj < N_META
    dist_meta = (N_META + n * BLOCK + qi - kj).astype(F32)
    lane_o = lax.broadcasted_iota(jnp.int32, (BLOCK, LANES), 1)
    lane_q = lax.broadcasted_iota(jnp.int32, (rows, LANES), 1)
    row_q = lax.broadcasted_iota(jnp.int32, (rows, LANES), 0)
    q_keep = (row_q >= BLOCK) == (lane_q >= HEAD)

    kp = kp_ref[...]
    kc = kc_ref[...]
    km = km_ref[...]
    vp = vp_ref[...]
    vc = vc_ref[...]
    vm = vm_ref[...]

    def pair_step(pr, q):
        q2 = jnp.where(q_keep, jnp.concatenate([q, q], axis=0), jnp.zeros((), q.dtype))
        qk_prev = _dot_nt(q2, kp)
        qk_cur = _dot_nt(q2, kc)
        qk_meta = _dot_nt(q2, km)
        yield
        head1 = (kvh * group + 2 * pr + 1).astype(F32)
        slope = jnp.exp2((jnp.where(second, 1.0, 0.0) + head1) * (-8.0 / n_heads))
        sink = jnp.where(second[:, :1], sk_ref[kvh, 2 * pr + 1], sk_ref[kvh, 2 * pr])
        s_band = jnp.where(valid_band, jnp.where(use_prev, qk_prev, qk_cur) - slope * dist_band, NEG)
        s_meta = jnp.where(valid_meta, qk_meta - slope * dist_meta, NEG)
        mx = jnp.maximum(jnp.max(jnp.maximum(s_band, s_meta), axis=1, keepdims=True), sink)
        e_band = jnp.exp(s_band - mx)
        e_meta = jnp.exp(s_meta - mx)
        den = jnp.sum(e_band + e_meta, axis=1, keepdims=True) + jnp.exp(sink - mx)
        acc = (_dot(jnp.where(use_prev, e_band, 0.0).astype(BF16), vp)
               + _dot(jnp.where(use_prev, 0.0, e_band).astype(BF16), vc)
               + _dot(e_meta.astype(BF16), vm))
        yield
        acc = acc / den
        return jnp.where(lane_o >= HEAD, acc[BLOCK:], acc[:BLOCK]).astype(o_ref.dtype)

    lanes = [slice(pr * LANES, (pr + 1) * LANES) for pr in range(group // 2)]
    results = _run_interleaved([pair_step(pr, q_ref[:, sl]) for pr, sl in enumerate(lanes)])
    for pr, sl in enumerate(lanes):
        o_ref[:, sl] = results[pr]


def _attention(q, k_dup, v_dup, km_dup, vm_dup, sinks, *, n_heads):
    s_len, d = q.shape
    group = n_heads // ATT_KV_HEADS
    gw = group * HEAD
    nb = s_len // BLOCK
    prev = lambda h, n: (jnp.maximum(n - 1, 0), h)
    cur = lambda h, n: (n, h)
    meta = lambda h, n: (0, h)
    return pl.pallas_call(
        functools.partial(_attn_body, group=group, n_heads=n_heads),
        out_shape=jax.ShapeDtypeStruct((s_len, d), BF16),
        grid=(ATT_KV_HEADS, nb),
        in_specs=[pl.BlockSpec((BLOCK, gw), cur),
                  pl.BlockSpec((BLOCK, LANES), prev), pl.BlockSpec((BLOCK, LANES), cur), pl.BlockSpec((BLOCK, LANES), meta),
                  pl.BlockSpec((BLOCK, LANES), prev), pl.BlockSpec((BLOCK, LANES), cur), pl.BlockSpec((BLOCK, LANES), meta),
                  pl.BlockSpec(memory_space=pltpu.SMEM)],
        out_specs=pl.BlockSpec((BLOCK, gw), cur),
        compiler_params=_cparams(("parallel", "parallel")),
        name="swa_attention",
    )(q, k_dup, k_dup, km_dup, v_dup, v_dup, vm_dup, sinks)


def _router_body(h_ref, g_ref, wr_ref, hn_ref, sel_ref, gate_ref, *, n_experts):
    y = _rms(h_ref[...]) * g_ref[...]
    hn_ref[...] = y
    logits = jnp.dot(y, wr_ref[...], preferred_element_type=F32, precision=lax.Precision.HIGHEST)
    lane = lax.broadcasted_iota(jnp.int32, logits.shape, 1)
    logits = jnp.where(lane < n_experts, logits, NEG)
    m1 = jnp.max(logits, axis=1, keepdims=True)
    i1 = jnp.min(jnp.where(logits == m1, lane, LANES), axis=1, keepdims=True)
    rest = jnp.where(lane == i1, NEG, logits)
    m2 = jnp.max(rest, axis=1, keepdims=True)
    i2 = jnp.min(jnp.where(rest == m2, lane, LANES), axis=1, keepdims=True)
    e2 = jnp.exp(m2 - m1)
    g1 = 1.0 / (1.0 + e2)
    g2 = e2 / (1.0 + e2)
    sel_ref[...] = jnp.where(lane == 0, i1, jnp.where(lane == 1, i2, 0))
    gate_ref[...] = jnp.where(lane == 0, g1, jnp.where(lane == 1, g2, 0.0))


def _router(h, gain, w_router_pad, *, n_experts, tm):
    m, d = h.shape
    return pl.pallas_call(
        functools.partial(_router_body, n_experts=n_experts),
        out_shape=[jax.ShapeDtypeStruct((m, d), F32), jax.ShapeDtypeStruct((m, LANES), jnp.int32),
                   jax.ShapeDtypeStruct((m, LANES), F32)],
        grid=(m // tm,),
        in_specs=[pl.BlockSpec((tm, d), lambda i: (i, 0)),
                  pl.BlockSpec((1, d), lambda i: (0, 0)),
                  pl.BlockSpec((d, LANES), lambda i: (0, 0))],
        out_specs=[pl.BlockSpec((tm, d), lambda i: (i, 0)), pl.BlockSpec((tm, LANES), lambda i: (i, 0)),
                   pl.BlockSpec((tm, LANES), lambda i: (i, 0))],
        compiler_params=_cparams(("parallel",)),
        name="moe_router",
    )(h, gain, w_router_pad)


EXPERT_TILE = 256


def _row_copy(src_hbm, dst, src_row, dst_row, sem):
    return pltpu.make_async_copy(src_hbm.at[pl.ds(src_row, 1)], dst.at[pl.ds(dst_row, 1)], sem)


ROW_DMA_UNROLL = 8


def _gather_rows_body(src_ref, x_hbm, o_ref, buf, sem, *, rows):
    i = pl.program_id(0)

    def issue(tile, slot):
        base = tile * rows

        def body(r, carry):
            _row_copy(x_hbm, buf.at[slot], src_ref[base + r], r, sem.at[slot]).start()
            return carry

        lax.fori_loop(0, rows, body, 0, unroll=ROW_DMA_UNROLL)

    @pl.when(i == 0)
    def _():
        issue(0, 0)

    @pl.when(i + 1 < pl.num_programs(0))
    def _():
        issue(i + 1, (i + 1) % 2)

    slot = i % 2
    pltpu.make_async_copy(x_hbm.at[pl.ds(0, rows)], buf.at[slot], sem.at[slot]).wait()
    o_ref[...] = buf[slot].astype(o_ref.dtype)


def _gather_rows(row_src, x, n_rows, *, rows, out_dtype):
    d = x.shape[1]
    return pl.pallas_call(
        functools.partial(_gather_rows_body, rows=rows),
        out_shape=jax.ShapeDtypeStruct((n_rows, d), out_dtype),
        grid_spec=pltpu.PrefetchScalarGridSpec(
            num_scalar_prefetch=1,
            grid=(n_rows // rows,),
            in_specs=[pl.BlockSpec(memory_space=pl.ANY)],
            out_specs=pl.BlockSpec((rows, d), lambda i, src: (i, 0)),
            scratch_shapes=[pltpu.VMEM((2, rows, d), x.dtype), pltpu.SemaphoreType.DMA((2,))],
        ),
        compiler_params=_cparams(("arbitrary",)),
        name="moe_gather",
    )(row_src, x)


def _stage_expert_weights(te_ref, nxt_ref, w_hbms, wbuf, w16, sem, slot_ref, tn):
    j = pl.program_id(0)
    i = pl.program_id(1)

    def copies(expert, col_block, slot):
        cols = pl.ds(pl.multiple_of(col_block * tn, tn), tn)
        return [pltpu.make_async_copy(w.at[expert, :, cols], wbuf.at[slot, k], sem.at[slot])
                for k, w in enumerate(w_hbms)]

    @pl.when((j == 0) & (i == 0))
    def _():
        slot_ref[0] = 0
        for c in copies(te_ref[0], 0, 0):
            c.start()

    @pl.when((i == 0) | (te_ref[i] != te_ref[jnp.maximum(i - 1, 0)]))
    def _():
        slot = slot_ref[0]
        for c in copies(te_ref[i], j, slot):
            c.wait()
        next_expert = nxt_ref[i]
        wraps = next_expert < 0

        @pl.when(jnp.logical_not(wraps & (j == pl.num_programs(0) - 1)))
        def _():
            for c in copies(jnp.where(wraps, te_ref[0], next_expert), jnp.where(wraps, j + 1, j), 1 - slot):
                c.start()

        for k in range(len(w_hbms)):
            w16[k] = wbuf[slot, k].astype(BF16)
        slot_ref[0] = 1 - slot


def _moe_swiglu_body(te_ref, nxt_ref, x_ref, wg_hbm, wu_hbm, o_ref, wbuf, w16, sem, slot_ref, *, tn):
    _stage_expert_weights(te_ref, nxt_ref, (wg_hbm, wu_hbm), wbuf, w16, sem, slot_ref, tn)
    x = x_ref[...]
    g = _dot(x, w16[0])
    u = _dot(x, w16[1])
    o_ref[...] = (g * jax.nn.sigmoid(g) * u).astype(o_ref.dtype)


def _moe_down_body(te_ref, nxt_ref, x_ref, w_hbm, o_ref, wbuf, w16, sem, slot_ref, *, tn):
    _stage_expert_weights(te_ref, nxt_ref, (w_hbm,), wbuf, w16, sem, slot_ref, tn)
    o_ref[...] = _dot(x_ref[...], w16[0])


def _moe_matmul(body, tile_expert, next_expert, xs, weights, *, tn, out_dtype, name):
    rows, k = xs.shape
    n = weights[0].shape[-1]
    n_w = len(weights)
    return pl.pallas_call(
        functools.partial(body, tn=tn),
        out_shape=jax.ShapeDtypeStruct((rows, n), out_dtype),
        grid_spec=pltpu.PrefetchScalarGridSpec(
            num_scalar_prefetch=2,
            grid=(n // tn, rows // EXPERT_TILE),
            in_specs=[pl.BlockSpec((EXPERT_TILE, k), lambda j, i, te, nxt: (i, 0))]
            + [pl.BlockSpec(memory_space=pl.ANY)] * n_w,
            out_specs=pl.BlockSpec((EXPERT_TILE, tn), lambda j, i, te, nxt: (i, j)),
            scratch_shapes=[pltpu.VMEM((2, n_w, k, tn), F32), pltpu.VMEM((n_w, k, tn), BF16),
                            pltpu.SemaphoreType.DMA((2,)), pltpu.SMEM((1,), jnp.int32)],
        ),
        compiler_params=_cparams(("arbitrary", "arbitrary")),
        name=name,
    )(tile_expert, next_expert, xs, *weights)


def _moe_combine_body(pos_ref, h_ref, gate_ref, fn_ref, eo_hbm, o_ref, buf, sem, *, tm):
    i = pl.program_id(0)

    def issue(tile, slot):
        base = tile * tm

        def body(t, carry):
            for j in range(TOP_K):
                _row_copy(eo_hbm, buf.at[slot, j], pos_ref[TOP_K * (base + t) + j], t, sem.at[slot]).start()
            return carry

        lax.fori_loop(0, tm, body, 0, unroll=ROW_DMA_UNROLL // TOP_K)

    @pl.when(i == 0)
    def _():
        issue(0, 0)

    @pl.when(i + 1 < pl.num_programs(0))
    def _():
        issue(i + 1, (i + 1) % 2)

    slot = i % 2
    for j in range(TOP_K):
        pltpu.make_async_copy(eo_hbm.at[pl.ds(0, tm)], buf.at[slot, j], sem.at[slot]).wait()
    y = h_ref[...]
    for j in range(TOP_K):
        y = y + gate_ref[:, j:j + 1] * buf[slot, j]
    o_ref[...] = _rms(y) * fn_ref[...]


def _moe_combine(pos, h, gate, final_gain, eo, *, tm):
    m, d = h.shape
    return pl.pallas_call(
        functools.partial(_moe_combine_body, tm=tm),
        out_shape=jax.ShapeDtypeStruct((m, d), F32),
        grid_spec=pltpu.PrefetchScalarGridSpec(
            num_scalar_prefetch=1,
            grid=(m // tm,),
            in_specs=[pl.BlockSpec((tm, d), lambda i, pos: (i, 0)),
                      pl.BlockSpec((tm, LANES), lambda i, pos: (i, 0)),
                      pl.BlockSpec((1, d), lambda i, pos: (0, 0)),
                      pl.BlockSpec(memory_space=pl.ANY)],
            out_specs=pl.BlockSpec((tm, d), lambda i, pos: (i, 0)),
            scratch_shapes=[pltpu.VMEM((2, TOP_K, tm, d), F32), pltpu.SemaphoreType.DMA((2,))],
        ),
        compiler_params=_cparams(("arbitrary",)),
        name="moe_combine",
    )(pos, h, gate, final_gain, eo)


def _moe_plan(sel, n_experts):
    m = sel.shape[0]
    flat_e = sel[:, :TOP_K].reshape(-1)
    onehot = (flat_e[:, None] == jnp.arange(n_experts, dtype=jnp.int32)[None, :]).astype(jnp.int32)
    csum = jnp.cumsum(onehot, axis=0)
    rank = jnp.sum(csum * onehot, axis=1) - 1
    counts = csum[-1]
    padded = (counts + EXPERT_TILE - 1) // EXPERT_TILE * EXPERT_TILE
    ends = jnp.cumsum(padded)
    starts = ends - padded
    pos = (jnp.sum(onehot * starts[None, :], axis=1) + rank).astype(jnp.int32)
    n_rows = TOP_K * m + n_experts * EXPERT_TILE
    token = jnp.arange(TOP_K * m, dtype=jnp.int32) // TOP_K
    row_src = jnp.zeros((n_rows,), jnp.int32).at[pos].set(token)
    tile_start = jnp.arange(n_rows // EXPERT_TILE, dtype=jnp.int32) * EXPERT_TILE
    tile_expert = jnp.minimum(jnp.sum((tile_start[:, None] >= ends[None, :]).astype(jnp.int32), axis=1),
                              n_experts - 1).astype(jnp.int32)
    run_end = jnp.sum((tile_expert[None, :] <= tile_expert[:, None]).astype(jnp.int32), axis=1)
    n_tiles = tile_expert.shape[0]
    next_expert = jnp.where(run_end < n_tiles, tile_expert[jnp.minimum(run_end, n_tiles - 1)], -1).astype(jnp.int32)
    return pos, row_src, tile_expert, next_expert, n_rows


def _largest_tile(n, cap, mult):
    best = None
    t = mult
    while t <= min(n, cap):
        if n % t == 0:
            best = t
        t += mult
    assert best is not None, (n, cap, mult)
    return best


def _pad_cols(w, n_to):
    return jnp.pad(w, ((0, 0), (0, n_to - w.shape[1])))


def _pad_rows(w, n_to):
    return jnp.pad(w, ((0, n_to - w.shape[0]), (0, 0)))


def _round_up(n, m):
    return -(-n // m) * m


def kernel(x, meta_tokens, a_norm, a_mix, a_w_rkv, a_w0, a_w1, a_w2, a_a0, a_a1, a_a2, a_g1, a_g2, a_k_k, a_k_a, a_r_k, a_lnx_w, a_lnx_b, a_w_out, kv_norm, w_kv, b_norm, b_w_q, b_sinks, b_w_out, f_norm, d_w_gate, d_w_up, d_w_down, e_router, e_w_gate, e_w_up, e_w_down, final_norm):
    assert x.shape[0] == 1
    seq, d = x.shape[1], x.shape[2]
    n_heads = d // HEAD
    n_experts = e_router.shape[-1]
    xs = x[0]

    l_real = N_META + seq
    l_pad = _round_up(l_real, CHUNK)
    h = jnp.concatenate([meta_tokens.astype(F32), xs, jnp.zeros((l_pad - l_real, d), F32)], axis=0)
    tm0 = _largest_tile(l_pad, 768, 16)
    tn = _largest_tile(d, 512, LANES)

    xr, xw, xk, xv, xa, xg = _premix(h, a_norm[0:1], a_mix[0], _largest_tile(l_pad, 256, 16))
    r = _mm(xr, a_w_rkv[0, 0], out_dtype=F32, tm=tm0, tn=tn)
    k = _mm(xk, a_w_rkv[0, 1], out_dtype=F32, tm=tm0, tn=tn)
    v = _mm(xv, a_w_rkv[0, 2], out_dtype=F32, tm=tm0, tn=tn)
    tml = _largest_tile(l_pad, 384, 16)
    lw = _lora(xw, a_w1[0].astype(BF16), a_w2[0].astype(BF16), a_w0[0][None], mid="tanh", post="log_decay",
               out_dtype=F32, tm=tml)
    a = _lora(xa, a_a1[0].astype(BF16), a_a2[0].astype(BF16), a_a0[0][None], mid="none", post="sigmoid",
              out_dtype=F32, tm=tml)
    gate_rank = _round_up(a_g1.shape[-1], LANES)
    g = _lora(xg, _pad_cols(a_g1[0], gate_rank).astype(BF16), _pad_rows(a_g2[0], gate_rank).astype(BF16),
              jnp.zeros((1, d), F32), mid="sigmoid", post="none", out_dtype=BF16, tm=tml)
    prm = jnp.concatenate([a_k_k[0][None], a_k_a[0][None], a_r_k[0].reshape(1, d), a_lnx_w[0][None],
                           a_lnx_b[0][None], jnp.zeros((3, d), F32)], axis=0)
    pairs = 16 if n_heads % 32 == 0 else n_heads // 2
    mixed = _wkv(r, k, v, lw, a, g, prm, pairs=pairs)
    h = _mm(mixed, a_w_out[0], out_dtype=F32, tm=tm0, tn=tn, res=h)

    (hn,) = _rmsnorm(h, f_norm[0:1], [BF16], _largest_tile(l_pad, 256, 16))
    d_ff = d_w_gate.shape[-1]
    act = _swiglu(hn, d_w_gate[0], d_w_up[0], tm=_largest_tile(l_pad, 1408, 16), tn=_largest_tile(d_ff, 256, LANES))
    h = _mm_acc(act, d_w_down[0].astype(BF16), h, tm=tm0, tn=tn, tk=_largest_tile(d_ff, 6144, LANES))

    hkv, hq = _rmsnorm(h, jnp.stack([kv_norm, b_norm[0]]), [BF16, BF16], _largest_tile(l_pad, 256, 16))
    kv_w = ATT_KV_HEADS * HEAD
    kv = _mm(hkv, w_kv, out_dtype=BF16, tm=tm0, tn=_largest_tile(2 * kv_w, 512, LANES))

    def dup_heads(t):
        t = t.reshape(t.shape[0], ATT_KV_HEADS, 1, HEAD)
        return jnp.broadcast_to(t, (t.shape[0], ATT_KV_HEADS, 2, HEAD)).reshape(t.shape[0], ATT_KV_HEADS * LANES)

    k_all, v_all = dup_heads(kv[:, :kv_w]), dup_heads(kv[:, kv_w:])
    k_meta = _pad_rows(k_all[:N_META], BLOCK)
    v_meta = _pad_rows(v_all[:N_META], BLOCK)
    k_real = k_all[N_META:l_real]
    v_real = v_all[N_META:l_real]

    h = h[N_META:l_real]
    hq = hq[N_META:l_real]
    tm1 = _largest_tile(seq, 512, 16)
    q = _mm(hq, b_w_q[0], out_dtype=BF16, tm=tm1, tn=tn, scale=HEAD ** -0.5)
    o = _attention(q, k_real, v_real, k_meta, v_meta, b_sinks[0].reshape(ATT_KV_HEADS, -1), n_heads=n_heads)
    h = _mm(o, b_w_out[0], out_dtype=F32, tm=tm1, tn=tn, res=h)

    hn, sel, gate = _router(h, f_norm[1:2], _pad_cols(e_router[0], LANES), n_experts=n_experts,
                            tm=_largest_tile(seq, 256, 16))
    pos, row_src, tile_expert, next_expert, n_rows = _moe_plan(sel, n_experts)
    xs_sorted = _gather_rows(row_src, hn, n_rows, rows=_largest_tile(n_rows, 512, EXPERT_TILE), out_dtype=BF16)
    d_exp = e_w_gate.shape[-1]
    act = _moe_matmul(_moe_swiglu_body, tile_expert, next_expert, xs_sorted, (e_w_gate[0], e_w_up[0]),
                      tn=_largest_tile(d_exp, 512, LANES), out_dtype=BF16, name="moe_swiglu")
    eo = _moe_matmul(_moe_down_body, tile_expert, next_expert, act, (e_w_down[0],),
                     tn=_largest_tile(d, 1024, LANES), out_dtype=F32, name="moe_down")
    out = _moe_combine(pos, h, gate, final_norm[None], eo, tm=_largest_tile(seq, 128, 8))
    return out[None]
```

```python
import functools

import jax
import jax.numpy as jnp
from jax import lax
from jax.experimental import pallas as pl
from jax.experimental.pallas import tpu as pltpu

F32 = jnp.float32
BF16 = jnp.bfloat16

LANES = 128
VMEM_LIMIT_BYTES = 56 * 1024 * 1024

N_META = 16
RMS_EPS = 1e-5
LN_X_EPS = 64e-5
HEAD = 64
ATT_KV_HEADS = 8
WINDOW = 128
BLOCK = 128
TOP_K = 2
CHUNK = 64
NEG = -1e30


def _cparams(sem):
    return pltpu.CompilerParams(dimension_semantics=sem, vmem_limit_bytes=VMEM_LIMIT_BYTES)


def _dot(a, b):
    return jnp.dot(a, b, preferred_element_type=F32)


def _dot_nt(a, b):
    return lax.dot_general(a, b, (((1,), (1,)), ((), ())), preferred_element_type=F32)


def _rms(x):
    return x * lax.rsqrt(jnp.mean(x * x, axis=-1, keepdims=True) + RMS_EPS)


def _run_interleaved(steps, shared=None):
    results = [None] * len(steps)
    pending = list(range(len(steps)))
    inbox = {idx: None for idx in pending}
    while pending:
        requests = {}
        for idx in list(pending):
            try:
                req = steps[idx].send(inbox[idx])
            except StopIteration as done:
                results[idx] = done.value
                pending.remove(idx)
                continue
            inbox[idx] = None
            if req is not None:
                requests.setdefault(req[0], []).append((idx, req[1]))
        for key, members in requests.items():
            out = shared[key](jnp.concatenate([rows for _, rows in members], axis=0))
            start = 0
            for idx, rows in members:
                inbox[idx] = out[start:start + rows.shape[0]]
                start += rows.shape[0]
    return results


def _rmsnorm_body(h_ref, g_ref, *o_refs):
    y = _rms(h_ref[...])
    for j, o_ref in enumerate(o_refs):
        o_ref[...] = (y * g_ref[j:j + 1, :]).astype(o_ref.dtype)


def _rmsnorm(h, gains, out_dtypes, tm):
    m, d = h.shape
    n_out = len(out_dtypes)
    outs = pl.pallas_call(
        _rmsnorm_body,
        out_shape=[jax.ShapeDtypeStruct((m, d), dt) for dt in out_dtypes],
        grid=(m // tm,),
        in_specs=[pl.BlockSpec((tm, d), lambda i: (i, 0)),
                  pl.BlockSpec((n_out, d), lambda i: (0, 0))],
        out_specs=[pl.BlockSpec((tm, d), lambda i: (i, 0)) for _ in out_dtypes],
        compiler_params=_cparams(("parallel",)),
        name="rmsnorm",
    )(h, gains)
    return outs


def _drop_meta_body(ha_ref, hb_ref, g_ref, h_ref, *o_refs):
    rows = jnp.concatenate([ha_ref[N_META:, :], hb_ref[...]], axis=0)
    h_ref[...] = rows
    y = _rms(rows)
    for j, o_ref in enumerate(o_refs):
        o_ref[...] = (y * g_ref[j:j + 1, :]).astype(o_ref.dtype)


def _drop_meta(h, gains, seq, tm):
    d = h.shape[1]
    n_out = gains.shape[0]
    per = tm // N_META
    return pl.pallas_call(
        _drop_meta_body,
        out_shape=[jax.ShapeDtypeStruct((seq, d), F32)] + [jax.ShapeDtypeStruct((seq, d), BF16)] * n_out,
        grid=(seq // tm,),
        in_specs=[pl.BlockSpec((tm, d), lambda i: (i, 0)),
                  pl.BlockSpec((N_META, d), lambda i: ((i + 1) * per, 0)),
                  pl.BlockSpec((n_out, d), lambda i: (0, 0))],
        out_specs=[pl.BlockSpec((tm, d), lambda i: (i, 0))] * (1 + n_out),
        compiler_params=_cparams(("parallel",)),
        name="drop_meta",
    )(h, h, gains)


def _premix_body(h_ref, hp_ref, g_ref, mix_ref, *o_refs):
    i = pl.program_id(0)
    g = g_ref[...]
    xn = _rms(h_ref[...]) * g
    pn = _rms(hp_ref[...]) * g
    prev_row = jnp.where(i > 0, pn[7:8, :], 0.0)
    sh = pltpu.roll(xn, 1, axis=0)
    row = lax.broadcasted_iota(jnp.int32, xn.shape, 0)
    sh = jnp.where(row == 0, prev_row, sh)
    xx = sh - xn
    for j, o_ref in enumerate(o_refs):
        o_ref[...] = (xn + xx * mix_ref[j:j + 1, :]).astype(o_ref.dtype)


def _premix(h, gain, mix, tm):
    m, d = h.shape
    n_mix = mix.shape[0]
    rows8 = tm // 8
    return pl.pallas_call(
        _premix_body,
        out_shape=[jax.ShapeDtypeStruct((m, d), BF16) for _ in range(n_mix)],
        grid=(m // tm,),
        in_specs=[pl.BlockSpec((tm, d), lambda i: (i, 0)),
                  pl.BlockSpec((8, d), lambda i: (jnp.maximum(i * rows8 - 1, 0), 0)),
                  pl.BlockSpec((1, d), lambda i: (0, 0)),
                  pl.BlockSpec((n_mix, d), lambda i: (0, 0))],
        out_specs=[pl.BlockSpec((tm, d), lambda i: (i, 0)) for _ in range(n_mix)],
        compiler_params=_cparams(("parallel",)),
        name="premix",
    )(h, h, gain, mix)


def _mm_body(x_ref, w_ref, *rest, has_res, scale):
    if has_res:
        res_ref, o_ref, w16_ref = rest
    else:
        o_ref, w16_ref = rest

    @pl.when(pl.program_id(1) == 0)
    def _():
        w16_ref[...] = w_ref[...].astype(BF16)

    acc = _dot(x_ref[...], w16_ref[...])
    if scale is not None:
        acc = acc * scale
    if has_res:
        acc = acc + res_ref[...]
    o_ref[...] = acc.astype(o_ref.dtype)


def _mm(x, w, *, out_dtype, tm, tn, res=None, scale=None):
    m, k = x.shape
    n = w.shape[1]
    in_specs = [pl.BlockSpec((tm, k), lambda j, i: (i, 0)),
                pl.BlockSpec((k, tn), lambda j, i: (0, j))]
    args = [x, w]
    if res is not None:
        in_specs.append(pl.BlockSpec((tm, tn), lambda j, i: (i, j)))
        args.append(res)
    return pl.pallas_call(
        functools.partial(_mm_body, has_res=res is not None, scale=scale),
        out_shape=jax.ShapeDtypeStruct((m, n), out_dtype),
        grid=(n // tn, m // tm),
        in_specs=in_specs,
        out_specs=pl.BlockSpec((tm, tn), lambda j, i: (i, j)),
        scratch_shapes=[pltpu.VMEM((k, tn), BF16)],
        compiler_params=_cparams(("parallel", "arbitrary")),
        name="matmul",
    )(*args)


def _mm_acc_body(x_ref, w_ref, res_ref, o_ref, acc_ref):
    kk = pl.program_id(2)

    @pl.when(kk == 0)
    def _():
        acc_ref[...] = jnp.zeros_like(acc_ref)

    acc_ref[...] += _dot(x_ref[...], w_ref[...])

    @pl.when(kk == pl.num_programs(2) - 1)
    def _():
        o_ref[...] = res_ref[...] + acc_ref[...]


def _mm_acc(x, w, res, *, tm, tn, tk):
    m, k = x.shape
    n = w.shape[1]
    return pl.pallas_call(
        _mm_acc_body,
        out_shape=jax.ShapeDtypeStruct((m, n), F32),
        grid=(n // tn, m // tm, k // tk),
        in_specs=[pl.BlockSpec((tm, tk), lambda j, i, q: (i, q)),
                  pl.BlockSpec((tk, tn), lambda j, i, q: (q, j)),
                  pl.BlockSpec((tm, tn), lambda j, i, q: (i, j))],
        out_specs=pl.BlockSpec((tm, tn), lambda j, i, q: (i, j)),
        scratch_shapes=[pltpu.VMEM((tm, tn), F32)],
        compiler_params=_cparams(("parallel", "parallel", "arbitrary")),
        name="matmul_acc",
    )(x, w, res)


def _swiglu_body(x_ref, wg_ref, wu_ref, o_ref, wg16_ref, wu16_ref):
    @pl.when(pl.program_id(1) == 0)
    def _():
        wg16_ref[...] = wg_ref[...].astype(BF16)
        wu16_ref[...] = wu_ref[...].astype(BF16)

    x = x_ref[...]
    g = _dot(x, wg16_ref[...])
    u = _dot(x, wu16_ref[...])
    o_ref[...] = (g * jax.nn.sigmoid(g) * u).astype(o_ref.dtype)


def _swiglu(x, wg, wu, *, tm, tn):
    m, k = x.shape
    n = wg.shape[1]
    return pl.pallas_call(
        _swiglu_body,
        out_shape=jax.ShapeDtypeStruct((m, n), BF16),
        grid=(n // tn, m // tm),
        in_specs=[pl.BlockSpec((tm, k), lambda j, i: (i, 0)),
                  pl.BlockSpec((k, tn), lambda j, i: (0, j)),
                  pl.BlockSpec((k, tn), lambda j, i: (0, j))],
        out_specs=pl.BlockSpec((tm, tn), lambda j, i: (i, j)),
        scratch_shapes=[pltpu.VMEM((k, tn), BF16), pltpu.VMEM((k, tn), BF16)],
        compiler_params=_cparams(("parallel", "arbitrary")),
        name="swiglu",
    )(x, wg, wu)


def _lora_body(x_ref, w1_ref, w2_ref, b_ref, o_ref, *, mid, post):
    t = _dot(x_ref[...], w1_ref[...])
    if mid == "tanh":
        t = jnp.tanh(t)
    elif mid == "sigmoid":
        t = jax.nn.sigmoid(t)
    z = _dot(t.astype(BF16), w2_ref[...]) + b_ref[...]
    if post == "log_decay":
        z = jax.nn.sigmoid(z) * (-0.6065306597126334)
    elif post == "sigmoid":
        z = jax.nn.sigmoid(z)
    o_ref[...] = z.astype(o_ref.dtype)


def _lora(x, w1, w2, bias, *, mid, post, out_dtype, tm):
    m, k = x.shape
    r = w1.shape[1]
    n = w2.shape[1]
    return pl.pallas_call(
        functools.partial(_lora_body, mid=mid, post=post),
        out_shape=jax.ShapeDtypeStruct((m, n), out_dtype),
        grid=(m // tm,),
        in_specs=[pl.BlockSpec((tm, k), lambda i: (i, 0)),
                  pl.BlockSpec((k, r), lambda i: (0, 0)),
                  pl.BlockSpec((r, n), lambda i: (0, 0)),
                  pl.BlockSpec((1, n), lambda i: (0, 0))],
        out_specs=pl.BlockSpec((tm, n), lambda i: (i, 0)),
        compiler_params=_cparams(("parallel",)),
        name="lora",
    )(x, w1, w2, bias)


def _split2(x):
    hi = x.astype(BF16)
    lo = (x - hi.astype(F32)).astype(BF16)
    return hi, lo


def _wkv_body(r_ref, k_ref, v_ref, lw_ref, a_ref, g_ref, prm_ref, o_ref, st_ref, *, pairs):
    c_idx = pl.program_id(1)

    @pl.when(c_idx == 0)
    def _():
        st_ref[...] = jnp.zeros_like(st_ref)

    C = CHUNK
    lane = lax.broadcasted_iota(jnp.int32, (C, LANES), 1)
    row = lax.broadcasted_iota(jnp.int32, (C, LANES), 0)
    col = lane % HEAD
    upper_half = lane >= HEAD
    tri_strict = col < row
    tri_incl = col <= row
    eye_pair = (col == row).astype(F32)

    def level_mask(bs):
        return ((row // bs) == (col // bs)) & ((row // (bs // 2)) != (col // (bs // 2)))

    row2 = lax.broadcasted_iota(jnp.int32, (2 * C, LANES), 0)
    lane2 = lax.broadcasted_iota(jnp.int32, (2 * C, LANES), 1)
    bd_mask = (row2 >= C) == (lane2 >= HEAD)
    ones_bd = bd_mask.astype(BF16)
    diag_mask = row2 == lane2

    def bd(x):
        return jnp.concatenate([jnp.where(upper_half, 0.0, x), jnp.where(upper_half, x, 0.0)], axis=0)

    def headsum(x16):
        return _dot(x16, ones_bd)

    zeros_cl = jnp.zeros((C, LANES), BF16)

    def pair_step(r, k, v, lw, a, g, prm, s):
        k_k, k_a, r_k, lnx_w, lnx_b = (prm[j:j + 1] for j in range(5))

        kk = k * k_k
        k2 = k * (1.0 + (a - 1.0) * k_a)
        sums = yield ("headsum", jnp.concatenate([kk * kk, r * k2 * r_k], axis=0).astype(BF16))
        cum = lw
        shift = 1
        while shift < C:
            cum = cum + jnp.where(row >= shift, pltpu.roll(cum, shift, axis=0), 0.0)
            shift *= 2
        kkn = kk * lax.rsqrt(jnp.maximum(sums[:C], 1e-24))
        bonus = sums[C:] * v
        avec = -kkn
        bvec = kkn * a
        cum_end = cum[C - 1:C, :]
        p_incl = jnp.exp(cum)
        p_inv = jnp.exp(-cum)
        a_t = avec * jnp.exp(cum - lw)
        r_t = r * p_incl
        b_t = bvec * p_inv
        k_t = k2 * p_inv
        p_end = jnp.exp(cum_end - cum)
        b_h = bvec * p_end
        k_h = k2 * p_end

        v16 = v.astype(BF16)
        gram = _dot_nt(jnp.concatenate([a_t, r_t], axis=0).astype(BF16),
                       jnp.concatenate([bd(b_t), bd(k_t)], axis=0).astype(BF16))
        yield
        a_ab = jnp.where(tri_strict, gram[:C, :LANES], 0.0)
        a_ak = jnp.where(tri_strict, gram[:C, LANES:], 0.0)
        a_rb = jnp.where(tri_incl, gram[C:, :LANES], 0.0)
        a_rk = jnp.where(tri_incl, gram[C:, LANES:], 0.0)
        akv = _dot(a_ak.astype(BF16), bd(v16))

        t_inv = eye_pair + jnp.where(level_mask(2), a_ab, 0.0)
        bs = 4
        while bs <= C:
            x = jnp.where(level_mask(bs), a_ab, 0.0)
            t16 = t_inv.astype(BF16)
            tx = _dot(t16, bd(x).astype(BF16))
            yield
            t_inv = t_inv + _dot(tx.astype(BF16), bd(t16))
            yield
            bs *= 2
        t16 = t_inv.astype(BF16)

        tt = _dot(t16, jnp.concatenate([bd(a_t.astype(BF16)), bd(akv.astype(BF16))], axis=1))
        yield
        a_hat = tt[:, :LANES].astype(BF16)
        u0 = tt[:, LANES:].astype(BF16)

        a_rb16 = a_rb.astype(BF16)
        qa = _dot(a_rb16, bd(a_hat))
        y0 = _dot(jnp.concatenate([a_rb16, a_rk.astype(BF16)], axis=1),
                  jnp.concatenate([bd(u0), bd(v16)], axis=0))
        lhs_t = jnp.concatenate([b_h, k_h], axis=0).T.astype(BF16)
        rhs = jnp.concatenate([jnp.concatenate([a_hat, u0], axis=1),
                               jnp.concatenate([zeros_cl, v16], axis=1)], axis=0)
        upd = _dot(lhs_t, rhs)
        yield
        q_hat = r_t + qa
        m_off = jnp.where(bd_mask, upd[:, :LANES], 0.0)
        n_new = jnp.where(bd_mask, upd[:, LANES:], 0.0)
        p_col = jnp.sum(jnp.where(diag_mask, jnp.exp(cum_end), 0.0), axis=1, keepdims=True)

        s_hi, s_lo = _split2(s)
        m16 = m_off.astype(BF16)
        ys = _dot(jnp.concatenate([q_hat.astype(BF16), m16], axis=0), s_hi)
        y = ys[:C] + y0
        s_new = p_col * s + ys[C:] + _dot(m16, s_lo) + n_new

        mu = (yield ("headsum", y.astype(BF16))) * (1.0 / HEAD)
        d = y - mu
        var = (yield ("headsum", (d * d).astype(BF16))) * (1.0 / HEAD)
        yn = d * lax.rsqrt(var + LN_X_EPS) * lnx_w + lnx_b
        return ((yn + bonus) * g.astype(F32)).astype(o_ref.dtype), s_new

    lanes = [slice(p * LANES, (p + 1) * LANES) for p in range(pairs)]
    results = _run_interleaved([pair_step(r_ref[:, sl], k_ref[:, sl], v_ref[:, sl], lw_ref[:, sl], a_ref[:, sl],
                                          g_ref[:, sl], prm_ref[:, sl], st_ref[p]) for p, sl in enumerate(lanes)],
                               shared={"headsum": headsum})
    for p, sl in enumerate(lanes):
        o_ref[:, sl] = results[p][0]
        st_ref[p] = results[p][1]


def _wkv(r, k, v, lw, a, g, prm, *, pairs):
    length, d = r.shape
    width = pairs * LANES
    blk = lambda hb, c: (c, hb)
    return pl.pallas_call(
        functools.partial(_wkv_body, pairs=pairs),
        out_shape=jax.ShapeDtypeStruct((length, d), BF16),
        grid=(d // width, length // CHUNK),
        in_specs=[pl.BlockSpec((CHUNK, width), blk)] * 6 + [pl.BlockSpec((8, width), lambda hb, c: (0, hb))],
        out_specs=pl.BlockSpec((CHUNK, width), blk),
        scratch_shapes=[pltpu.VMEM((pairs, LANES, LANES), F32)],
        compiler_params=_cparams(("parallel", "arbitrary")),
        name="wkv7",
    )(r, k, v, lw, a, g, prm)


def _attn_body(q_ref, kp_ref, kc_ref, km_ref, vp_ref, vc_ref, vm_ref, sk_ref, o_ref, *, group, n_heads):
    kvh = pl.program_id(0)
    n = pl.program_id(1)
    rows = 2 * BLOCK
    rowi = lax.broadcasted_iota(jnp.int32, (rows, BLOCK), 0)
    kj = lax.broadcasted_iota(jnp.int32, (rows, BLOCK), 1)
    qi = rowi % BLOCK
    second = rowi >= BLOCK
    use_prev = kj > qi
    dist_band = (qi - kj + jnp.where(use_prev, BLOCK, 0)).astype(F32)
    valid_band = jnp.logical_not(use_prev & (n == 0))
    valid_meta = kj < N_META
    dist_meta = (N_META + n * BLOCK + qi - kj).astype(F32)
    lane_o = lax.broadcasted_iota(jnp.int32, (BLOCK, LANES), 1)
    lane_q = lax.broadcasted_iota(jnp.int32, (rows, LANES), 1)
    row_q = lax.broadcasted_iota(jnp.int32, (rows, LANES), 0)
    q_keep = (row_q >= BLOCK) == (lane_q >= HEAD)

    kp = kp_ref[...]
    kc = kc_ref[...]
    km = km_ref[...]
    vp = vp_ref[...]
    vc = vc_ref[...]
    vm = vm_ref[...]

    def pair_step(pr, q):
        q2 = jnp.where(q_keep, jnp.concatenate([q, q], axis=0), jnp.zeros((), q.dtype))
        qk_prev = _dot_nt(q2, kp)
        qk_cur = _dot_nt(q2, kc)
        qk_meta = _dot_nt(q2, km)
        yield
        head1 = (kvh * group + 2 * pr + 1).astype(F32)
        slope = jnp.exp2((jnp.where(second, 1.0, 0.0) + head1) * (-8.0 / n_heads))
        sink = jnp.where(second[:, :1], sk_ref[kvh, 2 * pr + 1], sk_ref[kvh, 2 * pr])
        s_band = jnp.where(valid_band, jnp.where(use_prev, qk_prev, qk_cur) - slope * dist_band, NEG)
        s_meta = jnp.where(valid_meta, qk_meta - slope * dist_meta, NEG)
        mx = jnp.maximum(jnp.max(jnp.maximum(s_band, s_meta), axis=1, keepdims=True), sink)
        e_band = jnp.exp(s_band - mx)
        e_meta = jnp.exp(s_meta - mx)
        den = jnp.sum(e_band + e_meta, axis=1, keepdims=True) + jnp.exp(sink - mx)
        acc = (_dot(jnp.where(use_prev, e_band, 0.0).astype(BF16), vp)
               + _dot(jnp.where(use_prev, 0.0, e_band).astype(BF16), vc)
               + _dot(e_meta.astype(BF16), vm))
        yield
        acc = acc / den
        return jnp.where(lane_o >= HEAD, acc[BLOCK:], acc[:BLOCK]).astype(o_ref.dtype)

    lanes = [slice(pr * LANES, (pr + 1) * LANES) for pr in range(group // 2)]
    results = _run_interleaved([pair_step(pr, q_ref[:, sl]) for pr, sl in enumerate(lanes)])
    for pr, sl in enumerate(lanes):
        o_ref[:, sl] = results[pr]


def _attention(q, kv_dup, kv_meta_dup, sinks, *, n_heads):
    s_len, d = q.shape
    group = n_heads // ATT_KV_HEADS
    gw = group * HEAD
    nb = s_len // BLOCK
    cur = lambda h, n: (n, h)
    k_prev = lambda h, n: (jnp.maximum(n - 1, 0), h)
    k_meta = lambda h, n: (0, h)
    v_prev = lambda h, n: (jnp.maximum(n - 1, 0), ATT_KV_HEADS + h)
    v_cur = lambda h, n: (n, ATT_KV_HEADS + h)
    v_meta = lambda h, n: (0, ATT_KV_HEADS + h)
    tile = lambda index_map: pl.BlockSpec((BLOCK, LANES), index_map)
    return pl.pallas_call(
        functools.partial(_attn_body, group=group, n_heads=n_heads),
        out_shape=jax.ShapeDtypeStruct((s_len, d), BF16),
        grid=(ATT_KV_HEADS, nb),
        in_specs=[pl.BlockSpec((BLOCK, gw), cur),
                  tile(k_prev), tile(cur), tile(k_meta), tile(v_prev), tile(v_cur), tile(v_meta),
                  pl.BlockSpec(memory_space=pltpu.SMEM)],
        out_specs=pl.BlockSpec((BLOCK, gw), cur),
        compiler_params=_cparams(("parallel", "parallel")),
        name="swa_attention",
    )(q, kv_dup, kv_dup, kv_meta_dup, kv_dup, kv_dup, kv_meta_dup, sinks)


def _router_body(h_ref, g_ref, wr_ref, hn_ref, sel_ref, gate_ref, *, n_experts):
    y = _rms(h_ref[...]) * g_ref[...]
    hn_ref[...] = y
    logits = jnp.dot(y, wr_ref[...], preferred_element_type=F32, precision=lax.Precision.HIGHEST)
    lane = lax.broadcasted_iota(jnp.int32, logits.shape, 1)
    logits = jnp.where(lane < n_experts, logits, NEG)
    m1 = jnp.max(logits, axis=1, keepdims=True)
    i1 = jnp.min(jnp.where(logits == m1, lane, LANES), axis=1, keepdims=True)
    rest = jnp.where(lane == i1, NEG, logits)
    m2 = jnp.max(rest, axis=1, keepdims=True)
    i2 = jnp.min(jnp.where(rest == m2, lane, LANES), axis=1, keepdims=True)
    e2 = jnp.exp(m2 - m1)
    g1 = 1.0 / (1.0 + e2)
    g2 = e2 / (1.0 + e2)
    sel_ref[...] = jnp.where(lane == 0, i1, jnp.where(lane == 1, i2, 0))
    gate_ref[...] = jnp.where(lane == 0, g1, jnp.where(lane == 1, g2, 0.0))


def _router(h, gain, w_router_pad, *, n_experts, tm):
    m, d = h.shape
    return pl.pallas_call(
        functools.partial(_router_body, n_experts=n_experts),
        out_shape=[jax.ShapeDtypeStruct((m, d), F32), jax.ShapeDtypeStruct((m, LANES), jnp.int32),
                   jax.ShapeDtypeStruct((m, LANES), F32)],
        grid=(m // tm,),
        in_specs=[pl.BlockSpec((tm, d), lambda i: (i, 0)),
                  pl.BlockSpec((1, d), lambda i: (0, 0)),
                  pl.BlockSpec((d, LANES), lambda i: (0, 0))],
        out_specs=[pl.BlockSpec((tm, d), lambda i: (i, 0)), pl.BlockSpec((tm, LANES), lambda i: (i, 0)),
                   pl.BlockSpec((tm, LANES), lambda i: (i, 0))],
        compiler_params=_cparams(("parallel",)),
        name="moe_router",
    )(h, gain, w_router_pad)


EXPERT_TILE = 256


def _row_copy(src_hbm, dst, src_row, dst_row, sem):
    return pltpu.make_async_copy(src_hbm.at[pl.ds(src_row, 1)], dst.at[pl.ds(dst_row, 1)], sem)


ROW_DMA_UNROLL = 8


def _gather_rows_body(src_ref, x_hbm, o_ref, buf, sem, *, rows):
    i = pl.program_id(0)

    def issue(tile, slot):
        base = tile * rows

        def body(r, carry):
            _row_copy(x_hbm, buf.at[slot], src_ref[base + r], r, sem.at[slot]).start()
            return carry

        lax.fori_loop(0, rows, body, 0, unroll=ROW_DMA_UNROLL)

    @pl.when(i == 0)
    def _():
        issue(0, 0)

    @pl.when(i + 1 < pl.num_programs(0))
    def _():
        issue(i + 1, (i + 1) % 2)

    slot = i % 2
    pltpu.make_async_copy(x_hbm.at[pl.ds(0, rows)], buf.at[slot], sem.at[slot]).wait()
    o_ref[...] = buf[slot].astype(o_ref.dtype)


def _gather_rows(row_src, x, n_rows, *, rows, out_dtype):
    d = x.shape[1]
    return pl.pallas_call(
        functools.partial(_gather_rows_body, rows=rows),
        out_shape=jax.ShapeDtypeStruct((n_rows, d), out_dtype),
        grid_spec=pltpu.PrefetchScalarGridSpec(
            num_scalar_prefetch=1,
            grid=(n_rows // rows,),
            in_specs=[pl.BlockSpec(memory_space=pl.ANY)],
            out_specs=pl.BlockSpec((rows, d), lambda i, src: (i, 0)),
            scratch_shapes=[pltpu.VMEM((2, rows, d), x.dtype), pltpu.SemaphoreType.DMA((2,))],
        ),
        compiler_params=_cparams(("arbitrary",)),
        name="moe_gather",
    )(row_src, x)


def _stage_expert_weights(te_ref, nxt_ref, w_hbms, wbuf, w16, sem, slot_ref, tn):
    j = pl.program_id(0)
    i = pl.program_id(1)

    def copies(expert, col_block, slot):
        cols = pl.ds(pl.multiple_of(col_block * tn, tn), tn)
        return [pltpu.make_async_copy(w.at[expert, :, cols], wbuf.at[slot, k], sem.at[slot])
                for k, w in enumerate(w_hbms)]

    @pl.when((j == 0) & (i == 0))
    def _():
        slot_ref[0] = 0
        for c in copies(te_ref[0], 0, 0):
            c.start()

    @pl.when((i == 0) | (te_ref[i] != te_ref[jnp.maximum(i - 1, 0)]))
    def _():
        slot = slot_ref[0]
        for c in copies(te_ref[i], j, slot):
            c.wait()
        next_expert = nxt_ref[i]
        wraps = next_expert < 0

        @pl.when(jnp.logical_not(wraps & (j == pl.num_programs(0) - 1)))
        def _():
            for c in copies(jnp.where(wraps, te_ref[0], next_expert), jnp.where(wraps, j + 1, j), 1 - slot):
                c.start()

        for k in range(len(w_hbms)):
            w16[k] = wbuf[slot, k].astype(BF16)
        slot_ref[0] = 1 - slot


def _moe_swiglu_body(te_ref, nxt_ref, used_ref, x_ref, wg_hbm, wu_hbm, o_ref, wbuf, w16, sem, slot_ref, *, tn):
    _stage_expert_weights(te_ref, nxt_ref, (wg_hbm, wu_hbm), wbuf, w16, sem, slot_ref, tn)
    has_tokens = pl.program_id(1) < used_ref[0]

    @pl.when(has_tokens)
    def _():
        x = x_ref[...]
        g = _dot(x, w16[0])
        u = _dot(x, w16[1])
        o_ref[...] = (g * jax.nn.sigmoid(g) * u).astype(o_ref.dtype)

    @pl.when(jnp.logical_not(has_tokens))
    def _():
        o_ref[...] = jnp.zeros_like(o_ref)


def _moe_down_body(te_ref, nxt_ref, used_ref, x_ref, w_hbm, o_ref, wbuf, w16, sem, slot_ref, *, tn):
    _stage_expert_weights(te_ref, nxt_ref, (w_hbm,), wbuf, w16, sem, slot_ref, tn)
    has_tokens = pl.program_id(1) < used_ref[0]

    @pl.when(has_tokens)
    def _():
        o_ref[...] = _dot(x_ref[...], w16[0])

    @pl.when(jnp.logical_not(has_tokens))
    def _():
        o_ref[...] = jnp.zeros_like(o_ref)


def _moe_matmul(body, tile_expert, next_expert, tiles_used, xs, weights, *, tn, out_dtype, name):
    rows, k = xs.shape
    n = weights[0].shape[-1]
    n_w = len(weights)
    return pl.pallas_call(
        functools.partial(body, tn=tn),
        out_shape=jax.ShapeDtypeStruct((rows, n), out_dtype),
        grid_spec=pltpu.PrefetchScalarGridSpec(
            num_scalar_prefetch=3,
            grid=(n // tn, rows // EXPERT_TILE),
            in_specs=[pl.BlockSpec((EXPERT_TILE, k), lambda j, i, te, nxt, used: (i, 0))]
            + [pl.BlockSpec(memory_space=pl.ANY)] * n_w,
            out_specs=pl.BlockSpec((EXPERT_TILE, tn), lambda j, i, te, nxt, used: (i, j)),
            scratch_shapes=[pltpu.VMEM((2, n_w, k, tn), F32), pltpu.VMEM((n_w, k, tn), BF16),
                            pltpu.SemaphoreType.DMA((2,)), pltpu.SMEM((1,), jnp.int32)],
        ),
        compiler_params=_cparams(("arbitrary", "arbitrary")),
        name=name,
    )(tile_expert, next_expert, tiles_used, xs, *weights)


def _moe_combine_body(pos_ref, h_ref, gate_ref, fn_ref, eo_hbm, o_ref, buf, sem, *, tm):
    i = pl.program_id(0)

    def issue(tile, slot):
        base = tile * tm

        def body(t, carry):
            for j in range(TOP_K):
                _row_copy(eo_hbm, buf.at[slot, j], pos_ref[TOP_K * (base + t) + j], t, sem.at[slot]).start()
            return carry

        lax.fori_loop(0, tm, body, 0, unroll=ROW_DMA_UNROLL // TOP_K)

    @pl.when(i == 0)
    def _():
        issue(0, 0)

    @pl.when(i + 1 < pl.num_programs(0))
    def _():
        issue(i + 1, (i + 1) % 2)

    slot = i % 2
    for j in range(TOP_K):
        pltpu.make_async_copy(eo_hbm.at[pl.ds(0, tm)], buf.at[slot, j], sem.at[slot]).wait()
    y = h_ref[...]
    for j in range(TOP_K):
        y = y + gate_ref[:, j:j + 1] * buf[slot, j]
    o_ref[...] = _rms(y) * fn_ref[...]


def _moe_combine(pos, h, gate, final_gain, eo, *, tm):
    m, d = h.shape
    return pl.pallas_call(
        functools.partial(_moe_combine_body, tm=tm),
        out_shape=jax.ShapeDtypeStruct((m, d), F32),
        grid_spec=pltpu.PrefetchScalarGridSpec(
            num_scalar_prefetch=1,
            grid=(m // tm,),
            in_specs=[pl.BlockSpec((tm, d), lambda i, pos: (i, 0)),
                      pl.BlockSpec((tm, LANES), lambda i, pos: (i, 0)),
                      pl.BlockSpec((1, d), lambda i, pos: (0, 0)),
                      pl.BlockSpec(memory_space=pl.ANY)],
            out_specs=pl.BlockSpec((tm, d), lambda i, pos: (i, 0)),
            scratch_shapes=[pltpu.VMEM((2, TOP_K, tm, d), F32), pltpu.SemaphoreType.DMA((2,))],
        ),
        compiler_params=_cparams(("arbitrary",)),
        name="moe_combine",
    )(pos, h, gate, final_gain, eo)


def _moe_plan(sel, n_experts):
    m = sel.shape[0]
    flat_e = sel[:, :TOP_K].reshape(-1)
    onehot = (flat_e[:, None] == jnp.arange(n_experts, dtype=jnp.int32)[None, :]).astype(jnp.int32)
    csum = jnp.cumsum(onehot, axis=0)
    rank = jnp.sum(csum * onehot, axis=1) - 1
    counts = csum[-1]
    padded = (counts + EXPERT_TILE - 1) // EXPERT_TILE * EXPERT_TILE
    ends = jnp.cumsum(padded)
    starts = ends - padded
    pos = (jnp.sum(onehot * starts[None, :], axis=1) + rank).astype(jnp.int32)
    n_rows = TOP_K * m + n_experts * EXPERT_TILE
    token = jnp.arange(TOP_K * m, dtype=jnp.int32) // TOP_K
    row_src = jnp.zeros((n_rows,), jnp.int32).at[pos].set(token)
    tile_start = jnp.arange(n_rows // EXPERT_TILE, dtype=jnp.int32) * EXPERT_TILE
    tile_expert = jnp.minimum(jnp.sum((tile_start[:, None] >= ends[None, :]).astype(jnp.int32), axis=1),
                              n_experts - 1).astype(jnp.int32)
    run_end = jnp.sum((tile_expert[None, :] <= tile_expert[:, None]).astype(jnp.int32), axis=1)
    n_tiles = tile_expert.shape[0]
    next_expert = jnp.where(run_end < n_tiles, tile_expert[jnp.minimum(run_end, n_tiles - 1)], -1).astype(jnp.int32)
    tiles_used = (ends[-1:] // EXPERT_TILE).astype(jnp.int32)
    return pos, row_src, tile_expert, next_expert, tiles_used, n_rows


def _largest_tile(n, cap, mult):
    best = None
    t = mult
    while t <= min(n, cap):
        if n % t == 0:
            best = t
        t += mult
    assert best is not None, (n, cap, mult)
    return best


def _pad_cols(w, n_to):
    return jnp.pad(w, ((0, 0), (0, n_to - w.shape[1])))


def _pad_rows(w, n_to):
    return jnp.pad(w, ((0, n_to - w.shape[0]), (0, 0)))


def _round_up(n, m):
    return -(-n // m) * m


def kernel(x, meta_tokens, a_norm, a_mix, a_w_rkv, a_w0, a_w1, a_w2, a_a0, a_a1, a_a2, a_g1, a_g2, a_k_k, a_k_a, a_r_k, a_lnx_w, a_lnx_b, a_w_out, kv_norm, w_kv, b_norm, b_w_q, b_sinks, b_w_out, f_norm, d_w_gate, d_w_up, d_w_down, e_router, e_w_gate, e_w_up, e_w_down, final_norm):
    assert x.shape[0] == 1
    seq, d = x.shape[1], x.shape[2]
    n_heads = d // HEAD
    n_experts = e_router.shape[-1]
    xs = x[0]

    l_real = N_META + seq
    l_pad = _round_up(l_real, CHUNK)
    h = jnp.concatenate([meta_tokens.astype(F32), xs, jnp.zeros((l_pad - l_real, d), F32)], axis=0)
    tm0 = _largest_tile(l_pad, 768, 16)
    tn = _largest_tile(d, 512, LANES)

    xr, xw, xk, xv, xa, xg = _premix(h, a_norm[0:1], a_mix[0], _largest_tile(l_pad, 256, 16))
    r = _mm(xr, a_w_rkv[0, 0], out_dtype=F32, tm=tm0, tn=tn)
    k = _mm(xk, a_w_rkv[0, 1], out_dtype=F32, tm=tm0, tn=tn)
    v = _mm(xv, a_w_rkv[0, 2], out_dtype=F32, tm=tm0, tn=tn)
    tml = _largest_tile(l_pad, 384, 16)
    lw = _lora(xw, a_w1[0].astype(BF16), a_w2[0].astype(BF16), a_w0[0][None], mid="tanh", post="log_decay",
               out_dtype=F32, tm=tml)
    a = _lora(xa, a_a1[0].astype(BF16), a_a2[0].astype(BF16), a_a0[0][None], mid="none", post="sigmoid",
              out_dtype=F32, tm=tml)
    gate_rank = _round_up(a_g1.shape[-1], LANES)
    g = _lora(xg, _pad_cols(a_g1[0], gate_rank).astype(BF16), _pad_rows(a_g2[0], gate_rank).astype(BF16),
              jnp.zeros((1, d), F32), mid="sigmoid", post="none", out_dtype=BF16, tm=tml)
    prm = jnp.concatenate([a_k_k[0][None], a_k_a[0][None], a_r_k[0].reshape(1, d), a_lnx_w[0][None],
                           a_lnx_b[0][None], jnp.zeros((3, d), F32)], axis=0)
    pairs = 16 if n_heads % 32 == 0 else n_heads // 2
    mixed = _wkv(r, k, v, lw, a, g, prm, pairs=pairs)
    h = _mm(mixed, a_w_out[0], out_dtype=F32, tm=tm0, tn=tn, res=h)

    (hn,) = _rmsnorm(h, f_norm[0:1], [BF16], _largest_tile(l_pad, 256, 16))
    d_ff = d_w_gate.shape[-1]
    act = _swiglu(hn, d_w_gate[0], d_w_up[0], tm=_largest_tile(l_pad, 1408, 16), tn=_largest_tile(d_ff, 256, LANES))
    h = _mm_acc(act, d_w_down[0].astype(BF16), h, tm=tm0, tn=_largest_tile(d, 1024, LANES),
                tk=_largest_tile(d_ff, 6144, LANES))

    w_kv_dup = jnp.broadcast_to(w_kv.reshape(d, 2 * ATT_KV_HEADS, 1, HEAD),
                                (d, 2 * ATT_KV_HEADS, 2, HEAD)).reshape(d, 2 * ATT_KV_HEADS * LANES)
    tm1 = _largest_tile(seq, 1024, 16)
    (hkv_meta,) = _rmsnorm(h[:CHUNK], kv_norm[None], [BF16], CHUNK)
    kv_meta = _mm(hkv_meta, w_kv_dup, out_dtype=BF16, tm=CHUNK, tn=tn)
    kv_meta = _pad_rows(kv_meta[:N_META], BLOCK)
    h, hkv, hq = _drop_meta(h, jnp.stack([kv_norm, b_norm[0]]), seq, _largest_tile(seq, 256, N_META))
    kv_real = _mm(hkv, w_kv_dup, out_dtype=BF16, tm=tm1, tn=tn)

    q = _mm(hq, b_w_q[0], out_dtype=BF16, tm=tm1, tn=tn, scale=HEAD ** -0.5)
    o = _attention(q, kv_real, kv_meta, b_sinks[0].reshape(ATT_KV_HEADS, -1), n_heads=n_heads)
    h = _mm(o, b_w_out[0], out_dtype=F32, tm=tm1, tn=tn, res=h)

    hn, sel, gate = _router(h, f_norm[1:2], _pad_cols(e_router[0], LANES), n_experts=n_experts,
                            tm=_largest_tile(seq, 256, 16))
    pos, row_src, tile_expert, next_expert, tiles_used, n_rows = _moe_plan(sel, n_experts)
    xs_sorted = _gather_rows(row_src, hn, n_rows, rows=_largest_tile(n_rows, 512, EXPERT_TILE), out_dtype=BF16)
    d_exp = e_w_gate.shape[-1]
    act = _moe_matmul(_moe_swiglu_body, tile_expert, next_expert, tiles_used, xs_sorted, (e_w_gate[0], e_w_up[0]),
                      tn=_largest_tile(d_exp, 512, LANES), out_dtype=BF16, name="moe_swiglu")
    eo = _moe_matmul(_moe_down_body, tile_expert, next_expert, tiles_used, act, (e_w_down[0],),
                     tn=_largest_tile(d, 1024, LANES), out_dtype=F32, name="moe_down")
    out = _moe_combine(pos, h, gate, final_norm[None], eo, tm=_largest_tile(seq, 128, 8))
    return out[None]
```

```python
import functools

import jax
import jax.numpy as jnp
from jax import lax
from jax.experimental import pallas as pl
from jax.experimental.pallas import tpu as pltpu

F32 = jnp.float32
BF16 = jnp.bfloat16

LANES = 128
VMEM_LIMIT_BYTES = 56 * 1024 * 1024

N_META = 16
RMS_EPS = 1e-5
LN_X_EPS = 64e-5
HEAD = 64
ATT_KV_HEADS = 8
WINDOW = 128
BLOCK = 128
TOP_K = 2
CHUNK = 64
NEG = -1e30


def _cparams(sem):
    return pltpu.CompilerParams(dimension_semantics=sem, vmem_limit_bytes=VMEM_LIMIT_BYTES)


def _dot(a, b):
    return jnp.dot(a, b, preferred_element_type=F32)


def _dot_nt(a, b):
    return lax.dot_general(a, b, (((1,), (1,)), ((), ())), preferred_element_type=F32)


def _rms(x):
    return x * lax.rsqrt(jnp.mean(x * x, axis=-1, keepdims=True) + RMS_EPS)


CAST_ROWS = 256


def _round_to_bf16(src_ref, dst_ref):
    rows = src_ref.shape[0]
    chunk = CAST_ROWS if rows % CAST_ROWS == 0 else rows

    def body(c, carry):
        r = pl.ds(pl.multiple_of(c * chunk, chunk), chunk)
        dst_ref[r, :] = src_ref[r, :].astype(BF16)
        return carry

    lax.fori_loop(0, rows // chunk, body, 0)


def _run_interleaved(steps, shared=None):
    results = [None] * len(steps)
    pending = list(range(len(steps)))
    inbox = {idx: None for idx in pending}
    while pending:
        requests = {}
        for idx in list(pending):
            try:
                req = steps[idx].send(inbox[idx])
            except StopIteration as done:
                results[idx] = done.value
                pending.remove(idx)
                continue
            inbox[idx] = None
            if req is not None:
                requests.setdefault(req[0], []).append((idx, req[1]))
        for key, members in requests.items():
            out = shared[key](jnp.concatenate([rows for _, rows in members], axis=0))
            start = 0
            for idx, rows in members:
                inbox[idx] = out[start:start + rows.shape[0]]
                start += rows.shape[0]
    return results


def _rmsnorm_body(h_ref, g_ref, *o_refs):
    y = _rms(h_ref[...])
    for j, o_ref in enumerate(o_refs):
        o_ref[...] = (y * g_ref[j:j + 1, :]).astype(o_ref.dtype)


def _rmsnorm(h, gains, out_dtypes, tm):
    m, d = h.shape
    n_out = len(out_dtypes)
    outs = pl.pallas_call(
        _rmsnorm_body,
        out_shape=[jax.ShapeDtypeStruct((m, d), dt) for dt in out_dtypes],
        grid=(m // tm,),
        in_specs=[pl.BlockSpec((tm, d), lambda i: (i, 0)),
                  pl.BlockSpec((n_out, d), lambda i: (0, 0))],
        out_specs=[pl.BlockSpec((tm, d), lambda i: (i, 0)) for _ in out_dtypes],
        compiler_params=_cparams(("parallel",)),
        name="rmsnorm",
    )(h, gains)
    return outs


def _drop_meta_body(ha_ref, hb_ref, g_ref, h_ref, *o_refs):
    rows = jnp.concatenate([ha_ref[N_META:, :], hb_ref[...]], axis=0)
    h_ref[...] = rows
    y = _rms(rows)
    for j, o_ref in enumerate(o_refs):
        o_ref[...] = (y * g_ref[j:j + 1, :]).astype(o_ref.dtype)


def _drop_meta(h, gains, seq, tm):
    d = h.shape[1]
    n_out = gains.shape[0]
    per = tm // N_META
    return pl.pallas_call(
        _drop_meta_body,
        out_shape=[jax.ShapeDtypeStruct((seq, d), F32)] + [jax.ShapeDtypeStruct((seq, d), BF16)] * n_out,
        grid=(seq // tm,),
        in_specs=[pl.BlockSpec((tm, d), lambda i: (i, 0)),
                  pl.BlockSpec((N_META, d), lambda i: ((i + 1) * per, 0)),
                  pl.BlockSpec((n_out, d), lambda i: (0, 0))],
        out_specs=[pl.BlockSpec((tm, d), lambda i: (i, 0))] * (1 + n_out),
        compiler_params=_cparams(("parallel",)),
        name="drop_meta",
    )(h, h, gains)


def _premix_body(h_ref, hp_ref, g_ref, mix_ref, *o_refs):
    i = pl.program_id(0)
    g = g_ref[...]
    xn = _rms(h_ref[...]) * g
    pn = _rms(hp_ref[...]) * g
    prev_row = jnp.where(i > 0, pn[7:8, :], 0.0)
    sh = pltpu.roll(xn, 1, axis=0)
    row = lax.broadcasted_iota(jnp.int32, xn.shape, 0)
    sh = jnp.where(row == 0, prev_row, sh)
    xx = sh - xn
    for j, o_ref in enumerate(o_refs):
        o_ref[...] = (xn + xx * mix_ref[j:j + 1, :]).astype(o_ref.dtype)


def _premix(h, gain, mix, tm):
    m, d = h.shape
    n_mix = mix.shape[0]
    rows8 = tm // 8
    return pl.pallas_call(
        _premix_body,
        out_shape=[jax.ShapeDtypeStruct((m, d), BF16) for _ in range(n_mix)],
        grid=(m // tm,),
        in_specs=[pl.BlockSpec((tm, d), lambda i: (i, 0)),
                  pl.BlockSpec((8, d), lambda i: (jnp.maximum(i * rows8 - 1, 0), 0)),
                  pl.BlockSpec((1, d), lambda i: (0, 0)),
                  pl.BlockSpec((n_mix, d), lambda i: (0, 0))],
        out_specs=[pl.BlockSpec((tm, d), lambda i: (i, 0)) for _ in range(n_mix)],
        compiler_params=_cparams(("parallel",)),
        name="premix",
    )(h, h, gain, mix)


def _mm_body(x_ref, w_ref, *rest, has_res, scale):
    if has_res:
        res_ref, o_ref, w16_ref = rest
    else:
        o_ref, w16_ref = rest

    @pl.when(pl.program_id(1) == 0)
    def _():
        _round_to_bf16(w_ref, w16_ref)

    acc = _dot(x_ref[...], w16_ref[...])
    if scale is not None:
        acc = acc * scale
    if has_res:
        acc = acc + res_ref[...]
    o_ref[...] = acc.astype(o_ref.dtype)


def _mm(x, w, *, out_dtype, tm, tn, res=None, scale=None, w_index=None):
    m, k = x.shape
    n = w.shape[-1]
    if w_index is None:
        w_spec = pl.BlockSpec((k, tn), lambda j, i: (0, j))
    else:
        w_spec = pl.BlockSpec((None, k, tn), lambda j, i: (w_index, 0, j))
    in_specs = [pl.BlockSpec((tm, k), lambda j, i: (i, 0)), w_spec]
    args = [x, w]
    if res is not None:
        in_specs.append(pl.BlockSpec((tm, tn), lambda j, i: (i, j)))
        args.append(res)
    return pl.pallas_call(
        functools.partial(_mm_body, has_res=res is not None, scale=scale),
        out_shape=jax.ShapeDtypeStruct((m, n), out_dtype),
        grid=(n // tn, m // tm),
        in_specs=in_specs,
        out_specs=pl.BlockSpec((tm, tn), lambda j, i: (i, j)),
        scratch_shapes=[pltpu.VMEM((k, tn), BF16)],
        compiler_params=_cparams(("parallel", "arbitrary")),
        name="matmul",
    )(*args)


def _mm_acc_body(x_ref, w_ref, res_ref, o_ref, acc_ref):
    kk = pl.program_id(2)

    @pl.when(kk == 0)
    def _():
        acc_ref[...] = jnp.zeros_like(acc_ref)

    acc_ref[...] += _dot(x_ref[...], w_ref[...])

    @pl.when(kk == pl.num_programs(2) - 1)
    def _():
        o_ref[...] = res_ref[...] + acc_ref[...]


def _mm_acc(x, w, res, *, tm, tn, tk):
    m, k = x.shape
    n = w.shape[1]
    return pl.pallas_call(
        _mm_acc_body,
        out_shape=jax.ShapeDtypeStruct((m, n), F32),
        grid=(n // tn, m // tm, k // tk),
        in_specs=[pl.BlockSpec((tm, tk), lambda j, i, q: (i, q)),
                  pl.BlockSpec((tk, tn), lambda j, i, q: (q, j)),
                  pl.BlockSpec((tm, tn), lambda j, i, q: (i, j))],
        out_specs=pl.BlockSpec((tm, tn), lambda j, i, q: (i, j)),
        scratch_shapes=[pltpu.VMEM((tm, tn), F32)],
        compiler_params=_cparams(("parallel", "parallel", "arbitrary")),
        name="matmul_acc",
    )(x, w, res)


def _swiglu_body(x_ref, wg_ref, wu_ref, o_ref, wg16_ref, wu16_ref):
    @pl.when(pl.program_id(1) == 0)
    def _():
        _round_to_bf16(wg_ref, wg16_ref)
        _round_to_bf16(wu_ref, wu16_ref)

    x = x_ref[...]
    g = _dot(x, wg16_ref[...])
    u = _dot(x, wu16_ref[...])
    o_ref[...] = (g * jax.nn.sigmoid(g) * u).astype(o_ref.dtype)


def _swiglu(x, wg, wu, *, tm, tn):
    m, k = x.shape
    n = wg.shape[1]
    return pl.pallas_call(
        _swiglu_body,
        out_shape=jax.ShapeDtypeStruct((m, n), BF16),
        grid=(n // tn, m // tm),
        in_specs=[pl.BlockSpec((tm, k), lambda j, i: (i, 0)),
                  pl.BlockSpec((k, tn), lambda j, i: (0, j)),
                  pl.BlockSpec((k, tn), lambda j, i: (0, j))],
        out_specs=pl.BlockSpec((tm, tn), lambda j, i: (i, j)),
        scratch_shapes=[pltpu.VMEM((k, tn), BF16), pltpu.VMEM((k, tn), BF16)],
        compiler_params=_cparams(("parallel", "arbitrary")),
        name="swiglu",
    )(x, wg, wu)


def _lora_body(x_ref, w1_ref, w2_ref, b_ref, o_ref, *, mid, post):
    t = _dot(x_ref[...], w1_ref[...])
    if mid == "tanh":
        t = jnp.tanh(t)
    elif mid == "sigmoid":
        t = jax.nn.sigmoid(t)
    z = _dot(t.astype(BF16), w2_ref[...]) + b_ref[...]
    if post == "log_decay":
        z = jax.nn.sigmoid(z) * (-0.6065306597126334)
    elif post == "sigmoid":
        z = jax.nn.sigmoid(z)
    o_ref[...] = z.astype(o_ref.dtype)


def _lora(x, w1, w2, bias, *, mid, post, out_dtype, tm):
    m, k = x.shape
    r = w1.shape[1]
    n = w2.shape[1]
    return pl.pallas_call(
        functools.partial(_lora_body, mid=mid, post=post),
        out_shape=jax.ShapeDtypeStruct((m, n), out_dtype),
        grid=(m // tm,),
        in_specs=[pl.BlockSpec((tm, k), lambda i: (i, 0)),
                  pl.BlockSpec((k, r), lambda i: (0, 0)),
                  pl.BlockSpec((r, n), lambda i: (0, 0)),
                  pl.BlockSpec((1, n), lambda i: (0, 0))],
        out_specs=pl.BlockSpec((tm, n), lambda i: (i, 0)),
        compiler_params=_cparams(("parallel",)),
        name="lora",
    )(x, w1, w2, bias)


def _split2(x):
    hi = x.astype(BF16)
    lo = (x - hi.astype(F32)).astype(BF16)
    return hi, lo


def _wkv_body(r_ref, k_ref, v_ref, lw_ref, a_ref, g_ref, prm_ref, o_ref, st_ref, *, pairs):
    c_idx = pl.program_id(1)

    @pl.when(c_idx == 0)
    def _():
        st_ref[...] = jnp.zeros_like(st_ref)

    C = CHUNK
    lane = lax.broadcasted_iota(jnp.int32, (C, LANES), 1)
    row = lax.broadcasted_iota(jnp.int32, (C, LANES), 0)
    col = lane % HEAD
    upper_half = lane >= HEAD
    tri_strict = col < row
    tri_incl = col <= row
    eye_pair = (col == row).astype(F32)

    def level_mask(bs):
        return ((row // bs) == (col // bs)) & ((row // (bs // 2)) != (col // (bs // 2)))

    row2 = lax.broadcasted_iota(jnp.int32, (2 * C, LANES), 0)
    lane2 = lax.broadcasted_iota(jnp.int32, (2 * C, LANES), 1)
    bd_mask = (row2 >= C) == (lane2 >= HEAD)
    ones_bd = bd_mask.astype(BF16)
    diag_mask = row2 == lane2

    def bd(x):
        return jnp.concatenate([jnp.where(upper_half, 0.0, x), jnp.where(upper_half, x, 0.0)], axis=0)

    def headsum(x16):
        return _dot(x16, ones_bd)

    zeros_cl = jnp.zeros((C, LANES), BF16)

    def pair_step(r, k, v, lw, a, g, prm, s):
        k_k, k_a, r_k, lnx_w, lnx_b = (prm[j:j + 1] for j in range(5))

        kk = k * k_k
        k2 = k * (1.0 + (a - 1.0) * k_a)
        sums = yield ("headsum", jnp.concatenate([kk * kk, r * k2 * r_k], axis=0).astype(BF16))
        cum = lw
        shift = 1
        while shift < C:
            cum = cum + jnp.where(row >= shift, pltpu.roll(cum, shift, axis=0), 0.0)
            shift *= 2
        kkn = kk * lax.rsqrt(jnp.maximum(sums[:C], 1e-24))
        bonus = sums[C:] * v
        avec = -kkn
        bvec = kkn * a
        cum_end = cum[C - 1:C, :]
        p_incl = jnp.exp(cum)
        p_inv = jnp.exp(-cum)
        a_t = avec * jnp.exp(cum - lw)
        r_t = r * p_incl
        b_t = bvec * p_inv
        k_t = k2 * p_inv
        p_end = jnp.exp(cum_end - cum)
        b_h = bvec * p_end
        k_h = k2 * p_end

        v16 = v.astype(BF16)
        gram = _dot_nt(jnp.concatenate([a_t, r_t], axis=0).astype(BF16),
                       jnp.concatenate([bd(b_t), bd(k_t)], axis=0).astype(BF16))
        yield
        a_ab = jnp.where(tri_strict, gram[:C, :LANES], 0.0)
        a_ak = jnp.where(tri_strict, gram[:C, LANES:], 0.0)
        a_rb = jnp.where(tri_incl, gram[C:, :LANES], 0.0)
        a_rk = jnp.where(tri_incl, gram[C:, LANES:], 0.0)
        akv = _dot(a_ak.astype(BF16), bd(v16))

        t_inv = eye_pair + jnp.where(level_mask(2), a_ab, 0.0)
        bs = 4
        while bs <= C:
            x = jnp.where(level_mask(bs), a_ab, 0.0)
            t16 = t_inv.astype(BF16)
            tx = _dot(t16, bd(x).astype(BF16))
            yield
            t_inv = t_inv + _dot(tx.astype(BF16), bd(t16))
            yield
            bs *= 2
        t16 = t_inv.astype(BF16)

        tt = _dot(t16, jnp.concatenate([bd(a_t.astype(BF16)), bd(akv.astype(BF16))], axis=1))
        yield
        a_hat = tt[:, :LANES].astype(BF16)
        u0 = tt[:, LANES:].astype(BF16)

        a_rb16 = a_rb.astype(BF16)
        qa = _dot(a_rb16, bd(a_hat))
        y0 = _dot(jnp.concatenate([a_rb16, a_rk.astype(BF16)], axis=1),
                  jnp.concatenate([bd(u0), bd(v16)], axis=0))
        lhs_t = jnp.concatenate([b_h, k_h], axis=0).T.astype(BF16)
        rhs = jnp.concatenate([jnp.concatenate([a_hat, u0], axis=1),
                               jnp.concatenate([zeros_cl, v16], axis=1)], axis=0)
        upd = _dot(lhs_t, rhs)
        yield
        q_hat = r_t + qa
        m_off = jnp.where(bd_mask, upd[:, :LANES], 0.0)
        n_new = jnp.where(bd_mask, upd[:, LANES:], 0.0)
        p_col = jnp.sum(jnp.where(diag_mask, jnp.exp(cum_end), 0.0), axis=1, keepdims=True)

        s_hi, s_lo = _split2(s)
        m16 = m_off.astype(BF16)
        ys = _dot(jnp.concatenate([q_hat.astype(BF16), m16], axis=0), s_hi)
        y = ys[:C] + y0
        s_new = p_col * s + ys[C:] + _dot(m16, s_lo) + n_new

        mu = (yield ("headsum", y.astype(BF16))) * (1.0 / HEAD)
        d = y - mu
        var = (yield ("headsum", (d * d).astype(BF16))) * (1.0 / HEAD)
        yn = d * lax.rsqrt(var + LN_X_EPS) * lnx_w + lnx_b
        return ((yn + bonus) * g.astype(F32)).astype(o_ref.dtype), s_new

    lanes = [slice(p * LANES, (p + 1) * LANES) for p in range(pairs)]
    results = _run_interleaved([pair_step(r_ref[:, sl], k_ref[:, sl], v_ref[:, sl], lw_ref[:, sl], a_ref[:, sl],
                                          g_ref[:, sl], prm_ref[:, sl], st_ref[p]) for p, sl in enumerate(lanes)],
                               shared={"headsum": headsum})
    for p, sl in enumerate(lanes):
        o_ref[:, sl] = results[p][0]
        st_ref[p] = results[p][1]


def _wkv(r, k, v, lw, a, g, prm, *, pairs):
    length, d = r.shape
    width = pairs * LANES
    blk = lambda hb, c: (c, hb)
    return pl.pallas_call(
        functools.partial(_wkv_body, pairs=pairs),
        out_shape=jax.ShapeDtypeStruct((length, d), BF16),
        grid=(d // width, length // CHUNK),
        in_specs=[pl.BlockSpec((CHUNK, width), blk)] * 6 + [pl.BlockSpec((8, width), lambda hb, c: (0, hb))],
        out_specs=pl.BlockSpec((CHUNK, width), blk),
        scratch_shapes=[pltpu.VMEM((pairs, LANES, LANES), F32)],
        compiler_params=_cparams(("parallel", "arbitrary")),
        name="wkv7",
    )(r, k, v, lw, a, g, prm)


def _attn_body(q_ref, kp_ref, kc_ref, km_ref, vp_ref, vc_ref, vm_ref, sk_ref, o_ref, *, group, n_heads):
    kvh = pl.program_id(0)
    n = pl.program_id(1)
    rows = 2 * BLOCK
    rowi = lax.broadcasted_iota(jnp.int32, (rows, BLOCK), 0)
    kj = lax.broadcasted_iota(jnp.int32, (rows, BLOCK), 1)
    qi = rowi % BLOCK
    second = rowi >= BLOCK
    use_prev = kj > qi
    dist_band = (qi - kj + jnp.where(use_prev, BLOCK, 0)).astype(F32)
    valid_band = jnp.logical_not(use_prev & (n == 0))
    valid_meta = kj < N_META
    dist_meta = (N_META + n * BLOCK + qi - kj).astype(F32)
    lane_o = lax.broadcasted_iota(jnp.int32, (BLOCK, LANES), 1)
    lane_q = lax.broadcasted_iota(jnp.int32, (rows, LANES), 1)
    row_q = lax.broadcasted_iota(jnp.int32, (rows, LANES), 0)
    q_keep = (row_q >= BLOCK) == (lane_q >= HEAD)

    kp = kp_ref[...]
    kc = kc_ref[...]
    km = km_ref[...]
    vp = vp_ref[...]
    vc = vc_ref[...]
    vm = vm_ref[...]

    def pair_step(pr, q):
        q2 = jnp.where(q_keep, jnp.concatenate([q, q], axis=0), jnp.zeros((), q.dtype))
        qk_prev = _dot_nt(q2, kp)
        qk_cur = _dot_nt(q2, kc)
        qk_meta = _dot_nt(q2, km)
        yield
        head1 = (kvh * group + 2 * pr + 1).astype(F32)
        slope = jnp.exp2((jnp.where(second, 1.0, 0.0) + head1) * (-8.0 / n_heads))
        sink = jnp.where(second[:, :1], sk_ref[kvh, 2 * pr + 1], sk_ref[kvh, 2 * pr])
        s_band = jnp.where(valid_band, jnp.where(use_prev, qk_prev, qk_cur) - slope * dist_band, NEG)
        s_meta = jnp.where(valid_meta, qk_meta - slope * dist_meta, NEG)
        mx = jnp.maximum(jnp.max(jnp.maximum(s_band, s_meta), axis=1, keepdims=True), sink)
        e_band = jnp.exp(s_band - mx)
        e_meta = jnp.exp(s_meta - mx)
        den = jnp.sum(e_band + e_meta, axis=1, keepdims=True) + jnp.exp(sink - mx)
        acc = (_dot(jnp.where(use_prev, e_band, 0.0).astype(BF16), vp)
               + _dot(jnp.where(use_prev, 0.0, e_band).astype(BF16), vc)
               + _dot(e_meta.astype(BF16), vm))
        yield
        acc = acc / den
        return jnp.where(lane_o >= HEAD, acc[BLOCK:], acc[:BLOCK]).astype(o_ref.dtype)

    lanes = [slice(pr * LANES, (pr + 1) * LANES) for pr in range(group // 2)]
    results = _run_interleaved([pair_step(pr, q_ref[:, sl]) for pr, sl in enumerate(lanes)])
    for pr, sl in enumerate(lanes):
        o_ref[:, sl] = results[pr]


def _attention(q, kv_dup, kv_meta_dup, sinks, *, n_heads):
    s_len, d = q.shape
    group = n_heads // ATT_KV_HEADS
    gw = group * HEAD
    nb = s_len // BLOCK
    cur = lambda h, n: (n, h)
    k_prev = lambda h, n: (jnp.maximum(n - 1, 0), h)
    k_meta = lambda h, n: (0, h)
    v_prev = lambda h, n: (jnp.maximum(n - 1, 0), ATT_KV_HEADS + h)
    v_cur = lambda h, n: (n, ATT_KV_HEADS + h)
    v_meta = lambda h, n: (0, ATT_KV_HEADS + h)
    tile = lambda index_map: pl.BlockSpec((BLOCK, LANES), index_map)
    return pl.pallas_call(
        functools.partial(_attn_body, group=group, n_heads=n_heads),
        out_shape=jax.ShapeDtypeStruct((s_len, d), BF16),
        grid=(ATT_KV_HEADS, nb),
        in_specs=[pl.BlockSpec((BLOCK, gw), cur),
                  tile(k_prev), tile(cur), tile(k_meta), tile(v_prev), tile(v_cur), tile(v_meta),
                  pl.BlockSpec(memory_space=pltpu.SMEM)],
        out_specs=pl.BlockSpec((BLOCK, gw), cur),
        compiler_params=_cparams(("parallel", "parallel")),
        name="swa_attention",
    )(q, kv_dup, kv_dup, kv_meta_dup, kv_dup, kv_dup, kv_meta_dup, sinks)


def _router_body(h_ref, g_ref, wr_ref, hn_ref, sel_ref, gate_ref, *, n_experts):
    y = _rms(h_ref[...]) * g_ref[...]
    hn_ref[...] = y
    logits = jnp.dot(y, wr_ref[...], preferred_element_type=F32, precision=lax.Precision.HIGHEST)
    lane = lax.broadcasted_iota(jnp.int32, logits.shape, 1)
    logits = jnp.where(lane < n_experts, logits, NEG)
    m1 = jnp.max(logits, axis=1, keepdims=True)
    i1 = jnp.min(jnp.where(logits == m1, lane, LANES), axis=1, keepdims=True)
    rest = jnp.where(lane == i1, NEG, logits)
    m2 = jnp.max(rest, axis=1, keepdims=True)
    i2 = jnp.min(jnp.where(rest == m2, lane, LANES), axis=1, keepdims=True)
    e2 = jnp.exp(m2 - m1)
    g1 = 1.0 / (1.0 + e2)
    g2 = e2 / (1.0 + e2)
    sel_ref[...] = jnp.where(lane == 0, i1, jnp.where(lane == 1, i2, 0))
    gate_ref[...] = jnp.where(lane == 0, g1, jnp.where(lane == 1, g2, 0.0))


def _router(h, gain, w_router_pad, *, n_experts, tm):
    m, d = h.shape
    return pl.pallas_call(
        functools.partial(_router_body, n_experts=n_experts),
        out_shape=[jax.ShapeDtypeStruct((m, d), F32), jax.ShapeDtypeStruct((m, LANES), jnp.int32),
                   jax.ShapeDtypeStruct((m, LANES), F32)],
        grid=(m // tm,),
        in_specs=[pl.BlockSpec((tm, d), lambda i: (i, 0)),
                  pl.BlockSpec((1, d), lambda i: (0, 0)),
                  pl.BlockSpec((d, LANES), lambda i: (0, 0))],
        out_specs=[pl.BlockSpec((tm, d), lambda i: (i, 0)), pl.BlockSpec((tm, LANES), lambda i: (i, 0)),
                   pl.BlockSpec((tm, LANES), lambda i: (i, 0))],
        compiler_params=_cparams(("parallel",)),
        name="moe_router",
    )(h, gain, w_router_pad)


EXPERT_TILE = 256


def _row_copy(src_hbm, dst, src_row, dst_row, sem):
    return pltpu.make_async_copy(src_hbm.at[pl.ds(src_row, 1)], dst.at[pl.ds(dst_row, 1)], sem)


ROW_DMA_UNROLL = 8


def _gather_rows_body(src_ref, x_hbm, o_ref, buf, sem, *, rows):
    i = pl.program_id(0)

    def issue(tile, slot):
        base = tile * rows

        def body(r, carry):
            _row_copy(x_hbm, buf.at[slot], src_ref[base + r], r, sem.at[slot]).start()
            return carry

        lax.fori_loop(0, rows, body, 0, unroll=ROW_DMA_UNROLL)

    @pl.when(i == 0)
    def _():
        issue(0, 0)

    @pl.when(i + 1 < pl.num_programs(0))
    def _():
        issue(i + 1, (i + 1) % 2)

    slot = i % 2
    pltpu.make_async_copy(x_hbm.at[pl.ds(0, rows)], buf.at[slot], sem.at[slot]).wait()
    o_ref[...] = buf[slot].astype(o_ref.dtype)


def _gather_rows(row_src, x, n_rows, *, rows, out_dtype):
    d = x.shape[1]
    return pl.pallas_call(
        functools.partial(_gather_rows_body, rows=rows),
        out_shape=jax.ShapeDtypeStruct((n_rows, d), out_dtype),
        grid_spec=pltpu.PrefetchScalarGridSpec(
            num_scalar_prefetch=1,
            grid=(n_rows // rows,),
            in_specs=[pl.BlockSpec(memory_space=pl.ANY)],
            out_specs=pl.BlockSpec((rows, d), lambda i, src: (i, 0)),
            scratch_shapes=[pltpu.VMEM((2, rows, d), x.dtype), pltpu.SemaphoreType.DMA((2,))],
        ),
        compiler_params=_cparams(("arbitrary",)),
        name="moe_gather",
    )(row_src, x)


def _stage_expert_weights(te_ref, nxt_ref, w_hbms, wbuf, w16, sem, tn):
    j = pl.program_id(0)
    i = pl.program_id(1)

    def copies(expert, col_block):
        cols = pl.ds(pl.multiple_of(col_block * tn, tn), tn)
        return [pltpu.make_async_copy(w.at[expert, :, cols], wbuf.at[k], sem) for k, w in enumerate(w_hbms)]

    @pl.when((j == 0) & (i == 0))
    def _():
        for c in copies(te_ref[0], 0):
            c.start()

    @pl.when((i == 0) | (te_ref[i] != te_ref[jnp.maximum(i - 1, 0)]))
    def _():
        for c in copies(te_ref[i], j):
            c.wait()
        for k in range(len(w_hbms)):
            _round_to_bf16(wbuf.at[k], w16.at[k])
        next_expert = nxt_ref[i]
        wraps = next_expert < 0

        @pl.when(jnp.logical_not(wraps & (j == pl.num_programs(0) - 1)))
        def _():
            for c in copies(jnp.where(wraps, te_ref[0], next_expert), jnp.where(wraps, j + 1, j)):
                c.start()


def _moe_swiglu_body(te_ref, nxt_ref, used_ref, x_ref, wg_hbm, wu_hbm, o_ref, wbuf, w16, sem, *, tn):
    _stage_expert_weights(te_ref, nxt_ref, (wg_hbm, wu_hbm), wbuf, w16, sem, tn)
    has_tokens = pl.program_id(1) < used_ref[0]

    @pl.when(has_tokens)
    def _():
        x = x_ref[...]
        g = _dot(x, w16[0])
        u = _dot(x, w16[1])
        o_ref[...] = (g * jax.nn.sigmoid(g) * u).astype(o_ref.dtype)

    @pl.when(jnp.logical_not(has_tokens))
    def _():
        o_ref[...] = jnp.zeros_like(o_ref)


def _moe_down_body(te_ref, nxt_ref, used_ref, x_ref, w_hbm, o_ref, wbuf, w16, sem, *, tn):
    _stage_expert_weights(te_ref, nxt_ref, (w_hbm,), wbuf, w16, sem, tn)
    has_tokens = pl.program_id(1) < used_ref[0]

    @pl.when(has_tokens)
    def _():
        o_ref[...] = _dot(x_ref[...], w16[0])

    @pl.when(jnp.logical_not(has_tokens))
    def _():
        o_ref[...] = jnp.zeros_like(o_ref)


def _moe_matmul(body, tile_expert, next_expert, tiles_used, xs, weights, *, tn, out_dtype, name):
    rows, k = xs.shape
    n = weights[0].shape[-1]
    n_w = len(weights)
    return pl.pallas_call(
        functools.partial(body, tn=tn),
        out_shape=jax.ShapeDtypeStruct((rows, n), out_dtype),
        grid_spec=pltpu.PrefetchScalarGridSpec(
            num_scalar_prefetch=3,
            grid=(n // tn, rows // EXPERT_TILE),
            in_specs=[pl.BlockSpec((EXPERT_TILE, k), lambda j, i, te, nxt, used: (i, 0))]
            + [pl.BlockSpec(memory_space=pl.ANY)] * n_w,
            out_specs=pl.BlockSpec((EXPERT_TILE, tn), lambda j, i, te, nxt, used: (i, j)),
            scratch_shapes=[pltpu.VMEM((n_w, k, tn), F32), pltpu.VMEM((n_w, k, tn), BF16),
                            pltpu.SemaphoreType.DMA],
        ),
        compiler_params=_cparams(("arbitrary", "arbitrary")),
        name=name,
    )(tile_expert, next_expert, tiles_used, xs, *weights)


def _moe_combine_body(pos_ref, h_ref, gate_ref, fn_ref, eo_hbm, o_ref, buf, sem, *, tm):
    i = pl.program_id(0)

    def issue(tile, slot):
        base = tile * tm

        def body(t, carry):
            for j in range(TOP_K):
                _row_copy(eo_hbm, buf.at[slot, j], pos_ref[TOP_K * (base + t) + j], t, sem.at[slot]).start()
            return carry

        lax.fori_loop(0, tm, body, 0, unroll=ROW_DMA_UNROLL // TOP_K)

    @pl.when(i == 0)
    def _():
        issue(0, 0)

    @pl.when(i + 1 < pl.num_programs(0))
    def _():
        issue(i + 1, (i + 1) % 2)

    slot = i % 2
    for j in range(TOP_K):
        pltpu.make_async_copy(eo_hbm.at[pl.ds(0, tm)], buf.at[slot, j], sem.at[slot]).wait()
    y = h_ref[...]
    for j in range(TOP_K):
        y = y + gate_ref[:, j:j + 1] * buf[slot, j]
    o_ref[...] = _rms(y) * fn_ref[...]


def _moe_combine(pos, h, gate, final_gain, eo, *, tm):
    m, d = h.shape
    return pl.pallas_call(
        functools.partial(_moe_combine_body, tm=tm),
        out_shape=jax.ShapeDtypeStruct((m, d), F32),
        grid_spec=pltpu.PrefetchScalarGridSpec(
            num_scalar_prefetch=1,
            grid=(m // tm,),
            in_specs=[pl.BlockSpec((tm, d), lambda i, pos: (i, 0)),
                      pl.BlockSpec((tm, LANES), lambda i, pos: (i, 0)),
                      pl.BlockSpec((1, d), lambda i, pos: (0, 0)),
                      pl.BlockSpec(memory_space=pl.ANY)],
            out_specs=pl.BlockSpec((tm, d), lambda i, pos: (i, 0)),
            scratch_shapes=[pltpu.VMEM((2, TOP_K, tm, d), F32), pltpu.SemaphoreType.DMA((2,))],
        ),
        compiler_params=_cparams(("arbitrary",)),
        name="moe_combine",
    )(pos, h, gate, final_gain, eo)


def _moe_plan(sel, n_experts):
    m = sel.shape[0]
    flat_e = sel[:, :TOP_K].reshape(-1)
    onehot = (flat_e[:, None] == jnp.arange(n_experts, dtype=jnp.int32)[None, :]).astype(jnp.int32)
    csum = jnp.cumsum(onehot, axis=0)
    rank = jnp.sum(csum * onehot, axis=1) - 1
    counts = csum[-1]
    padded = (counts + EXPERT_TILE - 1) // EXPERT_TILE * EXPERT_TILE
    ends = jnp.cumsum(padded)
    starts = ends - padded
    pos = (jnp.sum(onehot * starts[None, :], axis=1) + rank).astype(jnp.int32)
    n_rows = TOP_K * m + n_experts * EXPERT_TILE
    token = jnp.arange(TOP_K * m, dtype=jnp.int32) // TOP_K
    row_src = jnp.zeros((n_rows,), jnp.int32).at[pos].set(token)
    tile_start = jnp.arange(n_rows // EXPERT_TILE, dtype=jnp.int32) * EXPERT_TILE
    tile_expert = jnp.minimum(jnp.sum((tile_start[:, None] >= ends[None, :]).astype(jnp.int32), axis=1),
                              n_experts - 1).astype(jnp.int32)
    run_end = jnp.sum((tile_expert[None, :] <= tile_expert[:, None]).astype(jnp.int32), axis=1)
    n_tiles = tile_expert.shape[0]
    next_expert = jnp.where(run_end < n_tiles, tile_expert[jnp.minimum(run_end, n_tiles - 1)], -1).astype(jnp.int32)
    tiles_used = (ends[-1:] // EXPERT_TILE).astype(jnp.int32)
    return pos, row_src, tile_expert, next_expert, tiles_used, n_rows


def _largest_tile(n, cap, mult):
    best = None
    t = mult
    while t <= min(n, cap):
        if n % t == 0:
            best = t
        t += mult
    assert best is not None, (n, cap, mult)
    return best


def _pad_cols(w, n_to):
    return jnp.pad(w, ((0, 0), (0, n_to - w.shape[1])))


def _pad_rows(w, n_to):
    return jnp.pad(w, ((0, n_to - w.shape[0]), (0, 0)))


def _round_up(n, m):
    return -(-n // m) * m


def kernel(x, meta_tokens, a_norm, a_mix, a_w_rkv, a_w0, a_w1, a_w2, a_a0, a_a1, a_a2, a_g1, a_g2, a_k_k, a_k_a, a_r_k, a_lnx_w, a_lnx_b, a_w_out, kv_norm, w_kv, b_norm, b_w_q, b_sinks, b_w_out, f_norm, d_w_gate, d_w_up, d_w_down, e_router, e_w_gate, e_w_up, e_w_down, final_norm):
    assert x.shape[0] == 1
    seq, d = x.shape[1], x.shape[2]
    n_heads = d // HEAD
    n_experts = e_router.shape[-1]
    xs = x[0]

    l_real = N_META + seq
    l_pad = _round_up(l_real, CHUNK)
    h = jnp.concatenate([meta_tokens.astype(F32), xs, jnp.zeros((l_pad - l_real, d), F32)], axis=0)
    tm0 = _largest_tile(l_pad, 768, 16)
    tn = _largest_tile(d, 512, LANES)

    xr, xw, xk, xv, xa, xg = _premix(h, a_norm[0:1], a_mix[0], _largest_tile(l_pad, 256, 16))
    r = _mm(xr, a_w_rkv[0], out_dtype=F32, tm=tm0, tn=tn, w_index=0)
    k = _mm(xk, a_w_rkv[0], out_dtype=F32, tm=tm0, tn=tn, w_index=1)
    v = _mm(xv, a_w_rkv[0], out_dtype=F32, tm=tm0, tn=tn, w_index=2)
    tml = _largest_tile(l_pad, 384, 16)
    lw = _lora(xw, a_w1[0].astype(BF16), a_w2[0].astype(BF16), a_w0[0][None], mid="tanh", post="log_decay",
               out_dtype=F32, tm=tml)
    a = _lora(xa, a_a1[0].astype(BF16), a_a2[0].astype(BF16), a_a0[0][None], mid="none", post="sigmoid",
              out_dtype=F32, tm=tml)
    gate_rank = _round_up(a_g1.shape[-1], LANES)
    g = _lora(xg, _pad_cols(a_g1[0], gate_rank).astype(BF16), _pad_rows(a_g2[0], gate_rank).astype(BF16),
              jnp.zeros((1, d), F32), mid="sigmoid", post="none", out_dtype=BF16, tm=tml)
    prm = jnp.concatenate([a_k_k[0][None], a_k_a[0][None], a_r_k[0].reshape(1, d), a_lnx_w[0][None],
                           a_lnx_b[0][None], jnp.zeros((3, d), F32)], axis=0)
    pairs = 16 if n_heads % 32 == 0 else n_heads // 2
    mixed = _wkv(r, k, v, lw, a, g, prm, pairs=pairs)
    h = _mm(mixed, a_w_out[0], out_dtype=F32, tm=tm0, tn=tn, res=h)

    (hn,) = _rmsnorm(h, f_norm[0:1], [BF16], _largest_tile(l_pad, 256, 16))
    d_ff = d_w_gate.shape[-1]
    act = _swiglu(hn, d_w_gate[0], d_w_up[0], tm=_largest_tile(l_pad, 1408, 16), tn=_largest_tile(d_ff, 256, LANES))
    h = _mm_acc(act, d_w_down[0].astype(BF16), h, tm=tm0, tn=_largest_tile(d, 1024, LANES),
                tk=_largest_tile(d_ff, 6144, LANES))

    w_kv_dup = jnp.broadcast_to(w_kv.reshape(d, 2 * ATT_KV_HEADS, 1, HEAD),
                                (d, 2 * ATT_KV_HEADS, 2, HEAD)).reshape(d, 2 * ATT_KV_HEADS * LANES)
    tm1 = _largest_tile(seq, 1024, 16)
    (hkv_meta,) = _rmsnorm(h[:CHUNK], kv_norm[None], [BF16], CHUNK)
    kv_meta = _mm(hkv_meta, w_kv_dup, out_dtype=BF16, tm=CHUNK, tn=tn)
    kv_meta = _pad_rows(kv_meta[:N_META], BLOCK)
    h, hkv, hq = _drop_meta(h, jnp.stack([kv_norm, b_norm[0]]), seq, _largest_tile(seq, 256, N_META))
    kv_real = _mm(hkv, w_kv_dup, out_dtype=BF16, tm=tm1, tn=tn)

    q = _mm(hq, b_w_q[0], out_dtype=BF16, tm=tm1, tn=tn, scale=HEAD ** -0.5)
    o = _attention(q, kv_real, kv_meta, b_sinks[0].reshape(ATT_KV_HEADS, -1), n_heads=n_heads)
    h = _mm(o, b_w_out[0], out_dtype=F32, tm=tm1, tn=tn, res=h)

    hn, sel, gate = _router(h, f_norm[1:2], _pad_cols(e_router[0], LANES), n_experts=n_experts,
                            tm=_largest_tile(seq, 256, 16))
    pos, row_src, tile_expert, next_expert, tiles_used, n_rows = _moe_plan(sel, n_experts)
    xs_sorted = _gather_rows(row_src, hn, n_rows, rows=_largest_tile(n_rows, 512, EXPERT_TILE), out_dtype=BF16)
    d_exp = e_w_gate.shape[-1]
    act = _moe_matmul(_moe_swiglu_body, tile_expert, next_expert, tiles_used, xs_sorted, (e_w_gate[0], e_w_up[0]),
                      tn=_largest_tile(d_exp, 896, LANES), out_dtype=BF16, name="moe_swiglu")
    eo = _moe_matmul(_moe_down_body, tile_expert, next_expert, tiles_used, act, (e_w_down[0],),
                     tn=_largest_tile(d, 1024, LANES), out_dtype=F32, name="moe_down")
    out = _moe_combine(pos, h, gate, final_norm[None], eo, tm=_largest_tile(seq, 128, 8))
    return out[None]
```

```python
import functools

import jax
import jax.numpy as jnp
from jax import lax
from jax.experimental import pallas as pl
from jax.experimental.pallas import tpu as pltpu

F32 = jnp.float32
BF16 = jnp.bfloat16

LANES = 128
VMEM_LIMIT_BYTES = 56 * 1024 * 1024

N_META = 16
RMS_EPS = 1e-5
LN_X_EPS = 64e-5
HEAD = 64
ATT_KV_HEADS = 8
WINDOW = 128
BLOCK = 128
TOP_K = 2
CHUNK = 64
NEG = -1e30


def _cparams(sem):
    return pltpu.CompilerParams(dimension_semantics=sem, vmem_limit_bytes=VMEM_LIMIT_BYTES)


def _dot(a, b):
    return jnp.dot(a, b, preferred_element_type=F32)


def _dot_nt(a, b):
    return lax.dot_general(a, b, (((1,), (1,)), ((), ())), preferred_element_type=F32)


def _rms(x):
    return x * lax.rsqrt(jnp.mean(x * x, axis=-1, keepdims=True) + RMS_EPS)


CAST_ROWS = 256


def _round_to_bf16(src_ref, dst_ref):
    rows = src_ref.shape[0]
    chunk = CAST_ROWS if rows % CAST_ROWS == 0 else rows

    def body(c, carry):
        r = pl.ds(pl.multiple_of(c * chunk, chunk), chunk)
        dst_ref[r, :] = src_ref[r, :].astype(BF16)
        return carry

    lax.fori_loop(0, rows // chunk, body, 0)


def _run_interleaved(steps, shared=None):
    results = [None] * len(steps)
    pending = list(range(len(steps)))
    inbox = {idx: None for idx in pending}
    while pending:
        requests = {}
        for idx in list(pending):
            try:
                req = steps[idx].send(inbox[idx])
            except StopIteration as done:
                results[idx] = done.value
                pending.remove(idx)
                continue
            inbox[idx] = None
            if req is not None:
                requests.setdefault(req[0], []).append((idx, req[1]))
        for key, members in requests.items():
            out = shared[key](jnp.concatenate([rows for _, rows in members], axis=0))
            start = 0
            for idx, rows in members:
                inbox[idx] = out[start:start + rows.shape[0]]
                start += rows.shape[0]
    return results


def _rmsnorm_body(h_ref, g_ref, *o_refs):
    y = _rms(h_ref[...])
    for j, o_ref in enumerate(o_refs):
        o_ref[...] = (y * g_ref[j:j + 1, :]).astype(o_ref.dtype)


def _rmsnorm(h, gains, out_dtypes, tm):
    m, d = h.shape
    n_out = len(out_dtypes)
    outs = pl.pallas_call(
        _rmsnorm_body,
        out_shape=[jax.ShapeDtypeStruct((m, d), dt) for dt in out_dtypes],
        grid=(m // tm,),
        in_specs=[pl.BlockSpec((tm, d), lambda i: (i, 0)),
                  pl.BlockSpec((n_out, d), lambda i: (0, 0))],
        out_specs=[pl.BlockSpec((tm, d), lambda i: (i, 0)) for _ in out_dtypes],
        compiler_params=_cparams(("parallel",)),
        name="rmsnorm",
    )(h, gains)
    return outs


def _drop_meta_body(ha_ref, hb_ref, g_ref, h_ref, *o_refs):
    rows = jnp.concatenate([ha_ref[N_META:, :], hb_ref[...]], axis=0)
    h_ref[...] = rows
    y = _rms(rows)
    for j, o_ref in enumerate(o_refs):
        o_ref[...] = (y * g_ref[j:j + 1, :]).astype(o_ref.dtype)


def _drop_meta(h, gains, seq, tm):
    d = h.shape[1]
    n_out = gains.shape[0]
    per = tm // N_META
    return pl.pallas_call(
        _drop_meta_body,
        out_shape=[jax.ShapeDtypeStruct((seq, d), F32)] + [jax.ShapeDtypeStruct((seq, d), BF16)] * n_out,
        grid=(seq // tm,),
        in_specs=[pl.BlockSpec((tm, d), lambda i: (i, 0)),
                  pl.BlockSpec((N_META, d), lambda i: ((i + 1) * per, 0)),
                  pl.BlockSpec((n_out, d), lambda i: (0, 0))],
        out_specs=[pl.BlockSpec((tm, d), lambda i: (i, 0))] * (1 + n_out),
        compiler_params=_cparams(("parallel",)),
        name="drop_meta",
    )(h, h, gains)


def _premix_body(h_ref, hp_ref, gain_ref, mix_ref, w1_ref, a1_ref, g1_ref, w2_ref, a2_ref, g2_ref, bias_ref,
                 xr_ref, xk_ref, xv_ref, lw_ref, a_ref, g_ref):
    i = pl.program_id(0)
    gain = gain_ref[...]
    xn = _rms(h_ref[...]) * gain
    pn = _rms(hp_ref[...]) * gain
    prev_row = jnp.where(i > 0, pn[7:8, :], 0.0)
    sh = pltpu.roll(xn, 1, axis=0)
    row = lax.broadcasted_iota(jnp.int32, xn.shape, 0)
    sh = jnp.where(row == 0, prev_row, sh)
    xx = sh - xn

    def mixed(j):
        return (xn + xx * mix_ref[j:j + 1, :]).astype(BF16)

    xr_ref[...] = mixed(0)
    xk_ref[...] = mixed(2)
    xv_ref[...] = mixed(3)
    t = jnp.tanh(_dot(mixed(1), w1_ref[...]))
    z = _dot(t.astype(BF16), w2_ref[...]) + bias_ref[0:1, :]
    lw_ref[...] = jax.nn.sigmoid(z) * (-0.6065306597126334)
    z = _dot(_dot(mixed(4), a1_ref[...]).astype(BF16), a2_ref[...]) + bias_ref[1:2, :]
    a_ref[...] = jax.nn.sigmoid(z).astype(a_ref.dtype)
    t = jax.nn.sigmoid(_dot(mixed(5), g1_ref[...]))
    g_ref[...] = _dot(t.astype(BF16), g2_ref[...]).astype(g_ref.dtype)


def _premix(h, gain, mix, w1, a1, g1, w2, a2, g2, bias, tm):
    m, d = h.shape
    rows8 = tm // 8
    row_blk = lambda i: (i, 0)
    whole = lambda arr: pl.BlockSpec(arr.shape, lambda i: (0, 0), pipeline_mode=pl.Buffered(1))
    return pl.pallas_call(
        _premix_body,
        out_shape=[jax.ShapeDtypeStruct((m, d), dt) for dt in (BF16, BF16, BF16, F32, BF16, BF16)],
        grid=(m // tm,),
        in_specs=[pl.BlockSpec((tm, d), row_blk),
                  pl.BlockSpec((8, d), lambda i: (jnp.maximum(i * rows8 - 1, 0), 0)),
                  whole(gain), whole(mix), whole(w1), whole(a1), whole(g1), whole(w2), whole(a2), whole(g2),
                  whole(bias)],
        out_specs=[pl.BlockSpec((tm, d), row_blk)] * 6,
        compiler_params=_cparams(("parallel",)),
        name="premix",
    )(h, h, gain, mix, w1, a1, g1, w2, a2, g2, bias)


def _mm_body(x_ref, w_ref, *rest, has_res, scale):
    if has_res:
        res_ref, o_ref, w16_ref = rest
    else:
        o_ref, w16_ref = rest

    @pl.when(pl.program_id(1) == 0)
    def _():
        _round_to_bf16(w_ref, w16_ref)

    acc = _dot(x_ref[...], w16_ref[...])
    if scale is not None:
        acc = acc * scale
    if has_res:
        acc = acc + res_ref[...]
    o_ref[...] = acc.astype(o_ref.dtype)


def _mm(x, w, *, out_dtype, tm, tn, res=None, scale=None, w_index=None):
    m, k = x.shape
    n = w.shape[-1]
    if w_index is None:
        w_spec = pl.BlockSpec((k, tn), lambda j, i: (0, j))
    else:
        w_spec = pl.BlockSpec((None, k, tn), lambda j, i: (w_index, 0, j))
    in_specs = [pl.BlockSpec((tm, k), lambda j, i: (i, 0)), w_spec]
    args = [x, w]
    if res is not None:
        in_specs.append(pl.BlockSpec((tm, tn), lambda j, i: (i, j)))
        args.append(res)
    return pl.pallas_call(
        functools.partial(_mm_body, has_res=res is not None, scale=scale),
        out_shape=jax.ShapeDtypeStruct((m, n), out_dtype),
        grid=(n // tn, m // tm),
        in_specs=in_specs,
        out_specs=pl.BlockSpec((tm, tn), lambda j, i: (i, j)),
        scratch_shapes=[pltpu.VMEM((k, tn), BF16)],
        compiler_params=_cparams(("parallel", "arbitrary")),
        name="matmul",
    )(*args)


def _mm_acc_body(x_ref, w_ref, res_ref, o_ref, acc_ref):
    kk = pl.program_id(2)

    @pl.when(kk == 0)
    def _():
        acc_ref[...] = jnp.zeros_like(acc_ref)

    acc_ref[...] += _dot(x_ref[...], w_ref[...])

    @pl.when(kk == pl.num_programs(2) - 1)
    def _():
        o_ref[...] = res_ref[...] + acc_ref[...]


def _mm_acc(x, w, res, *, tm, tn, tk):
    m, k = x.shape
    n = w.shape[1]
    return pl.pallas_call(
        _mm_acc_body,
        out_shape=jax.ShapeDtypeStruct((m, n), F32),
        grid=(n // tn, m // tm, k // tk),
        in_specs=[pl.BlockSpec((tm, tk), lambda j, i, q: (i, q)),
                  pl.BlockSpec((tk, tn), lambda j, i, q: (q, j)),
                  pl.BlockSpec((tm, tn), lambda j, i, q: (i, j))],
        out_specs=pl.BlockSpec((tm, tn), lambda j, i, q: (i, j)),
        scratch_shapes=[pltpu.VMEM((tm, tn), F32)],
        compiler_params=_cparams(("parallel", "parallel", "arbitrary")),
        name="matmul_acc",
    )(x, w, res)


def _swiglu_body(x_ref, wg_ref, wu_ref, o_ref, wg16_ref, wu16_ref):
    @pl.when(pl.program_id(1) == 0)
    def _():
        _round_to_bf16(wg_ref, wg16_ref)
        _round_to_bf16(wu_ref, wu16_ref)

    x = x_ref[...]
    g = _dot(x, wg16_ref[...])
    u = _dot(x, wu16_ref[...])
    o_ref[...] = (g * jax.nn.sigmoid(g) * u).astype(o_ref.dtype)


def _swiglu(x, wg, wu, *, tm, tn):
    m, k = x.shape
    n = wg.shape[1]
    return pl.pallas_call(
        _swiglu_body,
        out_shape=jax.ShapeDtypeStruct((m, n), BF16),
        grid=(n // tn, m // tm),
        in_specs=[pl.BlockSpec((tm, k), lambda j, i: (i, 0)),
                  pl.BlockSpec((k, tn), lambda j, i: (0, j)),
                  pl.BlockSpec((k, tn), lambda j, i: (0, j))],
        out_specs=pl.BlockSpec((tm, tn), lambda j, i: (i, j)),
        scratch_shapes=[pltpu.VMEM((k, tn), BF16), pltpu.VMEM((k, tn), BF16)],
        compiler_params=_cparams(("parallel", "arbitrary")),
        name="swiglu",
    )(x, wg, wu)


def _split2(x):
    hi = x.astype(BF16)
    lo = (x - hi.astype(F32)).astype(BF16)
    return hi, lo


def _wkv_body(r_ref, k_ref, v_ref, lw_ref, a_ref, g_ref, prm_ref, o_ref, st_ref, *, pairs):
    c_idx = pl.program_id(1)

    @pl.when(c_idx == 0)
    def _():
        st_ref[...] = jnp.zeros_like(st_ref)

    C = CHUNK
    lane = lax.broadcasted_iota(jnp.int32, (C, LANES), 1)
    row = lax.broadcasted_iota(jnp.int32, (C, LANES), 0)
    col = lane % HEAD
    upper_half = lane >= HEAD
    tri_strict = col < row
    tri_incl = col <= row
    eye_pair = (col == row).astype(F32)

    def level_mask(bs):
        return ((row // bs) == (col // bs)) & ((row // (bs // 2)) != (col // (bs // 2)))

    row2 = lax.broadcasted_iota(jnp.int32, (2 * C, LANES), 0)
    lane2 = lax.broadcasted_iota(jnp.int32, (2 * C, LANES), 1)
    bd_mask = (row2 >= C) == (lane2 >= HEAD)
    ones_bd = bd_mask.astype(BF16)
    diag_mask = row2 == lane2

    def bd(x):
        return jnp.concatenate([jnp.where(upper_half, 0.0, x), jnp.where(upper_half, x, 0.0)], axis=0)

    def headsum(x16):
        return _dot(x16, ones_bd)

    zeros_cl = jnp.zeros((C, LANES), BF16)

    def pair_step(r, k, v, lw, a, g, prm, s):
        k_k, k_a, r_k, lnx_w, lnx_b = (prm[j:j + 1] for j in range(5))

        kk = k * k_k
        k2 = k * (1.0 + (a - 1.0) * k_a)
        sums = yield ("headsum", jnp.concatenate([kk * kk, r * k2 * r_k], axis=0).astype(BF16))
        cum = lw
        shift = 1
        while shift < C:
            cum = cum + jnp.where(row >= shift, pltpu.roll(cum, shift, axis=0), 0.0)
            shift *= 2
        kkn = kk * lax.rsqrt(jnp.maximum(sums[:C], 1e-24))
        bonus = sums[C:] * v
        avec = -kkn
        bvec = kkn * a
        cum_end = cum[C - 1:C, :]
        p_incl = jnp.exp(cum)
        p_inv = jnp.exp(-cum)
        a_t = avec * jnp.exp(cum - lw)
        r_t = r * p_incl
        b_t = bvec * p_inv
        k_t = k2 * p_inv
        p_end = jnp.exp(cum_end - cum)
        b_h = bvec * p_end
        k_h = k2 * p_end

        v16 = v.astype(BF16)
        gram = _dot_nt(jnp.concatenate([a_t, r_t], axis=0).astype(BF16),
                       jnp.concatenate([bd(b_t), bd(k_t)], axis=0).astype(BF16))
        yield
        a_ab = jnp.where(tri_strict, gram[:C, :LANES], 0.0)
        a_ak = jnp.where(tri_strict, gram[:C, LANES:], 0.0)
        a_rb = jnp.where(tri_incl, gram[C:, :LANES], 0.0)
        a_rk = jnp.where(tri_incl, gram[C:, LANES:], 0.0)
        akv = _dot(a_ak.astype(BF16), bd(v16))

        t_inv = eye_pair + jnp.where(level_mask(2), a_ab, 0.0)
        bs = 4
        while bs <= C:
            x = jnp.where(level_mask(bs), a_ab, 0.0)
            t16 = t_inv.astype(BF16)
            tx = _dot(t16, bd(x).astype(BF16))
            yield
            t_inv = t_inv + _dot(tx.astype(BF16), bd(t16))
            yield
            bs *= 2
        t16 = t_inv.astype(BF16)

        tt = _dot(t16, jnp.concatenate([bd(a_t.astype(BF16)), bd(akv.astype(BF16))], axis=1))
        yield
        a_hat = tt[:, :LANES].astype(BF16)
        u0 = tt[:, LANES:].astype(BF16)

        a_rb16 = a_rb.astype(BF16)
        qa = _dot(a_rb16, bd(a_hat))
        y0 = _dot(jnp.concatenate([a_rb16, a_rk.astype(BF16)], axis=1),
                  jnp.concatenate([bd(u0), bd(v16)], axis=0))
        lhs_t = jnp.concatenate([b_h, k_h], axis=0).T.astype(BF16)
        rhs = jnp.concatenate([jnp.concatenate([a_hat, u0], axis=1),
                               jnp.concatenate([zeros_cl, v16], axis=1)], axis=0)
        upd = _dot(lhs_t, rhs)
        yield
        q_hat = r_t + qa
        m_off = jnp.where(bd_mask, upd[:, :LANES], 0.0)
        n_new = jnp.where(bd_mask, upd[:, LANES:], 0.0)
        p_col = jnp.sum(jnp.where(diag_mask, jnp.exp(cum_end), 0.0), axis=1, keepdims=True)

        s_hi, s_lo = _split2(s)
        m16 = m_off.astype(BF16)
        ys = _dot(jnp.concatenate([q_hat.astype(BF16), m16], axis=0), s_hi)
        y = ys[:C] + y0
        s_new = p_col * s + ys[C:] + _dot(m16, s_lo) + n_new

        mu = (yield ("headsum", y.astype(BF16))) * (1.0 / HEAD)
        d = y - mu
        var = (yield ("headsum", (d * d).astype(BF16))) * (1.0 / HEAD)
        yn = d * lax.rsqrt(var + LN_X_EPS) * lnx_w + lnx_b
        return ((yn + bonus) * g.astype(F32)).astype(o_ref.dtype), s_new

    lanes = [slice(p * LANES, (p + 1) * LANES) for p in range(pairs)]
    results = _run_interleaved([pair_step(r_ref[:, sl], k_ref[:, sl], v_ref[:, sl], lw_ref[:, sl],
                                          a_ref[:, sl].astype(F32), g_ref[:, sl], prm_ref[:, sl], st_ref[p])
                                for p, sl in enumerate(lanes)],
                               shared={"headsum": headsum})
    for p, sl in enumerate(lanes):
        o_ref[:, sl] = results[p][0]
        st_ref[p] = results[p][1]


def _wkv(r, k, v, lw, a, g, prm, *, pairs):
    length, d = r.shape
    width = pairs * LANES
    blk = lambda hb, c: (c, hb)
    return pl.pallas_call(
        functools.partial(_wkv_body, pairs=pairs),
        out_shape=jax.ShapeDtypeStruct((length, d), BF16),
        grid=(d // width, length // CHUNK),
        in_specs=[pl.BlockSpec((CHUNK, width), blk)] * 6 + [pl.BlockSpec((8, width), lambda hb, c: (0, hb))],
        out_specs=pl.BlockSpec((CHUNK, width), blk),
        scratch_shapes=[pltpu.VMEM((pairs, LANES, LANES), F32)],
        compiler_params=_cparams(("parallel", "arbitrary")),
        name="wkv7",
    )(r, k, v, lw, a, g, prm)


def _attn_body(q_ref, kp_ref, kc_ref, km_ref, vp_ref, vc_ref, vm_ref, sk_ref, o_ref, *, group, n_heads):
    kvh = pl.program_id(0)
    n = pl.program_id(1)
    rows = 2 * BLOCK
    rowi = lax.broadcasted_iota(jnp.int32, (rows, BLOCK), 0)
    kj = lax.broadcasted_iota(jnp.int32, (rows, BLOCK), 1)
    qi = rowi % BLOCK
    second = rowi >= BLOCK
    use_prev = kj > qi
    dist_band = (qi - kj + jnp.where(use_prev, BLOCK, 0)).astype(F32)
    valid_band = jnp.logical_not(use_prev & (n == 0))
    valid_meta = kj < N_META
    dist_meta = (N_META + n * BLOCK + qi - kj).astype(F32)
    lane_o = lax.broadcasted_iota(jnp.int32, (BLOCK, LANES), 1)
    lane_q = lax.broadcasted_iota(jnp.int32, (rows, LANES), 1)
    row_q = lax.broadcasted_iota(jnp.int32, (rows, LANES), 0)
    q_keep = (row_q >= BLOCK) == (lane_q >= HEAD)

    kp = kp_ref[...]
    kc = kc_ref[...]
    km = km_ref[...]
    vp = vp_ref[...]
    vc = vc_ref[...]
    vm = vm_ref[...]

    def pair_step(pr, q):
        q2 = jnp.where(q_keep, jnp.concatenate([q, q], axis=0), jnp.zeros((), q.dtype))
        qk_prev = _dot_nt(q2, kp)
        qk_cur = _dot_nt(q2, kc)
        qk_meta = _dot_nt(q2, km)
        yield
        head1 = (kvh * group + 2 * pr + 1).astype(F32)
        slope = jnp.exp2((jnp.where(second, 1.0, 0.0) + head1) * (-8.0 / n_heads))
        sink = jnp.where(second[:, :1], sk_ref[kvh, 2 * pr + 1], sk_ref[kvh, 2 * pr])
        s_band = jnp.where(valid_band, jnp.where(use_prev, qk_prev, qk_cur) - slope * dist_band, NEG)
        s_meta = jnp.where(valid_meta, qk_meta - slope * dist_meta, NEG)
        mx = jnp.maximum(jnp.max(jnp.maximum(s_band, s_meta), axis=1, keepdims=True), sink)
        e_band = jnp.exp(s_band - mx)
        e_meta = jnp.exp(s_meta - mx)
        den = jnp.sum(e_band + e_meta, axis=1, keepdims=True) + jnp.exp(sink - mx)
        acc = (_dot(jnp.where(use_prev, e_band, 0.0).astype(BF16), vp)
               + _dot(jnp.where(use_prev, 0.0, e_band).astype(BF16), vc)
               + _dot(e_meta.astype(BF16), vm))
        yield
        acc = acc / den
        return jnp.where(lane_o >= HEAD, acc[BLOCK:], acc[:BLOCK]).astype(o_ref.dtype)

    lanes = [slice(pr * LANES, (pr + 1) * LANES) for pr in range(group // 2)]
    results = _run_interleaved([pair_step(pr, q_ref[:, sl]) for pr, sl in enumerate(lanes)])
    for pr, sl in enumerate(lanes):
        o_ref[:, sl] = results[pr]


def _attention(q, kv_dup, kv_meta_dup, sinks, *, n_heads):
    s_len, d = q.shape
    group = n_heads // ATT_KV_HEADS
    gw = group * HEAD
    nb = s_len // BLOCK
    cur = lambda h, n: (n, h)
    k_prev = lambda h, n: (jnp.maximum(n - 1, 0), h)
    k_meta = lambda h, n: (0, h)
    v_prev = lambda h, n: (jnp.maximum(n - 1, 0), ATT_KV_HEADS + h)
    v_cur = lambda h, n: (n, ATT_KV_HEADS + h)
    v_meta = lambda h, n: (0, ATT_KV_HEADS + h)
    tile = lambda index_map: pl.BlockSpec((BLOCK, LANES), index_map)
    return pl.pallas_call(
        functools.partial(_attn_body, group=group, n_heads=n_heads),
        out_shape=jax.ShapeDtypeStruct((s_len, d), BF16),
        grid=(ATT_KV_HEADS, nb),
        in_specs=[pl.BlockSpec((BLOCK, gw), cur),
                  tile(k_prev), tile(cur), tile(k_meta), tile(v_prev), tile(v_cur), tile(v_meta),
                  pl.BlockSpec(memory_space=pltpu.SMEM)],
        out_specs=pl.BlockSpec((BLOCK, gw), cur),
        compiler_params=_cparams(("parallel", "parallel")),
        name="swa_attention",
    )(q, kv_dup, kv_dup, kv_meta_dup, kv_dup, kv_dup, kv_meta_dup, sinks)


def _router_body(h_ref, g_ref, wr_ref, hn_ref, sel_ref, gate_ref, *, n_experts):
    y = _rms(h_ref[...]) * g_ref[...]
    hn_ref[...] = y
    logits = jnp.dot(y, wr_ref[...], preferred_element_type=F32, precision=lax.Precision.HIGHEST)
    lane = lax.broadcasted_iota(jnp.int32, logits.shape, 1)
    logits = jnp.where(lane < n_experts, logits, NEG)
    m1 = jnp.max(logits, axis=1, keepdims=True)
    i1 = jnp.min(jnp.where(logits == m1, lane, LANES), axis=1, keepdims=True)
    rest = jnp.where(lane == i1, NEG, logits)
    m2 = jnp.max(rest, axis=1, keepdims=True)
    i2 = jnp.min(jnp.where(rest == m2, lane, LANES), axis=1, keepdims=True)
    e2 = jnp.exp(m2 - m1)
    g1 = 1.0 / (1.0 + e2)
    g2 = e2 / (1.0 + e2)
    sel_ref[...] = jnp.where(lane == 0, i1, jnp.where(lane == 1, i2, 0))
    gate_ref[...] = jnp.where(lane == 0, g1, jnp.where(lane == 1, g2, 0.0))


def _router(h, gain, w_router_pad, *, n_experts, tm):
    m, d = h.shape
    return pl.pallas_call(
        functools.partial(_router_body, n_experts=n_experts),
        out_shape=[jax.ShapeDtypeStruct((m, d), F32), jax.ShapeDtypeStruct((m, LANES), jnp.int32),
                   jax.ShapeDtypeStruct((m, LANES), F32)],
        grid=(m // tm,),
        in_specs=[pl.BlockSpec((tm, d), lambda i: (i, 0)),
                  pl.BlockSpec((1, d), lambda i: (0, 0)),
                  pl.BlockSpec((d, LANES), lambda i: (0, 0))],
        out_specs=[pl.BlockSpec((tm, d), lambda i: (i, 0)), pl.BlockSpec((tm, LANES), lambda i: (i, 0)),
                   pl.BlockSpec((tm, LANES), lambda i: (i, 0))],
        compiler_params=_cparams(("parallel",)),
        name="moe_router",
    )(h, gain, w_router_pad)


EXPERT_TILE = 256


def _row_copy(src_hbm, dst, src_row, dst_row, sem):
    return pltpu.make_async_copy(src_hbm.at[pl.ds(src_row, 1)], dst.at[pl.ds(dst_row, 1)], sem)


ROW_DMA_UNROLL = 8


def _gather_rows_body(src_ref, x_hbm, o_ref, buf, sem, *, rows):
    i = pl.program_id(0)

    def issue(tile, slot):
        base = tile * rows

        def body(r, carry):
            _row_copy(x_hbm, buf.at[slot], src_ref[base + r], r, sem.at[slot]).start()
            return carry

        lax.fori_loop(0, rows, body, 0, unroll=ROW_DMA_UNROLL)

    @pl.when(i == 0)
    def _():
        issue(0, 0)

    @pl.when(i + 1 < pl.num_programs(0))
    def _():
        issue(i + 1, (i + 1) % 2)

    slot = i % 2
    pltpu.make_async_copy(x_hbm.at[pl.ds(0, rows)], buf.at[slot], sem.at[slot]).wait()
    o_ref[...] = buf[slot].astype(o_ref.dtype)


def _gather_rows(row_src, x, n_rows, *, rows, out_dtype):
    d = x.shape[1]
    return pl.pallas_call(
        functools.partial(_gather_rows_body, rows=rows),
        out_shape=jax.ShapeDtypeStruct((n_rows, d), out_dtype),
        grid_spec=pltpu.PrefetchScalarGridSpec(
            num_scalar_prefetch=1,
            grid=(n_rows // rows,),
            in_specs=[pl.BlockSpec(memory_space=pl.ANY)],
            out_specs=pl.BlockSpec((rows, d), lambda i, src: (i, 0)),
            scratch_shapes=[pltpu.VMEM((2, rows, d), x.dtype), pltpu.SemaphoreType.DMA((2,))],
        ),
        compiler_params=_cparams(("arbitrary",)),
        name="moe_gather",
    )(row_src, x)


def _stage_expert_weights(te_ref, nxt_ref, w_hbms, wbuf, w16, sem, tn):
    j = pl.program_id(0)
    i = pl.program_id(1)

    def copies(expert, col_block):
        cols = pl.ds(pl.multiple_of(col_block * tn, tn), tn)
        return [pltpu.make_async_copy(w.at[expert, :, cols], wbuf.at[k], sem) for k, w in enumerate(w_hbms)]

    @pl.when((j == 0) & (i == 0))
    def _():
        for c in copies(te_ref[0], 0):
            c.start()

    @pl.when((i == 0) | (te_ref[i] != te_ref[jnp.maximum(i - 1, 0)]))
    def _():
        for c in copies(te_ref[i], j):
            c.wait()
        for k in range(len(w_hbms)):
            _round_to_bf16(wbuf.at[k], w16.at[k])
        next_expert = nxt_ref[i]
        wraps = next_expert < 0

        @pl.when(jnp.logical_not(wraps & (j == pl.num_programs(0) - 1)))
        def _():
            for c in copies(jnp.where(wraps, te_ref[0], next_expert), jnp.where(wraps, j + 1, j)):
                c.start()


def _moe_swiglu_body(te_ref, nxt_ref, used_ref, x_ref, wg_hbm, wu_hbm, o_ref, wbuf, w16, sem, *, tn):
    _stage_expert_weights(te_ref, nxt_ref, (wg_hbm, wu_hbm), wbuf, w16, sem, tn)
    has_tokens = pl.program_id(1) < used_ref[0]

    @pl.when(has_tokens)
    def _():
        x = x_ref[...]
        g = _dot(x, w16[0])
        u = _dot(x, w16[1])
        o_ref[...] = (g * jax.nn.sigmoid(g) * u).astype(o_ref.dtype)

    @pl.when(jnp.logical_not(has_tokens))
    def _():
        o_ref[...] = jnp.zeros_like(o_ref)


def _moe_down_body(te_ref, nxt_ref, used_ref, x_ref, w_hbm, o_ref, wbuf, w16, sem, *, tn):
    _stage_expert_weights(te_ref, nxt_ref, (w_hbm,), wbuf, w16, sem, tn)
    has_tokens = pl.program_id(1) < used_ref[0]

    @pl.when(has_tokens)
    def _():
        o_ref[...] = _dot(x_ref[...], w16[0])

    @pl.when(jnp.logical_not(has_tokens))
    def _():
        o_ref[...] = jnp.zeros_like(o_ref)


def _moe_matmul(body, tile_expert, next_expert, tiles_used, xs, weights, *, tn, out_dtype, name):
    rows, k = xs.shape
    n = weights[0].shape[-1]
    n_w = len(weights)
    return pl.pallas_call(
        functools.partial(body, tn=tn),
        out_shape=jax.ShapeDtypeStruct((rows, n), out_dtype),
        grid_spec=pltpu.PrefetchScalarGridSpec(
            num_scalar_prefetch=3,
            grid=(n // tn, rows // EXPERT_TILE),
            in_specs=[pl.BlockSpec((EXPERT_TILE, k), lambda j, i, te, nxt, used: (i, 0))]
            + [pl.BlockSpec(memory_space=pl.ANY)] * n_w,
            out_specs=pl.BlockSpec((EXPERT_TILE, tn), lambda j, i, te, nxt, used: (i, j)),
            scratch_shapes=[pltpu.VMEM((n_w, k, tn), F32), pltpu.VMEM((n_w, k, tn), BF16),
                            pltpu.SemaphoreType.DMA],
        ),
        compiler_params=_cparams(("arbitrary", "arbitrary")),
        name=name,
    )(tile_expert, next_expert, tiles_used, xs, *weights)


def _moe_combine_body(pos_ref, h_ref, gate_ref, fn_ref, eo_hbm, o_ref, buf, sem, *, tm):
    i = pl.program_id(0)

    def issue(tile, slot):
        base = tile * tm

        def body(t, carry):
            for j in range(TOP_K):
                _row_copy(eo_hbm, buf.at[slot, j], pos_ref[TOP_K * (base + t) + j], t, sem.at[slot]).start()
            return carry

        lax.fori_loop(0, tm, body, 0, unroll=ROW_DMA_UNROLL // TOP_K)

    @pl.when(i == 0)
    def _():
        issue(0, 0)

    @pl.when(i + 1 < pl.num_programs(0))
    def _():
        issue(i + 1, (i + 1) % 2)

    slot = i % 2
    for j in range(TOP_K):
        pltpu.make_async_copy(eo_hbm.at[pl.ds(0, tm)], buf.at[slot, j], sem.at[slot]).wait()
    y = h_ref[...]
    for j in range(TOP_K):
        y = y + gate_ref[:, j:j + 1] * buf[slot, j]
    o_ref[...] = _rms(y) * fn_ref[...]


def _moe_combine(pos, h, gate, final_gain, eo, *, tm):
    m, d = h.shape
    return pl.pallas_call(
        functools.partial(_moe_combine_body, tm=tm),
        out_shape=jax.ShapeDtypeStruct((m, d), F32),
        grid_spec=pltpu.PrefetchScalarGridSpec(
            num_scalar_prefetch=1,
            grid=(m // tm,),
            in_specs=[pl.BlockSpec((tm, d), lambda i, pos: (i, 0)),
                      pl.BlockSpec((tm, LANES), lambda i, pos: (i, 0)),
                      pl.BlockSpec((1, d), lambda i, pos: (0, 0)),
                      pl.BlockSpec(memory_space=pl.ANY)],
            out_specs=pl.BlockSpec((tm, d), lambda i, pos: (i, 0)),
            scratch_shapes=[pltpu.VMEM((2, TOP_K, tm, d), F32), pltpu.SemaphoreType.DMA((2,))],
        ),
        compiler_params=_cparams(("arbitrary",)),
        name="moe_combine",
    )(pos, h, gate, final_gain, eo)


def _moe_plan(sel, n_experts):
    m = sel.shape[0]
    flat_e = sel[:, :TOP_K].reshape(-1)
    onehot = (flat_e[:, None] == jnp.arange(n_experts, dtype=jnp.int32)[None, :]).astype(jnp.int32)
    csum = jnp.cumsum(onehot, axis=0)
    rank = jnp.sum(csum * onehot, axis=1) - 1
    counts = csum[-1]
    padded = (counts + EXPERT_TILE - 1) // EXPERT_TILE * EXPERT_TILE
    ends = jnp.cumsum(padded)
    starts = ends - padded
    pos = (jnp.sum(onehot * starts[None, :], axis=1) + rank).astype(jnp.int32)
    n_rows = TOP_K * m + n_experts * EXPERT_TILE
    token = jnp.arange(TOP_K * m, dtype=jnp.int32) // TOP_K
    row_src = jnp.zeros((n_rows,), jnp.int32).at[pos].set(token)
    tile_start = jnp.arange(n_rows // EXPERT_TILE, dtype=jnp.int32) * EXPERT_TILE
    tile_expert = jnp.minimum(jnp.sum((tile_start[:, None] >= ends[None, :]).astype(jnp.int32), axis=1),
                              n_experts - 1).astype(jnp.int32)
    run_end = jnp.sum((tile_expert[None, :] <= tile_expert[:, None]).astype(jnp.int32), axis=1)
    n_tiles = tile_expert.shape[0]
    next_expert = jnp.where(run_end < n_tiles, tile_expert[jnp.minimum(run_end, n_tiles - 1)], -1).astype(jnp.int32)
    tiles_used = (ends[-1:] // EXPERT_TILE).astype(jnp.int32)
    return pos, row_src, tile_expert, next_expert, tiles_used, n_rows


def _largest_tile(n, cap, mult):
    best = None
    t = mult
    while t <= min(n, cap):
        if n % t == 0:
            best = t
        t += mult
    assert best is not None, (n, cap, mult)
    return best


def _pad_cols(w, n_to):
    return jnp.pad(w, ((0, 0), (0, n_to - w.shape[1])))


def _pad_rows(w, n_to):
    return jnp.pad(w, ((0, n_to - w.shape[0]), (0, 0)))


def _round_up(n, m):
    return -(-n // m) * m


def kernel(x, meta_tokens, a_norm, a_mix, a_w_rkv, a_w0, a_w1, a_w2, a_a0, a_a1, a_a2, a_g1, a_g2, a_k_k, a_k_a, a_r_k, a_lnx_w, a_lnx_b, a_w_out, kv_norm, w_kv, b_norm, b_w_q, b_sinks, b_w_out, f_norm, d_w_gate, d_w_up, d_w_down, e_router, e_w_gate, e_w_up, e_w_down, final_norm):
    assert x.shape[0] == 1
    seq, d = x.shape[1], x.shape[2]
    n_heads = d // HEAD
    n_experts = e_router.shape[-1]
    xs = x[0]

    l_real = N_META + seq
    l_pad = _round_up(l_real, CHUNK)
    h = jnp.concatenate([meta_tokens.astype(F32), xs, jnp.zeros((l_pad - l_real, d), F32)], axis=0)
    tm0 = _largest_tile(l_pad, 768, 16)
    tn = _largest_tile(d, 512, LANES)

    gate_rank = _round_up(a_g1.shape[-1], LANES)
    xr, xk, xv, lw, a, g = _premix(
        h, a_norm[0:1], a_mix[0],
        a_w1[0].astype(BF16), a_a1[0].astype(BF16), _pad_cols(a_g1[0], gate_rank).astype(BF16),
        a_w2[0].astype(BF16), a_a2[0].astype(BF16), _pad_rows(a_g2[0], gate_rank).astype(BF16),
        jnp.stack([a_w0[0], a_a0[0]]), _largest_tile(l_pad, 256, 16))
    r = _mm(xr, a_w_rkv[0], out_dtype=F32, tm=tm0, tn=tn, w_index=0)
    k = _mm(xk, a_w_rkv[0], out_dtype=F32, tm=tm0, tn=tn, w_index=1)
    v = _mm(xv, a_w_rkv[0], out_dtype=F32, tm=tm0, tn=tn, w_index=2)
    prm = jnp.concatenate([a_k_k[0][None], a_k_a[0][None], a_r_k[0].reshape(1, d), a_lnx_w[0][None],
                           a_lnx_b[0][None], jnp.zeros((3, d), F32)], axis=0)
    pairs = 16 if n_heads % 32 == 0 else n_heads // 2
    mixed = _wkv(r, k, v, lw, a, g, prm, pairs=pairs)
    h = _mm(mixed, a_w_out[0], out_dtype=F32, tm=tm0, tn=tn, res=h)

    (hn,) = _rmsnorm(h, f_norm[0:1], [BF16], _largest_tile(l_pad, 256, 16))
    d_ff = d_w_gate.shape[-1]
    act = _swiglu(hn, d_w_gate[0], d_w_up[0], tm=_largest_tile(l_pad, 1408, 16), tn=_largest_tile(d_ff, 256, LANES))
    h = _mm_acc(act, d_w_down[0].astype(BF16), h, tm=tm0, tn=_largest_tile(d, 1024, LANES),
                tk=_largest_tile(d_ff, 6144, LANES))

    w_kv_dup = jnp.broadcast_to(w_kv.reshape(d, 2 * ATT_KV_HEADS, 1, HEAD),
                                (d, 2 * ATT_KV_HEADS, 2, HEAD)).reshape(d, 2 * ATT_KV_HEADS * LANES)
    tm1 = _largest_tile(seq, 1024, 16)
    (hkv_meta,) = _rmsnorm(h[:CHUNK], kv_norm[None], [BF16], CHUNK)
    kv_meta = _mm(hkv_meta, w_kv_dup, out_dtype=BF16, tm=CHUNK, tn=tn)
    kv_meta = _pad_rows(kv_meta[:N_META], BLOCK)
    h, hkv, hq = _drop_meta(h, jnp.stack([kv_norm, b_norm[0]]), seq, _largest_tile(seq, 256, N_META))
    kv_real = _mm(hkv, w_kv_dup, out_dtype=BF16, tm=tm1, tn=tn)

    q = _mm(hq, b_w_q[0], out_dtype=BF16, tm=tm1, tn=tn, scale=HEAD ** -0.5)
    o = _attention(q, kv_real, kv_meta, b_sinks[0].reshape(ATT_KV_HEADS, -1), n_heads=n_heads)
    h = _mm(o, b_w_out[0], out_dtype=F32, tm=tm1, tn=tn, res=h)

    hn, sel, gate = _router(h, f_norm[1:2], _pad_cols(e_router[0], LANES), n_experts=n_experts,
                            tm=_largest_tile(seq, 256, 16))
    pos, row_src, tile_expert, next_expert, tiles_used, n_rows = _moe_plan(sel, n_experts)
    xs_sorted = _gather_rows(row_src, hn, n_rows, rows=_largest_tile(n_rows, 512, EXPERT_TILE), out_dtype=BF16)
    d_exp = e_w_gate.shape[-1]
    act = _moe_matmul(_moe_swiglu_body, tile_expert, next_expert, tiles_used, xs_sorted, (e_w_gate[0], e_w_up[0]),
                      tn=_largest_tile(d_exp, 896, LANES), out_dtype=BF16, name="moe_swiglu")
    eo = _moe_matmul(_moe_down_body, tile_expert, next_expert, tiles_used, act, (e_w_down[0],),
                     tn=_largest_tile(d, 1024, LANES), out_dtype=F32, name="moe_down")
    out = _moe_combine(pos, h, gate, final_norm[None], eo, tm=_largest_tile(seq, 128, 8))
    return out[None]
```

```python
import functools

import jax
import jax.numpy as jnp
from jax import lax
from jax.experimental import pallas as pl
from jax.experimental.pallas import tpu as pltpu

F32 = jnp.float32
BF16 = jnp.bfloat16

LANES = 128
VMEM_LIMIT_BYTES = 56 * 1024 * 1024

N_META = 16
RMS_EPS = 1e-5
LN_X_EPS = 64e-5
HEAD = 64
ATT_KV_HEADS = 8
WINDOW = 128
BLOCK = 128
TOP_K = 2
CHUNK = 64
NEG = -1e30


def _cparams(sem):
    return pltpu.CompilerParams(dimension_semantics=sem, vmem_limit_bytes=VMEM_LIMIT_BYTES)


def _dot(a, b):
    return jnp.dot(a, b, preferred_element_type=F32)


def _dot_nt(a, b):
    return lax.dot_general(a, b, (((1,), (1,)), ((), ())), preferred_element_type=F32)


def _rms(x):
    return x * lax.rsqrt(jnp.mean(x * x, axis=-1, keepdims=True) + RMS_EPS)


CAST_ROWS = 256


def _round_to_bf16(src_ref, dst_ref):
    rows = src_ref.shape[0]
    chunk = CAST_ROWS if rows % CAST_ROWS == 0 else rows

    def body(c, carry):
        r = pl.ds(pl.multiple_of(c * chunk, chunk), chunk)
        dst_ref[r, :] = src_ref[r, :].astype(BF16)
        return carry

    lax.fori_loop(0, rows // chunk, body, 0)


def _run_interleaved(steps, shared=None):
    results = [None] * len(steps)
    pending = list(range(len(steps)))
    inbox = {idx: None for idx in pending}
    while pending:
        requests = {}
        for idx in list(pending):
            try:
                req = steps[idx].send(inbox[idx])
            except StopIteration as done:
                results[idx] = done.value
                pending.remove(idx)
                continue
            inbox[idx] = None
            if req is not None:
                requests.setdefault(req[0], []).append((idx, req[1]))
        for key, members in requests.items():
            out = shared[key](jnp.concatenate([rows for _, rows in members], axis=0))
            start = 0
            for idx, rows in members:
                inbox[idx] = out[start:start + rows.shape[0]]
                start += rows.shape[0]
    return results


def _rmsnorm_body(h_ref, g_ref, *o_refs):
    y = _rms(h_ref[...])
    for j, o_ref in enumerate(o_refs):
        o_ref[...] = (y * g_ref[j:j + 1, :]).astype(o_ref.dtype)


def _rmsnorm(h, gains, out_dtypes, tm):
    m, d = h.shape
    n_out = len(out_dtypes)
    outs = pl.pallas_call(
        _rmsnorm_body,
        out_shape=[jax.ShapeDtypeStruct((m, d), dt) for dt in out_dtypes],
        grid=(m // tm,),
        in_specs=[pl.BlockSpec((tm, d), lambda i: (i, 0)),
                  pl.BlockSpec((n_out, d), lambda i: (0, 0))],
        out_specs=[pl.BlockSpec((tm, d), lambda i: (i, 0)) for _ in out_dtypes],
        compiler_params=_cparams(("parallel",)),
        name="rmsnorm",
    )(h, gains)
    return outs


def _drop_meta_body(ha_ref, hb_ref, g_ref, h_ref, *o_refs):
    rows = jnp.concatenate([ha_ref[N_META:, :], hb_ref[...]], axis=0)
    h_ref[...] = rows
    y = _rms(rows)
    for j, o_ref in enumerate(o_refs):
        o_ref[...] = (y * g_ref[j:j + 1, :]).astype(o_ref.dtype)


def _drop_meta(h, gains, seq, tm):
    d = h.shape[1]
    n_out = gains.shape[0]
    per = tm // N_META
    return pl.pallas_call(
        _drop_meta_body,
        out_shape=[jax.ShapeDtypeStruct((seq, d), F32)] + [jax.ShapeDtypeStruct((seq, d), BF16)] * n_out,
        grid=(seq // tm,),
        in_specs=[pl.BlockSpec((tm, d), lambda i: (i, 0)),
                  pl.BlockSpec((N_META, d), lambda i: ((i + 1) * per, 0)),
                  pl.BlockSpec((n_out, d), lambda i: (0, 0))],
        out_specs=[pl.BlockSpec((tm, d), lambda i: (i, 0))] * (1 + n_out),
        compiler_params=_cparams(("parallel",)),
        name="drop_meta",
    )(h, h, gains)


def _premix_body(h_ref, hp_ref, gain_ref, mix_ref, w1_ref, a1_ref, g1_ref, w2_ref, a2_ref, g2_ref, bias_ref,
                 xr_ref, xk_ref, xv_ref, lw_ref, a_ref, g_ref):
    i = pl.program_id(0)
    gain = gain_ref[...]
    xn = _rms(h_ref[...]) * gain
    pn = _rms(hp_ref[...]) * gain
    prev_row = jnp.where(i > 0, pn[7:8, :], 0.0)
    sh = pltpu.roll(xn, 1, axis=0)
    row = lax.broadcasted_iota(jnp.int32, xn.shape, 0)
    sh = jnp.where(row == 0, prev_row, sh)
    xx = sh - xn

    def mixed(j):
        return (xn + xx * mix_ref[j:j + 1, :]).astype(BF16)

    xr_ref[...] = mixed(0)
    xk_ref[...] = mixed(2)
    xv_ref[...] = mixed(3)
    t = jnp.tanh(_dot(mixed(1), w1_ref[...]))
    z = _dot(t.astype(BF16), w2_ref[...]) + bias_ref[0:1, :]
    lw_ref[...] = jax.nn.sigmoid(z) * (-0.6065306597126334)
    z = _dot(_dot(mixed(4), a1_ref[...]).astype(BF16), a2_ref[...]) + bias_ref[1:2, :]
    a_ref[...] = jax.nn.sigmoid(z).astype(a_ref.dtype)
    t = jax.nn.sigmoid(_dot(mixed(5), g1_ref[...]))
    g_ref[...] = _dot(t.astype(BF16), g2_ref[...]).astype(g_ref.dtype)


def _premix(h, gain, mix, w1, a1, g1, w2, a2, g2, bias, tm):
    m, d = h.shape
    rows8 = tm // 8
    row_blk = lambda i: (i, 0)
    whole = lambda arr: pl.BlockSpec(arr.shape, lambda i: (0, 0), pipeline_mode=pl.Buffered(1))
    return pl.pallas_call(
        _premix_body,
        out_shape=[jax.ShapeDtypeStruct((m, d), dt) for dt in (BF16, BF16, BF16, F32, BF16, BF16)],
        grid=(m // tm,),
        in_specs=[pl.BlockSpec((tm, d), row_blk),
                  pl.BlockSpec((8, d), lambda i: (jnp.maximum(i * rows8 - 1, 0), 0)),
                  whole(gain), whole(mix), whole(w1), whole(a1), whole(g1), whole(w2), whole(a2), whole(g2),
                  whole(bias)],
        out_specs=[pl.BlockSpec((tm, d), row_blk)] * 6,
        compiler_params=_cparams(("parallel",)),
        name="premix",
    )(h, h, gain, mix, w1, a1, g1, w2, a2, g2, bias)


def _mm_body(x_ref, w_ref, *rest, has_res, scale):
    if has_res:
        res_ref, o_ref, w16_ref = rest
    else:
        o_ref, w16_ref = rest

    @pl.when(pl.program_id(1) == 0)
    def _():
        _round_to_bf16(w_ref, w16_ref)

    acc = _dot(x_ref[...], w16_ref[...])
    if scale is not None:
        acc = acc * scale
    if has_res:
        acc = acc + res_ref[...]
    o_ref[...] = acc.astype(o_ref.dtype)


def _mm(x, w, *, out_dtype, tm, tn, res=None, scale=None, w_index=None):
    m, k = x.shape
    n = w.shape[-1]
    if w_index is None:
        w_spec = pl.BlockSpec((k, tn), lambda j, i: (0, j))
    else:
        w_spec = pl.BlockSpec((None, k, tn), lambda j, i: (w_index, 0, j))
    in_specs = [pl.BlockSpec((tm, k), lambda j, i: (i, 0)), w_spec]
    args = [x, w]
    if res is not None:
        in_specs.append(pl.BlockSpec((tm, tn), lambda j, i: (i, j)))
        args.append(res)
    return pl.pallas_call(
        functools.partial(_mm_body, has_res=res is not None, scale=scale),
        out_shape=jax.ShapeDtypeStruct((m, n), out_dtype),
        grid=(n // tn, m // tm),
        in_specs=in_specs,
        out_specs=pl.BlockSpec((tm, tn), lambda j, i: (i, j)),
        scratch_shapes=[pltpu.VMEM((k, tn), BF16)],
        compiler_params=_cparams(("parallel", "arbitrary")),
        name="matmul",
    )(*args)


def _mm_acc_body(x_ref, w_ref, res_ref, o_ref, acc_ref):
    kk = pl.program_id(2)

    @pl.when(kk == 0)
    def _():
        acc_ref[...] = jnp.zeros_like(acc_ref)

    acc_ref[...] += _dot(x_ref[...], w_ref[...])

    @pl.when(kk == pl.num_programs(2) - 1)
    def _():
        o_ref[...] = res_ref[...] + acc_ref[...]


def _mm_acc(x, w, res, *, tm, tn, tk):
    m, k = x.shape
    n = w.shape[1]
    return pl.pallas_call(
        _mm_acc_body,
        out_shape=jax.ShapeDtypeStruct((m, n), F32),
        grid=(n // tn, m // tm, k // tk),
        in_specs=[pl.BlockSpec((tm, tk), lambda j, i, q: (i, q)),
                  pl.BlockSpec((tk, tn), lambda j, i, q: (q, j)),
                  pl.BlockSpec((tm, tn), lambda j, i, q: (i, j))],
        out_specs=pl.BlockSpec((tm, tn), lambda j, i, q: (i, j)),
        scratch_shapes=[pltpu.VMEM((tm, tn), F32)],
        compiler_params=_cparams(("parallel", "parallel", "arbitrary")),
        name="matmul_acc",
    )(x, w, res)


def _swiglu_body(x_ref, wg_ref, wu_ref, o_ref, wg16_ref, wu16_ref):
    @pl.when(pl.program_id(1) == 0)
    def _():
        _round_to_bf16(wg_ref, wg16_ref)
        _round_to_bf16(wu_ref, wu16_ref)

    x = x_ref[...]
    g = _dot(x, wg16_ref[...])
    u = _dot(x, wu16_ref[...])
    o_ref[...] = (g * jax.nn.sigmoid(g) * u).astype(o_ref.dtype)


def _swiglu(x, wg, wu, *, tm, tn):
    m, k = x.shape
    n = wg.shape[1]
    return pl.pallas_call(
        _swiglu_body,
        out_shape=jax.ShapeDtypeStruct((m, n), BF16),
        grid=(n // tn, m // tm),
        in_specs=[pl.BlockSpec((tm, k), lambda j, i: (i, 0)),
                  pl.BlockSpec((k, tn), lambda j, i: (0, j)),
                  pl.BlockSpec((k, tn), lambda j, i: (0, j))],
        out_specs=pl.BlockSpec((tm, tn), lambda j, i: (i, j)),
        scratch_shapes=[pltpu.VMEM((k, tn), BF16), pltpu.VMEM((k, tn), BF16)],
        compiler_params=_cparams(("parallel", "arbitrary")),
        name="swiglu",
    )(x, wg, wu)


def _split2(x):
    hi = x.astype(BF16)
    lo = (x - hi.astype(F32)).astype(BF16)
    return hi, lo


def _wkv_body(r_ref, k_ref, v_ref, lw_ref, a_ref, g_ref, prm_ref, o_ref, st_ref, *, pairs):
    c_idx = pl.program_id(1)

    @pl.when(c_idx == 0)
    def _():
        st_ref[...] = jnp.zeros_like(st_ref)

    C = CHUNK
    lane = lax.broadcasted_iota(jnp.int32, (C, LANES), 1)
    row = lax.broadcasted_iota(jnp.int32, (C, LANES), 0)
    col = lane % HEAD
    upper_half = lane >= HEAD
    tri_strict = col < row
    tri_incl = col <= row
    eye_pair = (col == row).astype(F32)

    def level_mask(bs):
        return ((row // bs) == (col // bs)) & ((row // (bs // 2)) != (col // (bs // 2)))

    row2 = lax.broadcasted_iota(jnp.int32, (2 * C, LANES), 0)
    lane2 = lax.broadcasted_iota(jnp.int32, (2 * C, LANES), 1)
    bd_mask = (row2 >= C) == (lane2 >= HEAD)
    ones_bd = bd_mask.astype(BF16)
    diag_mask = row2 == lane2

    def bd(x):
        return jnp.concatenate([jnp.where(upper_half, 0.0, x), jnp.where(upper_half, x, 0.0)], axis=0)

    def headsum(x16):
        return _dot(x16, ones_bd)

    zeros_cl = jnp.zeros((C, LANES), BF16)

    def pair_step(r, k, v, lw, a, g, prm, s):
        k_k, k_a, r_k, lnx_w, lnx_b = (prm[j:j + 1] for j in range(5))

        kk = k * k_k
        k2 = k * (1.0 + (a - 1.0) * k_a)
        sums = yield ("headsum", jnp.concatenate([kk * kk, r * k2 * r_k], axis=0).astype(BF16))
        cum = lw
        shift = 1
        while shift < C:
            cum = cum + jnp.where(row >= shift, pltpu.roll(cum, shift, axis=0), 0.0)
            shift *= 2
        kkn = kk * lax.rsqrt(jnp.maximum(sums[:C], 1e-24))
        bonus = sums[C:] * v
        avec = -kkn
        bvec = kkn * a
        cum_end = cum[C - 1:C, :]
        p_incl = jnp.exp(cum)
        p_inv = jnp.exp(-cum)
        a_t = avec * jnp.exp(cum - lw)
        r_t = r * p_incl
        b_t = bvec * p_inv
        k_t = k2 * p_inv
        p_end = jnp.exp(cum_end - cum)
        b_h = bvec * p_end
        k_h = k2 * p_end

        v16 = v.astype(BF16)
        gram = _dot_nt(jnp.concatenate([a_t, r_t], axis=0).astype(BF16),
                       jnp.concatenate([bd(b_t), bd(k_t)], axis=0).astype(BF16))
        yield
        a_ab = jnp.where(tri_strict, gram[:C, :LANES], 0.0)
        a_ak = jnp.where(tri_strict, gram[:C, LANES:], 0.0)
        a_rb = jnp.where(tri_incl, gram[C:, :LANES], 0.0)
        a_rk = jnp.where(tri_incl, gram[C:, LANES:], 0.0)
        akv = _dot(a_ak.astype(BF16), bd(v16))

        t_inv = eye_pair + jnp.where(level_mask(2), a_ab, 0.0)
        bs = 4
        while bs <= C:
            x = jnp.where(level_mask(bs), a_ab, 0.0)
            t16 = t_inv.astype(BF16)
            tx = _dot(t16, bd(x).astype(BF16))
            yield
            t_inv = t_inv + _dot(tx.astype(BF16), bd(t16))
            yield
            bs *= 2
        t16 = t_inv.astype(BF16)

        tt = _dot(t16, jnp.concatenate([bd(a_t.astype(BF16)), bd(akv.astype(BF16))], axis=1))
        yield
        a_hat = tt[:, :LANES].astype(BF16)
        u0 = tt[:, LANES:].astype(BF16)

        a_rb16 = a_rb.astype(BF16)
        qa = _dot(a_rb16, bd(a_hat))
        y0 = _dot(jnp.concatenate([a_rb16, a_rk.astype(BF16)], axis=1),
                  jnp.concatenate([bd(u0), bd(v16)], axis=0))
        lhs_t = jnp.concatenate([b_h, k_h], axis=0).T.astype(BF16)
        rhs = jnp.concatenate([jnp.concatenate([a_hat, u0], axis=1),
                               jnp.concatenate([zeros_cl, v16], axis=1)], axis=0)
        upd = _dot(lhs_t, rhs)
        yield
        q_hat = r_t + qa
        m_off = jnp.where(bd_mask, upd[:, :LANES], 0.0)
        n_new = jnp.where(bd_mask, upd[:, LANES:], 0.0)
        p_col = jnp.sum(jnp.where(diag_mask, jnp.exp(cum_end), 0.0), axis=1, keepdims=True)

        s_hi, s_lo = _split2(s)
        m16 = m_off.astype(BF16)
        ys = _dot(jnp.concatenate([q_hat.astype(BF16), m16], axis=0), s_hi)
        y = ys[:C] + y0
        s_new = p_col * s + ys[C:] + _dot(m16, s_lo) + n_new

        mu = (yield ("headsum", y.astype(BF16))) * (1.0 / HEAD)
        d = y - mu
        var = (yield ("headsum", (d * d).astype(BF16))) * (1.0 / HEAD)
        yn = d * lax.rsqrt(var + LN_X_EPS) * lnx_w + lnx_b
        return ((yn + bonus) * g.astype(F32)).astype(o_ref.dtype), s_new

    lanes = [slice(p * LANES, (p + 1) * LANES) for p in range(pairs)]
    results = _run_interleaved([pair_step(r_ref[:, sl], k_ref[:, sl], v_ref[:, sl], lw_ref[:, sl],
                                          a_ref[:, sl].astype(F32), g_ref[:, sl], prm_ref[:, sl], st_ref[p])
                                for p, sl in enumerate(lanes)],
                               shared={"headsum": headsum})
    for p, sl in enumerate(lanes):
        o_ref[:, sl] = results[p][0]
        st_ref[p] = results[p][1]


def _wkv(r, k, v, lw, a, g, prm, *, pairs):
    length, d = r.shape
    width = pairs * LANES
    blk = lambda hb, c: (c, hb)
    return pl.pallas_call(
        functools.partial(_wkv_body, pairs=pairs),
        out_shape=jax.ShapeDtypeStruct((length, d), BF16),
        grid=(d // width, length // CHUNK),
        in_specs=[pl.BlockSpec((CHUNK, width), blk)] * 6 + [pl.BlockSpec((8, width), lambda hb, c: (0, hb))],
        out_specs=pl.BlockSpec((CHUNK, width), blk),
        scratch_shapes=[pltpu.VMEM((pairs, LANES, LANES), F32)],
        compiler_params=_cparams(("parallel", "arbitrary")),
        name="wkv7",
    )(r, k, v, lw, a, g, prm)


def _attn_body(q_ref, kp_ref, kc_ref, km_ref, vp_ref, vc_ref, vm_ref, sk_ref, o_ref, *, group, n_heads, q_blocks):
    kvh = pl.program_id(0)
    first_block = pl.program_id(1) * q_blocks
    rows = 2 * BLOCK
    rowi = lax.broadcasted_iota(jnp.int32, (rows, BLOCK), 0)
    kj = lax.broadcasted_iota(jnp.int32, (rows, BLOCK), 1)
    qi = rowi % BLOCK
    second = rowi >= BLOCK
    use_prev = kj > qi
    dist_band = (qi - kj + jnp.where(use_prev, BLOCK, 0)).astype(F32)
    valid_meta = kj < N_META
    lane_o = lax.broadcasted_iota(jnp.int32, (BLOCK, LANES), 1)
    lane_q = lax.broadcasted_iota(jnp.int32, (rows, LANES), 1)
    row_q = lax.broadcasted_iota(jnp.int32, (rows, LANES), 0)
    q_keep = (row_q >= BLOCK) == (lane_q >= HEAD)

    km = km_ref[...]
    vm = vm_ref[...]
    k_tiles = [kp_ref[...]] + [kc_ref[b * BLOCK:(b + 1) * BLOCK, :] for b in range(q_blocks)]
    v_tiles = [vp_ref[...]] + [vc_ref[b * BLOCK:(b + 1) * BLOCK, :] for b in range(q_blocks)]

    def pair_bias(pr):
        head1 = (kvh * group + 2 * pr + 1).astype(F32)
        slope = jnp.exp2((jnp.where(second, 1.0, 0.0) + head1) * (-8.0 / n_heads))
        sink = jnp.where(second[:, :1], sk_ref[kvh, 2 * pr + 1], sk_ref[kvh, 2 * pr])
        return slope, slope * dist_band, sink

    biases = [pair_bias(pr) for pr in range(group // 2)]

    def pair_step(pr, b, q):
        n = first_block + b
        q2 = jnp.where(q_keep, jnp.concatenate([q, q], axis=0), jnp.zeros((), q.dtype))
        qk_prev = _dot_nt(q2, k_tiles[b])
        qk_cur = _dot_nt(q2, k_tiles[b + 1])
        qk_meta = _dot_nt(q2, km)
        yield
        slope, bias_band, sink = biases[pr]
        dist_meta = (N_META + n * BLOCK + qi - kj).astype(F32)
        s_band = jnp.where(use_prev, qk_prev, qk_cur) - bias_band
        if b == 0:
            s_band = jnp.where(use_prev & (n == 0), NEG, s_band)
        s_meta = jnp.where(valid_meta, qk_meta - slope * dist_meta, NEG)
        mx = jnp.maximum(jnp.max(jnp.maximum(s_band, s_meta), axis=1, keepdims=True), sink)
        e_band = jnp.exp(s_band - mx)
        e_meta = jnp.exp(s_meta - mx)
        den = jnp.sum(e_band + e_meta, axis=1, keepdims=True) + jnp.exp(sink - mx)
        acc = (_dot(jnp.where(use_prev, e_band, 0.0).astype(BF16), v_tiles[b])
               + _dot(jnp.where(use_prev, 0.0, e_band).astype(BF16), v_tiles[b + 1])
               + _dot(e_meta.astype(BF16), vm))
        yield
        acc = acc / den
        return jnp.where(lane_o >= HEAD, acc[BLOCK:], acc[:BLOCK]).astype(o_ref.dtype)

    tiles = [(slice(b * BLOCK, (b + 1) * BLOCK), slice(pr * LANES, (pr + 1) * LANES), pr, b)
             for b in range(q_blocks) for pr in range(group // 2)]
    results = _run_interleaved([pair_step(pr, b, q_ref[rs, ls]) for rs, ls, pr, b in tiles])
    for (rs, ls, _, _), res in zip(tiles, results):
        o_ref[rs, ls] = res


def _attention(q, kv_dup, kv_meta_dup, sinks, *, n_heads):
    s_len, d = q.shape
    group = n_heads // ATT_KV_HEADS
    gw = group * HEAD
    nb = s_len // BLOCK
    q_blocks = 2 if nb % 2 == 0 else 1
    cur = lambda h, m: (m, h)
    k_prev = lambda h, m: (jnp.maximum(m * q_blocks - 1, 0), h)
    k_meta = lambda h, m: (0, h)
    v_prev = lambda h, m: (jnp.maximum(m * q_blocks - 1, 0), ATT_KV_HEADS + h)
    v_cur = lambda h, m: (m, ATT_KV_HEADS + h)
    v_meta = lambda h, m: (0, ATT_KV_HEADS + h)
    one = lambda index_map: pl.BlockSpec((BLOCK, LANES), index_map)
    own = lambda index_map: pl.BlockSpec((q_blocks * BLOCK, LANES), index_map)
    return pl.pallas_call(
        functools.partial(_attn_body, group=group, n_heads=n_heads, q_blocks=q_blocks),
        out_shape=jax.ShapeDtypeStruct((s_len, d), BF16),
        grid=(ATT_KV_HEADS, nb // q_blocks),
        in_specs=[pl.BlockSpec((q_blocks * BLOCK, gw), cur),
                  one(k_prev), own(cur), one(k_meta), one(v_prev), own(v_cur), one(v_meta),
                  pl.BlockSpec(memory_space=pltpu.SMEM)],
        out_specs=pl.BlockSpec((q_blocks * BLOCK, gw), cur),
        compiler_params=_cparams(("parallel", "parallel")),
        name="swa_attention",
    )(q, kv_dup, kv_dup, kv_meta_dup, kv_dup, kv_dup, kv_meta_dup, sinks)


def _router_body(h_ref, g_ref, wr_ref, sel_ref, gate_ref, *, n_experts):
    y = _rms(h_ref[...]) * g_ref[...]
    logits = jnp.dot(y, wr_ref[...], preferred_element_type=F32, precision=lax.Precision.HIGHEST)
    lane = lax.broadcasted_iota(jnp.int32, logits.shape, 1)
    logits = jnp.where(lane < n_experts, logits, NEG)
    m1 = jnp.max(logits, axis=1, keepdims=True)
    i1 = jnp.min(jnp.where(logits == m1, lane, LANES), axis=1, keepdims=True)
    rest = jnp.where(lane == i1, NEG, logits)
    m2 = jnp.max(rest, axis=1, keepdims=True)
    i2 = jnp.min(jnp.where(rest == m2, lane, LANES), axis=1, keepdims=True)
    e2 = jnp.exp(m2 - m1)
    g1 = 1.0 / (1.0 + e2)
    g2 = e2 / (1.0 + e2)
    sel_ref[...] = jnp.where(lane == 0, i1, jnp.where(lane == 1, i2, 0))
    gate_ref[...] = jnp.where(lane == 0, g1, jnp.where(lane == 1, g2, 0.0))


def _router(h, gain, w_router_pad, *, n_experts, tm):
    m, d = h.shape
    return pl.pallas_call(
        functools.partial(_router_body, n_experts=n_experts),
        out_shape=[jax.ShapeDtypeStruct((m, LANES), jnp.int32), jax.ShapeDtypeStruct((m, LANES), F32)],
        grid=(m // tm,),
        in_specs=[pl.BlockSpec((tm, d), lambda i: (i, 0)),
                  pl.BlockSpec((1, d), lambda i: (0, 0)),
                  pl.BlockSpec((d, LANES), lambda i: (0, 0))],
        out_specs=[pl.BlockSpec((tm, LANES), lambda i: (i, 0)), pl.BlockSpec((tm, LANES), lambda i: (i, 0))],
        compiler_params=_cparams(("parallel",)),
        name="moe_router",
    )(h, gain, w_router_pad)


EXPERT_TILE = 256


def _row_copy(src_hbm, dst, src_row, dst_row, sem):
    return pltpu.make_async_copy(src_hbm.at[pl.ds(src_row, 1)], dst.at[pl.ds(dst_row, 1)], sem)


ROW_DMA_UNROLL = 8


def _gather_rows_body(src_ref, x_hbm, gain_ref, o_ref, buf, sem, *, rows):
    i = pl.program_id(0)

    def issue(tile, slot):
        base = tile * rows

        def body(r, carry):
            _row_copy(x_hbm, buf.at[slot], src_ref[base + r], r, sem.at[slot]).start()
            return carry

        lax.fori_loop(0, rows, body, 0, unroll=ROW_DMA_UNROLL)

    @pl.when(i == 0)
    def _():
        issue(0, 0)

    @pl.when(i + 1 < pl.num_programs(0))
    def _():
        issue(i + 1, (i + 1) % 2)

    slot = i % 2
    pltpu.make_async_copy(x_hbm.at[pl.ds(0, rows)], buf.at[slot], sem.at[slot]).wait()
    o_ref[...] = (_rms(buf[slot]) * gain_ref[...]).astype(o_ref.dtype)


def _gather_rows(row_src, x, gain, n_rows, *, rows, out_dtype):
    d = x.shape[1]
    return pl.pallas_call(
        functools.partial(_gather_rows_body, rows=rows),
        out_shape=jax.ShapeDtypeStruct((n_rows, d), out_dtype),
        grid_spec=pltpu.PrefetchScalarGridSpec(
            num_scalar_prefetch=1,
            grid=(n_rows // rows,),
            in_specs=[pl.BlockSpec(memory_space=pl.ANY), pl.BlockSpec((1, d), lambda i, src: (0, 0))],
            out_specs=pl.BlockSpec((rows, d), lambda i, src: (i, 0)),
            scratch_shapes=[pltpu.VMEM((2, rows, d), x.dtype), pltpu.SemaphoreType.DMA((2,))],
        ),
        compiler_params=_cparams(("arbitrary",)),
        name="moe_gather",
    )(row_src, x, gain)


def _stage_expert_weights(te_ref, nxt_ref, w_hbms, wbuf, w16, sem, tn):
    j = pl.program_id(0)
    i = pl.program_id(1)

    def copies(expert, col_block):
        cols = pl.ds(pl.multiple_of(col_block * tn, tn), tn)
        return [pltpu.make_async_copy(w.at[expert, :, cols], wbuf.at[k], sem) for k, w in enumerate(w_hbms)]

    @pl.when((j == 0) & (i == 0))
    def _():
        for c in copies(te_ref[0], 0):
            c.start()

    @pl.when((i == 0) | (te_ref[i] != te_ref[jnp.maximum(i - 1, 0)]))
    def _():
        for c in copies(te_ref[i], j):
            c.wait()
        for k in range(len(w_hbms)):
            _round_to_bf16(wbuf.at[k], w16.at[k])
        next_expert = nxt_ref[i]
        wraps = next_expert < 0

        @pl.when(jnp.logical_not(wraps & (j == pl.num_programs(0) - 1)))
        def _():
            for c in copies(jnp.where(wraps, te_ref[0], next_expert), jnp.where(wraps, j + 1, j)):
                c.start()


def _moe_swiglu_body(te_ref, nxt_ref, used_ref, x_ref, wg_hbm, wu_hbm, o_ref, wbuf, w16, sem, *, tn):
    _stage_expert_weights(te_ref, nxt_ref, (wg_hbm, wu_hbm), wbuf, w16, sem, tn)
    has_tokens = pl.program_id(1) < used_ref[0]

    @pl.when(has_tokens)
    def _():
        x = x_ref[...]
        g = _dot(x, w16[0])
        u = _dot(x, w16[1])
        o_ref[...] = (g * jax.nn.sigmoid(g) * u).astype(o_ref.dtype)

    @pl.when(jnp.logical_not(has_tokens))
    def _():
        o_ref[...] = jnp.zeros_like(o_ref)


def _moe_down_body(te_ref, nxt_ref, used_ref, x_ref, w_hbm, o_ref, wbuf, w16, sem, *, tn):
    _stage_expert_weights(te_ref, nxt_ref, (w_hbm,), wbuf, w16, sem, tn)
    has_tokens = pl.program_id(1) < used_ref[0]

    @pl.when(has_tokens)
    def _():
        o_ref[...] = _dot(x_ref[...], w16[0])

    @pl.when(jnp.logical_not(has_tokens))
    def _():
        o_ref[...] = jnp.zeros_like(o_ref)


def _moe_matmul(body, tile_expert, next_expert, tiles_used, xs, weights, *, tn, out_dtype, name):
    rows, k = xs.shape
    n = weights[0].shape[-1]
    n_w = len(weights)
    return pl.pallas_call(
        functools.partial(body, tn=tn),
        out_shape=jax.ShapeDtypeStruct((rows, n), out_dtype),
        grid_spec=pltpu.PrefetchScalarGridSpec(
            num_scalar_prefetch=3,
            grid=(n // tn, rows // EXPERT_TILE),
            in_specs=[pl.BlockSpec((EXPERT_TILE, k), lambda j, i, te, nxt, used: (i, 0))]
            + [pl.BlockSpec(memory_space=pl.ANY)] * n_w,
            out_specs=pl.BlockSpec((EXPERT_TILE, tn), lambda j, i, te, nxt, used: (i, j)),
            scratch_shapes=[pltpu.VMEM((n_w, k, tn), F32), pltpu.VMEM((n_w, k, tn), BF16),
                            pltpu.SemaphoreType.DMA],
        ),
        compiler_params=_cparams(("arbitrary", "arbitrary")),
        name=name,
    )(tile_expert, next_expert, tiles_used, xs, *weights)


def _moe_combine_body(pos_ref, h_ref, gate_ref, fn_ref, eo_hbm, o_ref, buf, sem, *, tm):
    i = pl.program_id(0)

    def issue(tile, slot):
        base = tile * tm

        def body(t, carry):
            for j in range(TOP_K):
                _row_copy(eo_hbm, buf.at[slot, j], pos_ref[TOP_K * (base + t) + j], t, sem.at[slot]).start()
            return carry

        lax.fori_loop(0, tm, body, 0, unroll=ROW_DMA_UNROLL // TOP_K)

    @pl.when(i == 0)
    def _():
        issue(0, 0)

    @pl.when(i + 1 < pl.num_programs(0))
    def _():
        issue(i + 1, (i + 1) % 2)

    slot = i % 2
    for j in range(TOP_K):
        pltpu.make_async_copy(eo_hbm.at[pl.ds(0, tm)], buf.at[slot, j], sem.at[slot]).wait()
    y = h_ref[...]
    for j in range(TOP_K):
        y = y + gate_ref[:, j:j + 1] * buf[slot, j]
    o_ref[...] = _rms(y) * fn_ref[...]


def _moe_combine(pos, h, gate, final_gain, eo, *, tm):
    m, d = h.shape
    return pl.pallas_call(
        functools.partial(_moe_combine_body, tm=tm),
        out_shape=jax.ShapeDtypeStruct((m, d), F32),
        grid_spec=pltpu.PrefetchScalarGridSpec(
            num_scalar_prefetch=1,
            grid=(m // tm,),
            in_specs=[pl.BlockSpec((tm, d), lambda i, pos: (i, 0)),
                      pl.BlockSpec((tm, LANES), lambda i, pos: (i, 0)),
                      pl.BlockSpec((1, d), lambda i, pos: (0, 0)),
                      pl.BlockSpec(memory_space=pl.ANY)],
            out_specs=pl.BlockSpec((tm, d), lambda i, pos: (i, 0)),
            scratch_shapes=[pltpu.VMEM((2, TOP_K, tm, d), F32), pltpu.SemaphoreType.DMA((2,))],
        ),
        compiler_params=_cparams(("arbitrary",)),
        name="moe_combine",
    )(pos, h, gate, final_gain, eo)


def _moe_plan(sel, n_experts):
    m = sel.shape[0]
    flat_e = sel[:, :TOP_K].reshape(-1)
    onehot = (flat_e[:, None] == jnp.arange(n_experts, dtype=jnp.int32)[None, :]).astype(jnp.int32)
    csum = jnp.cumsum(onehot, axis=0)
    rank = jnp.sum(csum * onehot, axis=1) - 1
    counts = csum[-1]
    padded = (counts + EXPERT_TILE - 1) // EXPERT_TILE * EXPERT_TILE
    ends = jnp.cumsum(padded)
    starts = ends - padded
    pos = (jnp.sum(onehot * starts[None, :], axis=1) + rank).astype(jnp.int32)
    n_rows = TOP_K * m + n_experts * EXPERT_TILE
    token = jnp.arange(TOP_K * m, dtype=jnp.int32) // TOP_K
    row_src = jnp.zeros((n_rows,), jnp.int32).at[pos].set(token)
    tile_start = jnp.arange(n_rows // EXPERT_TILE, dtype=jnp.int32) * EXPERT_TILE
    tile_expert = jnp.minimum(jnp.sum((tile_start[:, None] >= ends[None, :]).astype(jnp.int32), axis=1),
                              n_experts - 1).astype(jnp.int32)
    run_end = jnp.sum((tile_expert[None, :] <= tile_expert[:, None]).astype(jnp.int32), axis=1)
    n_tiles = tile_expert.shape[0]
    next_expert = jnp.where(run_end < n_tiles, tile_expert[jnp.minimum(run_end, n_tiles - 1)], -1).astype(jnp.int32)
    tiles_used = (ends[-1:] // EXPERT_TILE).astype(jnp.int32)
    return pos, row_src, tile_expert, next_expert, tiles_used, n_rows


def _largest_tile(n, cap, mult):
    best = None
    t = mult
    while t <= min(n, cap):
        if n % t == 0:
            best = t
        t += mult
    assert best is not None, (n, cap, mult)
    return best


def _pad_cols(w, n_to):
    return jnp.pad(w, ((0, 0), (0, n_to - w.shape[1])))


def _pad_rows(w, n_to):
    return jnp.pad(w, ((0, n_to - w.shape[0]), (0, 0)))


def _round_up(n, m):
    return -(-n // m) * m


def kernel(x, meta_tokens, a_norm, a_mix, a_w_rkv, a_w0, a_w1, a_w2, a_a0, a_a1, a_a2, a_g1, a_g2, a_k_k, a_k_a, a_r_k, a_lnx_w, a_lnx_b, a_w_out, kv_norm, w_kv, b_norm, b_w_q, b_sinks, b_w_out, f_norm, d_w_gate, d_w_up, d_w_down, e_router, e_w_gate, e_w_up, e_w_down, final_norm):
    assert x.shape[0] == 1
    seq, d = x.shape[1], x.shape[2]
    n_heads = d // HEAD
    n_experts = e_router.shape[-1]
    xs = x[0]

    l_real = N_META + seq
    l_pad = _round_up(l_real, CHUNK)
    h = jnp.concatenate([meta_tokens.astype(F32), xs, jnp.zeros((l_pad - l_real, d), F32)], axis=0)
    tm0 = _largest_tile(l_pad, 768, 16)
    tn = _largest_tile(d, 512, LANES)

    gate_rank = _round_up(a_g1.shape[-1], LANES)
    xr, xk, xv, lw, a, g = _premix(
        h, a_norm[0:1], a_mix[0],
        a_w1[0].astype(BF16), a_a1[0].astype(BF16), _pad_cols(a_g1[0], gate_rank).astype(BF16),
        a_w2[0].astype(BF16), a_a2[0].astype(BF16), _pad_rows(a_g2[0], gate_rank).astype(BF16),
        jnp.stack([a_w0[0], a_a0[0]]), _largest_tile(l_pad, 256, 16))
    r = _mm(xr, a_w_rkv[0], out_dtype=F32, tm=tm0, tn=tn, w_index=0)
    k = _mm(xk, a_w_rkv[0], out_dtype=F32, tm=tm0, tn=tn, w_index=1)
    v = _mm(xv, a_w_rkv[0], out_dtype=F32, tm=tm0, tn=tn, w_index=2)
    prm = jnp.concatenate([a_k_k[0][None], a_k_a[0][None], a_r_k[0].reshape(1, d), a_lnx_w[0][None],
                           a_lnx_b[0][None], jnp.zeros((3, d), F32)], axis=0)
    pairs = 16 if n_heads % 32 == 0 else n_heads // 2
    mixed = _wkv(r, k, v, lw, a, g, prm, pairs=pairs)
    h = _mm(mixed, a_w_out[0], out_dtype=F32, tm=tm0, tn=tn, res=h)

    (hn,) = _rmsnorm(h, f_norm[0:1], [BF16], _largest_tile(l_pad, 256, 16))
    d_ff = d_w_gate.shape[-1]
    act = _swiglu(hn, d_w_gate[0], d_w_up[0], tm=_largest_tile(l_pad, 1408, 16), tn=_largest_tile(d_ff, 256, LANES))
    h = _mm_acc(act, d_w_down[0].astype(BF16), h, tm=tm0, tn=_largest_tile(d, 1024, LANES),
                tk=_largest_tile(d_ff, 6144, LANES))

    w_kv_dup = jnp.broadcast_to(w_kv.reshape(d, 2 * ATT_KV_HEADS, 1, HEAD),
                                (d, 2 * ATT_KV_HEADS, 2, HEAD)).reshape(d, 2 * ATT_KV_HEADS * LANES)
    tm1 = _largest_tile(seq, 1024, 16)
    (hkv_meta,) = _rmsnorm(h[:CHUNK], kv_norm[None], [BF16], CHUNK)
    kv_meta = _mm(hkv_meta, w_kv_dup, out_dtype=BF16, tm=CHUNK, tn=tn)
    kv_meta = _pad_rows(kv_meta[:N_META], BLOCK)
    h, hkv, hq = _drop_meta(h, jnp.stack([kv_norm, b_norm[0]]), seq, _largest_tile(seq, 256, N_META))
    kv_real = _mm(hkv, w_kv_dup, out_dtype=BF16, tm=tm1, tn=tn)

    q = _mm(hq, b_w_q[0], out_dtype=BF16, tm=tm1, tn=tn, scale=HEAD ** -0.5)
    o = _attention(q, kv_real, kv_meta, b_sinks[0].reshape(ATT_KV_HEADS, -1), n_heads=n_heads)
    h = _mm(o, b_w_out[0], out_dtype=F32, tm=tm1, tn=tn, res=h)

    sel, gate = _router(h, f_norm[1:2], _pad_cols(e_router[0], LANES), n_experts=n_experts,
                        tm=_largest_tile(seq, 256, 16))
    pos, row_src, tile_expert, next_expert, tiles_used, n_rows = _moe_plan(sel, n_experts)
    xs_sorted = _gather_rows(row_src, h, f_norm[1:2], n_rows, rows=_largest_tile(n_rows, 512, EXPERT_TILE),
                             out_dtype=BF16)
    d_exp = e_w_gate.shape[-1]
    act = _moe_matmul(_moe_swiglu_body, tile_expert, next_expert, tiles_used, xs_sorted, (e_w_gate[0], e_w_up[0]),
                      tn=_largest_tile(d_exp, 896, LANES), out_dtype=BF16, name="moe_swiglu")
    eo = _moe_matmul(_moe_down_body, tile_expert, next_expert, tiles_used, act, (e_w_down[0],),
                     tn=_largest_tile(d, 1024, LANES), out_dtype=F32, name="moe_down")
    out = _moe_combine(pos, h, gate, final_norm[None], eo, tm=_largest_tile(seq, 128, 8))
    return out[None]
```

```python
import functools

import jax
import jax.numpy as jnp
from jax import lax
from jax.experimental import pallas as pl
from jax.experimental.pallas import tpu as pltpu

F32 = jnp.float32
BF16 = jnp.bfloat16

LANES = 128
VMEM_LIMIT_BYTES = 56 * 1024 * 1024

N_META = 16
RMS_EPS = 1e-5
LN_X_EPS = 64e-5
HEAD = 64
ATT_KV_HEADS = 8
WINDOW = 128
BLOCK = 128
TOP_K = 2
CHUNK = 64
NEG = -1e30


def _cparams(sem):
    return pltpu.CompilerParams(dimension_semantics=sem, vmem_limit_bytes=VMEM_LIMIT_BYTES)


def _dot(a, b):
    return jnp.dot(a, b, preferred_element_type=F32)


def _dot_nt(a, b):
    return lax.dot_general(a, b, (((1,), (1,)), ((), ())), preferred_element_type=F32)


def _rms(x):
    return x * lax.rsqrt(jnp.mean(x * x, axis=-1, keepdims=True) + RMS_EPS)


CAST_ROWS = 256


def _round_to_bf16(src_ref, dst_ref):
    rows = src_ref.shape[0]
    chunk = CAST_ROWS if rows % CAST_ROWS == 0 else rows

    def body(c, carry):
        r = pl.ds(pl.multiple_of(c * chunk, chunk), chunk)
        dst_ref[r, :] = src_ref[r, :].astype(BF16)
        return carry

    lax.fori_loop(0, rows // chunk, body, 0)


def _run_interleaved(steps, shared=None):
    results = [None] * len(steps)
    pending = list(range(len(steps)))
    inbox = {idx: None for idx in pending}
    while pending:
        requests = {}
        for idx in list(pending):
            try:
                req = steps[idx].send(inbox[idx])
            except StopIteration as done:
                results[idx] = done.value
                pending.remove(idx)
                continue
            inbox[idx] = None
            if req is not None:
                requests.setdefault(req[0], []).append((idx, req[1]))
        for key, members in requests.items():
            out = shared[key](jnp.concatenate([rows for _, rows in members], axis=0))
            start = 0
            for idx, rows in members:
                inbox[idx] = out[start:start + rows.shape[0]]
                start += rows.shape[0]
    return results


def _rmsnorm_body(h_ref, g_ref, *o_refs):
    y = _rms(h_ref[...])
    for j, o_ref in enumerate(o_refs):
        o_ref[...] = (y * g_ref[j:j + 1, :]).astype(o_ref.dtype)


def _rmsnorm(h, gains, out_dtypes, tm):
    m, d = h.shape
    n_out = len(out_dtypes)
    outs = pl.pallas_call(
        _rmsnorm_body,
        out_shape=[jax.ShapeDtypeStruct((m, d), dt) for dt in out_dtypes],
        grid=(m // tm,),
        in_specs=[pl.BlockSpec((tm, d), lambda i: (i, 0)),
                  pl.BlockSpec((n_out, d), lambda i: (0, 0))],
        out_specs=[pl.BlockSpec((tm, d), lambda i: (i, 0)) for _ in out_dtypes],
        compiler_params=_cparams(("parallel",)),
        name="rmsnorm",
    )(h, gains)
    return outs


def _drop_meta_body(ha_ref, hb_ref, g_ref, h_ref, *o_refs):
    rows = jnp.concatenate([ha_ref[N_META:, :], hb_ref[...]], axis=0)
    h_ref[...] = rows
    y = _rms(rows)
    for j, o_ref in enumerate(o_refs):
        o_ref[...] = (y * g_ref[j:j + 1, :]).astype(o_ref.dtype)


def _drop_meta(h, gains, seq, tm):
    d = h.shape[1]
    n_out = gains.shape[0]
    per = tm // N_META
    return pl.pallas_call(
        _drop_meta_body,
        out_shape=[jax.ShapeDtypeStruct((seq, d), F32)] + [jax.ShapeDtypeStruct((seq, d), BF16)] * n_out,
        grid=(seq // tm,),
        in_specs=[pl.BlockSpec((tm, d), lambda i: (i, 0)),
                  pl.BlockSpec((N_META, d), lambda i: ((i + 1) * per, 0)),
                  pl.BlockSpec((n_out, d), lambda i: (0, 0))],
        out_specs=[pl.BlockSpec((tm, d), lambda i: (i, 0))] * (1 + n_out),
        compiler_params=_cparams(("parallel",)),
        name="drop_meta",
    )(h, h, gains)


def _premix_body(h_ref, hp_ref, gain_ref, mix_ref, w1_ref, a1_ref, g1_ref, w2_ref, a2_ref, g2_ref, bias_ref,
                 xr_ref, xk_ref, xv_ref, lw_ref, a_ref, g_ref):
    i = pl.program_id(0)
    gain = gain_ref[...]
    xn = _rms(h_ref[...]) * gain
    pn = _rms(hp_ref[...]) * gain
    prev_row = jnp.where(i > 0, pn[7:8, :], 0.0)
    sh = pltpu.roll(xn, 1, axis=0)
    row = lax.broadcasted_iota(jnp.int32, xn.shape, 0)
    sh = jnp.where(row == 0, prev_row, sh)
    xx = sh - xn

    def mixed(j):
        return (xn + xx * mix_ref[j:j + 1, :]).astype(BF16)

    xr_ref[...] = mixed(0)
    xk_ref[...] = mixed(2)
    xv_ref[...] = mixed(3)
    t = jnp.tanh(_dot(mixed(1), w1_ref[...]))
    z = _dot(t.astype(BF16), w2_ref[...]) + bias_ref[0:1, :]
    lw_ref[...] = jax.nn.sigmoid(z) * (-0.6065306597126334)
    z = _dot(_dot(mixed(4), a1_ref[...]).astype(BF16), a2_ref[...]) + bias_ref[1:2, :]
    a_ref[...] = jax.nn.sigmoid(z).astype(a_ref.dtype)
    t = jax.nn.sigmoid(_dot(mixed(5), g1_ref[...]))
    g_ref[...] = _dot(t.astype(BF16), g2_ref[...]).astype(g_ref.dtype)


def _premix(h, gain, mix, w1, a1, g1, w2, a2, g2, bias, tm):
    m, d = h.shape
    rows8 = tm // 8
    row_blk = lambda i: (i, 0)
    whole = lambda arr: pl.BlockSpec(arr.shape, lambda i: (0, 0), pipeline_mode=pl.Buffered(1))
    return pl.pallas_call(
        _premix_body,
        out_shape=[jax.ShapeDtypeStruct((m, d), dt) for dt in (BF16, BF16, BF16, F32, BF16, BF16)],
        grid=(m // tm,),
        in_specs=[pl.BlockSpec((tm, d), row_blk),
                  pl.BlockSpec((8, d), lambda i: (jnp.maximum(i * rows8 - 1, 0), 0)),
                  whole(gain), whole(mix), whole(w1), whole(a1), whole(g1), whole(w2), whole(a2), whole(g2),
                  whole(bias)],
        out_specs=[pl.BlockSpec((tm, d), row_blk)] * 6,
        compiler_params=_cparams(("parallel",)),
        name="premix",
    )(h, h, gain, mix, w1, a1, g1, w2, a2, g2, bias)


def _mm_body(x_ref, w_ref, *rest, has_res, scale):
    if has_res:
        res_ref, o_ref, w16_ref = rest
    else:
        o_ref, w16_ref = rest

    @pl.when(pl.program_id(1) == 0)
    def _():
        _round_to_bf16(w_ref, w16_ref)

    acc = _dot(x_ref[...], w16_ref[...])
    if scale is not None:
        acc = acc * scale
    if has_res:
        acc = acc + res_ref[...]
    o_ref[...] = acc.astype(o_ref.dtype)


def _mm(x, w, *, out_dtype, tm, tn, res=None, scale=None, w_index=None):
    m, k = x.shape
    n = w.shape[-1]
    if w_index is None:
        w_spec = pl.BlockSpec((k, tn), lambda j, i: (0, j))
    else:
        w_spec = pl.BlockSpec((None, k, tn), lambda j, i: (w_index, 0, j))
    in_specs = [pl.BlockSpec((tm, k), lambda j, i: (i, 0)), w_spec]
    args = [x, w]
    if res is not None:
        in_specs.append(pl.BlockSpec((tm, tn), lambda j, i: (i, j)))
        args.append(res)
    return pl.pallas_call(
        functools.partial(_mm_body, has_res=res is not None, scale=scale),
        out_shape=jax.ShapeDtypeStruct((m, n), out_dtype),
        grid=(n // tn, m // tm),
        in_specs=in_specs,
        out_specs=pl.BlockSpec((tm, tn), lambda j, i: (i, j)),
        scratch_shapes=[pltpu.VMEM((k, tn), BF16)],
        compiler_params=_cparams(("parallel", "arbitrary")),
        name="matmul",
    )(*args)


def _mm_acc_body(x_ref, w_ref, res_ref, o_ref, acc_ref):
    kk = pl.program_id(2)

    @pl.when(kk == 0)
    def _():
        acc_ref[...] = jnp.zeros_like(acc_ref)

    acc_ref[...] += _dot(x_ref[...], w_ref[...])

    @pl.when(kk == pl.num_programs(2) - 1)
    def _():
        o_ref[...] = res_ref[...] + acc_ref[...]


def _mm_acc(x, w, res, *, tm, tn, tk):
    m, k = x.shape
    n = w.shape[1]
    return pl.pallas_call(
        _mm_acc_body,
        out_shape=jax.ShapeDtypeStruct((m, n), F32),
        grid=(n // tn, m // tm, k // tk),
        in_specs=[pl.BlockSpec((tm, tk), lambda j, i, q: (i, q)),
                  pl.BlockSpec((tk, tn), lambda j, i, q: (q, j)),
                  pl.BlockSpec((tm, tn), lambda j, i, q: (i, j))],
        out_specs=pl.BlockSpec((tm, tn), lambda j, i, q: (i, j)),
        scratch_shapes=[pltpu.VMEM((tm, tn), F32)],
        compiler_params=_cparams(("parallel", "parallel", "arbitrary")),
        name="matmul_acc",
    )(x, w, res)


def _swiglu_body(x_ref, wg_ref, wu_ref, o_ref, wg16_ref, wu16_ref):
    @pl.when(pl.program_id(1) == 0)
    def _():
        _round_to_bf16(wg_ref, wg16_ref)
        _round_to_bf16(wu_ref, wu16_ref)

    x = x_ref[...]
    g = _dot(x, wg16_ref[...])
    u = _dot(x, wu16_ref[...])
    o_ref[...] = (g * jax.nn.sigmoid(g) * u).astype(o_ref.dtype)


def _swiglu(x, wg, wu, *, tm, tn):
    m, k = x.shape
    n = wg.shape[1]
    return pl.pallas_call(
        _swiglu_body,
        out_shape=jax.ShapeDtypeStruct((m, n), BF16),
        grid=(n // tn, m // tm),
        in_specs=[pl.BlockSpec((tm, k), lambda j, i: (i, 0)),
                  pl.BlockSpec((k, tn), lambda j, i: (0, j)),
                  pl.BlockSpec((k, tn), lambda j, i: (0, j))],
        out_specs=pl.BlockSpec((tm, tn), lambda j, i: (i, j)),
        scratch_shapes=[pltpu.VMEM((k, tn), BF16), pltpu.VMEM((k, tn), BF16)],
        compiler_params=_cparams(("parallel", "arbitrary")),
        name="swiglu",
    )(x, wg, wu)


def _split2(x):
    hi = x.astype(BF16)
    lo = (x - hi.astype(F32)).astype(BF16)
    return hi, lo


def _wkv_body(r_ref, k_ref, v_ref, lw_ref, a_ref, g_ref, prm_ref, o_ref, st_ref, *, pairs):
    c_idx = pl.program_id(1)

    @pl.when(c_idx == 0)
    def _():
        st_ref[...] = jnp.zeros_like(st_ref)

    C = CHUNK
    lane = lax.broadcasted_iota(jnp.int32, (C, LANES), 1)
    row = lax.broadcasted_iota(jnp.int32, (C, LANES), 0)
    col = lane % HEAD
    upper_half = lane >= HEAD
    tri_strict = col < row
    tri_incl = col <= row
    eye_pair = (col == row).astype(F32)

    def level_mask(bs):
        return ((row // bs) == (col // bs)) & ((row // (bs // 2)) != (col // (bs // 2)))

    row2 = lax.broadcasted_iota(jnp.int32, (2 * C, LANES), 0)
    lane2 = lax.broadcasted_iota(jnp.int32, (2 * C, LANES), 1)
    bd_mask = (row2 >= C) == (lane2 >= HEAD)
    ones_bd = bd_mask.astype(BF16)
    diag_mask = row2 == lane2

    def bd(x):
        return jnp.concatenate([jnp.where(upper_half, 0.0, x), jnp.where(upper_half, x, 0.0)], axis=0)

    def headsum(x16):
        return _dot(x16, ones_bd)

    def pair_step(r, k, v, lw, a, g, prm, s):
        k_k, k_a, r_k, lnx_w, lnx_b = (prm[j:j + 1] for j in range(5))

        kk = k * k_k
        k2 = k * (1.0 + (a - 1.0) * k_a)
        sums = yield ("headsum", jnp.concatenate([kk * kk, r * k2 * r_k], axis=0).astype(BF16))
        cum = lw
        shift = 1
        while shift < C:
            cum = cum + jnp.where(row >= shift, pltpu.roll(cum, shift, axis=0), 0.0)
            shift *= 2
        kkn = kk * lax.rsqrt(jnp.maximum(sums[:C], 1e-24))
        bonus = sums[C:] * v
        avec = -kkn
        bvec = kkn * a
        cum_end = cum[C - 1:C, :]
        p_incl = jnp.exp(cum)
        p_inv = jnp.exp(-cum)
        a_t = avec * jnp.exp(cum - lw)
        r_t = r * p_incl
        b_t = bvec * p_inv
        k_t = k2 * p_inv
        p_end = jnp.exp(cum_end - cum)
        b_h = bvec * p_end
        k_h = k2 * p_end

        v16 = v.astype(BF16)
        gram = _dot_nt(jnp.concatenate([a_t, r_t], axis=0).astype(BF16),
                       jnp.concatenate([bd(b_t), bd(k_t)], axis=0).astype(BF16))
        yield
        a_ab = jnp.where(tri_strict, gram[:C, :LANES], 0.0)
        a_ak = jnp.where(tri_strict, gram[:C, LANES:], 0.0)
        a_rb = jnp.where(tri_incl, gram[C:, :LANES], 0.0)
        a_rk = jnp.where(tri_incl, gram[C:, LANES:], 0.0)
        akv = _dot(a_ak.astype(BF16), bd(v16))

        t_inv = eye_pair + jnp.where(level_mask(2), a_ab, 0.0)
        bs = 4
        while bs <= C:
            x = jnp.where(level_mask(bs), a_ab, 0.0)
            t16 = t_inv.astype(BF16)
            tx = _dot(t16, bd(x).astype(BF16))
            yield
            t_inv = t_inv + _dot(tx.astype(BF16), bd(t16))
            yield
            bs *= 2
        t16 = t_inv.astype(BF16)

        tt = _dot(t16, jnp.concatenate([bd(a_t.astype(BF16)), bd(akv.astype(BF16))], axis=1))
        yield
        a_hat = tt[:, :LANES].astype(BF16)
        u0 = tt[:, LANES:].astype(BF16)

        a_rb16 = a_rb.astype(BF16)
        qa = _dot(a_rb16, bd(a_hat))
        y0 = _dot(jnp.concatenate([a_rb16, a_rk.astype(BF16)], axis=1),
                  jnp.concatenate([bd(u0), bd(v16)], axis=0))
        lhs_t = jnp.concatenate([b_h, k_h], axis=0).T.astype(BF16)
        upd_m = _dot(lhs_t[:, :C], a_hat)
        upd_n = _dot(lhs_t, jnp.concatenate([u0, v16], axis=0))
        yield
        q_hat = r_t + qa
        m_off = jnp.where(bd_mask, upd_m, 0.0)
        n_new = jnp.where(bd_mask, upd_n, 0.0)
        p_col = jnp.sum(jnp.where(diag_mask, jnp.exp(cum_end), 0.0), axis=1, keepdims=True)

        s_hi, s_lo = _split2(s)
        m16 = m_off.astype(BF16)
        ys = _dot(jnp.concatenate([q_hat.astype(BF16), m16], axis=0), s_hi)
        y = ys[:C] + y0
        s_new = p_col * s + ys[C:] + _dot(m16, s_lo) + n_new

        mu = (yield ("headsum", y.astype(BF16))) * (1.0 / HEAD)
        d = y - mu
        var = (yield ("headsum", (d * d).astype(BF16))) * (1.0 / HEAD)
        yn = d * lax.rsqrt(var + LN_X_EPS) * lnx_w + lnx_b
        return ((yn + bonus) * g.astype(F32)).astype(o_ref.dtype), s_new

    lanes = [slice(p * LANES, (p + 1) * LANES) for p in range(pairs)]
    results = _run_interleaved([pair_step(r_ref[:, sl], k_ref[:, sl], v_ref[:, sl], lw_ref[:, sl],
                                          a_ref[:, sl].astype(F32), g_ref[:, sl], prm_ref[:, sl], st_ref[p])
                                for p, sl in enumerate(lanes)],
                               shared={"headsum": headsum})
    for p, sl in enumerate(lanes):
        o_ref[:, sl] = results[p][0]
        st_ref[p] = results[p][1]


def _wkv(r, k, v, lw, a, g, prm, *, pairs):
    length, d = r.shape
    width = pairs * LANES
    blk = lambda hb, c: (c, hb)
    return pl.pallas_call(
        functools.partial(_wkv_body, pairs=pairs),
        out_shape=jax.ShapeDtypeStruct((length, d), BF16),
        grid=(d // width, length // CHUNK),
        in_specs=[pl.BlockSpec((CHUNK, width), blk)] * 6 + [pl.BlockSpec((8, width), lambda hb, c: (0, hb))],
        out_specs=pl.BlockSpec((CHUNK, width), blk),
        scratch_shapes=[pltpu.VMEM((pairs, LANES, LANES), F32)],
        compiler_params=_cparams(("parallel", "arbitrary")),
        name="wkv7",
    )(r, k, v, lw, a, g, prm)


def _attn_body(q_ref, kp_ref, kc_ref, km_ref, vp_ref, vc_ref, vm_ref, sk_ref, o_ref, *, group, n_heads, q_blocks):
    kvh = pl.program_id(0)
    first_block = pl.program_id(1) * q_blocks
    rows = 2 * BLOCK
    rowi = lax.broadcasted_iota(jnp.int32, (rows, BLOCK), 0)
    kj = lax.broadcasted_iota(jnp.int32, (rows, BLOCK), 1)
    qi = rowi % BLOCK
    second = rowi >= BLOCK
    use_prev = kj > qi
    dist_band = (qi - kj + jnp.where(use_prev, BLOCK, 0)).astype(F32)
    valid_meta = kj < N_META
    lane_o = lax.broadcasted_iota(jnp.int32, (BLOCK, LANES), 1)
    lane_q = lax.broadcasted_iota(jnp.int32, (rows, LANES), 1)
    row_q = lax.broadcasted_iota(jnp.int32, (rows, LANES), 0)
    q_keep = (row_q >= BLOCK) == (lane_q >= HEAD)

    km = km_ref[...]
    vm = vm_ref[...]
    k_tiles = [kp_ref[...]] + [kc_ref[b * BLOCK:(b + 1) * BLOCK, :] for b in range(q_blocks)]
    v_tiles = [vp_ref[...]] + [vc_ref[b * BLOCK:(b + 1) * BLOCK, :] for b in range(q_blocks)]

    def pair_bias(pr):
        head1 = (kvh * group + 2 * pr + 1).astype(F32)
        slope = jnp.exp2((jnp.where(second, 1.0, 0.0) + head1) * (-8.0 / n_heads))
        sink = jnp.where(second[:, :1], sk_ref[kvh, 2 * pr + 1], sk_ref[kvh, 2 * pr])
        return slope, slope * dist_band, sink

    biases = [pair_bias(pr) for pr in range(group // 2)]

    def pair_step(pr, b, q):
        n = first_block + b
        q2 = jnp.where(q_keep, jnp.concatenate([q, q], axis=0), jnp.zeros((), q.dtype))
        qk_prev = _dot_nt(q2, k_tiles[b])
        qk_cur = _dot_nt(q2, k_tiles[b + 1])
        qk_meta = _dot_nt(q2, km)
        yield
        slope, bias_band, sink = biases[pr]
        dist_meta = (N_META + n * BLOCK + qi - kj).astype(F32)
        s_band = jnp.where(use_prev, qk_prev, qk_cur) - bias_band
        if b == 0:
            s_band = jnp.where(use_prev & (n == 0), NEG, s_band)
        s_meta = jnp.where(valid_meta, qk_meta - slope * dist_meta, NEG)
        mx = jnp.maximum(jnp.max(jnp.maximum(s_band, s_meta), axis=1, keepdims=True), sink)
        e_band = jnp.exp(s_band - mx)
        e_meta = jnp.exp(s_meta - mx)
        den = jnp.sum(e_band + e_meta, axis=1, keepdims=True) + jnp.exp(sink - mx)
        acc = (_dot(jnp.where(use_prev, e_band, 0.0).astype(BF16), v_tiles[b])
               + _dot(jnp.where(use_prev, 0.0, e_band).astype(BF16), v_tiles[b + 1])
               + _dot(e_meta.astype(BF16), vm))
        yield
        acc = acc / den
        return jnp.where(lane_o >= HEAD, acc[BLOCK:], acc[:BLOCK]).astype(o_ref.dtype)

    tiles = [(slice(b * BLOCK, (b + 1) * BLOCK), slice(pr * LANES, (pr + 1) * LANES), pr, b)
             for b in range(q_blocks) for pr in range(group // 2)]
    results = _run_interleaved([pair_step(pr, b, q_ref[rs, ls]) for rs, ls, pr, b in tiles])
    for (rs, ls, _, _), res in zip(tiles, results):
        o_ref[rs, ls] = res


def _attention(q, kv_dup, kv_meta_dup, sinks, *, n_heads):
    s_len, d = q.shape
    group = n_heads // ATT_KV_HEADS
    gw = group * HEAD
    nb = s_len // BLOCK
    q_blocks = 2 if nb % 2 == 0 else 1
    cur = lambda h, m: (m, h)
    k_prev = lambda h, m: (jnp.maximum(m * q_blocks - 1, 0), h)
    k_meta = lambda h, m: (0, h)
    v_prev = lambda h, m: (jnp.maximum(m * q_blocks - 1, 0), ATT_KV_HEADS + h)
    v_cur = lambda h, m: (m, ATT_KV_HEADS + h)
    v_meta = lambda h, m: (0, ATT_KV_HEADS + h)
    one = lambda index_map: pl.BlockSpec((BLOCK, LANES), index_map)
    own = lambda index_map: pl.BlockSpec((q_blocks * BLOCK, LANES), index_map)
    return pl.pallas_call(
        functools.partial(_attn_body, group=group, n_heads=n_heads, q_blocks=q_blocks),
        out_shape=jax.ShapeDtypeStruct((s_len, d), BF16),
        grid=(ATT_KV_HEADS, nb // q_blocks),
        in_specs=[pl.BlockSpec((q_blocks * BLOCK, gw), cur),
                  one(k_prev), own(cur), one(k_meta), one(v_prev), own(v_cur), one(v_meta),
                  pl.BlockSpec(memory_space=pltpu.SMEM)],
        out_specs=pl.BlockSpec((q_blocks * BLOCK, gw), cur),
        compiler_params=_cparams(("parallel", "parallel")),
        name="swa_attention",
    )(q, kv_dup, kv_dup, kv_meta_dup, kv_dup, kv_dup, kv_meta_dup, sinks)


def _router_body(h_ref, g_ref, wr_ref, sel_ref, gate_ref, *, n_experts):
    y = _rms(h_ref[...]) * g_ref[...]
    y_hi, y_lo = _split2(y)
    w_hi, w_lo = _split2(wr_ref[...])
    logits = _dot(y_hi, w_hi) + _dot(y_lo, w_hi) + _dot(y_hi, w_lo)
    lane = lax.broadcasted_iota(jnp.int32, logits.shape, 1)
    logits = jnp.where(lane < n_experts, logits, NEG)
    m1 = jnp.max(logits, axis=1, keepdims=True)
    i1 = jnp.min(jnp.where(logits == m1, lane, LANES), axis=1, keepdims=True)
    rest = jnp.where(lane == i1, NEG, logits)
    m2 = jnp.max(rest, axis=1, keepdims=True)
    i2 = jnp.min(jnp.where(rest == m2, lane, LANES), axis=1, keepdims=True)
    e2 = jnp.exp(m2 - m1)
    g1 = 1.0 / (1.0 + e2)
    g2 = e2 / (1.0 + e2)
    sel_ref[...] = jnp.where(lane == 0, i1, jnp.where(lane == 1, i2, 0))
    gate_ref[...] = jnp.where(lane == 0, g1, jnp.where(lane == 1, g2, 0.0))


def _router(h, gain, w_router_pad, *, n_experts, tm):
    m, d = h.shape
    return pl.pallas_call(
        functools.partial(_router_body, n_experts=n_experts),
        out_shape=[jax.ShapeDtypeStruct((m, LANES), jnp.int32), jax.ShapeDtypeStruct((m, LANES), F32)],
        grid=(m // tm,),
        in_specs=[pl.BlockSpec((tm, d), lambda i: (i, 0)),
                  pl.BlockSpec((1, d), lambda i: (0, 0)),
                  pl.BlockSpec((d, LANES), lambda i: (0, 0))],
        out_specs=[pl.BlockSpec((tm, LANES), lambda i: (i, 0)), pl.BlockSpec((tm, LANES), lambda i: (i, 0))],
        compiler_params=_cparams(("parallel",)),
        name="moe_router",
    )(h, gain, w_router_pad)


EXPERT_TILE = 256


def _row_copy(src_hbm, dst, src_row, dst_row, sem):
    return pltpu.make_async_copy(src_hbm.at[pl.ds(src_row, 1)], dst.at[pl.ds(dst_row, 1)], sem)


ROW_DMA_UNROLL = 8


def _gather_rows_body(src_ref, x_hbm, gain_ref, o_ref, buf, sem, *, rows):
    i = pl.program_id(0)

    def issue(tile, slot):
        base = tile * rows

        def body(r, carry):
            _row_copy(x_hbm, buf.at[slot], src_ref[base + r], r, sem.at[slot]).start()
            return carry

        lax.fori_loop(0, rows, body, 0, unroll=ROW_DMA_UNROLL)

    @pl.when(i == 0)
    def _():
        issue(0, 0)

    @pl.when(i + 1 < pl.num_programs(0))
    def _():
        issue(i + 1, (i + 1) % 2)

    slot = i % 2
    pltpu.make_async_copy(x_hbm.at[pl.ds(0, rows)], buf.at[slot], sem.at[slot]).wait()
    o_ref[...] = (_rms(buf[slot]) * gain_ref[...]).astype(o_ref.dtype)


def _gather_rows(row_src, x, gain, n_rows, *, rows, out_dtype):
    d = x.shape[1]
    return pl.pallas_call(
        functools.partial(_gather_rows_body, rows=rows),
        out_shape=jax.ShapeDtypeStruct((n_rows, d), out_dtype),
        grid_spec=pltpu.PrefetchScalarGridSpec(
            num_scalar_prefetch=1,
            grid=(n_rows // rows,),
            in_specs=[pl.BlockSpec(memory_space=pl.ANY), pl.BlockSpec((1, d), lambda i, src: (0, 0))],
            out_specs=pl.BlockSpec((rows, d), lambda i, src: (i, 0)),
            scratch_shapes=[pltpu.VMEM((2, rows, d), x.dtype), pltpu.SemaphoreType.DMA((2,))],
        ),
        compiler_params=_cparams(("arbitrary",)),
        name="moe_gather",
    )(row_src, x, gain)


def _stage_expert_weights(te_ref, nxt_ref, w_hbms, wbuf, w16, sem, tn):
    j = pl.program_id(0)
    i = pl.program_id(1)

    def copies(expert, col_block):
        cols = pl.ds(pl.multiple_of(col_block * tn, tn), tn)
        return [pltpu.make_async_copy(w.at[expert, :, cols], wbuf.at[k], sem) for k, w in enumerate(w_hbms)]

    @pl.when((j == 0) & (i == 0))
    def _():
        for c in copies(te_ref[0], 0):
            c.start()

    @pl.when((i == 0) | (te_ref[i] != te_ref[jnp.maximum(i - 1, 0)]))
    def _():
        for c in copies(te_ref[i], j):
            c.wait()
        for k in range(len(w_hbms)):
            _round_to_bf16(wbuf.at[k], w16.at[k])
        next_expert = nxt_ref[i]
        wraps = next_expert < 0

        @pl.when(jnp.logical_not(wraps & (j == pl.num_programs(0) - 1)))
        def _():
            for c in copies(jnp.where(wraps, te_ref[0], next_expert), jnp.where(wraps, j + 1, j)):
                c.start()


def _moe_swiglu_body(te_ref, nxt_ref, used_ref, x_ref, wg_hbm, wu_hbm, o_ref, wbuf, w16, sem, *, tn):
    _stage_expert_weights(te_ref, nxt_ref, (wg_hbm, wu_hbm), wbuf, w16, sem, tn)
    has_tokens = pl.program_id(1) < used_ref[0]

    @pl.when(has_tokens)
    def _():
        x = x_ref[...]
        g = _dot(x, w16[0])
        u = _dot(x, w16[1])
        o_ref[...] = (g * jax.nn.sigmoid(g) * u).astype(o_ref.dtype)

    @pl.when(jnp.logical_not(has_tokens))
    def _():
        o_ref[...] = jnp.zeros_like(o_ref)


def _moe_down_body(te_ref, nxt_ref, used_ref, x_ref, w_hbm, o_ref, wbuf, w16, sem, *, tn):
    _stage_expert_weights(te_ref, nxt_ref, (w_hbm,), wbuf, w16, sem, tn)
    has_tokens = pl.program_id(1) < used_ref[0]

    @pl.when(has_tokens)
    def _():
        o_ref[...] = _dot(x_ref[...], w16[0])

    @pl.when(jnp.logical_not(has_tokens))
    def _():
        o_ref[...] = jnp.zeros_like(o_ref)


def _moe_matmul(body, tile_expert, next_expert, tiles_used, xs, weights, *, tn, out_dtype, name):
    rows, k = xs.shape
    n = weights[0].shape[-1]
    n_w = len(weights)
    return pl.pallas_call(
        functools.partial(body, tn=tn),
        out_shape=jax.ShapeDtypeStruct((rows, n), out_dtype),
        grid_spec=pltpu.PrefetchScalarGridSpec(
            num_scalar_prefetch=3,
            grid=(n // tn, rows // EXPERT_TILE),
            in_specs=[pl.BlockSpec((EXPERT_TILE, k), lambda j, i, te, nxt, used: (i, 0))]
            + [pl.BlockSpec(memory_space=pl.ANY)] * n_w,
            out_specs=pl.BlockSpec((EXPERT_TILE, tn), lambda j, i, te, nxt, used: (i, j)),
            scratch_shapes=[pltpu.VMEM((n_w, k, tn), F32), pltpu.VMEM((n_w, k, tn), BF16),
                            pltpu.SemaphoreType.DMA],
        ),
        compiler_params=_cparams(("arbitrary", "arbitrary")),
        name=name,
    )(tile_expert, next_expert, tiles_used, xs, *weights)


def _moe_combine_body(pos_ref, h_ref, gate_ref, fn_ref, eo_hbm, o_ref, buf, sem, *, tm):
    i = pl.program_id(0)

    def issue(tile, slot):
        base = tile * tm

        def body(t, carry):
            for j in range(TOP_K):
                _row_copy(eo_hbm, buf.at[slot, j], pos_ref[TOP_K * (base + t) + j], t, sem.at[slot]).start()
            return carry

        lax.fori_loop(0, tm, body, 0, unroll=ROW_DMA_UNROLL // TOP_K)

    @pl.when(i == 0)
    def _():
        issue(0, 0)

    @pl.when(i + 1 < pl.num_programs(0))
    def _():
        issue(i + 1, (i + 1) % 2)

    slot = i % 2
    for j in range(TOP_K):
        pltpu.make_async_copy(eo_hbm.at[pl.ds(0, tm)], buf.at[slot, j], sem.at[slot]).wait()
    y = h_ref[...]
    for j in range(TOP_K):
        y = y + gate_ref[:, j:j + 1] * buf[slot, j]
    o_ref[...] = _rms(y) * fn_ref[...]


def _moe_combine(pos, h, gate, final_gain, eo, *, tm):
    m, d = h.shape
    return pl.pallas_call(
        functools.partial(_moe_combine_body, tm=tm),
        out_shape=jax.ShapeDtypeStruct((m, d), F32),
        grid_spec=pltpu.PrefetchScalarGridSpec(
            num_scalar_prefetch=1,
            grid=(m // tm,),
            in_specs=[pl.BlockSpec((tm, d), lambda i, pos: (i, 0)),
                      pl.BlockSpec((tm, LANES), lambda i, pos: (i, 0)),
                      pl.BlockSpec((1, d), lambda i, pos: (0, 0)),
                      pl.BlockSpec(memory_space=pl.ANY)],
            out_specs=pl.BlockSpec((tm, d), lambda i, pos: (i, 0)),
            scratch_shapes=[pltpu.VMEM((2, TOP_K, tm, d), F32), pltpu.SemaphoreType.DMA((2,))],
        ),
        compiler_params=_cparams(("arbitrary",)),
        name="moe_combine",
    )(pos, h, gate, final_gain, eo)


def _moe_plan(sel, n_experts):
    m = sel.shape[0]
    flat_e = sel[:, :TOP_K].reshape(-1)
    onehot = (flat_e[:, None] == jnp.arange(n_experts, dtype=jnp.int32)[None, :]).astype(jnp.int32)
    csum = jnp.cumsum(onehot, axis=0)
    rank = jnp.sum(csum * onehot, axis=1) - 1
    counts = csum[-1]
    padded = (counts + EXPERT_TILE - 1) // EXPERT_TILE * EXPERT_TILE
    ends = jnp.cumsum(padded)
    starts = ends - padded
    pos = (jnp.sum(onehot * starts[None, :], axis=1) + rank).astype(jnp.int32)
    n_rows = TOP_K * m + n_experts * EXPERT_TILE
    token = jnp.arange(TOP_K * m, dtype=jnp.int32) // TOP_K
    row_src = jnp.zeros((n_rows,), jnp.int32).at[pos].set(token)
    tile_start = jnp.arange(n_rows // EXPERT_TILE, dtype=jnp.int32) * EXPERT_TILE
    tile_expert = jnp.minimum(jnp.sum((tile_start[:, None] >= ends[None, :]).astype(jnp.int32), axis=1),
                              n_experts - 1).astype(jnp.int32)
    run_end = jnp.sum((tile_expert[None, :] <= tile_expert[:, None]).astype(jnp.int32), axis=1)
    n_tiles = tile_expert.shape[0]
    next_expert = jnp.where(run_end < n_tiles, tile_expert[jnp.minimum(run_end, n_tiles - 1)], -1).astype(jnp.int32)
    tiles_used = (ends[-1:] // EXPERT_TILE).astype(jnp.int32)
    return pos, row_src, tile_expert, next_expert, tiles_used, n_rows


def _largest_tile(n, cap, mult):
    best = None
    t = mult
    while t <= min(n, cap):
        if n % t == 0:
            best = t
        t += mult
    assert best is not None, (n, cap, mult)
    return best


def _pad_cols(w, n_to):
    return jnp.pad(w, ((0, 0), (0, n_to - w.shape[1])))


def _pad_rows(w, n_to):
    return jnp.pad(w, ((0, n_to - w.shape[0]), (0, 0)))


def _round_up(n, m):
    return -(-n // m) * m


def kernel(x, meta_tokens, a_norm, a_mix, a_w_rkv, a_w0, a_w1, a_w2, a_a0, a_a1, a_a2, a_g1, a_g2, a_k_k, a_k_a, a_r_k, a_lnx_w, a_lnx_b, a_w_out, kv_norm, w_kv, b_norm, b_w_q, b_sinks, b_w_out, f_norm, d_w_gate, d_w_up, d_w_down, e_router, e_w_gate, e_w_up, e_w_down, final_norm):
    assert x.shape[0] == 1
    seq, d = x.shape[1], x.shape[2]
    n_heads = d // HEAD
    n_experts = e_router.shape[-1]
    xs = x[0]

    l_real = N_META + seq
    l_pad = _round_up(l_real, CHUNK)
    h = jnp.concatenate([meta_tokens.astype(F32), xs, jnp.zeros((l_pad - l_real, d), F32)], axis=0)
    tm0 = _largest_tile(l_pad, 768, 16)
    tn = _largest_tile(d, 512, LANES)

    gate_rank = _round_up(a_g1.shape[-1], LANES)
    xr, xk, xv, lw, a, g = _premix(
        h, a_norm[0:1], a_mix[0],
        a_w1[0].astype(BF16), a_a1[0].astype(BF16), _pad_cols(a_g1[0], gate_rank).astype(BF16),
        a_w2[0].astype(BF16), a_a2[0].astype(BF16), _pad_rows(a_g2[0], gate_rank).astype(BF16),
        jnp.stack([a_w0[0], a_a0[0]]), _largest_tile(l_pad, 256, 16))
    tm_rkv = _largest_tile(l_pad, 1408, 16)
    r = _mm(xr, a_w_rkv[0], out_dtype=F32, tm=tm_rkv, tn=tn, w_index=0)
    k = _mm(xk, a_w_rkv[0], out_dtype=F32, tm=tm_rkv, tn=tn, w_index=1)
    v = _mm(xv, a_w_rkv[0], out_dtype=F32, tm=tm_rkv, tn=tn, w_index=2)
    prm = jnp.concatenate([a_k_k[0][None], a_k_a[0][None], a_r_k[0].reshape(1, d), a_lnx_w[0][None],
                           a_lnx_b[0][None], jnp.zeros((3, d), F32)], axis=0)
    pairs = 16 if n_heads % 32 == 0 else n_heads // 2
    mixed = _wkv(r, k, v, lw, a, g, prm, pairs=pairs)
    h = _mm(mixed, a_w_out[0], out_dtype=F32, tm=tm0, tn=tn, res=h)

    (hn,) = _rmsnorm(h, f_norm[0:1], [BF16], _largest_tile(l_pad, 256, 16))
    d_ff = d_w_gate.shape[-1]
    act = _swiglu(hn, d_w_gate[0], d_w_up[0], tm=_largest_tile(l_pad, 1408, 16), tn=_largest_tile(d_ff, 256, LANES))
    h = _mm_acc(act, d_w_down[0].astype(BF16), h, tm=tm0, tn=_largest_tile(d, 1024, LANES),
                tk=_largest_tile(d_ff, 6144, LANES))

    w_kv_dup = jnp.broadcast_to(w_kv.reshape(d, 2 * ATT_KV_HEADS, 1, HEAD),
                                (d, 2 * ATT_KV_HEADS, 2, HEAD)).reshape(d, 2 * ATT_KV_HEADS * LANES)
    tm1 = _largest_tile(seq, 1024, 16)
    (hkv_meta,) = _rmsnorm(h[:CHUNK], kv_norm[None], [BF16], CHUNK)
    kv_meta = _mm(hkv_meta, w_kv_dup, out_dtype=BF16, tm=CHUNK, tn=tn)
    kv_meta = _pad_rows(kv_meta[:N_META], BLOCK)
    h, hkv, hq = _drop_meta(h, jnp.stack([kv_norm, b_norm[0]]), seq, _largest_tile(seq, 256, N_META))
    kv_real = _mm(hkv, w_kv_dup, out_dtype=BF16, tm=tm1, tn=tn)

    q = _mm(hq, b_w_q[0], out_dtype=BF16, tm=tm1, tn=tn, scale=HEAD ** -0.5)
    o = _attention(q, kv_real, kv_meta, b_sinks[0].reshape(ATT_KV_HEADS, -1), n_heads=n_heads)
    h = _mm(o, b_w_out[0], out_dtype=F32, tm=tm1, tn=tn, res=h)

    sel, gate = _router(h, f_norm[1:2], _pad_cols(e_router[0], LANES), n_experts=n_experts,
                        tm=_largest_tile(seq, 256, 16))
    pos, row_src, tile_expert, next_expert, tiles_used, n_rows = _moe_plan(sel, n_experts)
    xs_sorted = _gather_rows(row_src, h, f_norm[1:2], n_rows, rows=_largest_tile(n_rows, 512, EXPERT_TILE),
                             out_dtype=BF16)
    d_exp = e_w_gate.shape[-1]
    act = _moe_matmul(_moe_swiglu_body, tile_expert, next_expert, tiles_used, xs_sorted, (e_w_gate[0], e_w_up[0]),
                      tn=_largest_tile(d_exp, 896, LANES), out_dtype=BF16, name="moe_swiglu")
    eo = _moe_matmul(_moe_down_body, tile_expert, next_expert, tiles_used, act, (e_w_down[0],),
                     tn=_largest_tile(d, 1024, LANES), out_dtype=F32, name="moe_down")
    out = _moe_combine(pos, h, gate, final_norm[None], eo, tm=_largest_tile(seq, 128, 8))
    return out[None]
```

```python
import functools

import jax
import jax.numpy as jnp
from jax import lax
from jax.experimental import pallas as pl
from jax.experimental.pallas import tpu as pltpu

F32 = jnp.float32
BF16 = jnp.bfloat16

LANES = 128
VMEM_LIMIT_BYTES = 56 * 1024 * 1024

N_META = 16
RMS_EPS = 1e-5
LN_X_EPS = 64e-5
HEAD = 64
ATT_KV_HEADS = 8
WINDOW = 128
BLOCK = 128
TOP_K = 2
CHUNK = 64
NEG = -1e30


def _cparams(sem):
    return pltpu.CompilerParams(dimension_semantics=sem, vmem_limit_bytes=VMEM_LIMIT_BYTES)


def _dot(a, b):
    return jnp.dot(a, b, preferred_element_type=F32)


def _dot_nt(a, b):
    return lax.dot_general(a, b, (((1,), (1,)), ((), ())), preferred_element_type=F32)


def _rms(x):
    return x * lax.rsqrt(jnp.mean(x * x, axis=-1, keepdims=True) + RMS_EPS)


CAST_ROWS = 256


def _round_to_bf16(src_ref, dst_ref):
    rows = src_ref.shape[0]
    chunk = CAST_ROWS if rows % CAST_ROWS == 0 else rows

    def body(c, carry):
        r = pl.ds(pl.multiple_of(c * chunk, chunk), chunk)
        dst_ref[r, :] = src_ref[r, :].astype(BF16)
        return carry

    lax.fori_loop(0, rows // chunk, body, 0)


def _run_interleaved(steps, shared=None):
    results = [None] * len(steps)
    pending = list(range(len(steps)))
    inbox = {idx: None for idx in pending}
    while pending:
        requests = {}
        for idx in list(pending):
            try:
                req = steps[idx].send(inbox[idx])
            except StopIteration as done:
                results[idx] = done.value
                pending.remove(idx)
                continue
            inbox[idx] = None
            if req is not None:
                requests.setdefault(req[0], []).append((idx, req[1]))
        for key, members in requests.items():
            out = shared[key](jnp.concatenate([rows for _, rows in members], axis=0))
            start = 0
            for idx, rows in members:
                inbox[idx] = out[start:start + rows.shape[0]]
                start += rows.shape[0]
    return results


def _rmsnorm_body(h_ref, g_ref, *o_refs):
    y = _rms(h_ref[...])
    for j, o_ref in enumerate(o_refs):
        o_ref[...] = (y * g_ref[j:j + 1, :]).astype(o_ref.dtype)


def _rmsnorm(h, gains, out_dtypes, tm):
    m, d = h.shape
    n_out = len(out_dtypes)
    outs = pl.pallas_call(
        _rmsnorm_body,
        out_shape=[jax.ShapeDtypeStruct((m, d), dt) for dt in out_dtypes],
        grid=(m // tm,),
        in_specs=[pl.BlockSpec((tm, d), lambda i: (i, 0)),
                  pl.BlockSpec((n_out, d), lambda i: (0, 0))],
        out_specs=[pl.BlockSpec((tm, d), lambda i: (i, 0)) for _ in out_dtypes],
        compiler_params=_cparams(("parallel",)),
        name="rmsnorm",
    )(h, gains)
    return outs


def _drop_meta_body(ha_ref, hb_ref, g_ref, h_ref, *o_refs):
    rows = jnp.concatenate([ha_ref[N_META:, :], hb_ref[...]], axis=0)
    h_ref[...] = rows
    y = _rms(rows)
    for j, o_ref in enumerate(o_refs):
        o_ref[...] = (y * g_ref[j:j + 1, :]).astype(o_ref.dtype)


def _drop_meta(h, gains, seq, tm):
    d = h.shape[1]
    n_out = gains.shape[0]
    per = tm // N_META
    return pl.pallas_call(
        _drop_meta_body,
        out_shape=[jax.ShapeDtypeStruct((seq, d), F32)] + [jax.ShapeDtypeStruct((seq, d), BF16)] * n_out,
        grid=(seq // tm,),
        in_specs=[pl.BlockSpec((tm, d), lambda i: (i, 0)),
                  pl.BlockSpec((N_META, d), lambda i: ((i + 1) * per, 0)),
                  pl.BlockSpec((n_out, d), lambda i: (0, 0))],
        out_specs=[pl.BlockSpec((tm, d), lambda i: (i, 0))] * (1 + n_out),
        compiler_params=_cparams(("parallel",)),
        name="drop_meta",
    )(h, h, gains)


def _premix_body(h_ref, hp_ref, gain_ref, mix_ref, w1_ref, a1_ref, g1_ref, w2_ref, a2_ref, g2_ref, bias_ref,
                 xr_ref, xk_ref, xv_ref, lw_ref, a_ref, g_ref):
    i = pl.program_id(0)
    gain = gain_ref[...]
    xn = _rms(h_ref[...]) * gain
    pn = _rms(hp_ref[...]) * gain
    prev_row = jnp.where(i > 0, pn[7:8, :], 0.0)
    sh = pltpu.roll(xn, 1, axis=0)
    row = lax.broadcasted_iota(jnp.int32, xn.shape, 0)
    sh = jnp.where(row == 0, prev_row, sh)
    xx = sh - xn

    def mixed(j):
        return (xn + xx * mix_ref[j:j + 1, :]).astype(BF16)

    xr_ref[...] = mixed(0)
    xk_ref[...] = mixed(2)
    xv_ref[...] = mixed(3)
    t = jnp.tanh(_dot(mixed(1), w1_ref[...]))
    z = _dot(t.astype(BF16), w2_ref[...]) + bias_ref[0:1, :]
    lw_ref[...] = jax.nn.sigmoid(z) * (-0.6065306597126334)
    z = _dot(_dot(mixed(4), a1_ref[...]).astype(BF16), a2_ref[...]) + bias_ref[1:2, :]
    a_ref[...] = jax.nn.sigmoid(z).astype(a_ref.dtype)
    t = jax.nn.sigmoid(_dot(mixed(5), g1_ref[...]))
    g_ref[...] = _dot(t.astype(BF16), g2_ref[...]).astype(g_ref.dtype)


def _premix(h, gain, mix, w1, a1, g1, w2, a2, g2, bias, tm):
    m, d = h.shape
    rows8 = tm // 8
    row_blk = lambda i: (i, 0)
    whole = lambda arr: pl.BlockSpec(arr.shape, lambda i: (0, 0), pipeline_mode=pl.Buffered(1))
    return pl.pallas_call(
        _premix_body,
        out_shape=[jax.ShapeDtypeStruct((m, d), dt) for dt in (BF16, BF16, BF16, F32, BF16, BF16)],
        grid=(m // tm,),
        in_specs=[pl.BlockSpec((tm, d), row_blk),
                  pl.BlockSpec((8, d), lambda i: (jnp.maximum(i * rows8 - 1, 0), 0)),
                  whole(gain), whole(mix), whole(w1), whole(a1), whole(g1), whole(w2), whole(a2), whole(g2),
                  whole(bias)],
        out_specs=[pl.BlockSpec((tm, d), row_blk)] * 6,
        compiler_params=_cparams(("parallel",)),
        name="premix",
    )(h, h, gain, mix, w1, a1, g1, w2, a2, g2, bias)


def _mm_body(x_ref, w_ref, *rest, has_res, scale):
    if has_res:
        res_ref, o_ref, w16_ref = rest
    else:
        o_ref, w16_ref = rest

    @pl.when(pl.program_id(1) == 0)
    def _():
        _round_to_bf16(w_ref, w16_ref)

    acc = _dot(x_ref[...], w16_ref[...])
    if scale is not None:
        acc = acc * scale
    if has_res:
        acc = acc + res_ref[...]
    o_ref[...] = acc.astype(o_ref.dtype)


def _mm(x, w, *, out_dtype, tm, tn, res=None, scale=None, w_index=None):
    m, k = x.shape
    n = w.shape[-1]
    if w_index is None:
        w_spec = pl.BlockSpec((k, tn), lambda j, i: (0, j))
    else:
        w_spec = pl.BlockSpec((None, k, tn), lambda j, i: (w_index, 0, j))
    in_specs = [pl.BlockSpec((tm, k), lambda j, i: (i, 0)), w_spec]
    args = [x, w]
    if res is not None:
        in_specs.append(pl.BlockSpec((tm, tn), lambda j, i: (i, j)))
        args.append(res)
    return pl.pallas_call(
        functools.partial(_mm_body, has_res=res is not None, scale=scale),
        out_shape=jax.ShapeDtypeStruct((m, n), out_dtype),
        grid=(n // tn, m // tm),
        in_specs=in_specs,
        out_specs=pl.BlockSpec((tm, tn), lambda j, i: (i, j)),
        scratch_shapes=[pltpu.VMEM((k, tn), BF16)],
        compiler_params=_cparams(("parallel", "arbitrary")),
        name="matmul",
    )(*args)


def _mm_acc_body(x_ref, w_ref, res_ref, o_ref, acc_ref):
    kk = pl.program_id(2)

    @pl.when(kk == 0)
    def _():
        acc_ref[...] = jnp.zeros_like(acc_ref)

    acc_ref[...] += _dot(x_ref[...], w_ref[...])

    @pl.when(kk == pl.num_programs(2) - 1)
    def _():
        o_ref[...] = res_ref[...] + acc_ref[...]


def _mm_acc(x, w, res, *, tm, tn, tk):
    m, k = x.shape
    n = w.shape[1]
    return pl.pallas_call(
        _mm_acc_body,
        out_shape=jax.ShapeDtypeStruct((m, n), F32),
        grid=(n // tn, m // tm, k // tk),
        in_specs=[pl.BlockSpec((tm, tk), lambda j, i, q: (i, q)),
                  pl.BlockSpec((tk, tn), lambda j, i, q: (q, j)),
                  pl.BlockSpec((tm, tn), lambda j, i, q: (i, j))],
        out_specs=pl.BlockSpec((tm, tn), lambda j, i, q: (i, j)),
        scratch_shapes=[pltpu.VMEM((tm, tn), F32)],
        compiler_params=_cparams(("parallel", "parallel", "arbitrary")),
        name="matmul_acc",
    )(x, w, res)


def _swiglu_body(x_ref, wg_ref, wu_ref, o_ref, wg16_ref, wu16_ref):
    @pl.when(pl.program_id(1) == 0)
    def _():
        _round_to_bf16(wg_ref, wg16_ref)
        _round_to_bf16(wu_ref, wu16_ref)

    x = x_ref[...]
    g = _dot(x, wg16_ref[...])
    u = _dot(x, wu16_ref[...])
    o_ref[...] = (g * jax.nn.sigmoid(g) * u).astype(o_ref.dtype)


def _swiglu(x, wg, wu, *, tm, tn):
    m, k = x.shape
    n = wg.shape[1]
    return pl.pallas_call(
        _swiglu_body,
        out_shape=jax.ShapeDtypeStruct((m, n), BF16),
        grid=(n // tn, m // tm),
        in_specs=[pl.BlockSpec((tm, k), lambda j, i: (i, 0)),
                  pl.BlockSpec((k, tn), lambda j, i: (0, j)),
                  pl.BlockSpec((k, tn), lambda j, i: (0, j))],
        out_specs=pl.BlockSpec((tm, tn), lambda j, i: (i, j)),
        scratch_shapes=[pltpu.VMEM((k, tn), BF16), pltpu.VMEM((k, tn), BF16)],
        compiler_params=_cparams(("parallel", "arbitrary")),
        name="swiglu",
    )(x, wg, wu)


def _split2(x):
    hi = x.astype(BF16)
    lo = (x - hi.astype(F32)).astype(BF16)
    return hi, lo


def _wkv_body(r_ref, k_ref, v_ref, lw_ref, a_ref, g_ref, prm_ref, o_ref, st_ref, *, pairs):
    c_idx = pl.program_id(1)

    @pl.when(c_idx == 0)
    def _():
        st_ref[...] = jnp.zeros_like(st_ref)

    C = CHUNK
    lane = lax.broadcasted_iota(jnp.int32, (C, LANES), 1)
    row = lax.broadcasted_iota(jnp.int32, (C, LANES), 0)
    col = lane % HEAD
    upper_half = lane >= HEAD
    tri_strict = col < row
    tri_incl = col <= row
    eye_pair = (col == row).astype(F32)

    def level_mask(bs):
        return ((row // bs) == (col // bs)) & ((row // (bs // 2)) != (col // (bs // 2)))

    row2 = lax.broadcasted_iota(jnp.int32, (2 * C, LANES), 0)
    lane2 = lax.broadcasted_iota(jnp.int32, (2 * C, LANES), 1)
    bd_mask = (row2 >= C) == (lane2 >= HEAD)
    ones_bd = bd_mask.astype(BF16)
    diag_mask = row2 == lane2

    def bd(x):
        return jnp.concatenate([jnp.where(upper_half, 0.0, x), jnp.where(upper_half, x, 0.0)], axis=0)

    def headsum(x16):
        return _dot(x16, ones_bd)

    def pair_step(r, k, v, lw, a, g, prm, s):
        k_k, k_a, r_k, lnx_w, lnx_b = (prm[j:j + 1] for j in range(5))

        kk = k * k_k
        k2 = k * (1.0 + (a - 1.0) * k_a)
        sums = yield ("headsum", jnp.concatenate([kk * kk, r * k2 * r_k], axis=0).astype(BF16))
        cum = lw
        shift = 1
        while shift < C:
            cum = cum + jnp.where(row >= shift, pltpu.roll(cum, shift, axis=0), 0.0)
            shift *= 2
        kkn = kk * lax.rsqrt(jnp.maximum(sums[:C], 1e-24))
        bonus = sums[C:] * v
        avec = -kkn
        bvec = kkn * a
        cum_end = cum[C - 1:C, :]
        p_incl = jnp.exp(cum)
        p_inv = jnp.exp(-cum)
        a_t = avec * jnp.exp(cum - lw)
        r_t = r * p_incl
        b_t = bvec * p_inv
        k_t = k2 * p_inv
        p_end = jnp.exp(cum_end - cum)
        b_h = bvec * p_end
        k_h = k2 * p_end

        v16 = v.astype(BF16)
        gram = _dot_nt(jnp.concatenate([a_t, r_t], axis=0).astype(BF16),
                       jnp.concatenate([bd(b_t), bd(k_t)], axis=0).astype(BF16))
        yield
        a_ab = jnp.where(tri_strict, gram[:C, :LANES], 0.0)
        a_ak = jnp.where(tri_strict, gram[:C, LANES:], 0.0)
        a_rb = jnp.where(tri_incl, gram[C:, :LANES], 0.0)
        a_rk = jnp.where(tri_incl, gram[C:, LANES:], 0.0)
        akv = _dot(a_ak.astype(BF16), bd(v16))

        t_inv = eye_pair + jnp.where(level_mask(2), a_ab, 0.0)
        bs = 4
        while bs <= C:
            x = jnp.where(level_mask(bs), a_ab, 0.0)
            t16 = t_inv.astype(BF16)
            tx = _dot(t16, bd(x).astype(BF16))
            yield
            t_inv = t_inv + _dot(tx.astype(BF16), bd(t16))
            yield
            bs *= 2
        t16 = t_inv.astype(BF16)

        tt = _dot(t16, jnp.concatenate([bd(a_t.astype(BF16)), bd(akv.astype(BF16))], axis=1))
        yield
        a_hat = tt[:, :LANES].astype(BF16)
        u0 = tt[:, LANES:].astype(BF16)

        a_rb16 = a_rb.astype(BF16)
        qa = _dot(a_rb16, bd(a_hat))
        y0 = _dot(jnp.concatenate([a_rb16, a_rk.astype(BF16)], axis=1),
                  jnp.concatenate([bd(u0), bd(v16)], axis=0))
        lhs_t = jnp.concatenate([b_h, k_h], axis=0).T.astype(BF16)
        upd_m = _dot(lhs_t[:, :C], a_hat)
        upd_n = _dot(lhs_t, jnp.concatenate([u0, v16], axis=0))
        yield
        q_hat = r_t + qa
        m_off = jnp.where(bd_mask, upd_m, 0.0)
        n_new = jnp.where(bd_mask, upd_n, 0.0)
        p_col = jnp.sum(jnp.where(diag_mask, jnp.exp(cum_end), 0.0), axis=1, keepdims=True)

        s_hi, s_lo = _split2(s)
        m16 = m_off.astype(BF16)
        ys = _dot(jnp.concatenate([q_hat.astype(BF16), m16], axis=0), s_hi)
        y = ys[:C] + y0
        s_new = p_col * s + ys[C:] + _dot(m16, s_lo) + n_new

        mu = (yield ("headsum", y.astype(BF16))) * (1.0 / HEAD)
        d = y - mu
        var = (yield ("headsum", (d * d).astype(BF16))) * (1.0 / HEAD)
        yn = d * lax.rsqrt(var + LN_X_EPS) * lnx_w + lnx_b
        return ((yn + bonus) * g.astype(F32)).astype(o_ref.dtype), s_new

    lanes = [slice(p * LANES, (p + 1) * LANES) for p in range(pairs)]
    results = _run_interleaved([pair_step(r_ref[:, sl], k_ref[:, sl], v_ref[:, sl], lw_ref[:, sl],
                                          a_ref[:, sl].astype(F32), g_ref[:, sl], prm_ref[:, sl], st_ref[p])
                                for p, sl in enumerate(lanes)],
                               shared={"headsum": headsum})
    for p, sl in enumerate(lanes):
        o_ref[:, sl] = results[p][0]
        st_ref[p] = results[p][1]


def _wkv(r, k, v, lw, a, g, prm, *, pairs):
    length, d = r.shape
    width = pairs * LANES
    blk = lambda hb, c: (c, hb)
    return pl.pallas_call(
        functools.partial(_wkv_body, pairs=pairs),
        out_shape=jax.ShapeDtypeStruct((length, d), BF16),
        grid=(d // width, length // CHUNK),
        in_specs=[pl.BlockSpec((CHUNK, width), blk)] * 6 + [pl.BlockSpec((8, width), lambda hb, c: (0, hb))],
        out_specs=pl.BlockSpec((CHUNK, width), blk),
        scratch_shapes=[pltpu.VMEM((pairs, LANES, LANES), F32)],
        compiler_params=_cparams(("parallel", "arbitrary")),
        name="wkv7",
    )(r, k, v, lw, a, g, prm)


def _attn_body(q_ref, kp_ref, kc_ref, km_ref, vp_ref, vc_ref, vm_ref, sk_ref, o_ref, *, group, n_heads, q_blocks):
    kvh = pl.program_id(0)
    first_block = pl.program_id(1) * q_blocks
    rows = 2 * BLOCK
    rowi = lax.broadcasted_iota(jnp.int32, (rows, BLOCK), 0)
    kj = lax.broadcasted_iota(jnp.int32, (rows, BLOCK), 1)
    qi = rowi % BLOCK
    second = rowi >= BLOCK
    use_prev = kj > qi
    dist_band = (qi - kj + jnp.where(use_prev, BLOCK, 0)).astype(F32)
    valid_meta = kj < N_META
    lane_o = lax.broadcasted_iota(jnp.int32, (BLOCK, LANES), 1)
    lane_q = lax.broadcasted_iota(jnp.int32, (rows, LANES), 1)
    row_q = lax.broadcasted_iota(jnp.int32, (rows, LANES), 0)
    q_keep = (row_q >= BLOCK) == (lane_q >= HEAD)

    lane_k = lax.broadcasted_iota(jnp.int32, (BLOCK, LANES), 1)
    keep = (lane_k < HEAD) != (kvh % 2 == 1)

    def own_head_twice(t):
        swapped = jnp.concatenate([t[:, HEAD:], t[:, :HEAD]], axis=1)
        return jnp.where(keep, t, swapped)

    km = own_head_twice(km_ref[...])
    vm = own_head_twice(vm_ref[...])
    k_tiles = [own_head_twice(kp_ref[...])] + [own_head_twice(kc_ref[b * BLOCK:(b + 1) * BLOCK, :])
                                               for b in range(q_blocks)]
    v_tiles = [own_head_twice(vp_ref[...])] + [own_head_twice(vc_ref[b * BLOCK:(b + 1) * BLOCK, :])
                                               for b in range(q_blocks)]

    def pair_bias(pr):
        head1 = (kvh * group + 2 * pr + 1).astype(F32)
        slope = jnp.exp2((jnp.where(second, 1.0, 0.0) + head1) * (-8.0 / n_heads))
        sink = jnp.where(second[:, :1], sk_ref[kvh, 2 * pr + 1], sk_ref[kvh, 2 * pr])
        return slope, slope * dist_band, sink

    biases = [pair_bias(pr) for pr in range(group // 2)]

    def pair_step(pr, b, q):
        n = first_block + b
        q2 = jnp.where(q_keep, jnp.concatenate([q, q], axis=0), jnp.zeros((), q.dtype))
        qk_prev = _dot_nt(q2, k_tiles[b])
        qk_cur = _dot_nt(q2, k_tiles[b + 1])
        qk_meta = _dot_nt(q2, km)
        yield
        slope, bias_band, sink = biases[pr]
        dist_meta = (N_META + n * BLOCK + qi - kj).astype(F32)
        s_band = jnp.where(use_prev, qk_prev, qk_cur) - bias_band
        if b == 0:
            s_band = jnp.where(use_prev & (n == 0), NEG, s_band)
        s_meta = jnp.where(valid_meta, qk_meta - slope * dist_meta, NEG)
        mx = jnp.maximum(jnp.max(jnp.maximum(s_band, s_meta), axis=1, keepdims=True), sink)
        e_band = jnp.exp(s_band - mx)
        e_meta = jnp.exp(s_meta - mx)
        den = jnp.sum(e_band + e_meta, axis=1, keepdims=True) + jnp.exp(sink - mx)
        acc = (_dot(jnp.where(use_prev, e_band, 0.0).astype(BF16), v_tiles[b])
               + _dot(jnp.where(use_prev, 0.0, e_band).astype(BF16), v_tiles[b + 1])
               + _dot(e_meta.astype(BF16), vm))
        yield
        acc = acc / den
        return jnp.where(lane_o >= HEAD, acc[BLOCK:], acc[:BLOCK]).astype(o_ref.dtype)

    tiles = [(slice(b * BLOCK, (b + 1) * BLOCK), slice(pr * LANES, (pr + 1) * LANES), pr, b)
             for b in range(q_blocks) for pr in range(group // 2)]
    results = _run_interleaved([pair_step(pr, b, q_ref[rs, ls]) for rs, ls, pr, b in tiles])
    for (rs, ls, _, _), res in zip(tiles, results):
        o_ref[rs, ls] = res


def _attention(q, kv, kv_meta, sinks, *, n_heads):
    s_len, d = q.shape
    group = n_heads // ATT_KV_HEADS
    gw = group * HEAD
    nb = s_len // BLOCK
    q_blocks = 2 if nb % 2 == 0 else 1
    v_off = ATT_KV_HEADS // 2
    cur = lambda h, m: (m, h)
    k_prev = lambda h, m: (jnp.maximum(m * q_blocks - 1, 0), h // 2)
    k_cur = lambda h, m: (m, h // 2)
    k_meta = lambda h, m: (0, h // 2)
    v_prev = lambda h, m: (jnp.maximum(m * q_blocks - 1, 0), v_off + h // 2)
    v_cur = lambda h, m: (m, v_off + h // 2)
    v_meta = lambda h, m: (0, v_off + h // 2)
    one = lambda index_map: pl.BlockSpec((BLOCK, LANES), index_map)
    own = lambda index_map: pl.BlockSpec((q_blocks * BLOCK, LANES), index_map)
    return pl.pallas_call(
        functools.partial(_attn_body, group=group, n_heads=n_heads, q_blocks=q_blocks),
        out_shape=jax.ShapeDtypeStruct((s_len, d), BF16),
        grid=(ATT_KV_HEADS, nb // q_blocks),
        in_specs=[pl.BlockSpec((q_blocks * BLOCK, gw), cur),
                  one(k_prev), own(k_cur), one(k_meta), one(v_prev), own(v_cur), one(v_meta),
                  pl.BlockSpec(memory_space=pltpu.SMEM)],
        out_specs=pl.BlockSpec((q_blocks * BLOCK, gw), cur),
        compiler_params=_cparams(("parallel", "parallel")),
        name="swa_attention",
    )(q, kv, kv, kv_meta, kv, kv, kv_meta, sinks)


def _router_body(h_ref, g_ref, wr_ref, sel_ref, gate_ref, *, n_experts):
    y = _rms(h_ref[...]) * g_ref[...]
    y_hi, y_lo = _split2(y)
    w_hi, w_lo = _split2(wr_ref[...])
    logits = _dot(y_hi, w_hi) + _dot(y_lo, w_hi) + _dot(y_hi, w_lo)
    lane = lax.broadcasted_iota(jnp.int32, logits.shape, 1)
    logits = jnp.where(lane < n_experts, logits, NEG)
    m1 = jnp.max(logits, axis=1, keepdims=True)
    i1 = jnp.min(jnp.where(logits == m1, lane, LANES), axis=1, keepdims=True)
    rest = jnp.where(lane == i1, NEG, logits)
    m2 = jnp.max(rest, axis=1, keepdims=True)
    i2 = jnp.min(jnp.where(rest == m2, lane, LANES), axis=1, keepdims=True)
    e2 = jnp.exp(m2 - m1)
    g1 = 1.0 / (1.0 + e2)
    g2 = e2 / (1.0 + e2)
    sel_ref[...] = jnp.where(lane == 0, i1, jnp.where(lane == 1, i2, 0))
    gate_ref[...] = jnp.where(lane == 0, g1, jnp.where(lane == 1, g2, 0.0))


def _router(h, gain, w_router_pad, *, n_experts, tm):
    m, d = h.shape
    return pl.pallas_call(
        functools.partial(_router_body, n_experts=n_experts),
        out_shape=[jax.ShapeDtypeStruct((m, LANES), jnp.int32), jax.ShapeDtypeStruct((m, LANES), F32)],
        grid=(m // tm,),
        in_specs=[pl.BlockSpec((tm, d), lambda i: (i, 0)),
                  pl.BlockSpec((1, d), lambda i: (0, 0)),
                  pl.BlockSpec((d, LANES), lambda i: (0, 0))],
        out_specs=[pl.BlockSpec((tm, LANES), lambda i: (i, 0)), pl.BlockSpec((tm, LANES), lambda i: (i, 0))],
        compiler_params=_cparams(("parallel",)),
        name="moe_router",
    )(h, gain, w_router_pad)


EXPERT_TILE = 256


def _row_copy(src_hbm, dst, src_row, dst_row, sem):
    return pltpu.make_async_copy(src_hbm.at[pl.ds(src_row, 1)], dst.at[pl.ds(dst_row, 1)], sem)


ROW_DMA_UNROLL = 8


def _gather_rows_body(src_ref, x_hbm, gain_ref, o_ref, buf, sem, *, rows):
    i = pl.program_id(0)

    def issue(tile, slot):
        base = tile * rows

        def body(r, carry):
            _row_copy(x_hbm, buf.at[slot], src_ref[base + r], r, sem.at[slot]).start()
            return carry

        lax.fori_loop(0, rows, body, 0, unroll=ROW_DMA_UNROLL)

    @pl.when(i == 0)
    def _():
        issue(0, 0)

    @pl.when(i + 1 < pl.num_programs(0))
    def _():
        issue(i + 1, (i + 1) % 2)

    slot = i % 2
    pltpu.make_async_copy(x_hbm.at[pl.ds(0, rows)], buf.at[slot], sem.at[slot]).wait()
    o_ref[...] = (_rms(buf[slot]) * gain_ref[...]).astype(o_ref.dtype)


def _gather_rows(row_src, x, gain, n_rows, *, rows, out_dtype):
    d = x.shape[1]
    return pl.pallas_call(
        functools.partial(_gather_rows_body, rows=rows),
        out_shape=jax.ShapeDtypeStruct((n_rows, d), out_dtype),
        grid_spec=pltpu.PrefetchScalarGridSpec(
            num_scalar_prefetch=1,
            grid=(n_rows // rows,),
            in_specs=[pl.BlockSpec(memory_space=pl.ANY), pl.BlockSpec((1, d), lambda i, src: (0, 0))],
            out_specs=pl.BlockSpec((rows, d), lambda i, src: (i, 0)),
            scratch_shapes=[pltpu.VMEM((2, rows, d), x.dtype), pltpu.SemaphoreType.DMA((2,))],
        ),
        compiler_params=_cparams(("arbitrary",)),
        name="moe_gather",
    )(row_src, x, gain)


def _stage_expert_weights(te_ref, nxt_ref, w_hbms, wbuf, w16, sem, tn):
    j = pl.program_id(0)
    i = pl.program_id(1)

    def copies(expert, col_block):
        cols = pl.ds(pl.multiple_of(col_block * tn, tn), tn)
        return [pltpu.make_async_copy(w.at[expert, :, cols], wbuf.at[k], sem) for k, w in enumerate(w_hbms)]

    @pl.when((j == 0) & (i == 0))
    def _():
        for c in copies(te_ref[0], 0):
            c.start()

    @pl.when((i == 0) | (te_ref[i] != te_ref[jnp.maximum(i - 1, 0)]))
    def _():
        for c in copies(te_ref[i], j):
            c.wait()
        for k in range(len(w_hbms)):
            _round_to_bf16(wbuf.at[k], w16.at[k])
        next_expert = nxt_ref[i]
        wraps = next_expert < 0

        @pl.when(jnp.logical_not(wraps & (j == pl.num_programs(0) - 1)))
        def _():
            for c in copies(jnp.where(wraps, te_ref[0], next_expert), jnp.where(wraps, j + 1, j)):
                c.start()


def _moe_swiglu_body(te_ref, nxt_ref, used_ref, x_ref, wg_hbm, wu_hbm, o_ref, wbuf, w16, sem, *, tn):
    _stage_expert_weights(te_ref, nxt_ref, (wg_hbm, wu_hbm), wbuf, w16, sem, tn)
    has_tokens = pl.program_id(1) < used_ref[0]

    @pl.when(has_tokens)
    def _():
        x = x_ref[...]
        g = _dot(x, w16[0])
        u = _dot(x, w16[1])
        o_ref[...] = (g * jax.nn.sigmoid(g) * u).astype(o_ref.dtype)

    @pl.when(jnp.logical_not(has_tokens))
    def _():
        o_ref[...] = jnp.zeros_like(o_ref)


def _moe_down_body(te_ref, nxt_ref, used_ref, x_ref, w_hbm, o_ref, wbuf, w16, sem, *, tn):
    _stage_expert_weights(te_ref, nxt_ref, (w_hbm,), wbuf, w16, sem, tn)
    has_tokens = pl.program_id(1) < used_ref[0]

    @pl.when(has_tokens)
    def _():
        o_ref[...] = _dot(x_ref[...], w16[0])

    @pl.when(jnp.logical_not(has_tokens))
    def _():
        o_ref[...] = jnp.zeros_like(o_ref)


def _moe_matmul(body, tile_expert, next_expert, tiles_used, xs, weights, *, tn, out_dtype, name):
    rows, k = xs.shape
    n = weights[0].shape[-1]
    n_w = len(weights)
    return pl.pallas_call(
        functools.partial(body, tn=tn),
        out_shape=jax.ShapeDtypeStruct((rows, n), out_dtype),
        grid_spec=pltpu.PrefetchScalarGridSpec(
            num_scalar_prefetch=3,
            grid=(n // tn, rows // EXPERT_TILE),
            in_specs=[pl.BlockSpec((EXPERT_TILE, k), lambda j, i, te, nxt, used: (i, 0))]
            + [pl.BlockSpec(memory_space=pl.ANY)] * n_w,
            out_specs=pl.BlockSpec((EXPERT_TILE, tn), lambda j, i, te, nxt, used: (i, j)),
            scratch_shapes=[pltpu.VMEM((n_w, k, tn), F32), pltpu.VMEM((n_w, k, tn), BF16),
                            pltpu.SemaphoreType.DMA],
        ),
        compiler_params=_cparams(("arbitrary", "arbitrary")),
        name=name,
    )(tile_expert, next_expert, tiles_used, xs, *weights)


def _moe_combine_body(pos_ref, h_ref, gate_ref, fn_ref, eo_hbm, o_ref, buf, sem, *, tm):
    i = pl.program_id(0)

    def issue(tile, slot):
        base = tile * tm

        def body(t, carry):
            for j in range(TOP_K):
                _row_copy(eo_hbm, buf.at[slot, j], pos_ref[TOP_K * (base + t) + j], t, sem.at[slot]).start()
            return carry

        lax.fori_loop(0, tm, body, 0, unroll=ROW_DMA_UNROLL // TOP_K)

    @pl.when(i == 0)
    def _():
        issue(0, 0)

    @pl.when(i + 1 < pl.num_programs(0))
    def _():
        issue(i + 1, (i + 1) % 2)

    slot = i % 2
    for j in range(TOP_K):
        pltpu.make_async_copy(eo_hbm.at[pl.ds(0, tm)], buf.at[slot, j], sem.at[slot]).wait()
    y = h_ref[...]
    for j in range(TOP_K):
        y = y + gate_ref[:, j:j + 1] * buf[slot, j]
    o_ref[...] = _rms(y) * fn_ref[...]


def _moe_combine(pos, h, gate, final_gain, eo, *, tm):
    m, d = h.shape
    return pl.pallas_call(
        functools.partial(_moe_combine_body, tm=tm),
        out_shape=jax.ShapeDtypeStruct((m, d), F32),
        grid_spec=pltpu.PrefetchScalarGridSpec(
            num_scalar_prefetch=1,
            grid=(m // tm,),
            in_specs=[pl.BlockSpec((tm, d), lambda i, pos: (i, 0)),
                      pl.BlockSpec((tm, LANES), lambda i, pos: (i, 0)),
                      pl.BlockSpec((1, d), lambda i, pos: (0, 0)),
                      pl.BlockSpec(memory_space=pl.ANY)],
            out_specs=pl.BlockSpec((tm, d), lambda i, pos: (i, 0)),
            scratch_shapes=[pltpu.VMEM((2, TOP_K, tm, d), F32), pltpu.SemaphoreType.DMA((2,))],
        ),
        compiler_params=_cparams(("arbitrary",)),
        name="moe_combine",
    )(pos, h, gate, final_gain, eo)


def _moe_plan(sel, n_experts):
    m = sel.shape[0]
    flat_e = sel[:, :TOP_K].reshape(-1)
    onehot = (flat_e[:, None] == jnp.arange(n_experts, dtype=jnp.int32)[None, :]).astype(jnp.int32)
    csum = jnp.cumsum(onehot, axis=0)
    rank = jnp.sum(csum * onehot, axis=1) - 1
    counts = csum[-1]
    padded = (counts + EXPERT_TILE - 1) // EXPERT_TILE * EXPERT_TILE
    ends = jnp.cumsum(padded)
    starts = ends - padded
    pos = (jnp.sum(onehot * starts[None, :], axis=1) + rank).astype(jnp.int32)
    n_rows = TOP_K * m + n_experts * EXPERT_TILE
    token = jnp.arange(TOP_K * m, dtype=jnp.int32) // TOP_K
    row_src = jnp.zeros((n_rows,), jnp.int32).at[pos].set(token)
    tile_start = jnp.arange(n_rows // EXPERT_TILE, dtype=jnp.int32) * EXPERT_TILE
    tile_expert = jnp.minimum(jnp.sum((tile_start[:, None] >= ends[None, :]).astype(jnp.int32), axis=1),
                              n_experts - 1).astype(jnp.int32)
    run_end = jnp.sum((tile_expert[None, :] <= tile_expert[:, None]).astype(jnp.int32), axis=1)
    n_tiles = tile_expert.shape[0]
    next_expert = jnp.where(run_end < n_tiles, tile_expert[jnp.minimum(run_end, n_tiles - 1)], -1).astype(jnp.int32)
    tiles_used = (ends[-1:] // EXPERT_TILE).astype(jnp.int32)
    return pos, row_src, tile_expert, next_expert, tiles_used, n_rows


def _largest_tile(n, cap, mult):
    best = None
    t = mult
    while t <= min(n, cap):
        if n % t == 0:
            best = t
        t += mult
    assert best is not None, (n, cap, mult)
    return best


def _pad_cols(w, n_to):
    return jnp.pad(w, ((0, 0), (0, n_to - w.shape[1])))


def _pad_rows(w, n_to):
    return jnp.pad(w, ((0, n_to - w.shape[0]), (0, 0)))


def _round_up(n, m):
    return -(-n // m) * m


def kernel(x, meta_tokens, a_norm, a_mix, a_w_rkv, a_w0, a_w1, a_w2, a_a0, a_a1, a_a2, a_g1, a_g2, a_k_k, a_k_a, a_r_k, a_lnx_w, a_lnx_b, a_w_out, kv_norm, w_kv, b_norm, b_w_q, b_sinks, b_w_out, f_norm, d_w_gate, d_w_up, d_w_down, e_router, e_w_gate, e_w_up, e_w_down, final_norm):
    assert x.shape[0] == 1
    seq, d = x.shape[1], x.shape[2]
    n_heads = d // HEAD
    n_experts = e_router.shape[-1]
    xs = x[0]

    l_real = N_META + seq
    l_pad = _round_up(l_real, CHUNK)
    h = jnp.concatenate([meta_tokens.astype(F32), xs, jnp.zeros((l_pad - l_real, d), F32)], axis=0)
    tm0 = _largest_tile(l_pad, 768, 16)
    tn = _largest_tile(d, 512, LANES)

    gate_rank = _round_up(a_g1.shape[-1], LANES)
    xr, xk, xv, lw, a, g = _premix(
        h, a_norm[0:1], a_mix[0],
        a_w1[0].astype(BF16), a_a1[0].astype(BF16), _pad_cols(a_g1[0], gate_rank).astype(BF16),
        a_w2[0].astype(BF16), a_a2[0].astype(BF16), _pad_rows(a_g2[0], gate_rank).astype(BF16),
        jnp.stack([a_w0[0], a_a0[0]]), _largest_tile(l_pad, 256, 16))
    tm_rkv = _largest_tile(l_pad, 1408, 16)
    r = _mm(xr, a_w_rkv[0], out_dtype=F32, tm=tm_rkv, tn=tn, w_index=0)
    k = _mm(xk, a_w_rkv[0], out_dtype=F32, tm=tm_rkv, tn=tn, w_index=1)
    v = _mm(xv, a_w_rkv[0], out_dtype=F32, tm=tm_rkv, tn=tn, w_index=2)
    prm = jnp.concatenate([a_k_k[0][None], a_k_a[0][None], a_r_k[0].reshape(1, d), a_lnx_w[0][None],
                           a_lnx_b[0][None], jnp.zeros((3, d), F32)], axis=0)
    pairs = 16 if n_heads % 32 == 0 else n_heads // 2
    mixed = _wkv(r, k, v, lw, a, g, prm, pairs=pairs)
    h = _mm(mixed, a_w_out[0], out_dtype=F32, tm=tm0, tn=tn, res=h)

    (hn,) = _rmsnorm(h, f_norm[0:1], [BF16], _largest_tile(l_pad, 256, 16))
    d_ff = d_w_gate.shape[-1]
    act = _swiglu(hn, d_w_gate[0], d_w_up[0], tm=_largest_tile(l_pad, 1408, 16), tn=_largest_tile(d_ff, 256, LANES))
    h = _mm_acc(act, d_w_down[0].astype(BF16), h, tm=tm0, tn=_largest_tile(d, 1024, LANES),
                tk=_largest_tile(d_ff, 6144, LANES))

    tm1 = _largest_tile(seq, 1024, 16)
    (hkv_meta,) = _rmsnorm(h[:CHUNK], kv_norm[None], [BF16], CHUNK)
    kv_meta = _mm(hkv_meta, w_kv, out_dtype=BF16, tm=CHUNK, tn=tn)
    kv_meta = _pad_rows(kv_meta[:N_META], BLOCK)
    h, hkv, hq = _drop_meta(h, jnp.stack([kv_norm, b_norm[0]]), seq, _largest_tile(seq, 256, N_META))
    kv_real = _mm(hkv, w_kv, out_dtype=BF16, tm=tm1, tn=tn)

    q = _mm(hq, b_w_q[0], out_dtype=BF16, tm=tm1, tn=tn, scale=HEAD ** -0.5)
    o = _attention(q, kv_real, kv_meta, b_sinks[0].reshape(ATT_KV_HEADS, -1), n_heads=n_heads)
    h = _mm(o, b_w_out[0], out_dtype=F32, tm=tm1, tn=tn, res=h)

    sel, gate = _router(h, f_norm[1:2], _pad_cols(e_router[0], LANES), n_experts=n_experts,
                        tm=_largest_tile(seq, 256, 16))
    pos, row_src, tile_expert, next_expert, tiles_used, n_rows = _moe_plan(sel, n_experts)
    xs_sorted = _gather_rows(row_src, h, f_norm[1:2], n_rows, rows=_largest_tile(n_rows, 512, EXPERT_TILE),
                             out_dtype=BF16)
    d_exp = e_w_gate.shape[-1]
    act = _moe_matmul(_moe_swiglu_body, tile_expert, next_expert, tiles_used, xs_sorted, (e_w_gate[0], e_w_up[0]),
                      tn=_largest_tile(d_exp, 896, LANES), out_dtype=BF16, name="moe_swiglu")
    eo = _moe_matmul(_moe_down_body, tile_expert, next_expert, tiles_used, act, (e_w_down[0],),
                     tn=_largest_tile(d, 1024, LANES), out_dtype=F32, name="moe_down")
    out = _moe_combine(pos, h, gate, final_norm[None], eo, tm=_largest_tile(seq, 128, 8))
    return out[None]
```

```python
import functools

import jax
import jax.numpy as jnp
from jax import lax
from jax.experimental import pallas as pl
from jax.experimental.pallas import tpu as pltpu

F32 = jnp.float32
BF16 = jnp.bfloat16

LANES = 128
VMEM_LIMIT_BYTES = 56 * 1024 * 1024

N_META = 16
RMS_EPS = 1e-5
LN_X_EPS = 64e-5
HEAD = 64
ATT_KV_HEADS = 8
WINDOW = 128
BLOCK = 128
TOP_K = 2
CHUNK = 64
NEG = -1e30


def _cparams(sem):
    return pltpu.CompilerParams(dimension_semantics=sem, vmem_limit_bytes=VMEM_LIMIT_BYTES)


def _dot(a, b):
    return jnp.dot(a, b, preferred_element_type=F32)


def _dot_nt(a, b):
    return lax.dot_general(a, b, (((1,), (1,)), ((), ())), preferred_element_type=F32)


def _rms(x):
    return x * lax.rsqrt(jnp.mean(x * x, axis=-1, keepdims=True) + RMS_EPS)


CAST_ROWS = 256


def _round_to_bf16(src_ref, dst_ref):
    rows = src_ref.shape[0]
    chunk = CAST_ROWS if rows % CAST_ROWS == 0 else rows

    def body(c, carry):
        r = pl.ds(pl.multiple_of(c * chunk, chunk), chunk)
        dst_ref[r, :] = src_ref[r, :].astype(BF16)
        return carry

    lax.fori_loop(0, rows // chunk, body, 0)


def _run_interleaved(steps, shared=None):
    results = [None] * len(steps)
    pending = list(range(len(steps)))
    inbox = {idx: None for idx in pending}
    while pending:
        requests = {}
        for idx in list(pending):
            try:
                req = steps[idx].send(inbox[idx])
            except StopIteration as done:
                results[idx] = done.value
                pending.remove(idx)
                continue
            inbox[idx] = None
            if req is not None:
                requests.setdefault(req[0], []).append((idx, req[1]))
        for key, members in requests.items():
            out = shared[key](jnp.concatenate([rows for _, rows in members], axis=0))
            start = 0
            for idx, rows in members:
                inbox[idx] = out[start:start + rows.shape[0]]
                start += rows.shape[0]
    return results


def _rmsnorm_body(h_ref, g_ref, *o_refs):
    y = _rms(h_ref[...])
    for j, o_ref in enumerate(o_refs):
        o_ref[...] = (y * g_ref[j:j + 1, :]).astype(o_ref.dtype)


def _rmsnorm(h, gains, out_dtypes, tm):
    m, d = h.shape
    n_out = len(out_dtypes)
    outs = pl.pallas_call(
        _rmsnorm_body,
        out_shape=[jax.ShapeDtypeStruct((m, d), dt) for dt in out_dtypes],
        grid=(m // tm,),
        in_specs=[pl.BlockSpec((tm, d), lambda i: (i, 0)),
                  pl.BlockSpec((n_out, d), lambda i: (0, 0))],
        out_specs=[pl.BlockSpec((tm, d), lambda i: (i, 0)) for _ in out_dtypes],
        compiler_params=_cparams(("parallel",)),
        name="rmsnorm",
    )(h, gains)
    return outs


def _drop_meta_body(ha_ref, hb_ref, g_ref, h_ref, *o_refs):
    rows = jnp.concatenate([ha_ref[N_META:, :], hb_ref[...]], axis=0)
    h_ref[...] = rows
    y = _rms(rows)
    for j, o_ref in enumerate(o_refs):
        o_ref[...] = (y * g_ref[j:j + 1, :]).astype(o_ref.dtype)


def _drop_meta(h, gains, seq, tm):
    d = h.shape[1]
    n_out = gains.shape[0]
    per = tm // N_META
    return pl.pallas_call(
        _drop_meta_body,
        out_shape=[jax.ShapeDtypeStruct((seq, d), F32)] + [jax.ShapeDtypeStruct((seq, d), BF16)] * n_out,
        grid=(seq // tm,),
        in_specs=[pl.BlockSpec((tm, d), lambda i: (i, 0)),
                  pl.BlockSpec((N_META, d), lambda i: ((i + 1) * per, 0)),
                  pl.BlockSpec((n_out, d), lambda i: (0, 0))],
        out_specs=[pl.BlockSpec((tm, d), lambda i: (i, 0))] * (1 + n_out),
        compiler_params=_cparams(("parallel",)),
        name="drop_meta",
    )(h, h, gains)


def _premix_body(h_ref, hp_ref, gain_ref, mix_ref, w1_ref, a1_ref, g1_ref, w2_ref, a2_ref, g2_ref, bias_ref,
                 xr_ref, xk_ref, xv_ref, lw_ref, a_ref, g_ref):
    i = pl.program_id(0)
    gain = gain_ref[...]
    xn = _rms(h_ref[...]) * gain
    pn = _rms(hp_ref[...]) * gain
    prev_row = jnp.where(i > 0, pn[7:8, :], 0.0)
    sh = pltpu.roll(xn, 1, axis=0)
    row = lax.broadcasted_iota(jnp.int32, xn.shape, 0)
    sh = jnp.where(row == 0, prev_row, sh)
    xx = sh - xn

    def mixed(j):
        return (xn + xx * mix_ref[j:j + 1, :]).astype(BF16)

    xr_ref[...] = mixed(0)
    xk_ref[...] = mixed(2)
    xv_ref[...] = mixed(3)
    t_w = _dot(mixed(1), w1_ref[...])
    t_a = _dot(mixed(4), a1_ref[...])
    t_g = _dot(mixed(5), g1_ref[...])
    z_w = _dot(jnp.tanh(t_w).astype(BF16), w2_ref[...]) + bias_ref[0:1, :]
    z_a = _dot(t_a.astype(BF16), a2_ref[...]) + bias_ref[1:2, :]
    z_g = _dot(jax.nn.sigmoid(t_g).astype(BF16), g2_ref[...])
    lw_ref[...] = jax.nn.sigmoid(z_w) * (-0.6065306597126334)
    a_ref[...] = jax.nn.sigmoid(z_a).astype(a_ref.dtype)
    g_ref[...] = z_g.astype(g_ref.dtype)


def _premix(h, gain, mix, w1, a1, g1, w2, a2, g2, bias, tm):
    m, d = h.shape
    rows8 = tm // 8
    row_blk = lambda i: (i, 0)
    whole = lambda arr: pl.BlockSpec(arr.shape, lambda i: (0, 0), pipeline_mode=pl.Buffered(1))
    return pl.pallas_call(
        _premix_body,
        out_shape=[jax.ShapeDtypeStruct((m, d), dt) for dt in (BF16, BF16, BF16, F32, BF16, BF16)],
        grid=(m // tm,),
        in_specs=[pl.BlockSpec((tm, d), row_blk),
                  pl.BlockSpec((8, d), lambda i: (jnp.maximum(i * rows8 - 1, 0), 0)),
                  whole(gain), whole(mix), whole(w1), whole(a1), whole(g1), whole(w2), whole(a2), whole(g2),
                  whole(bias)],
        out_specs=[pl.BlockSpec((tm, d), row_blk)] * 6,
        compiler_params=_cparams(("parallel",)),
        name="premix",
    )(h, h, gain, mix, w1, a1, g1, w2, a2, g2, bias)


def _mm_body(x_ref, w_ref, *rest, has_res, scale):
    if has_res:
        res_ref, o_ref, w16_ref = rest
    else:
        o_ref, w16_ref = rest

    @pl.when(pl.program_id(1) == 0)
    def _():
        _round_to_bf16(w_ref, w16_ref)

    acc = _dot(x_ref[...], w16_ref[...])
    if scale is not None:
        acc = acc * scale
    if has_res:
        acc = acc + res_ref[...]
    o_ref[...] = acc.astype(o_ref.dtype)


def _mm(x, w, *, out_dtype, tm, tn, res=None, scale=None, w_index=None):
    m, k = x.shape
    n = w.shape[-1]
    if w_index is None:
        w_spec = pl.BlockSpec((k, tn), lambda j, i: (0, j))
    else:
        w_spec = pl.BlockSpec((None, k, tn), lambda j, i: (w_index, 0, j))
    in_specs = [pl.BlockSpec((tm, k), lambda j, i: (i, 0)), w_spec]
    args = [x, w]
    if res is not None:
        in_specs.append(pl.BlockSpec((tm, tn), lambda j, i: (i, j)))
        args.append(res)
    return pl.pallas_call(
        functools.partial(_mm_body, has_res=res is not None, scale=scale),
        out_shape=jax.ShapeDtypeStruct((m, n), out_dtype),
        grid=(n // tn, m // tm),
        in_specs=in_specs,
        out_specs=pl.BlockSpec((tm, tn), lambda j, i: (i, j)),
        scratch_shapes=[pltpu.VMEM((k, tn), BF16)],
        compiler_params=_cparams(("parallel", "arbitrary")),
        name="matmul",
    )(*args)


def _mm_acc_body(x_ref, w_ref, res_ref, o_ref, acc_ref):
    kk = pl.program_id(2)

    @pl.when(kk == 0)
    def _():
        acc_ref[...] = jnp.zeros_like(acc_ref)

    acc_ref[...] += _dot(x_ref[...], w_ref[...])

    @pl.when(kk == pl.num_programs(2) - 1)
    def _():
        o_ref[...] = res_ref[...] + acc_ref[...]


def _mm_acc(x, w, res, *, tm, tn, tk):
    m, k = x.shape
    n = w.shape[1]
    return pl.pallas_call(
        _mm_acc_body,
        out_shape=jax.ShapeDtypeStruct((m, n), F32),
        grid=(n // tn, m // tm, k // tk),
        in_specs=[pl.BlockSpec((tm, tk), lambda j, i, q: (i, q)),
                  pl.BlockSpec((tk, tn), lambda j, i, q: (q, j)),
                  pl.BlockSpec((tm, tn), lambda j, i, q: (i, j))],
        out_specs=pl.BlockSpec((tm, tn), lambda j, i, q: (i, j)),
        scratch_shapes=[pltpu.VMEM((tm, tn), F32)],
        compiler_params=_cparams(("parallel", "parallel", "arbitrary")),
        name="matmul_acc",
    )(x, w, res)


def _swiglu_body(x_ref, wg_ref, wu_ref, o_ref, wg16_ref, wu16_ref):
    @pl.when(pl.program_id(1) == 0)
    def _():
        _round_to_bf16(wg_ref, wg16_ref)
        _round_to_bf16(wu_ref, wu16_ref)

    x = x_ref[...]
    g = _dot(x, wg16_ref[...])
    u = _dot(x, wu16_ref[...])
    o_ref[...] = (g * jax.nn.sigmoid(g) * u).astype(o_ref.dtype)


def _swiglu(x, wg, wu, *, tm, tn):
    m, k = x.shape
    n = wg.shape[1]
    return pl.pallas_call(
        _swiglu_body,
        out_shape=jax.ShapeDtypeStruct((m, n), BF16),
        grid=(n // tn, m // tm),
        in_specs=[pl.BlockSpec((tm, k), lambda j, i: (i, 0)),
                  pl.BlockSpec((k, tn), lambda j, i: (0, j)),
                  pl.BlockSpec((k, tn), lambda j, i: (0, j))],
        out_specs=pl.BlockSpec((tm, tn), lambda j, i: (i, j)),
        scratch_shapes=[pltpu.VMEM((k, tn), BF16), pltpu.VMEM((k, tn), BF16)],
        compiler_params=_cparams(("parallel", "arbitrary")),
        name="swiglu",
    )(x, wg, wu)


def _split2(x):
    hi = x.astype(BF16)
    lo = (x - hi.astype(F32)).astype(BF16)
    return hi, lo


def _wkv_body(r_ref, k_ref, v_ref, lw_ref, a_ref, g_ref, prm_ref, o_ref, st_ref, *, pairs):
    c_idx = pl.program_id(1)

    @pl.when(c_idx == 0)
    def _():
        st_ref[...] = jnp.zeros_like(st_ref)

    C = CHUNK
    lane = lax.broadcasted_iota(jnp.int32, (C, LANES), 1)
    row = lax.broadcasted_iota(jnp.int32, (C, LANES), 0)
    col = lane % HEAD
    upper_half = lane >= HEAD
    tri_strict = col < row
    tri_incl = col <= row
    eye_pair = (col == row).astype(F32)

    def level_mask(bs):
        return ((row // bs) == (col // bs)) & ((row // (bs // 2)) != (col // (bs // 2)))

    row2 = lax.broadcasted_iota(jnp.int32, (2 * C, LANES), 0)
    lane2 = lax.broadcasted_iota(jnp.int32, (2 * C, LANES), 1)
    bd_mask = (row2 >= C) == (lane2 >= HEAD)
    ones_bd = bd_mask.astype(BF16)
    diag_mask = row2 == lane2

    def bd(x):
        return jnp.concatenate([jnp.where(upper_half, 0.0, x), jnp.where(upper_half, x, 0.0)], axis=0)

    def headsum(x16):
        return _dot(x16, ones_bd)

    def pair_step(r, k, v, lw, a, g, prm, s):
        k_k, k_a, r_k, lnx_w, lnx_b = (prm[j:j + 1] for j in range(5))

        kk = k * k_k
        k2 = k * (1.0 + (a - 1.0) * k_a)
        sums = yield ("headsum", jnp.concatenate([kk * kk, r * k2 * r_k], axis=0).astype(BF16))
        cum = lw
        shift = 1
        while shift < C:
            cum = cum + jnp.where(row >= shift, pltpu.roll(cum, shift, axis=0), 0.0)
            shift *= 2
        kkn = kk * lax.rsqrt(jnp.maximum(sums[:C], 1e-24))
        bonus = sums[C:] * v
        avec = -kkn
        bvec = kkn * a
        cum_end = cum[C - 1:C, :]
        p_incl = jnp.exp(cum)
        p_inv = jnp.exp(-cum)
        a_t = avec * jnp.exp(cum - lw)
        r_t = r * p_incl
        b_t = bvec * p_inv
        k_t = k2 * p_inv
        p_end = jnp.exp(cum_end - cum)
        b_h = bvec * p_end
        k_h = k2 * p_end

        v16 = v.astype(BF16)
        gram = _dot_nt(jnp.concatenate([a_t, r_t], axis=0).astype(BF16),
                       jnp.concatenate([bd(b_t), bd(k_t)], axis=0).astype(BF16))
        yield
        a_ab = jnp.where(tri_strict, gram[:C, :LANES], 0.0)
        a_ak = jnp.where(tri_strict, gram[:C, LANES:], 0.0)
        a_rb = jnp.where(tri_incl, gram[C:, :LANES], 0.0)
        a_rk = jnp.where(tri_incl, gram[C:, LANES:], 0.0)
        akv = _dot(a_ak.astype(BF16), bd(v16))

        t_inv = eye_pair + jnp.where(level_mask(2), a_ab, 0.0)
        bs = 4
        while bs <= C:
            x = jnp.where(level_mask(bs), a_ab, 0.0)
            t16 = t_inv.astype(BF16)
            tx = _dot(t16, bd(x).astype(BF16))
            yield
            t_inv = t_inv + _dot(tx.astype(BF16), bd(t16))
            yield
            bs *= 2
        t16 = t_inv.astype(BF16)

        tt = _dot(t16, jnp.concatenate([bd(a_t.astype(BF16)), bd(akv.astype(BF16))], axis=1))
        yield
        a_hat = tt[:, :LANES].astype(BF16)
        u0 = tt[:, LANES:].astype(BF16)

        a_rb16 = a_rb.astype(BF16)
        qa = _dot(a_rb16, bd(a_hat))
        y0 = _dot(jnp.concatenate([a_rb16, a_rk.astype(BF16)], axis=1),
                  jnp.concatenate([bd(u0), bd(v16)], axis=0))
        lhs_t = jnp.concatenate([b_h, k_h], axis=0).T.astype(BF16)
        upd_m = _dot(lhs_t[:, :C], a_hat)
        upd_n = _dot(lhs_t, jnp.concatenate([u0, v16], axis=0))
        yield
        q_hat = r_t + qa
        m_off = jnp.where(bd_mask, upd_m, 0.0)
        n_new = jnp.where(bd_mask, upd_n, 0.0)
        p_col = jnp.sum(jnp.where(diag_mask, jnp.exp(cum_end), 0.0), axis=1, keepdims=True)

        s_hi, s_lo = _split2(s)
        m16 = m_off.astype(BF16)
        ys = _dot(jnp.concatenate([q_hat.astype(BF16), m16], axis=0), s_hi)
        y = ys[:C] + y0
        s_new = p_col * s + ys[C:] + _dot(m16, s_lo) + n_new

        mu = (yield ("headsum", y.astype(BF16))) * (1.0 / HEAD)
        d = y - mu
        var = (yield ("headsum", (d * d).astype(BF16))) * (1.0 / HEAD)
        yn = d * lax.rsqrt(var + LN_X_EPS) * lnx_w + lnx_b
        return ((yn + bonus) * g.astype(F32)).astype(o_ref.dtype), s_new

    lanes = [slice(p * LANES, (p + 1) * LANES) for p in range(pairs)]
    results = _run_interleaved([pair_step(r_ref[:, sl], k_ref[:, sl], v_ref[:, sl], lw_ref[:, sl],
                                          a_ref[:, sl].astype(F32), g_ref[:, sl], prm_ref[:, sl], st_ref[p])
                                for p, sl in enumerate(lanes)],
                               shared={"headsum": headsum})
    for p, sl in enumerate(lanes):
        o_ref[:, sl] = results[p][0]
        st_ref[p] = results[p][1]


def _wkv(r, k, v, lw, a, g, prm, *, pairs):
    length, d = r.shape
    width = pairs * LANES
    blk = lambda hb, c: (c, hb)
    return pl.pallas_call(
        functools.partial(_wkv_body, pairs=pairs),
        out_shape=jax.ShapeDtypeStruct((length, d), BF16),
        grid=(d // width, length // CHUNK),
        in_specs=[pl.BlockSpec((CHUNK, width), blk)] * 6 + [pl.BlockSpec((8, width), lambda hb, c: (0, hb))],
        out_specs=pl.BlockSpec((CHUNK, width), blk),
        scratch_shapes=[pltpu.VMEM((pairs, LANES, LANES), F32)],
        compiler_params=_cparams(("parallel", "arbitrary")),
        name="wkv7",
    )(r, k, v, lw, a, g, prm)


def _attn_body(q_ref, kp_ref, kc_ref, km_ref, vp_ref, vc_ref, vm_ref, sk_ref, o_ref, *, group, n_heads, q_blocks):
    kvh = pl.program_id(0)
    first_block = pl.program_id(1) * q_blocks
    rows = 2 * BLOCK
    rowi = lax.broadcasted_iota(jnp.int32, (rows, BLOCK), 0)
    kj = lax.broadcasted_iota(jnp.int32, (rows, BLOCK), 1)
    qi = rowi % BLOCK
    second = rowi >= BLOCK
    use_prev = kj > qi
    dist_band = (qi - kj + jnp.where(use_prev, BLOCK, 0)).astype(F32)
    valid_meta = kj < N_META
    lane_o = lax.broadcasted_iota(jnp.int32, (BLOCK, LANES), 1)
    lane_q = lax.broadcasted_iota(jnp.int32, (rows, LANES), 1)
    row_q = lax.broadcasted_iota(jnp.int32, (rows, LANES), 0)
    q_keep = (row_q >= BLOCK) == (lane_q >= HEAD)

    lane_k = lax.broadcasted_iota(jnp.int32, (BLOCK, LANES), 1)
    keep = (lane_k < HEAD) != (kvh % 2 == 1)

    def own_head_twice(t):
        swapped = jnp.concatenate([t[:, HEAD:], t[:, :HEAD]], axis=1)
        return jnp.where(keep, t, swapped)

    km = own_head_twice(km_ref[...])
    vm = own_head_twice(vm_ref[...])
    k_tiles = [own_head_twice(kp_ref[...])] + [own_head_twice(kc_ref[b * BLOCK:(b + 1) * BLOCK, :])
                                               for b in range(q_blocks)]
    v_tiles = [own_head_twice(vp_ref[...])] + [own_head_twice(vc_ref[b * BLOCK:(b + 1) * BLOCK, :])
                                               for b in range(q_blocks)]

    def pair_bias(pr):
        head1 = (kvh * group + 2 * pr + 1).astype(F32)
        slope = jnp.exp2((jnp.where(second, 1.0, 0.0) + head1) * (-8.0 / n_heads))
        sink = jnp.where(second[:, :1], sk_ref[kvh, 2 * pr + 1], sk_ref[kvh, 2 * pr])
        return slope, slope * dist_band, sink

    biases = [pair_bias(pr) for pr in range(group // 2)]

    def pair_step(pr, b, q):
        n = first_block + b
        q2 = jnp.where(q_keep, jnp.concatenate([q, q], axis=0), jnp.zeros((), q.dtype))
        qk_prev = _dot_nt(q2, k_tiles[b])
        qk_cur = _dot_nt(q2, k_tiles[b + 1])
        qk_meta = _dot_nt(q2, km)
        yield
        slope, bias_band, sink = biases[pr]
        dist_meta = (N_META + n * BLOCK + qi - kj).astype(F32)
        s_band = jnp.where(use_prev, qk_prev, qk_cur) - bias_band
        if b == 0:
            s_band = jnp.where(use_prev & (n == 0), NEG, s_band)
        s_meta = jnp.where(valid_meta, qk_meta - slope * dist_meta, NEG)
        mx = jnp.maximum(jnp.max(jnp.maximum(s_band, s_meta), axis=1, keepdims=True), sink)
        e_band = jnp.exp(s_band - mx)
        e_meta = jnp.exp(s_meta - mx)
        den = jnp.sum(e_band + e_meta, axis=1, keepdims=True) + jnp.exp(sink - mx)
        acc = (_dot(jnp.where(use_prev, e_band, 0.0).astype(BF16), v_tiles[b])
               + _dot(jnp.where(use_prev, 0.0, e_band).astype(BF16), v_tiles[b + 1])
               + _dot(e_meta.astype(BF16), vm))
        yield
        acc = acc / den
        return jnp.where(lane_o >= HEAD, acc[BLOCK:], acc[:BLOCK]).astype(o_ref.dtype)

    tiles = [(slice(b * BLOCK, (b + 1) * BLOCK), slice(pr * LANES, (pr + 1) * LANES), pr, b)
             for b in range(q_blocks) for pr in range(group // 2)]
    results = _run_interleaved([pair_step(pr, b, q_ref[rs, ls]) for rs, ls, pr, b in tiles])
    for (rs, ls, _, _), res in zip(tiles, results):
        o_ref[rs, ls] = res


def _attention(q, kv, kv_meta, sinks, *, n_heads):
    s_len, d = q.shape
    group = n_heads // ATT_KV_HEADS
    gw = group * HEAD
    nb = s_len // BLOCK
    q_blocks = 2 if nb % 2 == 0 else 1
    v_off = ATT_KV_HEADS // 2
    cur = lambda h, m: (m, h)
    k_prev = lambda h, m: (jnp.maximum(m * q_blocks - 1, 0), h // 2)
    k_cur = lambda h, m: (m, h // 2)
    k_meta = lambda h, m: (0, h // 2)
    v_prev = lambda h, m: (jnp.maximum(m * q_blocks - 1, 0), v_off + h // 2)
    v_cur = lambda h, m: (m, v_off + h // 2)
    v_meta = lambda h, m: (0, v_off + h // 2)
    one = lambda index_map: pl.BlockSpec((BLOCK, LANES), index_map)
    own = lambda index_map: pl.BlockSpec((q_blocks * BLOCK, LANES), index_map)
    return pl.pallas_call(
        functools.partial(_attn_body, group=group, n_heads=n_heads, q_blocks=q_blocks),
        out_shape=jax.ShapeDtypeStruct((s_len, d), BF16),
        grid=(ATT_KV_HEADS, nb // q_blocks),
        in_specs=[pl.BlockSpec((q_blocks * BLOCK, gw), cur),
                  one(k_prev), own(k_cur), one(k_meta), one(v_prev), own(v_cur), one(v_meta),
                  pl.BlockSpec(memory_space=pltpu.SMEM)],
        out_specs=pl.BlockSpec((q_blocks * BLOCK, gw), cur),
        compiler_params=_cparams(("parallel", "parallel")),
        name="swa_attention",
    )(q, kv, kv, kv_meta, kv, kv, kv_meta, sinks)


def _router_body(h_ref, g_ref, wr_ref, sel_ref, gate_ref, *, n_experts):
    y = _rms(h_ref[...]) * g_ref[...]
    y_hi, y_lo = _split2(y)
    w_hi, w_lo = _split2(wr_ref[...])
    logits = _dot(y_hi, w_hi) + _dot(y_lo, w_hi) + _dot(y_hi, w_lo)
    lane = lax.broadcasted_iota(jnp.int32, logits.shape, 1)
    logits = jnp.where(lane < n_experts, logits, NEG)
    m1 = jnp.max(logits, axis=1, keepdims=True)
    i1 = jnp.min(jnp.where(logits == m1, lane, LANES), axis=1, keepdims=True)
    rest = jnp.where(lane == i1, NEG, logits)
    m2 = jnp.max(rest, axis=1, keepdims=True)
    i2 = jnp.min(jnp.where(rest == m2, lane, LANES), axis=1, keepdims=True)
    e2 = jnp.exp(m2 - m1)
    g1 = 1.0 / (1.0 + e2)
    g2 = e2 / (1.0 + e2)
    sel_ref[...] = jnp.where(lane == 0, i1, jnp.where(lane == 1, i2, 0))
    gate_ref[...] = jnp.where(lane == 0, g1, jnp.where(lane == 1, g2, 0.0))


def _router(h, gain, w_router_pad, *, n_experts, tm):
    m, d = h.shape
    return pl.pallas_call(
        functools.partial(_router_body, n_experts=n_experts),
        out_shape=[jax.ShapeDtypeStruct((m, LANES), jnp.int32), jax.ShapeDtypeStruct((m, LANES), F32)],
        grid=(m // tm,),
        in_specs=[pl.BlockSpec((tm, d), lambda i: (i, 0)),
                  pl.BlockSpec((1, d), lambda i: (0, 0)),
                  pl.BlockSpec((d, LANES), lambda i: (0, 0))],
        out_specs=[pl.BlockSpec((tm, LANES), lambda i: (i, 0)), pl.BlockSpec((tm, LANES), lambda i: (i, 0))],
        compiler_params=_cparams(("parallel",)),
        name="moe_router",
    )(h, gain, w_router_pad)


EXPERT_TILE = 256


def _row_copy(src_hbm, dst, src_row, dst_row, sem):
    return pltpu.make_async_copy(src_hbm.at[pl.ds(src_row, 1)], dst.at[pl.ds(dst_row, 1)], sem)


ROW_DMA_UNROLL = 8


def _gather_rows_body(src_ref, x_hbm, gain_ref, o_ref, buf, sem, *, rows):
    i = pl.program_id(0)

    def issue(tile, slot):
        base = tile * rows

        def body(r, carry):
            _row_copy(x_hbm, buf.at[slot], src_ref[base + r], r, sem.at[slot]).start()
            return carry

        lax.fori_loop(0, rows, body, 0, unroll=ROW_DMA_UNROLL)

    @pl.when(i == 0)
    def _():
        issue(0, 0)

    @pl.when(i + 1 < pl.num_programs(0))
    def _():
        issue(i + 1, (i + 1) % 2)

    slot = i % 2
    pltpu.make_async_copy(x_hbm.at[pl.ds(0, rows)], buf.at[slot], sem.at[slot]).wait()
    o_ref[...] = (_rms(buf[slot]) * gain_ref[...]).astype(o_ref.dtype)


def _gather_rows(row_src, x, gain, n_rows, *, rows, out_dtype):
    d = x.shape[1]
    return pl.pallas_call(
        functools.partial(_gather_rows_body, rows=rows),
        out_shape=jax.ShapeDtypeStruct((n_rows, d), out_dtype),
        grid_spec=pltpu.PrefetchScalarGridSpec(
            num_scalar_prefetch=1,
            grid=(n_rows // rows,),
            in_specs=[pl.BlockSpec(memory_space=pl.ANY), pl.BlockSpec((1, d), lambda i, src: (0, 0))],
            out_specs=pl.BlockSpec((rows, d), lambda i, src: (i, 0)),
            scratch_shapes=[pltpu.VMEM((2, rows, d), x.dtype), pltpu.SemaphoreType.DMA((2,))],
        ),
        compiler_params=_cparams(("arbitrary",)),
        name="moe_gather",
    )(row_src, x, gain)


def _stage_expert_weights(te_ref, nxt_ref, w_hbms, wbuf, w16, sem, tn):
    j = pl.program_id(0)
    i = pl.program_id(1)

    def copies(expert, col_block):
        cols = pl.ds(pl.multiple_of(col_block * tn, tn), tn)
        return [pltpu.make_async_copy(w.at[expert, :, cols], wbuf.at[k], sem) for k, w in enumerate(w_hbms)]

    @pl.when((j == 0) & (i == 0))
    def _():
        for c in copies(te_ref[0], 0):
            c.start()

    @pl.when((i == 0) | (te_ref[i] != te_ref[jnp.maximum(i - 1, 0)]))
    def _():
        for c in copies(te_ref[i], j):
            c.wait()
        for k in range(len(w_hbms)):
            _round_to_bf16(wbuf.at[k], w16.at[k])
        next_expert = nxt_ref[i]
        wraps = next_expert < 0

        @pl.when(jnp.logical_not(wraps & (j == pl.num_programs(0) - 1)))
        def _():
            for c in copies(jnp.where(wraps, te_ref[0], next_expert), jnp.where(wraps, j + 1, j)):
                c.start()


def _moe_swiglu_body(te_ref, nxt_ref, used_ref, x_ref, wg_hbm, wu_hbm, o_ref, wbuf, w16, sem, *, tn):
    _stage_expert_weights(te_ref, nxt_ref, (wg_hbm, wu_hbm), wbuf, w16, sem, tn)
    has_tokens = pl.program_id(1) < used_ref[0]

    @pl.when(has_tokens)
    def _():
        x = x_ref[...]
        g = _dot(x, w16[0])
        u = _dot(x, w16[1])
        o_ref[...] = (g * jax.nn.sigmoid(g) * u).astype(o_ref.dtype)

    @pl.when(jnp.logical_not(has_tokens))
    def _():
        o_ref[...] = jnp.zeros_like(o_ref)


def _moe_down_body(te_ref, nxt_ref, used_ref, x_ref, w_hbm, o_ref, wbuf, w16, sem, *, tn):
    _stage_expert_weights(te_ref, nxt_ref, (w_hbm,), wbuf, w16, sem, tn)
    has_tokens = pl.program_id(1) < used_ref[0]

    @pl.when(has_tokens)
    def _():
        o_ref[...] = _dot(x_ref[...], w16[0])

    @pl.when(jnp.logical_not(has_tokens))
    def _():
        o_ref[...] = jnp.zeros_like(o_ref)


def _moe_matmul(body, tile_expert, next_expert, tiles_used, xs, weights, *, tn, out_dtype, name):
    rows, k = xs.shape
    n = weights[0].shape[-1]
    n_w = len(weights)
    return pl.pallas_call(
        functools.partial(body, tn=tn),
        out_shape=jax.ShapeDtypeStruct((rows, n), out_dtype),
        grid_spec=pltpu.PrefetchScalarGridSpec(
            num_scalar_prefetch=3,
            grid=(n // tn, rows // EXPERT_TILE),
            in_specs=[pl.BlockSpec((EXPERT_TILE, k), lambda j, i, te, nxt, used: (i, 0))]
            + [pl.BlockSpec(memory_space=pl.ANY)] * n_w,
            out_specs=pl.BlockSpec((EXPERT_TILE, tn), lambda j, i, te, nxt, used: (i, j)),
            scratch_shapes=[pltpu.VMEM((n_w, k, tn), F32), pltpu.VMEM((n_w, k, tn), BF16),
                            pltpu.SemaphoreType.DMA],
        ),
        compiler_params=_cparams(("arbitrary", "arbitrary")),
        name=name,
    )(tile_expert, next_expert, tiles_used, xs, *weights)


def _moe_combine_body(pos_ref, h_ref, gate_ref, fn_ref, eo_hbm, o_ref, buf, sem, *, tm):
    i = pl.program_id(0)

    def issue(tile, slot):
        base = tile * tm

        def body(t, carry):
            for j in range(TOP_K):
                _row_copy(eo_hbm, buf.at[slot, j], pos_ref[TOP_K * (base + t) + j], t, sem.at[slot]).start()
            return carry

        lax.fori_loop(0, tm, body, 0, unroll=ROW_DMA_UNROLL // TOP_K)

    @pl.when(i == 0)
    def _():
        issue(0, 0)

    @pl.when(i + 1 < pl.num_programs(0))
    def _():
        issue(i + 1, (i + 1) % 2)

    slot = i % 2
    for j in range(TOP_K):
        pltpu.make_async_copy(eo_hbm.at[pl.ds(0, tm)], buf.at[slot, j], sem.at[slot]).wait()
    y = h_ref[...]
    for j in range(TOP_K):
        y = y + gate_ref[:, j:j + 1] * buf[slot, j]
    o_ref[...] = _rms(y) * fn_ref[...]


def _moe_combine(pos, h, gate, final_gain, eo, *, tm):
    m, d = h.shape
    return pl.pallas_call(
        functools.partial(_moe_combine_body, tm=tm),
        out_shape=jax.ShapeDtypeStruct((m, d), F32),
        grid_spec=pltpu.PrefetchScalarGridSpec(
            num_scalar_prefetch=1,
            grid=(m // tm,),
            in_specs=[pl.BlockSpec((tm, d), lambda i, pos: (i, 0)),
                      pl.BlockSpec((tm, LANES), lambda i, pos: (i, 0)),
                      pl.BlockSpec((1, d), lambda i, pos: (0, 0)),
                      pl.BlockSpec(memory_space=pl.ANY)],
            out_specs=pl.BlockSpec((tm, d), lambda i, pos: (i, 0)),
            scratch_shapes=[pltpu.VMEM((2, TOP_K, tm, d), F32), pltpu.SemaphoreType.DMA((2,))],
        ),
        compiler_params=_cparams(("arbitrary",)),
        name="moe_combine",
    )(pos, h, gate, final_gain, eo)


def _moe_plan(sel, n_experts):
    m = sel.shape[0]
    flat_e = sel[:, :TOP_K].reshape(-1)
    onehot = (flat_e[:, None] == jnp.arange(n_experts, dtype=jnp.int32)[None, :]).astype(jnp.int32)
    csum = jnp.cumsum(onehot, axis=0)
    rank = jnp.sum(csum * onehot, axis=1) - 1
    counts = csum[-1]
    padded = (counts + EXPERT_TILE - 1) // EXPERT_TILE * EXPERT_TILE
    ends = jnp.cumsum(padded)
    starts = ends - padded
    pos = (jnp.sum(onehot * starts[None, :], axis=1) + rank).astype(jnp.int32)
    n_rows = TOP_K * m + n_experts * EXPERT_TILE
    token = jnp.arange(TOP_K * m, dtype=jnp.int32) // TOP_K
    row_src = jnp.zeros((n_rows,), jnp.int32).at[pos].set(token)
    tile_start = jnp.arange(n_rows // EXPERT_TILE, dtype=jnp.int32) * EXPERT_TILE
    tile_expert = jnp.minimum(jnp.sum((tile_start[:, None] >= ends[None, :]).astype(jnp.int32), axis=1),
                              n_experts - 1).astype(jnp.int32)
    run_end = jnp.sum((tile_expert[None, :] <= tile_expert[:, None]).astype(jnp.int32), axis=1)
    n_tiles = tile_expert.shape[0]
    next_expert = jnp.where(run_end < n_tiles, tile_expert[jnp.minimum(run_end, n_tiles - 1)], -1).astype(jnp.int32)
    tiles_used = (ends[-1:] // EXPERT_TILE).astype(jnp.int32)
    return pos, row_src, tile_expert, next_expert, tiles_used, n_rows


def _largest_tile(n, cap, mult):
    best = None
    t = mult
    while t <= min(n, cap):
        if n % t == 0:
            best = t
        t += mult
    assert best is not None, (n, cap, mult)
    return best


def _pad_cols(w, n_to):
    return jnp.pad(w, ((0, 0), (0, n_to - w.shape[1])))


def _pad_rows(w, n_to):
    return jnp.pad(w, ((0, n_to - w.shape[0]), (0, 0)))


def _round_up(n, m):
    return -(-n // m) * m


def kernel(x, meta_tokens, a_norm, a_mix, a_w_rkv, a_w0, a_w1, a_w2, a_a0, a_a1, a_a2, a_g1, a_g2, a_k_k, a_k_a, a_r_k, a_lnx_w, a_lnx_b, a_w_out, kv_norm, w_kv, b_norm, b_w_q, b_sinks, b_w_out, f_norm, d_w_gate, d_w_up, d_w_down, e_router, e_w_gate, e_w_up, e_w_down, final_norm):
    assert x.shape[0] == 1
    seq, d = x.shape[1], x.shape[2]
    n_heads = d // HEAD
    n_experts = e_router.shape[-1]
    xs = x[0]

    l_real = N_META + seq
    l_pad = _round_up(l_real, CHUNK)
    h = jnp.concatenate([meta_tokens.astype(F32), xs, jnp.zeros((l_pad - l_real, d), F32)], axis=0)
    tm0 = _largest_tile(l_pad, 768, 16)
    tn = _largest_tile(d, 512, LANES)

    gate_rank = _round_up(a_g1.shape[-1], LANES)
    xr, xk, xv, lw, a, g = _premix(
        h, a_norm[0:1], a_mix[0],
        a_w1[0].astype(BF16), a_a1[0].astype(BF16), _pad_cols(a_g1[0], gate_rank).astype(BF16),
        a_w2[0].astype(BF16), a_a2[0].astype(BF16), _pad_rows(a_g2[0], gate_rank).astype(BF16),
        jnp.stack([a_w0[0], a_a0[0]]), _largest_tile(l_pad, 256, 16))
    tm_rkv = _largest_tile(l_pad, 1408, 16)
    r = _mm(xr, a_w_rkv[0], out_dtype=F32, tm=tm_rkv, tn=tn, w_index=0)
    k = _mm(xk, a_w_rkv[0], out_dtype=F32, tm=tm_rkv, tn=tn, w_index=1)
    v = _mm(xv, a_w_rkv[0], out_dtype=F32, tm=tm_rkv, tn=tn, w_index=2)
    prm = jnp.concatenate([a_k_k[0][None], a_k_a[0][None], a_r_k[0].reshape(1, d), a_lnx_w[0][None],
                           a_lnx_b[0][None], jnp.zeros((3, d), F32)], axis=0)
    pairs = 32 if n_heads % 64 == 0 else n_heads // 2
    mixed = _wkv(r, k, v, lw, a, g, prm, pairs=pairs)
    h = _mm(mixed, a_w_out[0], out_dtype=F32, tm=tm0, tn=tn, res=h)

    (hn,) = _rmsnorm(h, f_norm[0:1], [BF16], _largest_tile(l_pad, 256, 16))
    d_ff = d_w_gate.shape[-1]
    act = _swiglu(hn, d_w_gate[0], d_w_up[0], tm=_largest_tile(l_pad, 1408, 16), tn=_largest_tile(d_ff, 256, LANES))
    h = _mm_acc(act, d_w_down[0].astype(BF16), h, tm=tm0, tn=_largest_tile(d, 1024, LANES),
                tk=_largest_tile(d_ff, 6144, LANES))

    tm1 = _largest_tile(seq, 1024, 16)
    (hkv_meta,) = _rmsnorm(h[:CHUNK], kv_norm[None], [BF16], CHUNK)
    kv_meta = _mm(hkv_meta, w_kv, out_dtype=BF16, tm=CHUNK, tn=tn)
    kv_meta = _pad_rows(kv_meta[:N_META], BLOCK)
    h, hkv, hq = _drop_meta(h, jnp.stack([kv_norm, b_norm[0]]), seq, _largest_tile(seq, 256, N_META))
    kv_real = _mm(hkv, w_kv, out_dtype=BF16, tm=tm1, tn=tn)

    q = _mm(hq, b_w_q[0], out_dtype=BF16, tm=tm1, tn=tn, scale=HEAD ** -0.5)
    o = _attention(q, kv_real, kv_meta, b_sinks[0].reshape(ATT_KV_HEADS, -1), n_heads=n_heads)
    h = _mm(o, b_w_out[0], out_dtype=F32, tm=tm1, tn=tn, res=h)

    sel, gate = _router(h, f_norm[1:2], _pad_cols(e_router[0], LANES), n_experts=n_experts,
                        tm=_largest_tile(seq, 256, 16))
    pos, row_src, tile_expert, next_expert, tiles_used, n_rows = _moe_plan(sel, n_experts)
    xs_sorted = _gather_rows(row_src, h, f_norm[1:2], n_rows, rows=_largest_tile(n_rows, 512, EXPERT_TILE),
                             out_dtype=BF16)
    d_exp = e_w_gate.shape[-1]
    act = _moe_matmul(_moe_swiglu_body, tile_expert, next_expert, tiles_used, xs_sorted, (e_w_gate[0], e_w_up[0]),
                      tn=_largest_tile(d_exp, 896, LANES), out_dtype=BF16, name="moe_swiglu")
    eo = _moe_matmul(_moe_down_body, tile_expert, next_expert, tiles_used, act, (e_w_down[0],),
                     tn=_largest_tile(d, 1024, LANES), out_dtype=F32, name="moe_down")
    out = _moe_combine(pos, h, gate, final_norm[None], eo, tm=_largest_tile(seq, 128, 8))
    return out[None]
```

```python
import functools

import jax
import jax.numpy as jnp
from jax import lax
from jax.experimental import pallas as pl
from jax.experimental.pallas import tpu as pltpu

F32 = jnp.float32
BF16 = jnp.bfloat16

LANES = 128
VMEM_LIMIT_BYTES = 56 * 1024 * 1024

N_META = 16
RMS_EPS = 1e-5
LN_X_EPS = 64e-5
HEAD = 64
ATT_KV_HEADS = 8
WINDOW = 128
BLOCK = 128
TOP_K = 2
CHUNK = 64
NEG = -1e30


def _cparams(sem):
    return pltpu.CompilerParams(dimension_semantics=sem, vmem_limit_bytes=VMEM_LIMIT_BYTES)


def _dot(a, b):
    return jnp.dot(a, b, preferred_element_type=F32)


def _dot_nt(a, b):
    return lax.dot_general(a, b, (((1,), (1,)), ((), ())), preferred_element_type=F32)


def _rms(x):
    return x * lax.rsqrt(jnp.mean(x * x, axis=-1, keepdims=True) + RMS_EPS)


CAST_ROWS = 256


def _round_to_bf16(src_ref, dst_ref):
    rows = src_ref.shape[0]
    chunk = CAST_ROWS if rows % CAST_ROWS == 0 else rows

    def body(c, carry):
        r = pl.ds(pl.multiple_of(c * chunk, chunk), chunk)
        dst_ref[r, :] = src_ref[r, :].astype(BF16)
        return carry

    lax.fori_loop(0, rows // chunk, body, 0)


def _run_interleaved(steps, shared=None):
    results = [None] * len(steps)
    pending = list(range(len(steps)))
    inbox = {idx: None for idx in pending}
    while pending:
        requests = {}
        for idx in list(pending):
            try:
                req = steps[idx].send(inbox[idx])
            except StopIteration as done:
                results[idx] = done.value
                pending.remove(idx)
                continue
            inbox[idx] = None
            if req is not None:
                requests.setdefault(req[0], []).append((idx, req[1]))
        for key, members in requests.items():
            out = shared[key](jnp.concatenate([rows for _, rows in members], axis=0))
            start = 0
            for idx, rows in members:
                inbox[idx] = out[start:start + rows.shape[0]]
                start += rows.shape[0]
    return results


def _rmsnorm_body(h_ref, g_ref, *o_refs):
    y = _rms(h_ref[...])
    for j, o_ref in enumerate(o_refs):
        o_ref[...] = (y * g_ref[j:j + 1, :]).astype(o_ref.dtype)


def _rmsnorm(h, gains, out_dtypes, tm):
    m, d = h.shape
    n_out = len(out_dtypes)
    outs = pl.pallas_call(
        _rmsnorm_body,
        out_shape=[jax.ShapeDtypeStruct((m, d), dt) for dt in out_dtypes],
        grid=(m // tm,),
        in_specs=[pl.BlockSpec((tm, d), lambda i: (i, 0)),
                  pl.BlockSpec((n_out, d), lambda i: (0, 0))],
        out_specs=[pl.BlockSpec((tm, d), lambda i: (i, 0)) for _ in out_dtypes],
        compiler_params=_cparams(("parallel",)),
        name="rmsnorm",
    )(h, gains)
    return outs


def _drop_meta_body(ha_ref, hb_ref, g_ref, h_ref, *o_refs):
    rows = jnp.concatenate([ha_ref[N_META:, :], hb_ref[...]], axis=0)
    h_ref[...] = rows
    y = _rms(rows)
    for j, o_ref in enumerate(o_refs):
        o_ref[...] = (y * g_ref[j:j + 1, :]).astype(o_ref.dtype)


def _drop_meta(h, gains, seq, tm):
    d = h.shape[1]
    n_out = gains.shape[0]
    per = tm // N_META
    return pl.pallas_call(
        _drop_meta_body,
        out_shape=[jax.ShapeDtypeStruct((seq, d), F32)] + [jax.ShapeDtypeStruct((seq, d), BF16)] * n_out,
        grid=(seq // tm,),
        in_specs=[pl.BlockSpec((tm, d), lambda i: (i, 0)),
                  pl.BlockSpec((N_META, d), lambda i: ((i + 1) * per, 0)),
                  pl.BlockSpec((n_out, d), lambda i: (0, 0))],
        out_specs=[pl.BlockSpec((tm, d), lambda i: (i, 0))] * (1 + n_out),
        compiler_params=_cparams(("parallel",)),
        name="drop_meta",
    )(h, h, gains)


def _premix_body(h_ref, hp_ref, gain_ref, mix_ref, w1_ref, a1_ref, g1_ref, w2_ref, a2_ref, g2_ref, bias_ref,
                 xr_ref, xk_ref, xv_ref, lw_ref, a_ref, g_ref):
    i = pl.program_id(0)
    gain = gain_ref[...]
    xn = _rms(h_ref[...]) * gain
    pn = _rms(hp_ref[...]) * gain
    prev_row = jnp.where(i > 0, pn[7:8, :], 0.0)
    sh = pltpu.roll(xn, 1, axis=0)
    row = lax.broadcasted_iota(jnp.int32, xn.shape, 0)
    sh = jnp.where(row == 0, prev_row, sh)
    xx = sh - xn

    def mixed(j):
        return (xn + xx * mix_ref[j:j + 1, :]).astype(BF16)

    xr_ref[...] = mixed(0)
    xk_ref[...] = mixed(2)
    xv_ref[...] = mixed(3)
    t_w = _dot(mixed(1), w1_ref[...])
    t_a = _dot(mixed(4), a1_ref[...])
    t_g = _dot(mixed(5), g1_ref[...])
    z_w = _dot(jnp.tanh(t_w).astype(BF16), w2_ref[...]) + bias_ref[0:1, :]
    z_a = _dot(t_a.astype(BF16), a2_ref[...]) + bias_ref[1:2, :]
    z_g = _dot(jax.nn.sigmoid(t_g).astype(BF16), g2_ref[...])
    lw_ref[...] = jax.nn.sigmoid(z_w) * (-0.6065306597126334)
    a_ref[...] = jax.nn.sigmoid(z_a).astype(a_ref.dtype)
    g_ref[...] = z_g.astype(g_ref.dtype)


def _premix(h, gain, mix, w1, a1, g1, w2, a2, g2, bias, tm):
    m, d = h.shape
    rows8 = tm // 8
    row_blk = lambda i: (i, 0)
    whole = lambda arr: pl.BlockSpec(arr.shape, lambda i: (0, 0), pipeline_mode=pl.Buffered(1))
    return pl.pallas_call(
        _premix_body,
        out_shape=[jax.ShapeDtypeStruct((m, d), dt) for dt in (BF16, BF16, BF16, F32, BF16, BF16)],
        grid=(m // tm,),
        in_specs=[pl.BlockSpec((tm, d), row_blk),
                  pl.BlockSpec((8, d), lambda i: (jnp.maximum(i * rows8 - 1, 0), 0)),
                  whole(gain), whole(mix), whole(w1), whole(a1), whole(g1), whole(w2), whole(a2), whole(g2),
                  whole(bias)],
        out_specs=[pl.BlockSpec((tm, d), row_blk)] * 6,
        compiler_params=_cparams(("parallel",)),
        name="premix",
    )(h, h, gain, mix, w1, a1, g1, w2, a2, g2, bias)


def _mm_body(x_ref, w_ref, *rest, has_res, scale):
    if has_res:
        res_ref, o_ref, w16_ref = rest
    else:
        o_ref, w16_ref = rest

    @pl.when(pl.program_id(1) == 0)
    def _():
        _round_to_bf16(w_ref, w16_ref)

    acc = _dot(x_ref[...], w16_ref[...])
    if scale is not None:
        acc = acc * scale
    if has_res:
        acc = acc + res_ref[...]
    o_ref[...] = acc.astype(o_ref.dtype)


def _mm(x, w, *, out_dtype, tm, tn, res=None, scale=None, w_index=None):
    m, k = x.shape
    n = w.shape[-1]
    if w_index is None:
        w_spec = pl.BlockSpec((k, tn), lambda j, i: (0, j))
    else:
        w_spec = pl.BlockSpec((None, k, tn), lambda j, i: (w_index, 0, j))
    in_specs = [pl.BlockSpec((tm, k), lambda j, i: (i, 0)), w_spec]
    args = [x, w]
    if res is not None:
        in_specs.append(pl.BlockSpec((tm, tn), lambda j, i: (i, j)))
        args.append(res)
    return pl.pallas_call(
        functools.partial(_mm_body, has_res=res is not None, scale=scale),
        out_shape=jax.ShapeDtypeStruct((m, n), out_dtype),
        grid=(n // tn, m // tm),
        in_specs=in_specs,
        out_specs=pl.BlockSpec((tm, tn), lambda j, i: (i, j)),
        scratch_shapes=[pltpu.VMEM((k, tn), BF16)],
        compiler_params=_cparams(("parallel", "arbitrary")),
        name="matmul",
    )(*args)


def _mm_acc_body(x_ref, w_ref, res_ref, o_ref, acc_ref):
    kk = pl.program_id(2)

    @pl.when(kk == 0)
    def _():
        acc_ref[...] = jnp.zeros_like(acc_ref)

    acc_ref[...] += _dot(x_ref[...], w_ref[...])

    @pl.when(kk == pl.num_programs(2) - 1)
    def _():
        o_ref[...] = res_ref[...] + acc_ref[...]


def _mm_acc(x, w, res, *, tm, tn, tk):
    m, k = x.shape
    n = w.shape[1]
    return pl.pallas_call(
        _mm_acc_body,
        out_shape=jax.ShapeDtypeStruct((m, n), F32),
        grid=(n // tn, m // tm, k // tk),
        in_specs=[pl.BlockSpec((tm, tk), lambda j, i, q: (i, q)),
                  pl.BlockSpec((tk, tn), lambda j, i, q: (q, j)),
                  pl.BlockSpec((tm, tn), lambda j, i, q: (i, j))],
        out_specs=pl.BlockSpec((tm, tn), lambda j, i, q: (i, j)),
        scratch_shapes=[pltpu.VMEM((tm, tn), F32)],
        compiler_params=_cparams(("parallel", "parallel", "arbitrary")),
        name="matmul_acc",
    )(x, w, res)


def _swiglu_body(x_ref, wg_ref, wu_ref, o_ref, wg16_ref, wu16_ref):
    @pl.when(pl.program_id(1) == 0)
    def _():
        _round_to_bf16(wg_ref, wg16_ref)
        _round_to_bf16(wu_ref, wu16_ref)

    x = x_ref[...]
    g = _dot(x, wg16_ref[...])
    u = _dot(x, wu16_ref[...])
    o_ref[...] = (g * jax.nn.sigmoid(g) * u).astype(o_ref.dtype)


def _swiglu(x, wg, wu, *, tm, tn):
    m, k = x.shape
    n = wg.shape[1]
    return pl.pallas_call(
        _swiglu_body,
        out_shape=jax.ShapeDtypeStruct((m, n), BF16),
        grid=(n // tn, m // tm),
        in_specs=[pl.BlockSpec((tm, k), lambda j, i: (i, 0)),
                  pl.BlockSpec((k, tn), lambda j, i: (0, j)),
                  pl.BlockSpec((k, tn), lambda j, i: (0, j))],
        out_specs=pl.BlockSpec((tm, tn), lambda j, i: (i, j)),
        scratch_shapes=[pltpu.VMEM((k, tn), BF16), pltpu.VMEM((k, tn), BF16)],
        compiler_params=_cparams(("parallel", "arbitrary")),
        name="swiglu",
    )(x, wg, wu)


def _split2(x):
    hi = x.astype(BF16)
    lo = (x - hi.astype(F32)).astype(BF16)
    return hi, lo


def _wkv_body(r_ref, k_ref, v_ref, lw_ref, a_ref, g_ref, prm_ref, o_ref, st_ref, *, pairs):
    c_idx = pl.program_id(1)

    @pl.when(c_idx == 0)
    def _():
        st_ref[...] = jnp.zeros_like(st_ref)

    C = CHUNK
    lane = lax.broadcasted_iota(jnp.int32, (C, LANES), 1)
    row = lax.broadcasted_iota(jnp.int32, (C, LANES), 0)
    col = lane % HEAD
    upper_half = lane >= HEAD
    tri_strict = col < row
    tri_incl = col <= row
    eye_pair = (col == row).astype(F32)

    def level_mask(bs):
        return ((row // bs) == (col // bs)) & ((row // (bs // 2)) != (col // (bs // 2)))

    row2 = lax.broadcasted_iota(jnp.int32, (2 * C, LANES), 0)
    lane2 = lax.broadcasted_iota(jnp.int32, (2 * C, LANES), 1)
    bd_mask = (row2 >= C) == (lane2 >= HEAD)
    ones_bd = bd_mask.astype(BF16)
    diag_mask = row2 == lane2

    def bd(x):
        return jnp.concatenate([jnp.where(upper_half, 0.0, x), jnp.where(upper_half, x, 0.0)], axis=0)

    def headsum(x16):
        return _dot(x16, ones_bd)

    def pair_step(r, k, v, lw, a, g, prm, s):
        k_k, k_a, r_k, lnx_w, lnx_b = (prm[j:j + 1] for j in range(5))

        kk = k * k_k
        k2 = k * (1.0 + (a - 1.0) * k_a)
        sums = yield ("headsum", jnp.concatenate([kk * kk, r * k2 * r_k], axis=0).astype(BF16))
        cum = lw
        shift = 1
        while shift < C:
            cum = cum + jnp.where(row >= shift, pltpu.roll(cum, shift, axis=0), 0.0)
            shift *= 2
        kkn = kk * lax.rsqrt(jnp.maximum(sums[:C], 1e-24))
        bonus = sums[C:] * v
        avec = -kkn
        bvec = kkn * a
        cum_end = cum[C - 1:C, :]
        p_incl = jnp.exp(cum)
        p_inv = jnp.exp(-cum)
        a_t = avec * jnp.exp(cum - lw)
        r_t = r * p_incl
        b_t = bvec * p_inv
        k_t = k2 * p_inv
        p_end = jnp.exp(cum_end - cum)
        b_h = bvec * p_end
        k_h = k2 * p_end

        v16 = v.astype(BF16)
        gram = _dot_nt(jnp.concatenate([a_t, r_t], axis=0).astype(BF16),
                       jnp.concatenate([bd(b_t), bd(k_t)], axis=0).astype(BF16))
        yield
        a_ab = jnp.where(tri_strict, gram[:C, :LANES], 0.0)
        a_ak = jnp.where(tri_strict, gram[:C, LANES:], 0.0)
        a_rb = jnp.where(tri_incl, gram[C:, :LANES], 0.0)
        a_rk = jnp.where(tri_incl, gram[C:, LANES:], 0.0)
        akv = _dot(a_ak.astype(BF16), bd(v16))

        t_inv = eye_pair + jnp.where(level_mask(2), a_ab, 0.0)
        bs = 4
        while bs <= C:
            x = jnp.where(level_mask(bs), a_ab, 0.0)
            t16 = t_inv.astype(BF16)
            tx = _dot(t16, bd(x).astype(BF16))
            yield
            t_inv = t_inv + _dot(tx.astype(BF16), bd(t16))
            yield
            bs *= 2
        t16 = t_inv.astype(BF16)

        tt = _dot(t16, jnp.concatenate([bd(a_t.astype(BF16)), bd(akv.astype(BF16))], axis=1))
        yield
        a_hat = tt[:, :LANES].astype(BF16)
        u0 = tt[:, LANES:].astype(BF16)

        a_rb16 = a_rb.astype(BF16)
        qa = _dot(a_rb16, bd(a_hat))
        y0 = _dot(jnp.concatenate([a_rb16, a_rk.astype(BF16)], axis=1),
                  jnp.concatenate([bd(u0), bd(v16)], axis=0))
        lhs_t = jnp.concatenate([b_h, k_h], axis=0).T.astype(BF16)
        upd_m = _dot(lhs_t[:, :C], a_hat)
        upd_n = _dot(lhs_t, jnp.concatenate([u0, v16], axis=0))
        yield
        q_hat = r_t + qa
        m_off = jnp.where(bd_mask, upd_m, 0.0)
        n_new = jnp.where(bd_mask, upd_n, 0.0)
        p_col = jnp.sum(jnp.where(diag_mask, jnp.exp(cum_end), 0.0), axis=1, keepdims=True)

        s_hi, s_lo = _split2(s)
        m16 = m_off.astype(BF16)
        ys = _dot(jnp.concatenate([q_hat.astype(BF16), m16], axis=0), s_hi)
        y = ys[:C] + y0
        s_new = p_col * s + ys[C:] + _dot(m16, s_lo) + n_new

        mu = (yield ("headsum", y.astype(BF16))) * (1.0 / HEAD)
        d = y - mu
        var = (yield ("headsum", (d * d).astype(BF16))) * (1.0 / HEAD)
        yn = d * lax.rsqrt(var + LN_X_EPS) * lnx_w + lnx_b
        return ((yn + bonus) * g.astype(F32)).astype(o_ref.dtype), s_new

    lanes = [slice(p * LANES, (p + 1) * LANES) for p in range(pairs)]
    results = _run_interleaved([pair_step(r_ref[:, sl], k_ref[:, sl], v_ref[:, sl], lw_ref[:, sl],
                                          a_ref[:, sl].astype(F32), g_ref[:, sl], prm_ref[:, sl], st_ref[p])
                                for p, sl in enumerate(lanes)],
                               shared={"headsum": headsum})
    for p, sl in enumerate(lanes):
        o_ref[:, sl] = results[p][0]
        st_ref[p] = results[p][1]


def _wkv(r, k, v, lw, a, g, prm, *, pairs):
    length, d = r.shape
    width = pairs * LANES
    blk = lambda hb, c: (c, hb)
    return pl.pallas_call(
        functools.partial(_wkv_body, pairs=pairs),
        out_shape=jax.ShapeDtypeStruct((length, d), BF16),
        grid=(d // width, length // CHUNK),
        in_specs=[pl.BlockSpec((CHUNK, width), blk)] * 6 + [pl.BlockSpec((8, width), lambda hb, c: (0, hb))],
        out_specs=pl.BlockSpec((CHUNK, width), blk),
        scratch_shapes=[pltpu.VMEM((pairs, LANES, LANES), F32)],
        compiler_params=_cparams(("parallel", "arbitrary")),
        name="wkv7",
    )(r, k, v, lw, a, g, prm)


def _attn_body(q_ref, kp_ref, kc_ref, km_ref, vp_ref, vc_ref, vm_ref, sk_ref, o_ref, *, group, n_heads, q_blocks):
    kvh = pl.program_id(0)
    first_block = pl.program_id(1) * q_blocks
    rows = 2 * BLOCK
    rowi = lax.broadcasted_iota(jnp.int32, (rows, BLOCK), 0)
    kj = lax.broadcasted_iota(jnp.int32, (rows, BLOCK), 1)
    qi = rowi % BLOCK
    second = rowi >= BLOCK
    use_prev = kj > qi
    dist_band = (qi - kj + jnp.where(use_prev, BLOCK, 0)).astype(F32)
    valid_meta = kj < N_META
    lane_o = lax.broadcasted_iota(jnp.int32, (BLOCK, LANES), 1)
    lane_q = lax.broadcasted_iota(jnp.int32, (rows, LANES), 1)
    row_q = lax.broadcasted_iota(jnp.int32, (rows, LANES), 0)
    q_keep = (row_q >= BLOCK) == (lane_q >= HEAD)

    lane_k = lax.broadcasted_iota(jnp.int32, (BLOCK, LANES), 1)
    keep = (lane_k < HEAD) != (kvh % 2 == 1)

    def own_head_twice(t):
        swapped = jnp.concatenate([t[:, HEAD:], t[:, :HEAD]], axis=1)
        return jnp.where(keep, t, swapped)

    km = own_head_twice(km_ref[...])
    vm = own_head_twice(vm_ref[...])
    k_tiles = [own_head_twice(kp_ref[...])] + [own_head_twice(kc_ref[b * BLOCK:(b + 1) * BLOCK, :])
                                               for b in range(q_blocks)]
    v_tiles = [own_head_twice(vp_ref[...])] + [own_head_twice(vc_ref[b * BLOCK:(b + 1) * BLOCK, :])
                                               for b in range(q_blocks)]

    def pair_bias(pr):
        head1 = (kvh * group + 2 * pr + 1).astype(F32)
        slope = jnp.exp2((jnp.where(second, 1.0, 0.0) + head1) * (-8.0 / n_heads))
        sink = jnp.where(second[:, :1], sk_ref[kvh, 2 * pr + 1], sk_ref[kvh, 2 * pr])
        return slope, slope * dist_band, sink

    biases = [pair_bias(pr) for pr in range(group // 2)]

    def pair_step(pr, b, q):
        n = first_block + b
        q2 = jnp.where(q_keep, jnp.concatenate([q, q], axis=0), jnp.zeros((), q.dtype))
        qk_prev = _dot_nt(q2, k_tiles[b])
        qk_cur = _dot_nt(q2, k_tiles[b + 1])
        qk_meta = _dot_nt(q2, km)
        yield
        slope, bias_band, sink = biases[pr]
        dist_meta = (N_META + n * BLOCK + qi - kj).astype(F32)
        s_band = jnp.where(use_prev, qk_prev, qk_cur) - bias_band
        if b == 0:
            s_band = jnp.where(use_prev & (n == 0), NEG, s_band)
        s_meta = jnp.where(valid_meta, qk_meta - slope * dist_meta, NEG)
        mx = jnp.maximum(jnp.max(jnp.maximum(s_band, s_meta), axis=1, keepdims=True), sink)
        e_band = jnp.exp(s_band - mx)
        e_meta = jnp.exp(s_meta - mx)
        den = jnp.sum(e_band + e_meta, axis=1, keepdims=True) + jnp.exp(sink - mx)
        acc = (_dot(jnp.where(use_prev, e_band, 0.0).astype(BF16), v_tiles[b])
               + _dot(jnp.where(use_prev, 0.0, e_band).astype(BF16), v_tiles[b + 1])
               + _dot(e_meta.astype(BF16), vm))
        yield
        acc = acc / den
        return jnp.where(lane_o >= HEAD, acc[BLOCK:], acc[:BLOCK]).astype(o_ref.dtype)

    tiles = [(slice(b * BLOCK, (b + 1) * BLOCK), slice(pr * LANES, (pr + 1) * LANES), pr, b)
             for b in range(q_blocks) for pr in range(group // 2)]
    results = _run_interleaved([pair_step(pr, b, q_ref[rs, ls]) for rs, ls, pr, b in tiles])
    for (rs, ls, _, _), res in zip(tiles, results):
        o_ref[rs, ls] = res


def _attention(q, kv, kv_meta, sinks, *, n_heads):
    assert WINDOW == BLOCK and N_META <= BLOCK
    s_len, d = q.shape
    group = n_heads // ATT_KV_HEADS
    gw = group * HEAD
    nb = s_len // BLOCK
    q_blocks = 2 if nb % 2 == 0 else 1
    v_off = ATT_KV_HEADS // 2
    cur = lambda h, m: (m, h)
    k_prev = lambda h, m: (jnp.maximum(m * q_blocks - 1, 0), h // 2)
    k_cur = lambda h, m: (m, h // 2)
    k_meta = lambda h, m: (0, h // 2)
    v_prev = lambda h, m: (jnp.maximum(m * q_blocks - 1, 0), v_off + h // 2)
    v_cur = lambda h, m: (m, v_off + h // 2)
    v_meta = lambda h, m: (0, v_off + h // 2)
    one = lambda index_map: pl.BlockSpec((BLOCK, LANES), index_map)
    own = lambda index_map: pl.BlockSpec((q_blocks * BLOCK, LANES), index_map)
    return pl.pallas_call(
        functools.partial(_attn_body, group=group, n_heads=n_heads, q_blocks=q_blocks),
        out_shape=jax.ShapeDtypeStruct((s_len, d), BF16),
        grid=(ATT_KV_HEADS, nb // q_blocks),
        in_specs=[pl.BlockSpec((q_blocks * BLOCK, gw), cur),
                  one(k_prev), own(k_cur), one(k_meta), one(v_prev), own(v_cur), one(v_meta),
                  pl.BlockSpec(memory_space=pltpu.SMEM)],
        out_specs=pl.BlockSpec((q_blocks * BLOCK, gw), cur),
        compiler_params=_cparams(("parallel", "parallel")),
        name="swa_attention",
    )(q, kv, kv, kv_meta, kv, kv, kv_meta, sinks)


def _router_body(h_ref, g_ref, wr_ref, sel_ref, gate_ref, *, n_experts):
    y = _rms(h_ref[...]) * g_ref[...]
    y_hi, y_lo = _split2(y)
    w_hi, w_lo = _split2(wr_ref[...])
    logits = _dot(y_hi, w_hi) + _dot(y_lo, w_hi) + _dot(y_hi, w_lo)
    lane = lax.broadcasted_iota(jnp.int32, logits.shape, 1)
    logits = jnp.where(lane < n_experts, logits, NEG)
    m1 = jnp.max(logits, axis=1, keepdims=True)
    i1 = jnp.min(jnp.where(logits == m1, lane, LANES), axis=1, keepdims=True)
    rest = jnp.where(lane == i1, NEG, logits)
    m2 = jnp.max(rest, axis=1, keepdims=True)
    i2 = jnp.min(jnp.where(rest == m2, lane, LANES), axis=1, keepdims=True)
    e2 = jnp.exp(m2 - m1)
    g1 = 1.0 / (1.0 + e2)
    g2 = e2 / (1.0 + e2)
    sel_ref[...] = jnp.where(lane == 0, i1, jnp.where(lane == 1, i2, 0))
    gate_ref[...] = jnp.where(lane == 0, g1, jnp.where(lane == 1, g2, 0.0))


def _router(h, gain, w_router_pad, *, n_experts, tm):
    m, d = h.shape
    return pl.pallas_call(
        functools.partial(_router_body, n_experts=n_experts),
        out_shape=[jax.ShapeDtypeStruct((m, LANES), jnp.int32), jax.ShapeDtypeStruct((m, LANES), F32)],
        grid=(m // tm,),
        in_specs=[pl.BlockSpec((tm, d), lambda i: (i, 0)),
                  pl.BlockSpec((1, d), lambda i: (0, 0)),
                  pl.BlockSpec((d, LANES), lambda i: (0, 0))],
        out_specs=[pl.BlockSpec((tm, LANES), lambda i: (i, 0)), pl.BlockSpec((tm, LANES), lambda i: (i, 0))],
        compiler_params=_cparams(("parallel",)),
        name="moe_router",
    )(h, gain, w_router_pad)


EXPERT_TILE = 256


def _row_copy(src_hbm, dst, src_row, dst_row, sem):
    return pltpu.make_async_copy(src_hbm.at[pl.ds(src_row, 1)], dst.at[pl.ds(dst_row, 1)], sem)


ROW_DMA_UNROLL = 8


def _gather_rows_body(src_ref, x_hbm, gain_ref, o_ref, buf, sem, *, rows):
    i = pl.program_id(0)

    def issue(tile, slot):
        base = tile * rows

        def body(r, carry):
            _row_copy(x_hbm, buf.at[slot], src_ref[base + r], r, sem.at[slot]).start()
            return carry

        lax.fori_loop(0, rows, body, 0, unroll=ROW_DMA_UNROLL)

    @pl.when(i == 0)
    def _():
        issue(0, 0)

    @pl.when(i + 1 < pl.num_programs(0))
    def _():
        issue(i + 1, (i + 1) % 2)

    slot = i % 2
    pltpu.make_async_copy(x_hbm.at[pl.ds(0, rows)], buf.at[slot], sem.at[slot]).wait()
    o_ref[...] = (_rms(buf[slot]) * gain_ref[...]).astype(o_ref.dtype)


def _gather_rows(row_src, x, gain, n_rows, *, rows, out_dtype):
    d = x.shape[1]
    return pl.pallas_call(
        functools.partial(_gather_rows_body, rows=rows),
        out_shape=jax.ShapeDtypeStruct((n_rows, d), out_dtype),
        grid_spec=pltpu.PrefetchScalarGridSpec(
            num_scalar_prefetch=1,
            grid=(n_rows // rows,),
            in_specs=[pl.BlockSpec(memory_space=pl.ANY), pl.BlockSpec((1, d), lambda i, src: (0, 0))],
            out_specs=pl.BlockSpec((rows, d), lambda i, src: (i, 0)),
            scratch_shapes=[pltpu.VMEM((2, rows, d), x.dtype), pltpu.SemaphoreType.DMA((2,))],
        ),
        compiler_params=_cparams(("arbitrary",)),
        name="moe_gather",
    )(row_src, x, gain)


def _stage_expert_weights(te_ref, nxt_ref, w_hbms, wbuf, w16, sem, tn):
    j = pl.program_id(0)
    i = pl.program_id(1)

    def copies(expert, col_block):
        cols = pl.ds(pl.multiple_of(col_block * tn, tn), tn)
        return [pltpu.make_async_copy(w.at[expert, :, cols], wbuf.at[k], sem) for k, w in enumerate(w_hbms)]

    @pl.when((j == 0) & (i == 0))
    def _():
        for c in copies(te_ref[0], 0):
            c.start()

    @pl.when((i == 0) | (te_ref[i] != te_ref[jnp.maximum(i - 1, 0)]))
    def _():
        for c in copies(te_ref[i], j):
            c.wait()
        for k in range(len(w_hbms)):
            _round_to_bf16(wbuf.at[k], w16.at[k])
        next_expert = nxt_ref[i]
        wraps = next_expert < 0

        @pl.when(jnp.logical_not(wraps & (j == pl.num_programs(0) - 1)))
        def _():
            for c in copies(jnp.where(wraps, te_ref[0], next_expert), jnp.where(wraps, j + 1, j)):
                c.start()


def _moe_swiglu_body(te_ref, nxt_ref, used_ref, x_ref, wg_hbm, wu_hbm, o_ref, wbuf, w16, sem, *, tn):
    _stage_expert_weights(te_ref, nxt_ref, (wg_hbm, wu_hbm), wbuf, w16, sem, tn)
    has_tokens = pl.program_id(1) < used_ref[0]

    @pl.when(has_tokens)
    def _():
        x = x_ref[...]
        g = _dot(x, w16[0])
        u = _dot(x, w16[1])
        o_ref[...] = (g * jax.nn.sigmoid(g) * u).astype(o_ref.dtype)

    @pl.when(jnp.logical_not(has_tokens))
    def _():
        o_ref[...] = jnp.zeros_like(o_ref)


def _moe_down_body(te_ref, nxt_ref, used_ref, x_ref, w_hbm, o_ref, wbuf, w16, sem, *, tn):
    _stage_expert_weights(te_ref, nxt_ref, (w_hbm,), wbuf, w16, sem, tn)
    has_tokens = pl.program_id(1) < used_ref[0]

    @pl.when(has_tokens)
    def _():
        o_ref[...] = _dot(x_ref[...], w16[0])

    @pl.when(jnp.logical_not(has_tokens))
    def _():
        o_ref[...] = jnp.zeros_like(o_ref)


def _moe_matmul(body, tile_expert, next_expert, tiles_used, xs, weights, *, tn, out_dtype, name):
    rows, k = xs.shape
    n = weights[0].shape[-1]
    n_w = len(weights)
    return pl.pallas_call(
        functools.partial(body, tn=tn),
        out_shape=jax.ShapeDtypeStruct((rows, n), out_dtype),
        grid_spec=pltpu.PrefetchScalarGridSpec(
            num_scalar_prefetch=3,
            grid=(n // tn, rows // EXPERT_TILE),
            in_specs=[pl.BlockSpec((EXPERT_TILE, k), lambda j, i, te, nxt, used: (i, 0))]
            + [pl.BlockSpec(memory_space=pl.ANY)] * n_w,
            out_specs=pl.BlockSpec((EXPERT_TILE, tn), lambda j, i, te, nxt, used: (i, j)),
            scratch_shapes=[pltpu.VMEM((n_w, k, tn), F32), pltpu.VMEM((n_w, k, tn), BF16),
                            pltpu.SemaphoreType.DMA],
        ),
        compiler_params=_cparams(("arbitrary", "arbitrary")),
        name=name,
    )(tile_expert, next_expert, tiles_used, xs, *weights)


def _moe_combine_body(pos_ref, h_ref, gate_ref, fn_ref, eo_hbm, o_ref, buf, sem, *, tm):
    i = pl.program_id(0)

    def issue(tile, slot):
        base = tile * tm

        def body(t, carry):
            for j in range(TOP_K):
                _row_copy(eo_hbm, buf.at[slot, j], pos_ref[TOP_K * (base + t) + j], t, sem.at[slot]).start()
            return carry

        lax.fori_loop(0, tm, body, 0, unroll=ROW_DMA_UNROLL // TOP_K)

    @pl.when(i == 0)
    def _():
        issue(0, 0)

    @pl.when(i + 1 < pl.num_programs(0))
    def _():
        issue(i + 1, (i + 1) % 2)

    slot = i % 2
    for j in range(TOP_K):
        pltpu.make_async_copy(eo_hbm.at[pl.ds(0, tm)], buf.at[slot, j], sem.at[slot]).wait()
    y = h_ref[...]
    for j in range(TOP_K):
        y = y + gate_ref[:, j:j + 1] * buf[slot, j]
    o_ref[...] = _rms(y) * fn_ref[...]


def _moe_combine(pos, h, gate, final_gain, eo, *, tm):
    m, d = h.shape
    return pl.pallas_call(
        functools.partial(_moe_combine_body, tm=tm),
        out_shape=jax.ShapeDtypeStruct((m, d), F32),
        grid_spec=pltpu.PrefetchScalarGridSpec(
            num_scalar_prefetch=1,
            grid=(m // tm,),
            in_specs=[pl.BlockSpec((tm, d), lambda i, pos: (i, 0)),
                      pl.BlockSpec((tm, LANES), lambda i, pos: (i, 0)),
                      pl.BlockSpec((1, d), lambda i, pos: (0, 0)),
                      pl.BlockSpec(memory_space=pl.ANY)],
            out_specs=pl.BlockSpec((tm, d), lambda i, pos: (i, 0)),
            scratch_shapes=[pltpu.VMEM((2, TOP_K, tm, d), F32), pltpu.SemaphoreType.DMA((2,))],
        ),
        compiler_params=_cparams(("arbitrary",)),
        name="moe_combine",
    )(pos, h, gate, final_gain, eo)


def _moe_plan(sel, n_experts):
    m = sel.shape[0]
    flat_e = sel[:, :TOP_K].reshape(-1)
    onehot = (flat_e[:, None] == jnp.arange(n_experts, dtype=jnp.int32)[None, :]).astype(jnp.int32)
    csum = jnp.cumsum(onehot, axis=0)
    rank = jnp.sum(csum * onehot, axis=1) - 1
    counts = csum[-1]
    padded = (counts + EXPERT_TILE - 1) // EXPERT_TILE * EXPERT_TILE
    ends = jnp.cumsum(padded)
    starts = ends - padded
    pos = (jnp.sum(onehot * starts[None, :], axis=1) + rank).astype(jnp.int32)
    n_rows = TOP_K * m + n_experts * EXPERT_TILE
    token = jnp.arange(TOP_K * m, dtype=jnp.int32) // TOP_K
    row_src = jnp.zeros((n_rows,), jnp.int32).at[pos].set(token)
    tile_start = jnp.arange(n_rows // EXPERT_TILE, dtype=jnp.int32) * EXPERT_TILE
    tile_expert = jnp.minimum(jnp.sum((tile_start[:, None] >= ends[None, :]).astype(jnp.int32), axis=1),
                              n_experts - 1).astype(jnp.int32)
    run_end = jnp.sum((tile_expert[None, :] <= tile_expert[:, None]).astype(jnp.int32), axis=1)
    n_tiles = tile_expert.shape[0]
    next_expert = jnp.where(run_end < n_tiles, tile_expert[jnp.minimum(run_end, n_tiles - 1)], -1).astype(jnp.int32)
    tiles_used = (ends[-1:] // EXPERT_TILE).astype(jnp.int32)
    return pos, row_src, tile_expert, next_expert, tiles_used, n_rows


def _largest_tile(n, cap, mult):
    best = None
    t = mult
    while t <= min(n, cap):
        if n % t == 0:
            best = t
        t += mult
    assert best is not None, (n, cap, mult)
    return best


def _pad_cols(w, n_to):
    return jnp.pad(w, ((0, 0), (0, n_to - w.shape[1])))


def _pad_rows(w, n_to):
    return jnp.pad(w, ((0, n_to - w.shape[0]), (0, 0)))


def _round_up(n, m):
    return -(-n // m) * m


def kernel(x, meta_tokens, a_norm, a_mix, a_w_rkv, a_w0, a_w1, a_w2, a_a0, a_a1, a_a2, a_g1, a_g2, a_k_k, a_k_a, a_r_k, a_lnx_w, a_lnx_b, a_w_out, kv_norm, w_kv, b_norm, b_w_q, b_sinks, b_w_out, f_norm, d_w_gate, d_w_up, d_w_down, e_router, e_w_gate, e_w_up, e_w_down, final_norm):
    assert x.shape[0] == 1
    seq, d = x.shape[1], x.shape[2]
    n_heads = d // HEAD
    n_experts = e_router.shape[-1]
    xs = x[0]

    l_real = N_META + seq
    l_pad = _round_up(l_real, CHUNK)
    h = jnp.concatenate([meta_tokens.astype(F32), xs, jnp.zeros((l_pad - l_real, d), F32)], axis=0)
    tm0 = _largest_tile(l_pad, 768, 16)
    tn = _largest_tile(d, 512, LANES)

    gate_rank = _round_up(a_g1.shape[-1], LANES)
    xr, xk, xv, lw, a, g = _premix(
        h, a_norm[0:1], a_mix[0],
        a_w1[0].astype(BF16), a_a1[0].astype(BF16), _pad_cols(a_g1[0], gate_rank).astype(BF16),
        a_w2[0].astype(BF16), a_a2[0].astype(BF16), _pad_rows(a_g2[0], gate_rank).astype(BF16),
        jnp.stack([a_w0[0], a_a0[0]]), _largest_tile(l_pad, 256, 16))
    tm_rkv = _largest_tile(l_pad, 1408, 16)
    r = _mm(xr, a_w_rkv[0], out_dtype=F32, tm=tm_rkv, tn=tn, w_index=0)
    k = _mm(xk, a_w_rkv[0], out_dtype=F32, tm=tm_rkv, tn=tn, w_index=1)
    v = _mm(xv, a_w_rkv[0], out_dtype=F32, tm=tm_rkv, tn=tn, w_index=2)
    prm = jnp.concatenate([a_k_k[0][None], a_k_a[0][None], a_r_k[0].reshape(1, d), a_lnx_w[0][None],
                           a_lnx_b[0][None], jnp.zeros((3, d), F32)], axis=0)
    pairs = 32 if n_heads % 64 == 0 else n_heads // 2
    mixed = _wkv(r, k, v, lw, a, g, prm, pairs=pairs)
    h = _mm(mixed, a_w_out[0], out_dtype=F32, tm=tm0, tn=tn, res=h)

    (hn,) = _rmsnorm(h, f_norm[0:1], [BF16], _largest_tile(l_pad, 256, 16))
    d_ff = d_w_gate.shape[-1]
    act = _swiglu(hn, d_w_gate[0], d_w_up[0], tm=_largest_tile(l_pad, 1408, 16), tn=_largest_tile(d_ff, 256, LANES))
    h = _mm_acc(act, d_w_down[0].astype(BF16), h, tm=tm0, tn=_largest_tile(d, 1024, LANES),
                tk=_largest_tile(d_ff, 6144, LANES))

    tm1 = _largest_tile(seq, 1024, 16)
    (hkv_meta,) = _rmsnorm(h[:CHUNK], kv_norm[None], [BF16], CHUNK)
    kv_meta = _mm(hkv_meta, w_kv, out_dtype=BF16, tm=CHUNK, tn=tn)
    kv_meta = _pad_rows(kv_meta[:N_META], BLOCK)
    h, hkv, hq = _drop_meta(h, jnp.stack([kv_norm, b_norm[0]]), seq, _largest_tile(seq, 256, N_META))
    kv_real = _mm(hkv, w_kv, out_dtype=BF16, tm=tm1, tn=tn)

    q = _mm(hq, b_w_q[0], out_dtype=BF16, tm=tm1, tn=tn, scale=HEAD ** -0.5)
    o = _attention(q, kv_real, kv_meta, b_sinks[0].reshape(ATT_KV_HEADS, -1), n_heads=n_heads)
    h = _mm(o, b_w_out[0], out_dtype=F32, tm=tm1, tn=tn, res=h)

    sel, gate = _router(h, f_norm[1:2], _pad_cols(e_router[0], LANES), n_experts=n_experts,
                        tm=_largest_tile(seq, 256, 16))
    pos, row_src, tile_expert, next_expert, tiles_used, n_rows = _moe_plan(sel, n_experts)
    xs_sorted = _gather_rows(row_src, h, f_norm[1:2], n_rows, rows=_largest_tile(n_rows, 512, EXPERT_TILE),
                             out_dtype=BF16)
    d_exp = e_w_gate.shape[-1]
    act = _moe_matmul(_moe_swiglu_body, tile_expert, next_expert, tiles_used, xs_sorted, (e_w_gate[0], e_w_up[0]),
                      tn=_largest_tile(d_exp, 896, LANES), out_dtype=BF16, name="moe_swiglu")
    eo = _moe_matmul(_moe_down_body, tile_expert, next_expert, tiles_used, act, (e_w_down[0],),
                     tn=_largest_tile(d, 1024, LANES), out_dtype=F32, name="moe_down")
    out = _moe_combine(pos, h, gate, final_norm[None], eo, tm=_largest_tile(seq, 128, 8))
    return out[None]
```

```python
import functools

import jax
import jax.numpy as jnp
from jax import lax
from jax.experimental import pallas as pl
from jax.experimental.pallas import tpu as pltpu

F32 = jnp.float32
BF16 = jnp.bfloat16

LANES = 128
VMEM_LIMIT_BYTES = 56 * 1024 * 1024

N_META = 16
RMS_EPS = 1e-5
LN_X_EPS = 64e-5
HEAD = 64
ATT_KV_HEADS = 8
WINDOW = 128
BLOCK = 128
TOP_K = 2
CHUNK = 64
NEG = -1e30


def _cparams(sem):
    return pltpu.CompilerParams(dimension_semantics=sem, vmem_limit_bytes=VMEM_LIMIT_BYTES)


def _dot(a, b):
    return jnp.dot(a, b, preferred_element_type=F32)


def _dot_nt(a, b):
    return lax.dot_general(a, b, (((1,), (1,)), ((), ())), preferred_element_type=F32)


def _rms(x):
    return x * lax.rsqrt(jnp.mean(x * x, axis=-1, keepdims=True) + RMS_EPS)


CAST_ROWS = 256


def _round_to_bf16(src_ref, dst_ref):
    rows = src_ref.shape[0]
    chunk = CAST_ROWS if rows % CAST_ROWS == 0 else rows

    def body(c, carry):
        r = pl.ds(pl.multiple_of(c * chunk, chunk), chunk)
        dst_ref[r, :] = src_ref[r, :].astype(BF16)
        return carry

    lax.fori_loop(0, rows // chunk, body, 0)


def _run_interleaved(steps, shared=None):
    results = [None] * len(steps)
    pending = list(range(len(steps)))
    inbox = {idx: None for idx in pending}
    while pending:
        requests = {}
        for idx in list(pending):
            try:
                req = steps[idx].send(inbox[idx])
            except StopIteration as done:
                results[idx] = done.value
                pending.remove(idx)
                continue
            inbox[idx] = None
            if req is not None:
                requests.setdefault(req[0], []).append((idx, req[1]))
        for key, members in requests.items():
            out = shared[key](jnp.concatenate([rows for _, rows in members], axis=0))
            start = 0
            for idx, rows in members:
                inbox[idx] = out[start:start + rows.shape[0]]
                start += rows.shape[0]
    return results


def _rmsnorm_body(h_ref, g_ref, *o_refs):
    y = _rms(h_ref[...])
    for j, o_ref in enumerate(o_refs):
        o_ref[...] = (y * g_ref[j:j + 1, :]).astype(o_ref.dtype)


def _rmsnorm(h, gains, out_dtypes, tm):
    m, d = h.shape
    n_out = len(out_dtypes)
    outs = pl.pallas_call(
        _rmsnorm_body,
        out_shape=[jax.ShapeDtypeStruct((m, d), dt) for dt in out_dtypes],
        grid=(m // tm,),
        in_specs=[pl.BlockSpec((tm, d), lambda i: (i, 0)),
                  pl.BlockSpec((n_out, d), lambda i: (0, 0))],
        out_specs=[pl.BlockSpec((tm, d), lambda i: (i, 0)) for _ in out_dtypes],
        compiler_params=_cparams(("parallel",)),
        name="rmsnorm",
    )(h, gains)
    return outs


def _drop_meta_body(ha_ref, hb_ref, g_ref, h_ref, *o_refs):
    rows = jnp.concatenate([ha_ref[N_META:, :], hb_ref[...]], axis=0)
    h_ref[...] = rows
    y = _rms(rows)
    for j, o_ref in enumerate(o_refs):
        o_ref[...] = (y * g_ref[j:j + 1, :]).astype(o_ref.dtype)


def _drop_meta(h, gains, seq, tm):
    d = h.shape[1]
    n_out = gains.shape[0]
    per = tm // N_META
    return pl.pallas_call(
        _drop_meta_body,
        out_shape=[jax.ShapeDtypeStruct((seq, d), F32)] + [jax.ShapeDtypeStruct((seq, d), BF16)] * n_out,
        grid=(seq // tm,),
        in_specs=[pl.BlockSpec((tm, d), lambda i: (i, 0)),
                  pl.BlockSpec((N_META, d), lambda i: ((i + 1) * per, 0)),
                  pl.BlockSpec((n_out, d), lambda i: (0, 0))],
        out_specs=[pl.BlockSpec((tm, d), lambda i: (i, 0))] * (1 + n_out),
        compiler_params=_cparams(("parallel",)),
        name="drop_meta",
    )(h, h, gains)


def _premix_body(h_ref, hp_ref, gain_ref, mix_ref, w1_ref, a1_ref, g1_ref, w2_ref, a2_ref, g2_ref, bias_ref,
                 xr_ref, xk_ref, xv_ref, lw_ref, a_ref, g_ref):
    i = pl.program_id(0)
    gain = gain_ref[...]
    xn = _rms(h_ref[...]) * gain
    pn = _rms(hp_ref[...]) * gain
    prev_row = jnp.where(i > 0, pn[7:8, :], 0.0)
    sh = pltpu.roll(xn, 1, axis=0)
    row = lax.broadcasted_iota(jnp.int32, xn.shape, 0)
    sh = jnp.where(row == 0, prev_row, sh)
    xx = sh - xn

    def mixed(j):
        return (xn + xx * mix_ref[j:j + 1, :]).astype(BF16)

    xr_ref[...] = mixed(0)
    xk_ref[...] = mixed(2)
    xv_ref[...] = mixed(3)
    t_w = _dot(mixed(1), w1_ref[...])
    t_a = _dot(mixed(4), a1_ref[...])
    t_g = _dot(mixed(5), g1_ref[...])
    z_w = _dot(jnp.tanh(t_w).astype(BF16), w2_ref[...]) + bias_ref[0:1, :]
    z_a = _dot(t_a.astype(BF16), a2_ref[...]) + bias_ref[1:2, :]
    z_g = _dot(jax.nn.sigmoid(t_g).astype(BF16), g2_ref[...])
    lw_ref[...] = jax.nn.sigmoid(z_w) * (-0.6065306597126334)
    a_ref[...] = jax.nn.sigmoid(z_a).astype(a_ref.dtype)
    g_ref[...] = z_g.astype(g_ref.dtype)


def _premix(h, gain, mix, w1, a1, g1, w2, a2, g2, bias, tm):
    m, d = h.shape
    rows8 = tm // 8
    row_blk = lambda i: (i, 0)
    whole = lambda arr: pl.BlockSpec(arr.shape, lambda i: (0, 0), pipeline_mode=pl.Buffered(1))
    return pl.pallas_call(
        _premix_body,
        out_shape=[jax.ShapeDtypeStruct((m, d), dt) for dt in (BF16, BF16, BF16, F32, BF16, BF16)],
        grid=(m // tm,),
        in_specs=[pl.BlockSpec((tm, d), row_blk),
                  pl.BlockSpec((8, d), lambda i: (jnp.maximum(i * rows8 - 1, 0), 0)),
                  whole(gain), whole(mix), whole(w1), whole(a1), whole(g1), whole(w2), whole(a2), whole(g2),
                  whole(bias)],
        out_specs=[pl.BlockSpec((tm, d), row_blk)] * 6,
        compiler_params=_cparams(("parallel",)),
        name="premix",
    )(h, h, gain, mix, w1, a1, g1, w2, a2, g2, bias)


def _mm_body(x_ref, w_ref, *rest, has_res, scale):
    if has_res:
        res_ref, o_ref, w16_ref = rest
    else:
        o_ref, w16_ref = rest

    @pl.when(pl.program_id(1) == 0)
    def _():
        _round_to_bf16(w_ref, w16_ref)

    acc = _dot(x_ref[...], w16_ref[...])
    if scale is not None:
        acc = acc * scale
    if has_res:
        acc = acc + res_ref[...]
    o_ref[...] = acc.astype(o_ref.dtype)


def _mm(x, w, *, out_dtype, tm, tn, res=None, scale=None, w_index=None):
    m, k = x.shape
    n = w.shape[-1]
    if w_index is None:
        w_spec = pl.BlockSpec((k, tn), lambda j, i: (0, j))
    else:
        w_spec = pl.BlockSpec((None, k, tn), lambda j, i: (w_index, 0, j))
    in_specs = [pl.BlockSpec((tm, k), lambda j, i: (i, 0)), w_spec]
    args = [x, w]
    if res is not None:
        in_specs.append(pl.BlockSpec((tm, tn), lambda j, i: (i, j)))
        args.append(res)
    return pl.pallas_call(
        functools.partial(_mm_body, has_res=res is not None, scale=scale),
        out_shape=jax.ShapeDtypeStruct((m, n), out_dtype),
        grid=(n // tn, m // tm),
        in_specs=in_specs,
        out_specs=pl.BlockSpec((tm, tn), lambda j, i: (i, j)),
        scratch_shapes=[pltpu.VMEM((k, tn), BF16)],
        compiler_params=_cparams(("parallel", "arbitrary")),
        name="matmul",
    )(*args)


def _mm_acc_body(x_ref, w_ref, res_ref, o_ref, acc_ref):
    kk = pl.program_id(2)

    @pl.when(kk == 0)
    def _():
        acc_ref[...] = jnp.zeros_like(acc_ref)

    acc_ref[...] += _dot(x_ref[...], w_ref[...])

    @pl.when(kk == pl.num_programs(2) - 1)
    def _():
        o_ref[...] = res_ref[...] + acc_ref[...]


def _mm_acc(x, w, res, *, tm, tn, tk):
    m, k = x.shape
    n = w.shape[1]
    return pl.pallas_call(
        _mm_acc_body,
        out_shape=jax.ShapeDtypeStruct((m, n), F32),
        grid=(n // tn, m // tm, k // tk),
        in_specs=[pl.BlockSpec((tm, tk), lambda j, i, q: (i, q)),
                  pl.BlockSpec((tk, tn), lambda j, i, q: (q, j)),
                  pl.BlockSpec((tm, tn), lambda j, i, q: (i, j))],
        out_specs=pl.BlockSpec((tm, tn), lambda j, i, q: (i, j)),
        scratch_shapes=[pltpu.VMEM((tm, tn), F32)],
        compiler_params=_cparams(("parallel", "parallel", "arbitrary")),
        name="matmul_acc",
    )(x, w, res)


def _swiglu_body(x_ref, wg_ref, wu_ref, o_ref, wg16_ref, wu16_ref):
    @pl.when(pl.program_id(1) == 0)
    def _():
        _round_to_bf16(wg_ref, wg16_ref)
        _round_to_bf16(wu_ref, wu16_ref)

    x = x_ref[...]
    g = _dot(x, wg16_ref[...])
    u = _dot(x, wu16_ref[...])
    o_ref[...] = (g * jax.nn.sigmoid(g) * u).astype(o_ref.dtype)


def _swiglu(x, wg, wu, *, tm, tn):
    m, k = x.shape
    n = wg.shape[1]
    return pl.pallas_call(
        _swiglu_body,
        out_shape=jax.ShapeDtypeStruct((m, n), BF16),
        grid=(n // tn, m // tm),
        in_specs=[pl.BlockSpec((tm, k), lambda j, i: (i, 0)),
                  pl.BlockSpec((k, tn), lambda j, i: (0, j)),
                  pl.BlockSpec((k, tn), lambda j, i: (0, j))],
        out_specs=pl.BlockSpec((tm, tn), lambda j, i: (i, j)),
        scratch_shapes=[pltpu.VMEM((k, tn), BF16), pltpu.VMEM((k, tn), BF16)],
        compiler_params=_cparams(("parallel", "arbitrary")),
        name="swiglu",
    )(x, wg, wu)


def _split2(x):
    hi = x.astype(BF16)
    lo = (x - hi.astype(F32)).astype(BF16)
    return hi, lo


def _wkv_body(r_ref, k_ref, v_ref, lw_ref, a_ref, g_ref, prm_ref, o_ref, st_ref, *, pairs):
    c_idx = pl.program_id(1)

    @pl.when(c_idx == 0)
    def _():
        st_ref[...] = jnp.zeros_like(st_ref)

    C = CHUNK
    lane = lax.broadcasted_iota(jnp.int32, (C, LANES), 1)
    row = lax.broadcasted_iota(jnp.int32, (C, LANES), 0)
    col = lane % HEAD
    upper_half = lane >= HEAD
    tri_strict = col < row
    tri_incl = col <= row
    eye_pair = (col == row).astype(F32)

    def level_mask(bs):
        return ((row // bs) == (col // bs)) & ((row // (bs // 2)) != (col // (bs // 2)))

    row2 = lax.broadcasted_iota(jnp.int32, (2 * C, LANES), 0)
    lane2 = lax.broadcasted_iota(jnp.int32, (2 * C, LANES), 1)
    bd_mask = (row2 >= C) == (lane2 >= HEAD)
    ones_bd = bd_mask.astype(BF16)
    diag_mask = row2 == lane2

    def bd(x):
        return jnp.concatenate([jnp.where(upper_half, 0.0, x), jnp.where(upper_half, x, 0.0)], axis=0)

    def headsum(x16):
        return _dot(x16, ones_bd)

    def pair_step(r, k, v, lw, a, g, prm, s):
        k_k, k_a, r_k, lnx_w, lnx_b = (prm[j:j + 1] for j in range(5))

        kk = k * k_k
        k2 = k * (1.0 + (a - 1.0) * k_a)
        sums = yield ("headsum", jnp.concatenate([kk * kk, r * k2 * r_k], axis=0).astype(BF16))
        cum = lw
        shift = 1
        while shift < C:
            cum = cum + jnp.where(row >= shift, pltpu.roll(cum, shift, axis=0), 0.0)
            shift *= 2
        kkn = kk * lax.rsqrt(jnp.maximum(sums[:C], 1e-24))
        bonus = sums[C:] * v
        avec = -kkn
        bvec = kkn * a
        cum_end = cum[C - 1:C, :]
        p_incl = jnp.exp(cum)
        p_inv = jnp.exp(-cum)
        a_t = avec * jnp.exp(cum - lw)
        r_t = r * p_incl
        b_t = bvec * p_inv
        k_t = k2 * p_inv
        p_end = jnp.exp(cum_end - cum)
        b_h = bvec * p_end
        k_h = k2 * p_end

        v16 = v.astype(BF16)
        gram = _dot_nt(jnp.concatenate([a_t, r_t], axis=0).astype(BF16),
                       jnp.concatenate([bd(b_t), bd(k_t)], axis=0).astype(BF16))
        yield
        a_ab = jnp.where(tri_strict, gram[:C, :LANES], 0.0)
        a_ak = jnp.where(tri_strict, gram[:C, LANES:], 0.0)
        a_rb = jnp.where(tri_incl, gram[C:, :LANES], 0.0)
        a_rk = jnp.where(tri_incl, gram[C:, LANES:], 0.0)
        akv = _dot(a_ak.astype(BF16), bd(v16))

        t_inv = eye_pair + jnp.where(level_mask(2), a_ab, 0.0)
        bs = 4
        while bs <= C:
            x = jnp.where(level_mask(bs), a_ab, 0.0)
            t16 = t_inv.astype(BF16)
            tx = _dot(t16, bd(x).astype(BF16))
            yield
            t_inv = t_inv + _dot(tx.astype(BF16), bd(t16))
            yield
            bs *= 2
        t16 = t_inv.astype(BF16)

        tt = _dot(t16, jnp.concatenate([bd(a_t.astype(BF16)), bd(akv.astype(BF16))], axis=1))
        yield
        a_hat = tt[:, :LANES].astype(BF16)
        u0 = tt[:, LANES:].astype(BF16)

        a_rb16 = a_rb.astype(BF16)
        qa = _dot(a_rb16, bd(a_hat))
        y0 = _dot(jnp.concatenate([a_rb16, a_rk.astype(BF16)], axis=1),
                  jnp.concatenate([bd(u0), bd(v16)], axis=0))
        lhs_t = jnp.concatenate([b_h, k_h], axis=0).T.astype(BF16)
        upd_m = _dot(lhs_t[:, :C], a_hat)
        upd_n = _dot(lhs_t, jnp.concatenate([u0, v16], axis=0))
        yield
        q_hat = r_t + qa
        m_off = jnp.where(bd_mask, upd_m, 0.0)
        n_new = jnp.where(bd_mask, upd_n, 0.0)
        p_col = jnp.sum(jnp.where(diag_mask, jnp.exp(cum_end), 0.0), axis=1, keepdims=True)

        s_hi, s_lo = _split2(s)
        m16 = m_off.astype(BF16)
        ys = _dot(jnp.concatenate([q_hat.astype(BF16), m16], axis=0), s_hi)
        y = ys[:C] + y0
        s_new = p_col * s + ys[C:] + _dot(m16, s_lo) + n_new

        mu = (yield ("headsum", y.astype(BF16))) * (1.0 / HEAD)
        d = y - mu
        var = (yield ("headsum", (d * d).astype(BF16))) * (1.0 / HEAD)
        yn = d * lax.rsqrt(var + LN_X_EPS) * lnx_w + lnx_b
        return ((yn + bonus) * g.astype(F32)).astype(o_ref.dtype), s_new

    lanes = [slice(p * LANES, (p + 1) * LANES) for p in range(pairs)]
    results = _run_interleaved([pair_step(r_ref[:, sl], k_ref[:, sl], v_ref[:, sl], lw_ref[:, sl],
                                          a_ref[:, sl].astype(F32), g_ref[:, sl], prm_ref[:, sl], st_ref[p])
                                for p, sl in enumerate(lanes)],
                               shared={"headsum": headsum})
    for p, sl in enumerate(lanes):
        o_ref[:, sl] = results[p][0]
        st_ref[p] = results[p][1]


def _wkv(r, k, v, lw, a, g, prm, *, pairs):
    length, d = r.shape
    width = pairs * LANES
    blk = lambda hb, c: (c, hb)
    return pl.pallas_call(
        functools.partial(_wkv_body, pairs=pairs),
        out_shape=jax.ShapeDtypeStruct((length, d), BF16),
        grid=(d // width, length // CHUNK),
        in_specs=[pl.BlockSpec((CHUNK, width), blk)] * 6 + [pl.BlockSpec((8, width), lambda hb, c: (0, hb))],
        out_specs=pl.BlockSpec((CHUNK, width), blk),
        scratch_shapes=[pltpu.VMEM((pairs, LANES, LANES), F32)],
        compiler_params=_cparams(("parallel", "arbitrary")),
        name="wkv7",
    )(r, k, v, lw, a, g, prm)


def _attn_body(q_ref, kp_ref, kc_ref, km_ref, vp_ref, vc_ref, vm_ref, sk_ref, o_ref, *, group, n_heads, q_blocks):
    kvh = pl.program_id(0)
    first_block = pl.program_id(1) * q_blocks
    rows = 2 * BLOCK
    rowi = lax.broadcasted_iota(jnp.int32, (rows, BLOCK), 0)
    kj = lax.broadcasted_iota(jnp.int32, (rows, BLOCK), 1)
    qi = rowi % BLOCK
    second = rowi >= BLOCK
    use_prev = kj > qi
    dist_band = (qi - kj + jnp.where(use_prev, BLOCK, 0)).astype(F32)
    valid_meta = kj < N_META
    lane_o = lax.broadcasted_iota(jnp.int32, (BLOCK, LANES), 1)
    lane_q = lax.broadcasted_iota(jnp.int32, (rows, LANES), 1)
    row_q = lax.broadcasted_iota(jnp.int32, (rows, LANES), 0)
    q_keep = (row_q >= BLOCK) == (lane_q >= HEAD)

    lane_k = lax.broadcasted_iota(jnp.int32, (BLOCK, LANES), 1)
    keep = (lane_k < HEAD) != (kvh % 2 == 1)

    def own_head_twice(t):
        swapped = jnp.concatenate([t[:, HEAD:], t[:, :HEAD]], axis=1)
        return jnp.where(keep, t, swapped)

    km = own_head_twice(km_ref[...])
    vm = own_head_twice(vm_ref[...])
    k_tiles = [own_head_twice(kp_ref[...])] + [own_head_twice(kc_ref[b * BLOCK:(b + 1) * BLOCK, :])
                                               for b in range(q_blocks)]
    v_tiles = [own_head_twice(vp_ref[...])] + [own_head_twice(vc_ref[b * BLOCK:(b + 1) * BLOCK, :])
                                               for b in range(q_blocks)]

    def pair_bias(pr):
        head1 = (kvh * group + 2 * pr + 1).astype(F32)
        slope = jnp.exp2((jnp.where(second, 1.0, 0.0) + head1) * (-8.0 / n_heads))
        sink = jnp.where(second[:, :1], sk_ref[kvh, 2 * pr + 1], sk_ref[kvh, 2 * pr])
        return slope, slope * dist_band, sink

    biases = [pair_bias(pr) for pr in range(group // 2)]

    def pair_step(pr, b, q):
        n = first_block + b
        q2 = jnp.where(q_keep, jnp.concatenate([q, q], axis=0), jnp.zeros((), q.dtype))
        qk_prev = _dot_nt(q2, k_tiles[b])
        qk_cur = _dot_nt(q2, k_tiles[b + 1])
        qk_meta = _dot_nt(q2, km)
        yield
        slope, bias_band, sink = biases[pr]
        dist_meta = (N_META + n * BLOCK + qi - kj).astype(F32)
        s_band = jnp.where(use_prev, qk_prev, qk_cur) - bias_band
        if b == 0:
            s_band = jnp.where(use_prev & (n == 0), NEG, s_band)
        s_meta = jnp.where(valid_meta, qk_meta - slope * dist_meta, NEG)
        mx = jnp.maximum(jnp.max(jnp.maximum(s_band, s_meta), axis=1, keepdims=True), sink)
        e_band = jnp.exp(s_band - mx)
        e_meta = jnp.exp(s_meta - mx)
        den = jnp.sum(e_band + e_meta, axis=1, keepdims=True) + jnp.exp(sink - mx)
        acc = (_dot(jnp.where(use_prev, e_band, 0.0).astype(BF16), v_tiles[b])
               + _dot(jnp.where(use_prev, 0.0, e_band).astype(BF16), v_tiles[b + 1])
               + _dot(e_meta.astype(BF16), vm))
        yield
        acc = acc / den
        return jnp.where(lane_o >= HEAD, acc[BLOCK:], acc[:BLOCK]).astype(o_ref.dtype)

    tiles = [(slice(b * BLOCK, (b + 1) * BLOCK), slice(pr * LANES, (pr + 1) * LANES), pr, b)
             for b in range(q_blocks) for pr in range(group // 2)]
    results = _run_interleaved([pair_step(pr, b, q_ref[rs, ls]) for rs, ls, pr, b in tiles])
    for (rs, ls, _, _), res in zip(tiles, results):
        o_ref[rs, ls] = res


def _attention(q, kv, kv_meta, sinks, *, n_heads):
    assert WINDOW == BLOCK and N_META <= BLOCK
    s_len, d = q.shape
    group = n_heads // ATT_KV_HEADS
    gw = group * HEAD
    nb = s_len // BLOCK
    q_blocks = 2 if nb % 2 == 0 else 1
    v_off = ATT_KV_HEADS // 2
    cur = lambda h, m: (m, h)
    k_prev = lambda h, m: (jnp.maximum(m * q_blocks - 1, 0), h // 2)
    k_cur = lambda h, m: (m, h // 2)
    k_meta = lambda h, m: (0, h // 2)
    v_prev = lambda h, m: (jnp.maximum(m * q_blocks - 1, 0), v_off + h // 2)
    v_cur = lambda h, m: (m, v_off + h // 2)
    v_meta = lambda h, m: (0, v_off + h // 2)
    one = lambda index_map: pl.BlockSpec((BLOCK, LANES), index_map)
    own = lambda index_map: pl.BlockSpec((q_blocks * BLOCK, LANES), index_map)
    return pl.pallas_call(
        functools.partial(_attn_body, group=group, n_heads=n_heads, q_blocks=q_blocks),
        out_shape=jax.ShapeDtypeStruct((s_len, d), BF16),
        grid=(ATT_KV_HEADS, nb // q_blocks),
        in_specs=[pl.BlockSpec((q_blocks * BLOCK, gw), cur),
                  one(k_prev), own(k_cur), one(k_meta), one(v_prev), own(v_cur), one(v_meta),
                  pl.BlockSpec(memory_space=pltpu.SMEM)],
        out_specs=pl.BlockSpec((q_blocks * BLOCK, gw), cur),
        compiler_params=_cparams(("parallel", "parallel")),
        name="swa_attention",
    )(q, kv, kv, kv_meta, kv, kv, kv_meta, sinks)


def _router_body(h_ref, g_ref, wr_ref, sel_ref, gate_ref, *, n_experts):
    y = _rms(h_ref[...]) * g_ref[...]
    y_hi, y_lo = _split2(y)
    w_hi, w_lo = _split2(wr_ref[...])
    logits = _dot(y_hi, w_hi) + _dot(y_lo, w_hi) + _dot(y_hi, w_lo)
    lane = lax.broadcasted_iota(jnp.int32, logits.shape, 1)
    logits = jnp.where(lane < n_experts, logits, NEG)
    m1 = jnp.max(logits, axis=1, keepdims=True)
    i1 = jnp.min(jnp.where(logits == m1, lane, LANES), axis=1, keepdims=True)
    rest = jnp.where(lane == i1, NEG, logits)
    m2 = jnp.max(rest, axis=1, keepdims=True)
    i2 = jnp.min(jnp.where(rest == m2, lane, LANES), axis=1, keepdims=True)
    e2 = jnp.exp(m2 - m1)
    g1 = 1.0 / (1.0 + e2)
    g2 = e2 / (1.0 + e2)
    sel_ref[...] = jnp.where(lane == 0, i1, jnp.where(lane == 1, i2, 0))
    gate_ref[...] = jnp.where(lane == 0, g1, jnp.where(lane == 1, g2, 0.0))


def _router(h, gain, w_router_pad, *, n_experts, tm):
    m, d = h.shape
    return pl.pallas_call(
        functools.partial(_router_body, n_experts=n_experts),
        out_shape=[jax.ShapeDtypeStruct((m, LANES), jnp.int32), jax.ShapeDtypeStruct((m, LANES), F32)],
        grid=(m // tm,),
        in_specs=[pl.BlockSpec((tm, d), lambda i: (i, 0)),
                  pl.BlockSpec((1, d), lambda i: (0, 0)),
                  pl.BlockSpec((d, LANES), lambda i: (0, 0))],
        out_specs=[pl.BlockSpec((tm, LANES), lambda i: (i, 0)), pl.BlockSpec((tm, LANES), lambda i: (i, 0))],
        compiler_params=_cparams(("parallel",)),
        name="moe_router",
    )(h, gain, w_router_pad)


EXPERT_TILE = 256


def _row_copy(src_hbm, dst, src_row, dst_row, sem):
    return pltpu.make_async_copy(src_hbm.at[pl.ds(src_row, 1)], dst.at[pl.ds(dst_row, 1)], sem)


ROW_DMA_UNROLL = 8


def _gather_rows_body(src_ref, x_hbm, gain_ref, o_ref, buf, sem, *, rows):
    i = pl.program_id(0)

    def issue(tile, slot):
        base = tile * rows

        def body(r2, carry):
            for prio in range(2):
                r = 2 * r2 + prio
                _row_copy(x_hbm, buf.at[slot], src_ref[base + r], r, sem.at[slot]).start(priority=prio)
            return carry

        lax.fori_loop(0, rows // 2, body, 0, unroll=ROW_DMA_UNROLL // 2)

    @pl.when(i == 0)
    def _():
        issue(0, 0)

    @pl.when(i + 1 < pl.num_programs(0))
    def _():
        issue(i + 1, (i + 1) % 2)

    slot = i % 2
    pltpu.make_async_copy(x_hbm.at[pl.ds(0, rows)], buf.at[slot], sem.at[slot]).wait()
    o_ref[...] = (_rms(buf[slot]) * gain_ref[...]).astype(o_ref.dtype)


def _gather_rows(row_src, x, gain, n_rows, *, rows, out_dtype):
    d = x.shape[1]
    return pl.pallas_call(
        functools.partial(_gather_rows_body, rows=rows),
        out_shape=jax.ShapeDtypeStruct((n_rows, d), out_dtype),
        grid_spec=pltpu.PrefetchScalarGridSpec(
            num_scalar_prefetch=1,
            grid=(n_rows // rows,),
            in_specs=[pl.BlockSpec(memory_space=pl.ANY), pl.BlockSpec((1, d), lambda i, src: (0, 0))],
            out_specs=pl.BlockSpec((rows, d), lambda i, src: (i, 0)),
            scratch_shapes=[pltpu.VMEM((2, rows, d), x.dtype), pltpu.SemaphoreType.DMA((2,))],
        ),
        compiler_params=_cparams(("arbitrary",)),
        name="moe_gather",
    )(row_src, x, gain)


def _stage_expert_weights(te_ref, nxt_ref, w_hbms, wbuf, w16, sem, tn):
    j = pl.program_id(0)
    i = pl.program_id(1)

    def copies(expert, col_block):
        cols = pl.ds(pl.multiple_of(col_block * tn, tn), tn)
        return [pltpu.make_async_copy(w.at[expert, :, cols], wbuf.at[k], sem) for k, w in enumerate(w_hbms)]

    @pl.when((j == 0) & (i == 0))
    def _():
        for c in copies(te_ref[0], 0):
            c.start()

    @pl.when((i == 0) | (te_ref[i] != te_ref[jnp.maximum(i - 1, 0)]))
    def _():
        for c in copies(te_ref[i], j):
            c.wait()
        for k in range(len(w_hbms)):
            _round_to_bf16(wbuf.at[k], w16.at[k])
        next_expert = nxt_ref[i]
        wraps = next_expert < 0

        @pl.when(jnp.logical_not(wraps & (j == pl.num_programs(0) - 1)))
        def _():
            for c in copies(jnp.where(wraps, te_ref[0], next_expert), jnp.where(wraps, j + 1, j)):
                c.start()


def _moe_swiglu_body(te_ref, nxt_ref, used_ref, x_ref, wg_hbm, wu_hbm, o_ref, wbuf, w16, sem, *, tn):
    _stage_expert_weights(te_ref, nxt_ref, (wg_hbm, wu_hbm), wbuf, w16, sem, tn)
    has_tokens = pl.program_id(1) < used_ref[0]

    @pl.when(has_tokens)
    def _():
        x = x_ref[...]
        g = _dot(x, w16[0])
        u = _dot(x, w16[1])
        o_ref[...] = (g * jax.nn.sigmoid(g) * u).astype(o_ref.dtype)

    @pl.when(jnp.logical_not(has_tokens))
    def _():
        o_ref[...] = jnp.zeros_like(o_ref)


def _moe_down_body(te_ref, nxt_ref, used_ref, x_ref, w_hbm, o_ref, wbuf, w16, sem, *, tn):
    _stage_expert_weights(te_ref, nxt_ref, (w_hbm,), wbuf, w16, sem, tn)
    has_tokens = pl.program_id(1) < used_ref[0]

    @pl.when(has_tokens)
    def _():
        o_ref[...] = _dot(x_ref[...], w16[0])

    @pl.when(jnp.logical_not(has_tokens))
    def _():
        o_ref[...] = jnp.zeros_like(o_ref)


def _moe_matmul(body, tile_expert, next_expert, tiles_used, xs, weights, *, tn, out_dtype, name):
    rows, k = xs.shape
    n = weights[0].shape[-1]
    n_w = len(weights)
    return pl.pallas_call(
        functools.partial(body, tn=tn),
        out_shape=jax.ShapeDtypeStruct((rows, n), out_dtype),
        grid_spec=pltpu.PrefetchScalarGridSpec(
            num_scalar_prefetch=3,
            grid=(n // tn, rows // EXPERT_TILE),
            in_specs=[pl.BlockSpec((EXPERT_TILE, k), lambda j, i, te, nxt, used: (i, 0))]
            + [pl.BlockSpec(memory_space=pl.ANY)] * n_w,
            out_specs=pl.BlockSpec((EXPERT_TILE, tn), lambda j, i, te, nxt, used: (i, j)),
            scratch_shapes=[pltpu.VMEM((n_w, k, tn), F32), pltpu.VMEM((n_w, k, tn), BF16),
                            pltpu.SemaphoreType.DMA],
        ),
        compiler_params=_cparams(("arbitrary", "arbitrary")),
        name=name,
    )(tile_expert, next_expert, tiles_used, xs, *weights)


def _moe_combine_body(pos_ref, h_ref, gate_ref, fn_ref, eo_hbm, o_ref, buf, sem, *, tm):
    i = pl.program_id(0)

    def issue(tile, slot):
        base = tile * tm

        def body(t, carry):
            for j in range(TOP_K):
                _row_copy(eo_hbm, buf.at[slot, j], pos_ref[TOP_K * (base + t) + j], t,
                          sem.at[slot]).start(priority=j % 2)
            return carry

        lax.fori_loop(0, tm, body, 0, unroll=ROW_DMA_UNROLL // TOP_K)

    @pl.when(i == 0)
    def _():
        issue(0, 0)

    @pl.when(i + 1 < pl.num_programs(0))
    def _():
        issue(i + 1, (i + 1) % 2)

    slot = i % 2
    for j in range(TOP_K):
        pltpu.make_async_copy(eo_hbm.at[pl.ds(0, tm)], buf.at[slot, j], sem.at[slot]).wait()
    y = h_ref[...]
    for j in range(TOP_K):
        y = y + gate_ref[:, j:j + 1] * buf[slot, j]
    o_ref[...] = _rms(y) * fn_ref[...]


def _moe_combine(pos, h, gate, final_gain, eo, *, tm):
    m, d = h.shape
    return pl.pallas_call(
        functools.partial(_moe_combine_body, tm=tm),
        out_shape=jax.ShapeDtypeStruct((m, d), F32),
        grid_spec=pltpu.PrefetchScalarGridSpec(
            num_scalar_prefetch=1,
            grid=(m // tm,),
            in_specs=[pl.BlockSpec((tm, d), lambda i, pos: (i, 0)),
                      pl.BlockSpec((tm, LANES), lambda i, pos: (i, 0)),
                      pl.BlockSpec((1, d), lambda i, pos: (0, 0)),
                      pl.BlockSpec(memory_space=pl.ANY)],
            out_specs=pl.BlockSpec((tm, d), lambda i, pos: (i, 0)),
            scratch_shapes=[pltpu.VMEM((2, TOP_K, tm, d), F32), pltpu.SemaphoreType.DMA((2,))],
        ),
        compiler_params=_cparams(("arbitrary",)),
        name="moe_combine",
    )(pos, h, gate, final_gain, eo)


def _moe_plan(sel, n_experts):
    m = sel.shape[0]
    flat_e = sel[:, :TOP_K].reshape(-1)
    onehot = (flat_e[:, None] == jnp.arange(n_experts, dtype=jnp.int32)[None, :]).astype(jnp.int32)
    csum = jnp.cumsum(onehot, axis=0)
    rank = jnp.sum(csum * onehot, axis=1) - 1
    counts = csum[-1]
    padded = (counts + EXPERT_TILE - 1) // EXPERT_TILE * EXPERT_TILE
    ends = jnp.cumsum(padded)
    starts = ends - padded
    pos = (jnp.sum(onehot * starts[None, :], axis=1) + rank).astype(jnp.int32)
    n_rows = TOP_K * m + n_experts * EXPERT_TILE
    token = jnp.arange(TOP_K * m, dtype=jnp.int32) // TOP_K
    row_src = jnp.zeros((n_rows,), jnp.int32).at[pos].set(token)
    tile_start = jnp.arange(n_rows // EXPERT_TILE, dtype=jnp.int32) * EXPERT_TILE
    tile_expert = jnp.minimum(jnp.sum((tile_start[:, None] >= ends[None, :]).astype(jnp.int32), axis=1),
                              n_experts - 1).astype(jnp.int32)
    run_end = jnp.sum((tile_expert[None, :] <= tile_expert[:, None]).astype(jnp.int32), axis=1)
    n_tiles = tile_expert.shape[0]
    next_expert = jnp.where(run_end < n_tiles, tile_expert[jnp.minimum(run_end, n_tiles - 1)], -1).astype(jnp.int32)
    tiles_used = (ends[-1:] // EXPERT_TILE).astype(jnp.int32)
    return pos, row_src, tile_expert, next_expert, tiles_used, n_rows


def _largest_tile(n, cap, mult):
    best = None
    t = mult
    while t <= min(n, cap):
        if n % t == 0:
            best = t
        t += mult
    assert best is not None, (n, cap, mult)
    return best


def _pad_cols(w, n_to):
    return jnp.pad(w, ((0, 0), (0, n_to - w.shape[1])))


def _pad_rows(w, n_to):
    return jnp.pad(w, ((0, n_to - w.shape[0]), (0, 0)))


def _round_up(n, m):
    return -(-n // m) * m


def kernel(x, meta_tokens, a_norm, a_mix, a_w_rkv, a_w0, a_w1, a_w2, a_a0, a_a1, a_a2, a_g1, a_g2, a_k_k, a_k_a, a_r_k, a_lnx_w, a_lnx_b, a_w_out, kv_norm, w_kv, b_norm, b_w_q, b_sinks, b_w_out, f_norm, d_w_gate, d_w_up, d_w_down, e_router, e_w_gate, e_w_up, e_w_down, final_norm):
    assert x.shape[0] == 1
    seq, d = x.shape[1], x.shape[2]
    n_heads = d // HEAD
    n_experts = e_router.shape[-1]
    xs = x[0]

    l_real = N_META + seq
    l_pad = _round_up(l_real, CHUNK)
    h = jnp.concatenate([meta_tokens.astype(F32), xs, jnp.zeros((l_pad - l_real, d), F32)], axis=0)
    tm0 = _largest_tile(l_pad, 768, 16)
    tn = _largest_tile(d, 512, LANES)

    gate_rank = _round_up(a_g1.shape[-1], LANES)
    xr, xk, xv, lw, a, g = _premix(
        h, a_norm[0:1], a_mix[0],
        a_w1[0].astype(BF16), a_a1[0].astype(BF16), _pad_cols(a_g1[0], gate_rank).astype(BF16),
        a_w2[0].astype(BF16), a_a2[0].astype(BF16), _pad_rows(a_g2[0], gate_rank).astype(BF16),
        jnp.stack([a_w0[0], a_a0[0]]), _largest_tile(l_pad, 256, 16))
    tm_rkv = _largest_tile(l_pad, 1408, 16)
    r = _mm(xr, a_w_rkv[0], out_dtype=F32, tm=tm_rkv, tn=tn, w_index=0)
    k = _mm(xk, a_w_rkv[0], out_dtype=F32, tm=tm_rkv, tn=tn, w_index=1)
    v = _mm(xv, a_w_rkv[0], out_dtype=F32, tm=tm_rkv, tn=tn, w_index=2)
    prm = jnp.concatenate([a_k_k[0][None], a_k_a[0][None], a_r_k[0].reshape(1, d), a_lnx_w[0][None],
                           a_lnx_b[0][None], jnp.zeros((3, d), F32)], axis=0)
    pairs = 32 if n_heads % 64 == 0 else n_heads // 2
    mixed = _wkv(r, k, v, lw, a, g, prm, pairs=pairs)
    h = _mm(mixed, a_w_out[0], out_dtype=F32, tm=tm0, tn=tn, res=h)

    (hn,) = _rmsnorm(h, f_norm[0:1], [BF16], _largest_tile(l_pad, 256, 16))
    d_ff = d_w_gate.shape[-1]
    act = _swiglu(hn, d_w_gate[0], d_w_up[0], tm=_largest_tile(l_pad, 1408, 16), tn=_largest_tile(d_ff, 256, LANES))
    h = _mm_acc(act, d_w_down[0].astype(BF16), h, tm=tm0, tn=_largest_tile(d, 1024, LANES),
                tk=_largest_tile(d_ff, 6144, LANES))

    tm1 = _largest_tile(seq, 1024, 16)
    (hkv_meta,) = _rmsnorm(h[:CHUNK], kv_norm[None], [BF16], CHUNK)
    kv_meta = _mm(hkv_meta, w_kv, out_dtype=BF16, tm=CHUNK, tn=tn)
    kv_meta = _pad_rows(kv_meta[:N_META], BLOCK)
    h, hkv, hq = _drop_meta(h, jnp.stack([kv_norm, b_norm[0]]), seq, _largest_tile(seq, 256, N_META))
    kv_real = _mm(hkv, w_kv, out_dtype=BF16, tm=tm1, tn=tn)

    q = _mm(hq, b_w_q[0], out_dtype=BF16, tm=tm1, tn=tn, scale=HEAD ** -0.5)
    o = _attention(q, kv_real, kv_meta, b_sinks[0].reshape(ATT_KV_HEADS, -1), n_heads=n_heads)
    h = _mm(o, b_w_out[0], out_dtype=F32, tm=tm1, tn=tn, res=h)

    sel, gate = _router(h, f_norm[1:2], _pad_cols(e_router[0], LANES), n_experts=n_experts,
                        tm=_largest_tile(seq, 256, 16))
    pos, row_src, tile_expert, next_expert, tiles_used, n_rows = _moe_plan(sel, n_experts)
    xs_sorted = _gather_rows(row_src, h, f_norm[1:2], n_rows, rows=_largest_tile(n_rows, 512, EXPERT_TILE),
                             out_dtype=BF16)
    d_exp = e_w_gate.shape[-1]
    act = _moe_matmul(_moe_swiglu_body, tile_expert, next_expert, tiles_used, xs_sorted, (e_w_gate[0], e_w_up[0]),
                      tn=_largest_tile(d_exp, 896, LANES), out_dtype=BF16, name="moe_swiglu")
    eo = _moe_matmul(_moe_down_body, tile_expert, next_expert, tiles_used, act, (e_w_down[0],),
                     tn=_largest_tile(d, 1024, LANES), out_dtype=F32, name="moe_down")
    out = _moe_combine(pos, h, gate, final_norm[None], eo, tm=_largest_tile(seq, 128, 8))
    return out[None]
```
